```python
import jax, jax.numpy as jnp
from jax import lax
import numpy as np

D_MODEL = 1024
BATCH = 8
SEQ = 2048
DEPTH = 1
DEC_BATCH = 128
DEC_SEQ = 8
PAST_LEN = 16384
PAGE_SIZE = 128

D_MIX = D_MODEL
D_SSD = D_MIX // 2
SSD_HEAD_DIM = 64
N_SSD_HEADS = D_SSD // SSD_HEAD_DIM
SSD_STATE = 128
SSD_GROUPS = 2
CONV_W = 4
CONV_DIM = D_SSD + 2 * SSD_GROUPS * SSD_STATE
SSD_CHUNK = 64
D_GLA = D_MIX - D_SSD
N_GLA_HEADS = 4
GLA_DV = D_GLA // N_GLA_HEADS
GLA_DK = GLA_DV // 2
GLA_RANK = 16
GLA_GATE_NORMALIZER = 16.0
GLA_CHUNK = 64
D_FF = 2816
EPS = 1e-6
SSD_SPLITS = (D_SSD, CONV_DIM, N_SSD_HEADS)
GLA_SPLITS = (N_GLA_HEADS * GLA_DK, N_GLA_HEADS * GLA_DK, D_GLA, D_GLA, GLA_RANK)
IN_COLS = sum(SSD_SPLITS) + sum(GLA_SPLITS)

kernel_name = 'hymba_ssd_gla_macaron_step'


def _split(x, sizes):
    cuts = [int(c) for c in np.cumsum(sizes)[:-1]]
    return jnp.split(x, cuts, axis=-1)


def _rms(x, w):
    xf = x.astype(jnp.float32)
    out = xf * lax.rsqrt(jnp.mean(xf * xf, axis=-1, keepdims=True) + EPS) * w.astype(jnp.float32)
    return out.astype(x.dtype)


def _swiglu(x, w_up, w_down):
    gate, up = jnp.split(x @ w_up, 2, axis=-1)
    return (jax.nn.silu(gate) * up) @ w_down


def _chunk_len(l, c):
    return c if l % c == 0 else l


def _to_chunks(t, lc):
    b, l = t.shape[:2]
    return jnp.moveaxis(t.reshape((b, l // lc, lc) + t.shape[2:]), 1, 0)


def _from_chunks(t):
    t = jnp.moveaxis(t, 0, 1)
    return t.reshape((t.shape[0], t.shape[1] * t.shape[2]) + t.shape[3:])


def _ssd_chunked(xh, dt, a, bh, ch, s0):
    lc = _chunk_len(xh.shape[1], SSD_CHUNK)
    mask = jnp.tril(jnp.ones((lc, lc), dtype=bool))

    def step(s, inp):
        xc, dtc, bc, cc = inp
        acum = jnp.cumsum(dtc * a, axis=1)
        seg = acum[:, :, None, :] - acum[:, None, :, :]
        decay = jnp.exp(jnp.where(mask[None, :, :, None], seg, -jnp.inf))
        xdt = xc * dtc[..., None]
        scores = jnp.einsum('bihn,bjhn->bijh', cc, bc) * decay
        y = (jnp.einsum('bijh,bjhp->bihp', scores, xdt)
             + jnp.einsum('bihn,bhpn->bihp', cc, s) * jnp.exp(acum)[..., None])
        to_end = jnp.exp(acum[:, -1:, :] - acum)
        s_new = (s * jnp.exp(acum[:, -1, :])[:, :, None, None]
                 + jnp.einsum('bjhn,bjhp->bhpn', bc * to_end[..., None], xdt))
        return s_new, y

    s_fin, ys = lax.scan(step, s0, tuple(_to_chunks(t, lc) for t in (xh, dt, bh, ch)))
    return _from_chunks(ys), s_fin


def _gla_chunked(q, k, v, log_a, s0):
    lc = _chunk_len(q.shape[1], GLA_CHUNK)
    mask = jnp.tril(jnp.ones((lc, lc), dtype=bool))

    def step(s, inp):
        qc, kc, vc, lac = inp
        bcum = jnp.cumsum(lac, axis=1)
        seg = bcum[:, :, None] - bcum[:, None, :]
        decay = jnp.exp(jnp.where(mask[None, :, :, None, None], seg, -jnp.inf))
        att = jnp.einsum('bihd,bjhd,bijhd->bijh', qc, kc, decay)
        o = (jnp.einsum('bijh,bjhv->bihv', att, vc)
             + jnp.einsum('bihd,bhdv->bihv', qc * jnp.exp(bcum), s))
        kd = kc * jnp.exp(bcum[:, -1:] - bcum)
        s_new = s * jnp.exp(bcum[:, -1])[..., None] + jnp.einsum('bjhd,bjhv->bhdv', kd, vc)
        return s_new, o

    s_fin, os_ = lax.scan(step, s0, tuple(_to_chunks(t, lc) for t in (q, k, v, log_a)))
    return _from_chunks(os_), s_fin


def _mixer(h, ssd0, conv0, gla0, w_in, conv_w, conv_b, dt_bias, a_log, d_skip, g_ssd,
           w_gk2, b_gk, g_gla, w_out):
    f32 = jnp.float32
    bsz, l, _ = h.shape
    z, xbc, dt, q, k, v, g, gk_lr = _split(h @ w_in, SSD_SPLITS + GLA_SPLITS)
    xpad = jnp.concatenate([conv0.astype(xbc.dtype), xbc], axis=1)
    conv = conv_b
    for i in range(CONV_W):
        conv = conv + xpad[:, i:i + l] * conv_w[i]
    new_conv = xpad[:, l:]
    xs, bm, cm = _split(jax.nn.silu(conv).astype(f32),
                        (D_SSD, SSD_GROUPS * SSD_STATE, SSD_GROUPS * SSD_STATE))
    rep = N_SSD_HEADS // SSD_GROUPS
    xh = xs.reshape(bsz, l, N_SSD_HEADS, SSD_HEAD_DIM)
    bh = jnp.repeat(bm.reshape(bsz, l, SSD_GROUPS, SSD_STATE), rep, axis=2)
    ch = jnp.repeat(cm.reshape(bsz, l, SSD_GROUPS, SSD_STATE), rep, axis=2)
    dtp = jax.nn.softplus(dt.astype(f32) + dt_bias.astype(f32))
    a = -jnp.exp(a_log.astype(f32))
    y, ssd_new = _ssd_chunked(xh, dtp, a, bh, ch, ssd0.astype(f32))
    y = y + d_skip.astype(f32)[:, None] * xh
    y = y.reshape(bsz, l, D_SSD) * jax.nn.silu(z.astype(f32))
    gsz = D_SSD // SSD_GROUPS
    y = _rms(y.reshape(bsz, l, SSD_GROUPS, gsz), g_ssd.reshape(SSD_GROUPS, gsz)).reshape(bsz, l, D_SSD)
    qh = q.astype(f32).reshape(bsz, l, N_GLA_HEADS, GLA_DK) * GLA_DK ** -0.5
    kh = k.astype(f32).reshape(bsz, l, N_GLA_HEADS, GLA_DK)
    vh = v.astype(f32).reshape(bsz, l, N_GLA_HEADS, GLA_DV)
    log_a = jax.nn.log_sigmoid((gk_lr @ w_gk2 + b_gk).astype(f32)) / GLA_GATE_NORMALIZER
    o, gla_new = _gla_chunked(qh, kh, vh, log_a.reshape(bsz, l, N_GLA_HEADS, GLA_DK), gla0.astype(f32))
    o = _rms(o, g_gla).reshape(bsz, l, D_GLA) * jax.nn.silu(g.astype(f32))
    mixed = jnp.concatenate([y, o], axis=-1).astype(h.dtype)
    return mixed @ w_out, (ssd_new.astype(ssd0.dtype), new_conv.astype(conv0.dtype), gla_new.astype(gla0.dtype))


def _layer(x, ssd0, conv0, gla0, g_ffn1, w_ffn1_in, w_ffn1_out, g_mix, w_in, conv_w, conv_b,
           dt_bias, a_log, d_skip, g_ssd_norm, w_gk2, b_gk, g_gla_norm, w_out, g_ffn2,
           w_ffn2_in, w_ffn2_out):
    x = x + 0.5 * _swiglu(_rms(x, g_ffn1), w_ffn1_in, w_ffn1_out)
    mix, new_state = _mixer(_rms(x, g_mix), ssd0, conv0, gla0, w_in, conv_w, conv_b, dt_bias, a_log,
                            d_skip, g_ssd_norm, w_gk2, b_gk, g_gla_norm, w_out)
    x = x + mix
    x = x + 0.5 * _swiglu(_rms(x, g_ffn2), w_ffn2_in, w_ffn2_out)
    return x, new_state


def setup_inputs(seed: int = 0) -> dict:
    key = jax.random.key(seed)
    ks = jax.random.split(key, 32)
    nrm = jax.random.normal
    f32 = jnp.float32
    x_prompt = nrm(ks[0], (BATCH, SEQ, D_MODEL), f32)
    x_sample = nrm(ks[1], (DEC_BATCH, DEC_SEQ, D_MODEL), f32)
    state_ssd = 0.3 * nrm(ks[2], (DEPTH, DEC_BATCH, N_SSD_HEADS, SSD_HEAD_DIM, SSD_STATE), f32)
    state_conv = nrm(ks[3], (DEPTH, DEC_BATCH, CONV_W - 1, CONV_DIM), f32)
    state_gla = 0.3 * nrm(ks[4], (DEPTH, DEC_BATCH, N_GLA_HEADS, GLA_DK, GLA_DV), f32)
    g_ffn1 = 1.0 + 0.05 * nrm(ks[5], (DEPTH, D_MODEL), f32)
    w_ffn1_in = nrm(ks[6], (DEPTH, D_MODEL, 2 * D_FF), f32) * D_MODEL ** -0.5
    w_ffn1_out = nrm(ks[7], (DEPTH, D_FF, D_MODEL), f32) * D_FF ** -0.5
    g_mix = 1.0 + 0.05 * nrm(ks[8], (DEPTH, D_MODEL), f32)
    w_in = nrm(ks[9], (DEPTH, D_MODEL, IN_COLS), f32) * D_MODEL ** -0.5
    conv_w = nrm(ks[10], (DEPTH, CONV_W, CONV_DIM), f32) * CONV_W ** -0.5
    conv_b = 0.02 * nrm(ks[11], (DEPTH, CONV_DIM), f32)
    dt0 = jnp.exp(jax.random.uniform(ks[12], (DEPTH, N_SSD_HEADS), f32, np.log(1e-3), np.log(1e-1)))
    dt_bias = dt0 + jnp.log(-jnp.expm1(-dt0))
    a_log = jnp.log(jax.random.uniform(ks[13], (DEPTH, N_SSD_HEADS), f32, 1.0, 16.0))
    d_skip = 1.0 + 0.1 * nrm(ks[14], (DEPTH, N_SSD_HEADS), f32)
    g_ssd_norm = 1.0 + 0.05 * nrm(ks[15], (DEPTH, D_SSD), f32)
    w_gk2 = nrm(ks[16], (DEPTH, GLA_RANK, N_GLA_HEADS * GLA_DK), f32) * GLA_RANK ** -0.5
    b_gk = 0.1 * nrm(ks[17], (DEPTH, N_GLA_HEADS * GLA_DK), f32)
    g_gla_norm = 1.0 + 0.05 * nrm(ks[18], (DEPTH, GLA_DV), f32)
    w_out = nrm(ks[19], (DEPTH, D_MIX, D_MODEL), f32) * D_MIX ** -0.5
    g_ffn2 = 1.0 + 0.05 * nrm(ks[20], (DEPTH, D_MODEL), f32)
    w_ffn2_in = nrm(ks[21], (DEPTH, D_MODEL, 2 * D_FF), f32) * D_MODEL ** -0.5
    w_ffn2_out = nrm(ks[22], (DEPTH, D_FF, D_MODEL), f32) * D_FF ** -0.5
    g_final = 1.0 + 0.05 * nrm(ks[23], (D_MODEL,), f32)
    return {'x_prompt': x_prompt, 'x_sample': x_sample, 'state_ssd': state_ssd,
            'state_conv': state_conv, 'state_gla': state_gla, 'g_ffn1': g_ffn1,
            'w_ffn1_in': w_ffn1_in, 'w_ffn1_out': w_ffn1_out, 'g_mix': g_mix, 'w_in': w_in,
            'conv_w': conv_w, 'conv_b': conv_b, 'dt_bias': dt_bias, 'a_log': a_log,
            'd_skip': d_skip, 'g_ssd_norm': g_ssd_norm, 'w_gk2': w_gk2, 'b_gk': b_gk,
            'g_gla_norm': g_gla_norm, 'w_out': w_out, 'g_ffn2': g_ffn2,
            'w_ffn2_in': w_ffn2_in, 'w_ffn2_out': w_ffn2_out, 'g_final': g_final}


def reference(x_prompt, x_sample, state_ssd, state_conv, state_gla, g_ffn1, w_ffn1_in, w_ffn1_out,
              g_mix, w_in, conv_w, conv_b, dt_bias, a_log, d_skip, g_ssd_norm, w_gk2, b_gk,
              g_gla_norm, w_out, g_ffn2, w_ffn2_in, w_ffn2_out, g_final):
    bp = x_prompt.shape[0]
    xp, xs = x_prompt, x_sample
    p_ssd, p_conv, p_gla, s_ssd, s_conv, s_gla = [], [], [], [], [], []
    for i in range(DEPTH):
        lw = (g_ffn1[i], w_ffn1_in[i], w_ffn1_out[i], g_mix[i], w_in[i], conv_w[i], conv_b[i],
              dt_bias[i], a_log[i], d_skip[i], g_ssd_norm[i], w_gk2[i], b_gk[i], g_gla_norm[i],
              w_out[i], g_ffn2[i], w_ffn2_in[i], w_ffn2_out[i])
        z_ssd = jnp.zeros((bp,) + state_ssd.shape[2:], state_ssd.dtype)
        z_conv = jnp.zeros((bp,) + state_conv.shape[2:], state_conv.dtype)
        z_gla = jnp.zeros((bp,) + state_gla.shape[2:], state_gla.dtype)
        xp, (a1, a2, a3) = _layer(xp, z_ssd, z_conv, z_gla, *lw)
        xs, (b1, b2, b3) = _layer(xs, state_ssd[i], state_conv[i], state_gla[i], *lw)
        p_ssd.append(a1); p_conv.append(a2); p_gla.append(a3)
        s_ssd.append(b1); s_conv.append(b2); s_gla.append(b3)
    y_prompt = _rms(xp, g_final)
    y_sample = _rms(xs, g_final)
    return (y_prompt, y_sample, jnp.stack(p_ssd), jnp.stack(p_conv), jnp.stack(p_gla),
            jnp.stack(s_ssd), jnp.stack(s_conv), jnp.stack(s_gla))
```

```python
import functools

import jax
import jax.numpy as jnp
import numpy as np
from jax import lax
from jax.experimental import pallas as pl
from jax.experimental.pallas import tpu as pltpu

f32 = jnp.float32
bf16 = jnp.bfloat16

D_MODEL = 1024
D_SSD = 512
SSD_HEAD_DIM = 64
N_SSD_HEADS = 8
SSD_STATE = 128
SSD_GROUPS = 2
CONV_W = 4
CONV_DIM = D_SSD + 2 * SSD_GROUPS * SSD_STATE
D_GLA = 512
N_GLA_HEADS = 4
GLA_DV = 128
GLA_DK = 64
GLA_RANK = 16
GLA_GATE_NORMALIZER = 16.0
D_FF = 2816
EPS = 1e-6
D_QK = N_GLA_HEADS * GLA_DK

SUBLANES = 8
LANES = 128
VMEM_LIMIT_BYTES = 60 * 1024 * 1024

CHUNK = 128
SUB = SUBLANES
TOKEN_TILE = 256

PA_COLS = D_SSD + CONV_DIM + 2 * D_QK + 2 * D_GLA
PB_SMALL = LANES
PB_COLS = D_SSD + PB_SMALL
OFF_Z, OFF_XBC, OFF_Q, OFF_K, OFF_V, OFF_G = 0, 512, 1536, 1792, 2048, 2560


def _dot(a, b):
    return jnp.dot(a, b, preferred_element_type=f32)


def _dot_nt(a, b):
    return lax.dot_general(a, b, (((1,), (1,)), ((), ())), preferred_element_type=f32)


def _split3(x):
    hi = x.astype(bf16)
    r1 = x - hi.astype(f32)
    mid = r1.astype(bf16)
    lo = (r1 - mid.astype(f32)).astype(bf16)
    return hi, mid, lo


def _sel_dot(sel, x):
    hi, mid, lo = _split3(x)
    return _dot(sel, hi) + _dot(sel, mid) + _dot(sel, lo)


def _dot_x3(a, b):
    a_hi = a.astype(bf16)
    a_lo = (a - a_hi.astype(f32)).astype(bf16)
    b_hi = b.astype(bf16)
    b_lo = (b - b_hi.astype(f32)).astype(bf16)
    return _dot(a_hi, b_hi) + _dot(a_hi, b_lo) + _dot(a_lo, b_hi)


def _rms(x, g):
    return x * lax.rsqrt(jnp.mean(x * x, axis=-1, keepdims=True) + EPS) * g


def _swiglu_half_step(x, g, w_up, w_dn):
    h = _rms(x, g).astype(bf16)
    gu = _dot(h, w_up)
    act = (jax.nn.silu(gu[:, :D_FF]) * gu[:, D_FF:]).astype(bf16)
    return x + 0.5 * _dot(act, w_dn)


def _token_a_body(n_prompt, xp, xs, g1, wup, wdn, gmix, win, x1p, x1s, pap, pas, pbp, pbs):
    def compute(x_ref, x1_ref, pa_ref, pb_ref):
        x1 = _swiglu_half_step(x_ref[...], g1[...], wup[...], wdn[...])
        x1_ref[...] = x1
        pr = _dot(_rms(x1, gmix[...]).astype(bf16), win[...])
        pa_ref[...] = pr[:, :PA_COLS].astype(bf16)
        pb_ref[...] = pr[:, PA_COLS:]

    i = pl.program_id(0)
    pl.when(i < n_prompt)(lambda: compute(xp, x1p, pap, pbp))
    pl.when(i >= n_prompt)(lambda: compute(xs, x1s, pas, pbs))


def _token_b_body(n_prompt, x1p, x1s, mxp, mxs, wout, g2, wup, wdn, gfin, yp, ys):
    def compute(x_ref, m_ref, y_ref):
        x2 = x_ref[...] + _dot(m_ref[...], wout[...])
        x3 = _swiglu_half_step(x2, g2[...], wup[...], wdn[...])
        y_ref[...] = _rms(x3, gfin[...])

    i = pl.program_id(0)
    pl.when(i < n_prompt)(lambda: compute(x1p, mxp, yp))
    pl.when(i >= n_prompt)(lambda: compute(x1s, mxs, ys))


def _two_group_specs(n_prompt, cols, tile):
    prompt = pl.BlockSpec((tile, cols), lambda i: (jnp.minimum(i, n_prompt - 1), 0))
    sample = pl.BlockSpec((tile, cols), lambda i: (jnp.maximum(i - n_prompt, 0), 0))
    return prompt, sample


def _resident(shape):
    return pl.BlockSpec(shape, lambda i: (0,) * len(shape), pipeline_mode=pl.Buffered(1))


def _token_a(xp, xs, g1, wup, wdn, gmix, win):
    tp, ts = xp.shape[0], xs.shape[0]
    tile = TOKEN_TILE
    n_prompt, n_sample = tp // tile, ts // tile
    assert tp % tile == 0 and ts % tile == 0
    xin = _two_group_specs(n_prompt, D_MODEL, tile)
    pa = _two_group_specs(n_prompt, PA_COLS, tile)
    pb = _two_group_specs(n_prompt, PB_COLS, tile)
    return pl.pallas_call(
        functools.partial(_token_a_body, n_prompt),
        grid=(n_prompt + n_sample,),
        in_specs=[*xin, _resident(g1.shape), _resident(wup.shape), _resident(wdn.shape),
                  _resident(gmix.shape), _resident(win.shape)],
        out_specs=[*xin, *pa, *pb],
        out_shape=[jax.ShapeDtypeStruct((tp, D_MODEL), f32), jax.ShapeDtypeStruct((ts, D_MODEL), f32),
                   jax.ShapeDtypeStruct((tp, PA_COLS), bf16), jax.ShapeDtypeStruct((ts, PA_COLS), bf16),
                   jax.ShapeDtypeStruct((tp, PB_COLS), f32), jax.ShapeDtypeStruct((ts, PB_COLS), f32)],
        compiler_params=pltpu.CompilerParams(dimension_semantics=("arbitrary",), vmem_limit_bytes=VMEM_LIMIT_BYTES),
        name="token_a",
    )(xp, xs, g1, wup, wdn, gmix, win)


def _token_b(x1p, x1s, mxp, mxs, wout, g2, wup, wdn, gfin):
    tp, ts = x1p.shape[0], x1s.shape[0]
    tile = TOKEN_TILE
    n_prompt, n_sample = tp // tile, ts // tile
    xin = _two_group_specs(n_prompt, D_MODEL, tile)
    return pl.pallas_call(
        functools.partial(_token_b_body, n_prompt),
        grid=(n_prompt + n_sample,),
        in_specs=[*xin, *xin, _resident(wout.shape), _resident(g2.shape), _resident(wup.shape),
                  _resident(wdn.shape), _resident(gfin.shape)],
        out_specs=[*xin],
        out_shape=[jax.ShapeDtypeStruct((tp, D_MODEL), f32), jax.ShapeDtypeStruct((ts, D_MODEL), f32)],
        compiler_params=pltpu.CompilerParams(dimension_semantics=("arbitrary",), vmem_limit_bytes=VMEM_LIMIT_BYTES),
        name="token_b",
    )(x1p, x1s, mxp, mxs, wout, g2, wup, wdn, gfin)


def _mixer_consts(seq_rows):
    t = np.arange(CHUNK)
    same = (t[:, None] // seq_rows) == (t[None, :] // seq_rows)
    tri = (same & (t[None, :] <= t[:, None])).astype(np.float32)
    last = ((t[:, None] // seq_rows) * seq_rows + seq_rows - 1 == t[None, :]).astype(np.float32)
    halves = [h for h in (8, 16, 32, 64) if 2 * h <= seq_rows]
    selref, lvlmask = [], []
    for h in halves:
        ref_row = (t // (2 * h)) * (2 * h) + h
        selref.append((ref_row[:, None] == t[None, :]).astype(np.float32))
        right = (t % (2 * h)) >= h
        blk = (t[:, None] // (2 * h)) == (t[None, :] // (2 * h))
        lvlmask.append((blk & right[:, None] & ~right[None, :]).astype(np.float32))
    hq = np.arange(D_QK) // GLA_DK
    hv = np.arange(D_GLA) // GLA_DV
    gmat = (hq[:, None] == hv[None, :]).astype(np.float32)
    consts = dict(tri=jnp.asarray(tri, bf16), last=jnp.asarray(last, bf16),
                  gmat=jnp.asarray(gmat, bf16), bdmask=jnp.asarray(gmat.T, f32))
    if halves:
        consts["selref"] = jnp.asarray(np.concatenate(selref, 0), bf16)
        consts["lvlmask"] = jnp.asarray(np.stack(lvlmask, 0), f32)
    return consts, len(halves)


def _mixer_body(seq_rows, n_lvl, *refs):
    is_sample = seq_rows < CHUNK
    n_seq = CHUNK // seq_rows
    it = iter(refs)
    pa_ref, pb_ref = next(it), next(it)
    if is_sample:
        ssd0_ref, conv0_ref, gla0_ref = next(it), next(it), next(it)
    (convw_ref, convb_ref, dtb_e_ref, alog_e_ref, dsk_e_ref, gssd_ref, dtb_c_ref, alog_c_ref,
     wgk_ref, bgk_ref, ggla_ref, tri_ref, last_ref, gmat_ref, bdmask_ref) = (next(it) for _ in range(15))
    if n_lvl:
        selref_ref, lvlmask_ref = next(it), next(it)
    mixed_ref, nconv_ref, nssd_ref, ngla_ref = next(it), next(it), next(it), next(it)
    cbuf, kbuf, bbuf, vbuf = next(it), next(it), next(it), next(it)
    if not is_sample:
        st_ref, stbd_ref = next(it), next(it)
        c_idx = pl.program_id(1)

        @pl.when(c_idx == 0)
        def _():
            cbuf[:, 0:SUBLANES, :] = jnp.zeros((n_seq, SUBLANES, CONV_DIM), f32)
            st_ref[...] = jnp.zeros(st_ref.shape, f32)
            stbd_ref[...] = jnp.zeros(stbd_ref.shape, f32)

    C = CHUNK
    rows = lax.broadcasted_iota(jnp.int32, (C, 1), 0)
    row_in_sub = rows & (SUB - 1)
    tri = tri_ref[...]
    lane_qk = lax.shift_right_logical(lax.broadcasted_iota(jnp.int32, (1, D_QK), 1), int(np.log2(GLA_DK)))
    causal = tri.astype(f32) > 0

    pb = pb_ref[...]
    z = pa_ref[:, OFF_Z:OFF_Z + D_SSD].astype(f32)
    q = pa_ref[:, OFF_Q:OFF_Q + D_QK].astype(f32) * (GLA_DK ** -0.5)
    k = pa_ref[:, OFF_K:OFF_K + D_QK].astype(f32)
    v_bf = pa_ref[:, OFF_V:OFF_V + D_GLA]
    g = pa_ref[:, OFF_G:OFF_G + D_GLA].astype(f32)

    xbc = pa_ref[:, OFF_XBC:OFF_XBC + CONV_DIM].astype(f32)
    if is_sample:
        cbuf[:, SUBLANES - (CONV_W - 1):SUBLANES, :] = conv0_ref[...]
    cbuf[:, SUBLANES:SUBLANES + seq_rows, :] = xbc.reshape(n_seq, seq_rows, CONV_DIM)
    conv = jnp.broadcast_to(convb_ref[...], (C, CONV_DIM))
    for i in range(CONV_W):
        shift = CONV_W - 1 - i
        win = cbuf[:, SUBLANES - shift:SUBLANES - shift + seq_rows, :].reshape(C, CONV_DIM)
        conv = conv + win * convw_ref[i:i + 1, :]
    nconv_new = cbuf[:, SUBLANES + seq_rows - (CONV_W - 1):SUBLANES + seq_rows, :]
    if not is_sample:
        cbuf[:, 0:SUBLANES, :] = cbuf[:, seq_rows:seq_rows + SUBLANES, :]
    xc = jax.nn.silu(conv)
    xs = xc[:, :D_SSD]
    bm = xc[:, D_SSD:D_SSD + SSD_GROUPS * SSD_STATE].astype(bf16)
    cm = xc[:, D_SSD + SSD_GROUPS * SSD_STATE:].astype(bf16)

    dtp_e = jax.nn.softplus(pb[:, :D_SSD] + dtb_e_ref[...])
    da_e = dtp_e * (-jnp.exp(alog_e_ref[...]))
    small = pb[:, D_SSD:]
    da_c = jax.nn.softplus(small + dtb_c_ref[...]) * (-jnp.exp(alog_c_ref[...]))
    acum_e = _sel_dot(tri, da_e)
    acum_c = _sel_dot(tri, da_c)
    acum_t = acum_c.T
    acum_last = _sel_dot(last_ref[...], acum_e)
    xdt = xs * dtp_e
    xdt_bf = xdt.astype(bf16)
    lane = lax.broadcasted_iota(jnp.int32, (1, LANES), 1)
    left = lane < SSD_HEAD_DIM
    heads_per_group = N_SSD_HEADS // SSD_GROUPS
    gw = heads_per_group * SSD_HEAD_DIM
    y_parts = []
    for grp in range(SSD_GROUPS):
        cb = _dot_nt(cm[:, grp * SSD_STATE:(grp + 1) * SSD_STATE], bm[:, grp * SSD_STATE:(grp + 1) * SSD_STATE])
        for pair in range(heads_per_group // 2):
            sc = []
            for hh in range(2):
                h = grp * heads_per_group + pair * 2 + hh
                seg = acum_c[:, h:h + 1] - acum_t[h:h + 1, :]
                sc.append((cb * jnp.exp(jnp.where(causal, seg, -jnp.inf))).astype(bf16))
            lo = (grp * heads_per_group + pair * 2) * SSD_HEAD_DIM
            xp = xdt_bf[:, lo:lo + LANES]
            bd = jnp.concatenate([jnp.where(left, xp, 0), jnp.where(left, 0, xp)], axis=0)
            y_parts.append(_dot(jnp.concatenate(sc, axis=1), bd))
    y = jnp.concatenate(y_parts, axis=1)

    to_end = jnp.exp(jnp.minimum(acum_last - acum_e, 0.0))
    xs_end = (xdt * to_end).astype(bf16)
    e_acum = jnp.exp(acum_e)
    bm_t = bm.T

    def ssd_inter(st):
        st_bf = st.astype(bf16)
        return jnp.concatenate([_dot(cm[:, grp * SSD_STATE:(grp + 1) * SSD_STATE], st_bf[:, grp * gw:(grp + 1) * gw])
                                for grp in range(SSD_GROUPS)], axis=1) * e_acum

    def ssd_update(st, decay_row, xs_rows):
        upd = jnp.concatenate([_dot(bm_t[grp * SSD_STATE:(grp + 1) * SSD_STATE, :], xs_rows[:, grp * gw:(grp + 1) * gw])
                               for grp in range(SSD_GROUPS)], axis=1)
        return st * decay_row + upd

    gk = _dot_x3(small, wgk_ref[...]) + bgk_ref[...]
    log_a = jax.nn.log_sigmoid(gk) / GLA_GATE_NORMALIZER
    bcum = _sel_dot(tri, log_a)
    b_last = _sel_dot(last_ref[...], bcum)
    zeros8 = jnp.zeros((SUBLANES, D_GLA), f32)
    kbuf[0:SUBLANES, :] = zeros8[:, :D_QK]
    bbuf[0:SUBLANES, :] = zeros8[:, :D_QK]
    vbuf[0:SUBLANES, :] = zeros8
    kbuf[SUBLANES:, :] = k
    bbuf[SUBLANES:, :] = bcum
    vbuf[SUBLANES:, :] = v_bf.astype(f32)
    p_parts = []
    for d in range(SUB):
        ks = kbuf[SUBLANES - d:SUBLANES - d + C, :]
        bs = bbuf[SUBLANES - d:SUBLANES - d + C, :]
        p = q * ks * jnp.exp(jnp.minimum(bcum - bs, 0.0))
        p_parts.append(jnp.where(row_in_sub >= d, p, 0.0).astype(bf16))
    r = _dot(jnp.concatenate(p_parts, axis=0), gmat_ref[...])
    o = r[0:C] * vbuf[SUBLANES:SUBLANES + C, :]
    for d in range(1, SUB):
        o = o + r[d * C:(d + 1) * C] * vbuf[SUBLANES - d:SUBLANES - d + C, :]
    if n_lvl:
        bref = _sel_dot(selref_ref[...], bcum)
        lane_q = lane_qk
        att = [jnp.zeros((C, C), f32) for _ in range(N_GLA_HEADS)]
        for lvl in range(n_lvl):
            br = bref[lvl * C:(lvl + 1) * C]
            qe = q * jnp.exp(jnp.minimum(bcum - br, 0.0))
            ke = (k * jnp.exp(jnp.minimum(br - bcum, 0.0))).astype(bf16)
            qh = jnp.concatenate([jnp.where(lane_q == h, qe, 0.0) for h in range(N_GLA_HEADS)], axis=0).astype(bf16)
            a = _dot_nt(qh, ke)
            m = lvlmask_ref[lvl]
            for h in range(N_GLA_HEADS):
                att[h] = att[h] + a[h * C:(h + 1) * C] * m
        o = o + jnp.concatenate([_dot(att[h].astype(bf16), v_bf[:, h * GLA_DV:(h + 1) * GLA_DV])
                                 for h in range(N_GLA_HEADS)], axis=1)

    q_in = (q * jnp.exp(bcum)).astype(bf16)
    kd = (k * jnp.exp(jnp.minimum(b_last - bcum, 0.0))).astype(bf16)
    v_t = v_bf.T
    bdmask = bdmask_ref[...]

    def gla_inter(stbd):
        return _dot_nt(q_in, stbd.astype(bf16))

    def gla_update(stbd, decay_row, kd_rows):
        return stbd * decay_row + bdmask * _dot(v_t, kd_rows)

    if not is_sample:
        st = st_ref[...]
        stbd = stbd_ref[...]
        y = y + ssd_inter(st)
        o = o + gla_inter(stbd)
        st_new = ssd_update(st, jnp.exp(acum_last[C - 1:C, :]), xs_end)
        stbd_new = gla_update(stbd, jnp.exp(b_last[C - 1:C, :]), kd)
        st_ref[...] = st_new
        stbd_ref[...] = stbd_new

        @pl.when(c_idx == pl.num_programs(1) - 1)
        def _():
            nconv_ref[...] = nconv_new
            nssd_ref[0] = st_new.T
            gl = stbd_new[0:GLA_DV, :]
            acc = jnp.where(lane_qk == 0, gl, 0.0)
            for h in range(1, N_GLA_HEADS):
                acc = acc + jnp.where(lane_qk == h, stbd_new[h * GLA_DV:(h + 1) * GLA_DV, :], 0.0)
            ngla_ref[0] = acc.T
    else:
        nconv_ref[...] = nconv_new
        seq_of_row = lax.shift_right_logical(rows, int(np.log2(seq_rows)))
        y_inter = jnp.zeros((C, D_SSD), f32)
        o_inter = jnp.zeros((C, D_GLA), f32)
        for s in range(n_seq):
            mine = seq_of_row == s
            r0 = s * seq_rows
            st = ssd0_ref[s].T
            y_inter = y_inter + jnp.where(mine, ssd_inter(st), 0.0)
            st_new = ssd_update(st, jnp.exp(acum_last[r0:r0 + 1, :]), jnp.where(mine, xs_end, 0))
            nssd_ref[s] = st_new.T
            g0 = gla0_ref[s].T
            stbd = jnp.concatenate([jnp.where(lane_qk == h, g0, 0.0) for h in range(N_GLA_HEADS)], axis=0)
            o_inter = o_inter + jnp.where(mine, gla_inter(stbd), 0.0)
            stbd_new = gla_update(stbd, jnp.exp(b_last[r0:r0 + 1, :]), jnp.where(mine, kd, 0))
            acc = jnp.where(lane_qk == 0, stbd_new[0:GLA_DV, :], 0.0)
            for h in range(1, N_GLA_HEADS):
                acc = acc + jnp.where(lane_qk == h, stbd_new[h * GLA_DV:(h + 1) * GLA_DV, :], 0.0)
            ngla_ref[s] = acc.T
        y = y + y_inter
        o = o + o_inter

    y = (y + dsk_e_ref[...] * xs) * jax.nn.silu(z)
    gsz = D_SSD // SSD_GROUPS
    y = jnp.concatenate([_rms(y[:, i * gsz:(i + 1) * gsz], gssd_ref[:, i * gsz:(i + 1) * gsz])
                         for i in range(SSD_GROUPS)], axis=1)
    o = jnp.concatenate([_rms(o[:, h * GLA_DV:(h + 1) * GLA_DV], ggla_ref[...]) for h in range(N_GLA_HEADS)], axis=1)
    o = o * jax.nn.silu(g)
    mixed_ref[:, :D_SSD] = y.astype(bf16)
    mixed_ref[:, D_SSD:] = o.astype(bf16)


def _mixer(pa, pb, params, states, n_batch, seq_len):
    is_sample = states is not None
    seq_rows = seq_len if is_sample else CHUNK
    assert CHUNK % seq_rows == 0 and (is_sample or seq_len % CHUNK == 0)
    n_seq = CHUNK // seq_rows
    consts, n_lvl = _mixer_consts(seq_rows)
    const_list = [consts["tri"], consts["last"], consts["gmat"], consts["bdmask"]]
    if n_lvl:
        const_list += [consts["selref"], consts["lvlmask"]]

    if is_sample:
        grid = (n_batch // n_seq,)
        row_map = lambda c: (c, 0)
        seq_map = lambda c: (c, 0, 0)
        full = lambda shape: pl.BlockSpec(shape, lambda c: (0,) * len(shape))
        sems = ("arbitrary",)
    else:
        n_chunks = seq_len // CHUNK
        grid = (n_batch, n_chunks)
        row_map = lambda b, c: (b * n_chunks + c, 0)
        seq_map = lambda b, c: (b, 0, 0)
        full = lambda shape: pl.BlockSpec(shape, lambda b, c: (0,) * len(shape))
        sems = ("arbitrary", "arbitrary")

    in_arrays = [pa, pb]
    in_specs = [pl.BlockSpec((CHUNK, PA_COLS), row_map), pl.BlockSpec((CHUNK, PB_COLS), row_map)]
    if is_sample:
        ssd0, conv0, gla0 = states
        in_arrays += [ssd0, conv0, gla0]
        in_specs += [pl.BlockSpec((n_seq, D_SSD, SSD_STATE), seq_map),
                     pl.BlockSpec((n_seq, CONV_W - 1, CONV_DIM), seq_map),
                     pl.BlockSpec((n_seq, D_QK, GLA_DV), seq_map)]
    for arr in list(params) + const_list:
        in_arrays.append(arr)
        in_specs.append(full(arr.shape))

    out_shape = [jax.ShapeDtypeStruct((n_batch * seq_len, D_MODEL), bf16),
                 jax.ShapeDtypeStruct((n_batch, CONV_W - 1, CONV_DIM), f32),
                 jax.ShapeDtypeStruct((n_batch, D_SSD, SSD_STATE), f32),
                 jax.ShapeDtypeStruct((n_batch, D_QK, GLA_DV), f32)]
    out_specs = [pl.BlockSpec((CHUNK, D_MODEL), row_map),
                 pl.BlockSpec((n_seq, CONV_W - 1, CONV_DIM), seq_map),
                 pl.BlockSpec((n_seq, D_SSD, SSD_STATE), seq_map),
                 pl.BlockSpec((n_seq, D_QK, GLA_DV), seq_map)]
    scratch = [pltpu.VMEM((n_seq, SUBLANES + seq_rows, CONV_DIM), f32),
               pltpu.VMEM((SUBLANES + CHUNK, D_QK), f32),
               pltpu.VMEM((SUBLANES + CHUNK, D_QK), f32),
               pltpu.VMEM((SUBLANES + CHUNK, D_GLA), f32)]
    if not is_sample:
        scratch += [pltpu.VMEM((SSD_STATE, D_SSD), f32), pltpu.VMEM((D_GLA, D_QK), f32)]
    return pl.pallas_call(
        functools.partial(_mixer_body, seq_rows, n_lvl),
        grid=grid, in_specs=in_specs, out_specs=out_specs, out_shape=out_shape, scratch_shapes=scratch,
        compiler_params=pltpu.CompilerParams(dimension_semantics=sems, vmem_limit_bytes=VMEM_LIMIT_BYTES),
        name="mixer_sample" if is_sample else "mixer_prompt",
    )(*in_arrays)


def _regroup_w_in(w_in):
    o_xbc = D_SSD
    o_dt = o_xbc + CONV_DIM
    o_q = o_dt + N_SSD_HEADS
    o_k = o_q + D_QK
    o_v = o_k + D_QK
    o_g = o_v + D_GLA
    o_lr = o_g + D_GLA
    dt_cols = w_in[:, o_dt:o_q]
    small = jnp.concatenate([dt_cols, w_in[:, o_lr:o_lr + GLA_RANK],
                             jnp.zeros((D_MODEL, PB_SMALL - N_SSD_HEADS - GLA_RANK), w_in.dtype)], axis=1)
    cols = [w_in[:, :o_dt], w_in[:, o_q:o_lr], jnp.repeat(dt_cols, SSD_HEAD_DIM, axis=1), small]
    return jnp.concatenate(cols, axis=1).astype(bf16)


def _row(vec):
    return vec.reshape(1, -1).astype(f32)


def _pad_lanes(vec, width):
    return jnp.concatenate([vec.astype(f32), jnp.zeros((width - vec.shape[0],), f32)]).reshape(1, width)


def _mixer_params(conv_w, conv_b, dt_bias, a_log, d_skip, g_ssd_norm, w_gk2, b_gk, g_gla_norm):
    rep = lambda vec: _row(jnp.repeat(vec, SSD_HEAD_DIM))
    wgk = jnp.zeros((PB_SMALL, D_QK), f32).at[N_SSD_HEADS:N_SSD_HEADS + GLA_RANK, :].set(w_gk2.astype(f32))
    return [conv_w.astype(f32), _row(conv_b), rep(dt_bias), rep(a_log), rep(d_skip), _row(g_ssd_norm),
            _pad_lanes(dt_bias, PB_SMALL), _pad_lanes(a_log, PB_SMALL), wgk, _row(b_gk), _row(g_gla_norm)]


def kernel(x_prompt, x_sample, state_ssd, state_conv, state_gla, g_ffn1, w_ffn1_in, w_ffn1_out, g_mix, w_in, conv_w, conv_b, dt_bias, a_log, d_skip, g_ssd_norm, w_gk2, b_gk, g_gla_norm, w_out, g_ffn2, w_ffn2_in, w_ffn2_out, g_final):
    bp, lp, _ = x_prompt.shape
    bs, ls, _ = x_sample.shape
    assert w_in.shape[0] == 1, "single-layer step: the final norm is fused into the layer's last kernel"
    xp = x_prompt.reshape(bp * lp, D_MODEL)
    xs = x_sample.reshape(bs * ls, D_MODEL)
    x1p, x1s, pap, pas, pbp, pbs = _token_a(
        xp, xs, _row(g_ffn1[0]), w_ffn1_in[0].astype(bf16), w_ffn1_out[0].astype(bf16),
        _row(g_mix[0]), _regroup_w_in(w_in[0]))
    params = _mixer_params(conv_w[0], conv_b[0], dt_bias[0], a_log[0], d_skip[0], g_ssd_norm[0],
                           w_gk2[0], b_gk[0], g_gla_norm[0])
    mxp, p_conv, p_ssd, p_gla = _mixer(pap, pbp, params, None, bp, lp)
    s_states = (state_ssd[0].reshape(bs, D_SSD, SSD_STATE), state_conv[0], state_gla[0].reshape(bs, D_QK, GLA_DV))
    mxs, s_conv, s_ssd, s_gla = _mixer(pas, pbs, params, s_states, bs, ls)
    yp, ys = _token_b(x1p, x1s, mxp, mxs, w_out[0].astype(bf16), _row(g_ffn2[0]),
                      w_ffn2_in[0].astype(bf16), w_ffn2_out[0].astype(bf16), _row(g_final))
    return (yp.reshape(bp, lp, D_MODEL), ys.reshape(bs, ls, D_MODEL),
            p_ssd.reshape(1, bp, N_SSD_HEADS, SSD_HEAD_DIM, SSD_STATE), p_conv[None],
            p_gla.reshape(1, bp, N_GLA_HEADS, GLA_DK, GLA_DV),
            s_ssd.reshape(1, bs, N_SSD_HEADS, SSD_HEAD_DIM, SSD_STATE), s_conv[None],
            s_gla.reshape(1, bs, N_GLA_HEADS, GLA_DK, GLA_DV))
```

```python
import functools

import jax
import jax.numpy as jnp
import numpy as np
from jax import lax
from jax.experimental import pallas as pl
from jax.experimental.pallas import tpu as pltpu

f32 = jnp.float32
bf16 = jnp.bfloat16

D_MODEL = 1024
D_SSD = 512
SSD_HEAD_DIM = 64
N_SSD_HEADS = 8
SSD_STATE = 128
SSD_GROUPS = 2
CONV_W = 4
CONV_DIM = D_SSD + 2 * SSD_GROUPS * SSD_STATE
D_GLA = 512
N_GLA_HEADS = 4
GLA_DV = 128
GLA_DK = 64
GLA_RANK = 16
GLA_GATE_NORMALIZER = 16.0
D_FF = 2816
EPS = 1e-6
D_QK = N_GLA_HEADS * GLA_DK

SUBLANES = 8
LANES = 128
BF16_ROWS = 16
VMEM_LIMIT_BYTES = 60 * 1024 * 1024

CHUNK = 128
TOKEN_TILE = 256

PA_COLS = D_SSD + CONV_DIM + 2 * D_QK + 2 * D_GLA
PB_SMALL = LANES
PB_COLS = D_SSD + PB_SMALL
OFF_Z, OFF_XBC, OFF_Q, OFF_K, OFF_V, OFF_G = 0, 512, 1536, 1792, 2048, 2560


def _dot(a, b):
    return jnp.dot(a, b, preferred_element_type=f32)


def _dot_nt(a, b):
    return lax.dot_general(a, b, (((1,), (1,)), ((), ())), preferred_element_type=f32)


def _split3(x):
    hi = x.astype(bf16)
    r1 = x - hi.astype(f32)
    mid = r1.astype(bf16)
    lo = (r1 - mid.astype(f32)).astype(bf16)
    return hi, mid, lo


def _sel_dot(sel, x):
    n = x.shape[1]
    r = _dot(sel, jnp.concatenate(_split3(x), axis=1))
    return r[:, :n] + r[:, n:2 * n] + r[:, 2 * n:]


def _dot_x3(a, b):
    a_hi = a.astype(bf16)
    a_lo = (a - a_hi.astype(f32)).astype(bf16)
    b_hi = b.astype(bf16)
    b_lo = (b - b_hi.astype(f32)).astype(bf16)
    return _dot(a_hi, b_hi) + _dot(a_hi, b_lo) + _dot(a_lo, b_hi)


def _rms(x, g):
    return x * lax.rsqrt(jnp.mean(x * x, axis=-1, keepdims=True) + EPS) * g


def _swiglu_half_step(x, g, w_up, w_dn):
    h = _rms(x, g).astype(bf16)
    gu = _dot(h, w_up)
    act = (jax.nn.silu(gu[:, :D_FF]) * gu[:, D_FF:]).astype(bf16)
    return x + 0.5 * _dot(act, w_dn)


def _token_a_body(n_prompt, xp, xs, g1, wup, wdn, gmix, win, x1p, x1s, pap, pas, pbp, pbs):
    def compute(x_ref, x1_ref, pa_ref, pb_ref):
        x1 = _swiglu_half_step(x_ref[...], g1[...], wup[...], wdn[...])
        x1_ref[...] = x1
        pr = _dot(_rms(x1, gmix[...]).astype(bf16), win[...])
        pa_ref[...] = pr[:, :PA_COLS].astype(bf16)
        pb_ref[...] = pr[:, PA_COLS:]

    i = pl.program_id(0)
    pl.when(i < n_prompt)(lambda: compute(xp, x1p, pap, pbp))
    pl.when(i >= n_prompt)(lambda: compute(xs, x1s, pas, pbs))


def _token_b_body(n_prompt, x1p, x1s, mxp, mxs, wout, g2, wup, wdn, gfin, yp, ys):
    def compute(x_ref, m_ref, y_ref):
        x2 = x_ref[...] + _dot(m_ref[...], wout[...])
        x3 = _swiglu_half_step(x2, g2[...], wup[...], wdn[...])
        y_ref[...] = _rms(x3, gfin[...])

    i = pl.program_id(0)
    pl.when(i < n_prompt)(lambda: compute(x1p, mxp, yp))
    pl.when(i >= n_prompt)(lambda: compute(x1s, mxs, ys))


def _two_group_specs(n_prompt, cols, tile):
    prompt = pl.BlockSpec((tile, cols), lambda i: (jnp.minimum(i, n_prompt - 1), 0))
    sample = pl.BlockSpec((tile, cols), lambda i: (jnp.maximum(i - n_prompt, 0), 0))
    return prompt, sample


def _resident(shape):
    return pl.BlockSpec(shape, lambda i: (0,) * len(shape), pipeline_mode=pl.Buffered(1))


def _token_a(xp, xs, g1, wup, wdn, gmix, win):
    tp, ts = xp.shape[0], xs.shape[0]
    tile = TOKEN_TILE
    n_prompt, n_sample = tp // tile, ts // tile
    assert tp % tile == 0 and ts % tile == 0
    xin = _two_group_specs(n_prompt, D_MODEL, tile)
    pa = _two_group_specs(n_prompt, PA_COLS, tile)
    pb = _two_group_specs(n_prompt, PB_COLS, tile)
    return pl.pallas_call(
        functools.partial(_token_a_body, n_prompt),
        grid=(n_prompt + n_sample,),
        in_specs=[*xin, _resident(g1.shape), _resident(wup.shape), _resident(wdn.shape),
                  _resident(gmix.shape), _resident(win.shape)],
        out_specs=[*xin, *pa, *pb],
        out_shape=[jax.ShapeDtypeStruct((tp, D_MODEL), f32), jax.ShapeDtypeStruct((ts, D_MODEL), f32),
                   jax.ShapeDtypeStruct((tp, PA_COLS), bf16), jax.ShapeDtypeStruct((ts, PA_COLS), bf16),
                   jax.ShapeDtypeStruct((tp, PB_COLS), f32), jax.ShapeDtypeStruct((ts, PB_COLS), f32)],
        compiler_params=pltpu.CompilerParams(dimension_semantics=("arbitrary",), vmem_limit_bytes=VMEM_LIMIT_BYTES),
        name="token_a",
    )(xp, xs, g1, wup, wdn, gmix, win)


def _token_b(x1p, x1s, mxp, mxs, wout, g2, wup, wdn, gfin):
    tp, ts = x1p.shape[0], x1s.shape[0]
    tile = TOKEN_TILE
    n_prompt, n_sample = tp // tile, ts // tile
    xin = _two_group_specs(n_prompt, D_MODEL, tile)
    return pl.pallas_call(
        functools.partial(_token_b_body, n_prompt),
        grid=(n_prompt + n_sample,),
        in_specs=[*xin, *xin, _resident(wout.shape), _resident(g2.shape), _resident(wup.shape),
                  _resident(wdn.shape), _resident(gfin.shape)],
        out_specs=[*xin],
        out_shape=[jax.ShapeDtypeStruct((tp, D_MODEL), f32), jax.ShapeDtypeStruct((ts, D_MODEL), f32)],
        compiler_params=pltpu.CompilerParams(dimension_semantics=("arbitrary",), vmem_limit_bytes=VMEM_LIMIT_BYTES),
        name="token_b",
    )(x1p, x1s, mxp, mxs, wout, g2, wup, wdn, gfin)


def _level_halves(seq_rows):
    return [h for h in (1, 2, 4, 8, 16, 32, 64) if 2 * h <= seq_rows]


def _mixer_consts(seq_rows):
    t = np.arange(CHUNK)
    same = (t[:, None] // seq_rows) == (t[None, :] // seq_rows)
    tri = (same & (t[None, :] <= t[:, None])).astype(np.float32)
    last = ((t[:, None] // seq_rows) * seq_rows + seq_rows - 1 == t[None, :]).astype(np.float32)
    cum_gla, pair_mask = [tri], [np.eye(CHUNK, dtype=np.float32)]
    for h in _level_halves(seq_rows):
        ref_row = (t // (2 * h)) * (2 * h) + h
        cum_gla.append((ref_row[:, None] == t[None, :]).astype(np.float32) @ tri)
        right = (t % (2 * h)) >= h
        blk = (t[:, None] // (2 * h)) == (t[None, :] // (2 * h))
        pair_mask.append((blk & right[:, None] & ~right[None, :]).astype(np.float32))
    cum_gla.append(last @ tri)
    hq = np.arange(D_QK) // GLA_DK
    hv = np.arange(D_GLA) // GLA_DV
    consts = [jnp.asarray(np.concatenate([tri, last @ tri], 0), bf16),
              jnp.asarray(np.concatenate(cum_gla, 0), bf16),
              jnp.asarray(np.stack(pair_mask, 0), f32),
              jnp.asarray((hv[:, None] == hq[None, :]).astype(np.float32), f32)]
    if seq_rows == CHUNK:
        shift = np.zeros((CONV_W * CHUNK, 2 * CHUNK), np.float32)
        for k in range(CONV_W):
            shift[k * CHUNK + t, CHUNK + t - k] = 1.0
        consts.append(jnp.asarray(shift, bf16))
    return consts


PROMPT_STREAMS = 2
SAMPLE_STREAMS = 1
N_SHARED_REFS = 15
EPILOGUE = "epilogue"


def _round_robin(streams):
    live = list(streams)
    waiting = []
    while live:
        for s in list(live):
            try:
                if next(s) == EPILOGUE:
                    live.remove(s)
                    waiting.append(s)
            except StopIteration:
                live.remove(s)
    for s in waiting:
        for _ in s:
            pass


def _mixer_body(seq_rows, *refs):
    is_sample = seq_rows < CHUNK
    n_streams = refs[0].shape[0]
    n_in = 5 if is_sample else 2
    n_shared = N_SHARED_REFS if is_sample else N_SHARED_REFS + 1
    ins, shared = refs[:n_in], refs[n_in:n_in + n_shared]
    outs = refs[n_in + n_shared:n_in + n_shared + 4]
    scratch = refs[n_in + n_shared + 4:]
    is_last = None
    if not is_sample:
        xext, st_ref, stbd_ref = scratch
        c_idx = pl.program_id(1)
        is_last = c_idx == pl.num_programs(1) - 1

        @pl.when(c_idx == 0)
        def _():
            xext[:, 0:CHUNK, :] = jnp.zeros((n_streams, CHUNK, CONV_DIM), bf16)
            st_ref[...] = jnp.zeros(st_ref.shape, f32)
            stbd_ref[...] = jnp.zeros(stbd_ref.shape, f32)

    at = lambda group, j: [r.at[j] for r in group]
    _round_robin([_mixer_phases(seq_rows, at(ins, j), shared, at(outs, j), at(scratch, j), is_last)
                  for j in range(n_streams)])


def _mixer_phases(seq_rows, ins, shared, outs, scratch, is_last):
    is_sample = seq_rows < CHUNK
    n_seq = CHUNK // seq_rows
    halves = _level_halves(seq_rows)
    n_lvl = len(halves)
    if is_sample:
        pa_ref, pb_ref, ssd0_ref, conv0_ref, gla0_ref = ins
        (cbuf,) = scratch
    else:
        pa_ref, pb_ref = ins
        xext, st_ref, stbd_ref = scratch
    (convw_ref, convb_ref, dtb_e_ref, alog_e_ref, dsk_e_ref, gssd_ref, dtb_c_ref, alog_c_ref,
     wgk_ref, bgk_ref, ggla_ref, cum_ssd_ref, cum_gla_ref, pair_mask_ref, bdmask_ref) = shared[:N_SHARED_REFS]
    if not is_sample:
        shift_ref = shared[N_SHARED_REFS]
    mixed_ref, nconv_ref, nssd_ref, ngla_ref = outs

    C = CHUNK
    rows = lax.broadcasted_iota(jnp.int32, (C, 1), 0)
    lane_qk = lax.shift_right_logical(lax.broadcasted_iota(jnp.int32, (1, D_QK), 1), int(np.log2(GLA_DK)))

    pb = pb_ref[...]
    z = pa_ref[:, OFF_Z:OFF_Z + D_SSD].astype(f32)
    q = pa_ref[:, OFF_Q:OFF_Q + D_QK].astype(f32) * (GLA_DK ** -0.5)
    k = pa_ref[:, OFF_K:OFF_K + D_QK].astype(f32)
    v_bf = pa_ref[:, OFF_V:OFF_V + D_GLA]
    g = pa_ref[:, OFF_G:OFF_G + D_GLA].astype(f32)

    conv = jnp.broadcast_to(convb_ref[...], (C, CONV_DIM))
    if is_sample:
        cbuf[:, SUBLANES - (CONV_W - 1):SUBLANES, :] = conv0_ref[...]
        cbuf[:, SUBLANES:SUBLANES + seq_rows, :] = (
            pa_ref[:, OFF_XBC:OFF_XBC + CONV_DIM].astype(f32).reshape(n_seq, seq_rows, CONV_DIM))
        for i in range(CONV_W):
            shift = CONV_W - 1 - i
            win = cbuf[:, SUBLANES - shift:SUBLANES - shift + seq_rows, :].reshape(C, CONV_DIM)
            conv = conv + win * convw_ref[i:i + 1, :]
        nconv_ref[...] = cbuf[:, SUBLANES + seq_rows - (CONV_W - 1):SUBLANES + seq_rows, :]
    else:
        xext[C:, :] = pa_ref[:, OFF_XBC:OFF_XBC + CONV_DIM]
        taps = _dot(shift_ref[...], xext[...])
        for i in range(CONV_W):
            shift = CONV_W - 1 - i
            conv = conv + taps[shift * C:(shift + 1) * C] * convw_ref[i:i + 1, :]
        xext[C - BF16_ROWS:C, :] = xext[2 * C - BF16_ROWS:, :]
    yield
    xc = jax.nn.silu(conv)
    xs = xc[:, :D_SSD]
    bm = xc[:, D_SSD:D_SSD + SSD_GROUPS * SSD_STATE].astype(bf16)
    cm = xc[:, D_SSD + SSD_GROUPS * SSD_STATE:].astype(bf16)

    dtp_e = jax.nn.softplus(pb[:, :D_SSD] + dtb_e_ref[...])
    small = pb[:, D_SSD:]
    da = jnp.concatenate([dtp_e * (-jnp.exp(alog_e_ref[...])),
                          jax.nn.softplus(small + dtb_c_ref[...]) * (-jnp.exp(alog_c_ref[...]))], axis=1)
    cums = _sel_dot(cum_ssd_ref[...], da)
    yield
    acum_e = cums[:C, :D_SSD]
    acum_c = cums[:C, D_SSD:]
    acum_last = cums[C:, :D_SSD]
    acum_t = acum_c.T
    causal = cum_ssd_ref[0:C, :].astype(f32) > 0
    xdt = xs * dtp_e
    xdt_bf = xdt.astype(bf16)
    lane = lax.broadcasted_iota(jnp.int32, (1, LANES), 1)
    left = lane < SSD_HEAD_DIM
    heads_per_group = N_SSD_HEADS // SSD_GROUPS
    gw = heads_per_group * SSD_HEAD_DIM
    y_parts = []
    for grp in range(SSD_GROUPS):
        cb = _dot_nt(cm[:, grp * SSD_STATE:(grp + 1) * SSD_STATE], bm[:, grp * SSD_STATE:(grp + 1) * SSD_STATE])
        for pair in range(heads_per_group // 2):
            sc = []
            for hh in range(2):
                h = grp * heads_per_group + pair * 2 + hh
                seg = acum_c[:, h:h + 1] - acum_t[h:h + 1, :]
                sc.append((cb * jnp.exp(jnp.where(causal, seg, -jnp.inf))).astype(bf16))
            lo = (grp * heads_per_group + pair * 2) * SSD_HEAD_DIM
            xp = xdt_bf[:, lo:lo + LANES]
            bd = jnp.concatenate([jnp.where(left, xp, 0), jnp.where(left, 0, xp)], axis=0)
            y_parts.append(_dot(jnp.concatenate(sc, axis=1), bd))
            yield
    y = jnp.concatenate(y_parts, axis=1)

    to_end = jnp.exp(jnp.minimum(acum_last - acum_e, 0.0))
    xs_end = (xdt * to_end).astype(bf16)
    e_acum = jnp.exp(acum_e)
    bm_t = bm.T

    def ssd_inter(st):
        st_bf = st.astype(bf16)
        return jnp.concatenate([_dot(cm[:, grp * SSD_STATE:(grp + 1) * SSD_STATE], st_bf[:, grp * gw:(grp + 1) * gw])
                                for grp in range(SSD_GROUPS)], axis=1) * e_acum

    def ssd_update(st, decay_row, xs_rows):
        upd = jnp.concatenate([_dot(bm_t[grp * SSD_STATE:(grp + 1) * SSD_STATE, :], xs_rows[:, grp * gw:(grp + 1) * gw])
                               for grp in range(SSD_GROUPS)], axis=1)
        return st * decay_row + upd

    gk = _dot_x3(small, wgk_ref[...]) + bgk_ref[...]
    log_a = jax.nn.log_sigmoid(gk) / GLA_GATE_NORMALIZER
    bsel = _sel_dot(cum_gla_ref[...], log_a)
    yield
    bcum = bsel[:C]
    b_last = bsel[(n_lvl + 1) * C:]
    k_bf = k.astype(bf16)

    def head_rows(x_bf):
        return jnp.concatenate([jnp.where(lane_qk == h, x_bf, jnp.zeros_like(x_bf)) for h in range(N_GLA_HEADS)], axis=0)

    a = _dot_nt(head_rows(q.astype(bf16)), k_bf)
    yield
    m = pair_mask_ref[0]
    att = [a[h * C:(h + 1) * C] * m for h in range(N_GLA_HEADS)]
    for lvl, half in enumerate(halves):
        decay = jnp.exp(-jnp.abs(bcum - bsel[(lvl + 1) * C:(lvl + 2) * C]))
        in_right = (rows & half) != 0
        u = (jnp.where(in_right, q, k) * decay).astype(bf16)
        a = _dot_nt(head_rows(u), u)
        yield
        m = pair_mask_ref[lvl + 1]
        for h in range(N_GLA_HEADS):
            att[h] = att[h] + a[h * C:(h + 1) * C] * m
    o = jnp.concatenate([_dot(att[h].astype(bf16), v_bf[:, h * GLA_DV:(h + 1) * GLA_DV])
                         for h in range(N_GLA_HEADS)], axis=1)
    yield

    q_in = (q * jnp.exp(bcum)).astype(bf16)
    kd = (k * jnp.exp(jnp.minimum(b_last - bcum, 0.0))).astype(bf16)
    v_t = v_bf.T
    bdmask = bdmask_ref[...]

    def gla_inter(stbd):
        return _dot_nt(q_in, stbd.astype(bf16))

    def gla_update(stbd, decay_row, kd_rows):
        return stbd * decay_row + bdmask * _dot(v_t, kd_rows)

    def gla_heads(stbd):
        acc = jnp.where(lane_qk == 0, stbd[0:GLA_DV, :], 0.0)
        for h in range(1, N_GLA_HEADS):
            acc = acc + jnp.where(lane_qk == h, stbd[h * GLA_DV:(h + 1) * GLA_DV, :], 0.0)
        return acc.T

    if not is_sample:
        st = st_ref[...]
        stbd = stbd_ref[...]
        y = y + ssd_inter(st)
        o = o + gla_inter(stbd)
        yield
        st_ref[...] = ssd_update(st, jnp.exp(acum_last[C - 1:C, :]), xs_end)
        stbd_ref[...] = gla_update(stbd, jnp.exp(b_last[C - 1:C, :]), kd)
        yield
    else:
        seq_of_row = lax.shift_right_logical(rows, int(np.log2(seq_rows)))
        y_inter = jnp.zeros((C, D_SSD), f32)
        o_inter = jnp.zeros((C, D_GLA), f32)
        for s in range(n_seq):
            mine = seq_of_row == s
            r0 = s * seq_rows
            st = ssd0_ref[s].T
            y_inter = y_inter + jnp.where(mine, ssd_inter(st), 0.0)
            nssd_ref[s] = ssd_update(st, jnp.exp(acum_last[r0:r0 + 1, :]), jnp.where(mine, xs_end, 0)).T
            g0 = gla0_ref[s].T
            stbd = jnp.concatenate([jnp.where(lane_qk == h, g0, 0.0) for h in range(N_GLA_HEADS)], axis=0)
            o_inter = o_inter + jnp.where(mine, gla_inter(stbd), 0.0)
            ngla_ref[s] = gla_heads(gla_update(stbd, jnp.exp(b_last[r0:r0 + 1, :]), jnp.where(mine, kd, 0)))
            yield
        y = y + y_inter
        o = o + o_inter

    y = (y + dsk_e_ref[...] * xs) * jax.nn.silu(z)
    gsz = D_SSD // SSD_GROUPS
    y = jnp.concatenate([_rms(y[:, i * gsz:(i + 1) * gsz], gssd_ref[:, i * gsz:(i + 1) * gsz])
                         for i in range(SSD_GROUPS)], axis=1)
    o = jnp.concatenate([_rms(o[:, h * GLA_DV:(h + 1) * GLA_DV], ggla_ref[...]) for h in range(N_GLA_HEADS)], axis=1)
    o = o * jax.nn.silu(g)
    mixed_ref[:, :D_SSD] = y.astype(bf16)
    mixed_ref[:, D_SSD:] = o.astype(bf16)

    if not is_sample:
        yield EPILOGUE

        @pl.when(is_last)
        def _():
            carried = xext[C - BF16_ROWS:C, :].astype(f32)
            nconv_ref[0] = carried[BF16_ROWS - (CONV_W - 1):, :]
            nssd_ref[0] = st_ref[...].T
            ngla_ref[0] = gla_heads(stbd_ref[...])


def _mixer(pa, pb, params, states, n_batch, seq_len):
    is_sample = states is not None
    seq_rows = seq_len if is_sample else CHUNK
    assert CHUNK % seq_rows == 0 and (is_sample or seq_len % CHUNK == 0)
    n_seq = CHUNK // seq_rows
    consts = _mixer_consts(seq_rows)
    ns = SAMPLE_STREAMS if is_sample else PROMPT_STREAMS
    assert n_batch % (ns * n_seq) == 0
    rows_per_stream = n_batch * seq_len // ns
    seqs_per_stream = n_batch // ns

    if is_sample:
        grid = (seqs_per_stream // n_seq,)
        row_map = lambda c: (0, c, 0)
        seq_map = lambda c: (0, c, 0, 0)
        full = lambda shape: pl.BlockSpec(shape, lambda c: (0,) * len(shape))
        sems = ("arbitrary",)
    else:
        n_chunks = seq_len // CHUNK
        grid = (seqs_per_stream, n_chunks)
        row_map = lambda b, c: (0, b * n_chunks + c, 0)
        seq_map = lambda b, c: (0, b, 0, 0)
        full = lambda shape: pl.BlockSpec(shape, lambda b, c: (0,) * len(shape))
        sems = ("arbitrary", "arbitrary")

    def streamed(arr):
        return arr.reshape((ns, arr.shape[0] // ns) + arr.shape[1:])

    state_blocks = [(n_seq, CONV_W - 1, CONV_DIM), (n_seq, D_SSD, SSD_STATE), (n_seq, D_QK, GLA_DV)]
    in_arrays = [streamed(pa), streamed(pb)]
    in_specs = [pl.BlockSpec((ns, CHUNK, PA_COLS), row_map), pl.BlockSpec((ns, CHUNK, PB_COLS), row_map)]
    if is_sample:
        ssd0, conv0, gla0 = states
        in_arrays += [streamed(ssd0), streamed(conv0), streamed(gla0)]
        in_specs += [pl.BlockSpec((ns,) + state_blocks[i], seq_map) for i in (1, 0, 2)]
    for arr in list(params) + consts:
        in_arrays.append(arr)
        in_specs.append(full(arr.shape))

    out_shape = [jax.ShapeDtypeStruct((ns, rows_per_stream, D_MODEL), bf16)] + [
        jax.ShapeDtypeStruct((ns, seqs_per_stream) + blk[1:], f32) for blk in state_blocks]
    out_specs = [pl.BlockSpec((ns, CHUNK, D_MODEL), row_map)] + [
        pl.BlockSpec((ns,) + blk, seq_map) for blk in state_blocks]
    if is_sample:
        scratch = [pltpu.VMEM((ns, n_seq, SUBLANES + seq_rows, CONV_DIM), f32)]
    else:
        scratch = [pltpu.VMEM((ns, 2 * CHUNK, CONV_DIM), bf16),
                   pltpu.VMEM((ns, SSD_STATE, D_SSD), f32), pltpu.VMEM((ns, D_GLA, D_QK), f32)]
    mixed, nconv, nssd, ngla = pl.pallas_call(
        functools.partial(_mixer_body, seq_rows),
        grid=grid, in_specs=in_specs, out_specs=out_specs, out_shape=out_shape, scratch_shapes=scratch,
        compiler_params=pltpu.CompilerParams(dimension_semantics=sems, vmem_limit_bytes=VMEM_LIMIT_BYTES),
        name="mixer_sample" if is_sample else "mixer_prompt",
    )(*in_arrays)
    merged = lambda arr: arr.reshape((arr.shape[0] * arr.shape[1],) + arr.shape[2:])
    return merged(mixed), merged(nconv), merged(nssd), merged(ngla)


def _regroup_w_in(w_in):
    o_xbc = D_SSD
    o_dt = o_xbc + CONV_DIM
    o_q = o_dt + N_SSD_HEADS
    o_k = o_q + D_QK
    o_v = o_k + D_QK
    o_g = o_v + D_GLA
    o_lr = o_g + D_GLA
    dt_cols = w_in[:, o_dt:o_q]
    small = jnp.concatenate([dt_cols, w_in[:, o_lr:o_lr + GLA_RANK],
                             jnp.zeros((D_MODEL, PB_SMALL - N_SSD_HEADS - GLA_RANK), w_in.dtype)], axis=1)
    cols = [w_in[:, :o_dt], w_in[:, o_q:o_lr], jnp.repeat(dt_cols, SSD_HEAD_DIM, axis=1), small]
    return jnp.concatenate(cols, axis=1).astype(bf16)


def _row(vec):
    return vec.reshape(1, -1).astype(f32)


def _pad_lanes(vec, width):
    return jnp.concatenate([vec.astype(f32), jnp.zeros((width - vec.shape[0],), f32)]).reshape(1, width)


def _mixer_params(conv_w, conv_b, dt_bias, a_log, d_skip, g_ssd_norm, w_gk2, b_gk, g_gla_norm):
    rep = lambda vec: _row(jnp.repeat(vec, SSD_HEAD_DIM))
    wgk = jnp.zeros((PB_SMALL, D_QK), f32).at[N_SSD_HEADS:N_SSD_HEADS + GLA_RANK, :].set(w_gk2.astype(f32))
    return [conv_w.astype(f32), _row(conv_b), rep(dt_bias), rep(a_log), rep(d_skip), _row(g_ssd_norm),
            _pad_lanes(dt_bias, PB_SMALL), _pad_lanes(a_log, PB_SMALL), wgk, _row(b_gk), _row(g_gla_norm)]


def kernel(x_prompt, x_sample, state_ssd, state_conv, state_gla, g_ffn1, w_ffn1_in, w_ffn1_out, g_mix, w_in, conv_w, conv_b, dt_bias, a_log, d_skip, g_ssd_norm, w_gk2, b_gk, g_gla_norm, w_out, g_ffn2, w_ffn2_in, w_ffn2_out, g_final):
    bp, lp, _ = x_prompt.shape
    bs, ls, _ = x_sample.shape
    assert w_in.shape[0] == 1, "single-layer step: the final norm is fused into the layer's last kernel"
    xp = x_prompt.reshape(bp * lp, D_MODEL)
    xs = x_sample.reshape(bs * ls, D_MODEL)
    x1p, x1s, pap, pas, pbp, pbs = _token_a(
        xp, xs, _row(g_ffn1[0]), w_ffn1_in[0].astype(bf16), w_ffn1_out[0].astype(bf16),
        _row(g_mix[0]), _regroup_w_in(w_in[0]))
    params = _mixer_params(conv_w[0], conv_b[0], dt_bias[0], a_log[0], d_skip[0], g_ssd_norm[0],
                           w_gk2[0], b_gk[0], g_gla_norm[0])
    mxp, p_conv, p_ssd, p_gla = _mixer(pap, pbp, params, None, bp, lp)
    s_states = (state_ssd[0].reshape(bs, D_SSD, SSD_STATE), state_conv[0], state_gla[0].reshape(bs, D_QK, GLA_DV))
    mxs, s_conv, s_ssd, s_gla = _mixer(pas, pbs, params, s_states, bs, ls)
    yp, ys = _token_b(x1p, x1s, mxp, mxs, w_out[0].astype(bf16), _row(g_ffn2[0]),
                      w_ffn2_in[0].astype(bf16), w_ffn2_out[0].astype(bf16), _row(g_final))
    return (yp.reshape(bp, lp, D_MODEL), ys.reshape(bs, ls, D_MODEL),
            p_ssd.reshape(1, bp, N_SSD_HEADS, SSD_HEAD_DIM, SSD_STATE), p_conv[None],
            p_gla.reshape(1, bp, N_GLA_HEADS, GLA_DK, GLA_DV),
            s_ssd.reshape(1, bs, N_SSD_HEADS, SSD_HEAD_DIM, SSD_STATE), s_conv[None],
            s_gla.reshape(1, bs, N_GLA_HEADS, GLA_DK, GLA_DV))
```

```python
import functools

import jax
import jax.numpy as jnp
import numpy as np
from jax import lax
from jax.experimental import pallas as pl
from jax.experimental.pallas import tpu as pltpu

f32 = jnp.float32
bf16 = jnp.bfloat16

D_MODEL = 1024
D_SSD = 512
SSD_HEAD_DIM = 64
N_SSD_HEADS = 8
SSD_STATE = 128
SSD_GROUPS = 2
CONV_W = 4
CONV_DIM = D_SSD + 2 * SSD_GROUPS * SSD_STATE
D_GLA = 512
N_GLA_HEADS = 4
GLA_DV = 128
GLA_DK = 64
GLA_RANK = 16
GLA_GATE_NORMALIZER = 16.0
D_FF = 2816
EPS = 1e-6
D_QK = N_GLA_HEADS * GLA_DK

SUBLANES = 8
LANES = 128
BF16_ROWS = 16
VMEM_LIMIT_BYTES = 60 * 1024 * 1024

CHUNK = 128
TOKEN_TILE = 256
TOKEN_B_TILE = 512

PA_COLS = D_SSD + CONV_DIM + 2 * D_QK + 2 * D_GLA
PB_COLS = LANES
OFF_Z, OFF_XBC, OFF_Q, OFF_K, OFF_V, OFF_G = 0, 512, 1536, 1792, 2048, 2560


def _dot(a, b):
    return jnp.dot(a, b, preferred_element_type=f32)


def _dot_nt(a, b):
    return lax.dot_general(a, b, (((1,), (1,)), ((), ())), preferred_element_type=f32)


def _split3(x):
    hi = x.astype(bf16)
    r1 = x - hi.astype(f32)
    mid = r1.astype(bf16)
    lo = (r1 - mid.astype(f32)).astype(bf16)
    return hi, mid, lo


def _sel_dot(sel, x):
    n = x.shape[1]
    r = _dot(sel, jnp.concatenate(_split3(x), axis=1))
    return r[:, :n] + r[:, n:2 * n] + r[:, 2 * n:]


def _dot_x3(a, b):
    a_hi = a.astype(bf16)
    a_lo = (a - a_hi.astype(f32)).astype(bf16)
    b_hi = b.astype(bf16)
    b_lo = (b - b_hi.astype(f32)).astype(bf16)
    return _dot(a_hi, b_hi) + _dot(a_hi, b_lo) + _dot(a_lo, b_hi)


def _rms(x, g):
    return x * lax.rsqrt(jnp.mean(x * x, axis=-1, keepdims=True) + EPS) * g


def _swiglu_half_step(x, g, w_up, w_dn):
    h = _rms(x, g).astype(bf16)
    gu = _dot(h, w_up)
    act = (jax.nn.silu(gu[:, :D_FF]) * gu[:, D_FF:]).astype(bf16)
    return x + 0.5 * _dot(act, w_dn)


def _token_a_body(n_prompt, xp, xs, g1, wup, wdn, gmix, win, x1p, x1s, pap, pas, pbp, pbs):
    def compute(x_ref, x1_ref, pa_ref, pb_ref):
        x1 = _swiglu_half_step(x_ref[...], g1[...], wup[...], wdn[...])
        x1_ref[...] = x1
        pr = _dot(_rms(x1, gmix[...]).astype(bf16), win[...])
        pa_ref[...] = pr[:, :PA_COLS].astype(bf16)
        pb_ref[...] = pr[:, PA_COLS:]

    i = pl.program_id(0)
    pl.when(i < n_prompt)(lambda: compute(xp, x1p, pap, pbp))
    pl.when(i >= n_prompt)(lambda: compute(xs, x1s, pas, pbs))


def _token_b_body(n_prompt, x1p, x1s, mxp, mxs, wout, g2, wup, wdn, gfin, yp, ys):
    def compute(x_ref, m_ref, y_ref):
        x2 = x_ref[...] + _dot(m_ref[...], wout[...])
        x3 = _swiglu_half_step(x2, g2[...], wup[...], wdn[...])
        y_ref[...] = _rms(x3, gfin[...])

    i = pl.program_id(0)
    pl.when(i < n_prompt)(lambda: compute(x1p, mxp, yp))
    pl.when(i >= n_prompt)(lambda: compute(x1s, mxs, ys))


def _two_group_specs(n_prompt, cols, tile):
    prompt = pl.BlockSpec((tile, cols), lambda i: (jnp.minimum(i, n_prompt - 1), 0))
    sample = pl.BlockSpec((tile, cols), lambda i: (jnp.maximum(i - n_prompt, 0), 0))
    return prompt, sample


def _resident(shape):
    return pl.BlockSpec(shape, lambda i: (0,) * len(shape), pipeline_mode=pl.Buffered(1))


def _token_a(xp, xs, g1, wup, wdn, gmix, win):
    tp, ts = xp.shape[0], xs.shape[0]
    tile = TOKEN_TILE
    n_prompt, n_sample = tp // tile, ts // tile
    assert tp % tile == 0 and ts % tile == 0
    xin = _two_group_specs(n_prompt, D_MODEL, tile)
    pa = _two_group_specs(n_prompt, PA_COLS, tile)
    pb = _two_group_specs(n_prompt, PB_COLS, tile)
    return pl.pallas_call(
        functools.partial(_token_a_body, n_prompt),
        grid=(n_prompt + n_sample,),
        in_specs=[*xin, _resident(g1.shape), _resident(wup.shape), _resident(wdn.shape),
                  _resident(gmix.shape), _resident(win.shape)],
        out_specs=[*xin, *pa, *pb],
        out_shape=[jax.ShapeDtypeStruct((tp, D_MODEL), f32), jax.ShapeDtypeStruct((ts, D_MODEL), f32),
                   jax.ShapeDtypeStruct((tp, PA_COLS), bf16), jax.ShapeDtypeStruct((ts, PA_COLS), bf16),
                   jax.ShapeDtypeStruct((tp, PB_COLS), f32), jax.ShapeDtypeStruct((ts, PB_COLS), f32)],
        compiler_params=pltpu.CompilerParams(dimension_semantics=("arbitrary",), vmem_limit_bytes=VMEM_LIMIT_BYTES),
        name="token_a",
    )(xp, xs, g1, wup, wdn, gmix, win)


def _token_b(x1p, x1s, mxp, mxs, wout, g2, wup, wdn, gfin):
    tp, ts = x1p.shape[0], x1s.shape[0]
    tile = TOKEN_B_TILE
    assert tp % tile == 0 and ts % tile == 0
    n_prompt, n_sample = tp // tile, ts // tile
    xin = _two_group_specs(n_prompt, D_MODEL, tile)
    return pl.pallas_call(
        functools.partial(_token_b_body, n_prompt),
        grid=(n_prompt + n_sample,),
        in_specs=[*xin, *xin, _resident(wout.shape), _resident(g2.shape), _resident(wup.shape),
                  _resident(wdn.shape), _resident(gfin.shape)],
        out_specs=[*xin],
        out_shape=[jax.ShapeDtypeStruct((tp, D_MODEL), f32), jax.ShapeDtypeStruct((ts, D_MODEL), f32)],
        compiler_params=pltpu.CompilerParams(dimension_semantics=("arbitrary",), vmem_limit_bytes=VMEM_LIMIT_BYTES),
        name="token_b",
    )(x1p, x1s, mxp, mxs, wout, g2, wup, wdn, gfin)


def _level_halves(seq_rows):
    return [h for h in (1, 2, 4, 8, 16, 32, 64) if 2 * h <= seq_rows]


def _mixer_consts(seq_rows):
    t = np.arange(CHUNK)
    same = (t[:, None] // seq_rows) == (t[None, :] // seq_rows)
    tri = (same & (t[None, :] <= t[:, None])).astype(np.float32)
    last = ((t[:, None] // seq_rows) * seq_rows + seq_rows - 1 == t[None, :]).astype(np.float32)
    cum_gla, pair_mask = [tri], [np.eye(CHUNK, dtype=np.float32)]
    for h in _level_halves(seq_rows):
        ref_row = (t // (2 * h)) * (2 * h) + h
        cum_gla.append((ref_row[:, None] == t[None, :]).astype(np.float32) @ tri)
        right = (t % (2 * h)) >= h
        blk = (t[:, None] // (2 * h)) == (t[None, :] // (2 * h))
        pair_mask.append((blk & right[:, None] & ~right[None, :]).astype(np.float32))
    cum_gla.append(last @ tri)
    hq = np.arange(D_QK) // GLA_DK
    hv = np.arange(D_GLA) // GLA_DV
    head_of_lane = np.arange(D_SSD) // SSD_HEAD_DIM
    expand = (np.arange(PB_COLS)[:, None] == head_of_lane[None, :]).astype(np.float32)
    consts = [jnp.asarray(np.concatenate([tri, last @ tri], 0), bf16),
              jnp.asarray(np.concatenate(cum_gla, 0), bf16),
              jnp.asarray(np.stack(pair_mask, 0), f32),
              jnp.asarray((hv[:, None] == hq[None, :]).astype(np.float32), f32),
              jnp.asarray(np.concatenate([expand] * 3, 0), bf16)]
    if seq_rows == CHUNK:
        shift = np.zeros((CONV_W * CHUNK, 2 * CHUNK), np.float32)
        for k in range(CONV_W):
            shift[k * CHUNK + t, CHUNK + t - k] = 1.0
        consts.append(jnp.asarray(shift, bf16))
    return consts


PROMPT_STREAMS = 2
SAMPLE_STREAMS = 1
N_SHARED_REFS = 14
EPILOGUE = "epilogue"


def _round_robin(streams):
    live = list(streams)
    waiting = []
    while live:
        for s in list(live):
            try:
                if next(s) == EPILOGUE:
                    live.remove(s)
                    waiting.append(s)
            except StopIteration:
                live.remove(s)
    for s in waiting:
        for _ in s:
            pass


def _mixer_body(seq_rows, *refs):
    is_sample = seq_rows < CHUNK
    n_streams = refs[0].shape[0]
    n_in = 5 if is_sample else 2
    n_shared = N_SHARED_REFS if is_sample else N_SHARED_REFS + 1
    ins, shared = refs[:n_in], refs[n_in:n_in + n_shared]
    outs = refs[n_in + n_shared:n_in + n_shared + 4]
    scratch = refs[n_in + n_shared + 4:]
    is_last = None
    if not is_sample:
        xext, st_ref, stbd_ref = scratch
        c_idx = pl.program_id(1)
        is_last = c_idx == pl.num_programs(1) - 1

        @pl.when(c_idx == 0)
        def _():
            xext[:, 0:CHUNK, :] = jnp.zeros((n_streams, CHUNK, CONV_DIM), bf16)
            st_ref[...] = jnp.zeros(st_ref.shape, f32)
            stbd_ref[...] = jnp.zeros(stbd_ref.shape, f32)

    at = lambda group, j: [r.at[j] for r in group]
    _round_robin([_mixer_phases(seq_rows, at(ins, j), shared, at(outs, j), at(scratch, j), is_last)
                  for j in range(n_streams)])


def _mixer_phases(seq_rows, ins, shared, outs, scratch, is_last):
    is_sample = seq_rows < CHUNK
    n_seq = CHUNK // seq_rows
    halves = _level_halves(seq_rows)
    n_lvl = len(halves)
    if is_sample:
        pa_ref, pb_ref, ssd0_ref, conv0_ref, gla0_ref = ins
        (cbuf,) = scratch
    else:
        pa_ref, pb_ref = ins
        xext, st_ref, stbd_ref = scratch
    (convw_ref, convb_ref, dsk_e_ref, gssd_ref, dtb_c_ref, alog_c_ref, wgk_ref, bgk_ref, ggla_ref,
     cum_ssd_ref, cum_gla_ref, pair_mask_ref, bdmask_ref, expand_ref) = shared[:N_SHARED_REFS]
    if not is_sample:
        shift_ref = shared[N_SHARED_REFS]
    mixed_ref, nconv_ref, nssd_ref, ngla_ref = outs

    C = CHUNK
    rows = lax.broadcasted_iota(jnp.int32, (C, 1), 0)
    lane_qk = lax.shift_right_logical(lax.broadcasted_iota(jnp.int32, (1, D_QK), 1), int(np.log2(GLA_DK)))

    z = pa_ref[:, OFF_Z:OFF_Z + D_SSD].astype(f32)
    q = pa_ref[:, OFF_Q:OFF_Q + D_QK].astype(f32) * (GLA_DK ** -0.5)
    k = pa_ref[:, OFF_K:OFF_K + D_QK].astype(f32)
    v_bf = pa_ref[:, OFF_V:OFF_V + D_GLA]
    g = pa_ref[:, OFF_G:OFF_G + D_GLA].astype(f32)

    conv = jnp.broadcast_to(convb_ref[...], (C, CONV_DIM))
    if is_sample:
        cbuf[:, SUBLANES - (CONV_W - 1):SUBLANES, :] = conv0_ref[...]
        cbuf[:, SUBLANES:SUBLANES + seq_rows, :] = (
            pa_ref[:, OFF_XBC:OFF_XBC + CONV_DIM].astype(f32).reshape(n_seq, seq_rows, CONV_DIM))
        for i in range(CONV_W):
            shift = CONV_W - 1 - i
            win = cbuf[:, SUBLANES - shift:SUBLANES - shift + seq_rows, :].reshape(C, CONV_DIM)
            conv = conv + win * convw_ref[i:i + 1, :]
        nconv_ref[...] = cbuf[:, SUBLANES + seq_rows - (CONV_W - 1):SUBLANES + seq_rows, :]
    else:
        xext[C:, :] = pa_ref[:, OFF_XBC:OFF_XBC + CONV_DIM]
        taps = _dot(shift_ref[...], xext[...])
        for i in range(CONV_W):
            shift = CONV_W - 1 - i
            conv = conv + taps[shift * C:(shift + 1) * C] * convw_ref[i:i + 1, :]
        xext[C - BF16_ROWS:C, :] = xext[2 * C - BF16_ROWS:, :]
    yield
    xc = jax.nn.silu(conv)
    xs = xc[:, :D_SSD]
    bm = xc[:, D_SSD:D_SSD + SSD_GROUPS * SSD_STATE].astype(bf16)
    cm = xc[:, D_SSD + SSD_GROUPS * SSD_STATE:].astype(bf16)

    small = pb_ref[...]
    dtp_c = jax.nn.softplus(small + dtb_c_ref[...])
    cums = _sel_dot(cum_ssd_ref[...], dtp_c * (-jnp.exp(alog_c_ref[...])))
    yield
    acum_c = cums[:C]
    acum_t = acum_c.T
    wide = _dot(jnp.concatenate(_split3(jnp.concatenate([cums, dtp_c], axis=0)), axis=1), expand_ref[...])
    yield
    acum_e, acum_last, dtp_e = wide[:C], wide[C:2 * C], wide[2 * C:]
    causal = cum_ssd_ref[0:C, :].astype(f32) > 0
    xdt = xs * dtp_e
    xdt_bf = xdt.astype(bf16)
    lane = lax.broadcasted_iota(jnp.int32, (1, LANES), 1)
    left = lane < SSD_HEAD_DIM
    heads_per_group = N_SSD_HEADS // SSD_GROUPS
    gw = heads_per_group * SSD_HEAD_DIM
    y_parts = []
    for grp in range(SSD_GROUPS):
        cb = _dot_nt(cm[:, grp * SSD_STATE:(grp + 1) * SSD_STATE], bm[:, grp * SSD_STATE:(grp + 1) * SSD_STATE])
        for pair in range(heads_per_group // 2):
            sc = []
            for hh in range(2):
                h = grp * heads_per_group + pair * 2 + hh
                seg = acum_c[:, h:h + 1] - acum_t[h:h + 1, :]
                sc.append((cb * jnp.exp(jnp.where(causal, seg, -jnp.inf))).astype(bf16))
            lo = (grp * heads_per_group + pair * 2) * SSD_HEAD_DIM
            xp = xdt_bf[:, lo:lo + LANES]
            bd = jnp.concatenate([jnp.where(left, xp, 0), jnp.where(left, 0, xp)], axis=0)
            y_parts.append(_dot(jnp.concatenate(sc, axis=1), bd))
            yield
    y = jnp.concatenate(y_parts, axis=1)

    to_end = jnp.exp(jnp.minimum(acum_last - acum_e, 0.0))
    xs_end = (xdt * to_end).astype(bf16)
    e_acum = jnp.exp(acum_e)
    bm_t = bm.T

    def ssd_inter(st):
        st_bf = st.astype(bf16)
        return jnp.concatenate([_dot(cm[:, grp * SSD_STATE:(grp + 1) * SSD_STATE], st_bf[:, grp * gw:(grp + 1) * gw])
                                for grp in range(SSD_GROUPS)], axis=1) * e_acum

    def ssd_update(st, decay_row, xs_rows):
        upd = jnp.concatenate([_dot(bm_t[grp * SSD_STATE:(grp + 1) * SSD_STATE, :], xs_rows[:, grp * gw:(grp + 1) * gw])
                               for grp in range(SSD_GROUPS)], axis=1)
        return st * decay_row + upd

    gk = _dot_x3(small, wgk_ref[...]) + bgk_ref[...]
    log_a = jax.nn.log_sigmoid(gk) / GLA_GATE_NORMALIZER
    bsel = _sel_dot(cum_gla_ref[...], log_a)
    yield
    bcum = bsel[:C]
    b_last = bsel[(n_lvl + 1) * C:]
    k_bf = k.astype(bf16)

    def head_rows(x_bf):
        return jnp.concatenate([jnp.where(lane_qk == h, x_bf, jnp.zeros_like(x_bf)) for h in range(N_GLA_HEADS)], axis=0)

    a = _dot_nt(head_rows(q.astype(bf16)), k_bf)
    yield
    m = pair_mask_ref[0]
    att = [a[h * C:(h + 1) * C] * m for h in range(N_GLA_HEADS)]
    for lvl, half in enumerate(halves):
        decay = jnp.exp(-jnp.abs(bcum - bsel[(lvl + 1) * C:(lvl + 2) * C]))
        in_right = (rows & half) != 0
        u = (jnp.where(in_right, q, k) * decay).astype(bf16)
        a = _dot_nt(head_rows(u), u)
        yield
        m = pair_mask_ref[lvl + 1]
        for h in range(N_GLA_HEADS):
            att[h] = att[h] + a[h * C:(h + 1) * C] * m
    o = jnp.concatenate([_dot(att[h].astype(bf16), v_bf[:, h * GLA_DV:(h + 1) * GLA_DV])
                         for h in range(N_GLA_HEADS)], axis=1)
    yield

    q_in = (q * jnp.exp(bcum)).astype(bf16)
    kd = (k * jnp.exp(jnp.minimum(b_last - bcum, 0.0))).astype(bf16)
    v_t = v_bf.T
    bdmask = bdmask_ref[...]

    def gla_inter(stbd):
        return _dot_nt(q_in, stbd.astype(bf16))

    def gla_update(stbd, decay_row, kd_rows):
        return stbd * decay_row + bdmask * _dot(v_t, kd_rows)

    def gla_heads(stbd):
        acc = jnp.where(lane_qk == 0, stbd[0:GLA_DV, :], 0.0)
        for h in range(1, N_GLA_HEADS):
            acc = acc + jnp.where(lane_qk == h, stbd[h * GLA_DV:(h + 1) * GLA_DV, :], 0.0)
        return acc.T

    if not is_sample:
        st = st_ref[...]
        stbd = stbd_ref[...]
        y = y + ssd_inter(st)
        o = o + gla_inter(stbd)
        yield
        st_ref[...] = ssd_update(st, jnp.exp(acum_last[C - 1:C, :]), xs_end)
        stbd_ref[...] = gla_update(stbd, jnp.exp(b_last[C - 1:C, :]), kd)
        yield
    else:
        seq_of_row = lax.shift_right_logical(rows, int(np.log2(seq_rows)))
        y_inter = jnp.zeros((C, D_SSD), f32)
        o_inter = jnp.zeros((C, D_GLA), f32)
        for s in range(n_seq):
            mine = seq_of_row == s
            r0 = s * seq_rows
            st = ssd0_ref[s].T
            y_inter = y_inter + jnp.where(mine, ssd_inter(st), 0.0)
            nssd_ref[s] = ssd_update(st, jnp.exp(acum_last[r0:r0 + 1, :]), jnp.where(mine, xs_end, 0)).T
            g0 = gla0_ref[s].T
            stbd = jnp.concatenate([jnp.where(lane_qk == h, g0, 0.0) for h in range(N_GLA_HEADS)], axis=0)
            o_inter = o_inter + jnp.where(mine, gla_inter(stbd), 0.0)
            ngla_ref[s] = gla_heads(gla_update(stbd, jnp.exp(b_last[r0:r0 + 1, :]), jnp.where(mine, kd, 0)))
            yield
        y = y + y_inter
        o = o + o_inter

    y = (y + dsk_e_ref[...] * xs) * jax.nn.silu(z)
    gsz = D_SSD // SSD_GROUPS
    y = jnp.concatenate([_rms(y[:, i * gsz:(i + 1) * gsz], gssd_ref[:, i * gsz:(i + 1) * gsz])
                         for i in range(SSD_GROUPS)], axis=1)
    o = jnp.concatenate([_rms(o[:, h * GLA_DV:(h + 1) * GLA_DV], ggla_ref[...]) for h in range(N_GLA_HEADS)], axis=1)
    o = o * jax.nn.silu(g)
    mixed_ref[:, :D_SSD] = y.astype(bf16)
    mixed_ref[:, D_SSD:] = o.astype(bf16)

    if not is_sample:
        yield EPILOGUE

        @pl.when(is_last)
        def _():
            carried = xext[C - BF16_ROWS:C, :].astype(f32)
            nconv_ref[0] = carried[BF16_ROWS - (CONV_W - 1):, :]
            nssd_ref[0] = st_ref[...].T
            ngla_ref[0] = gla_heads(stbd_ref[...])


def _mixer(pa, pb, params, states, n_batch, seq_len):
    is_sample = states is not None
    seq_rows = seq_len if is_sample else CHUNK
    assert CHUNK % seq_rows == 0 and (is_sample or seq_len % CHUNK == 0)
    n_seq = CHUNK // seq_rows
    consts = _mixer_consts(seq_rows)
    ns = SAMPLE_STREAMS if is_sample else PROMPT_STREAMS
    assert n_batch % (ns * n_seq) == 0
    rows_per_stream = n_batch * seq_len // ns
    seqs_per_stream = n_batch // ns

    if is_sample:
        grid = (seqs_per_stream // n_seq,)
        row_map = lambda c: (0, c, 0)
        seq_map = lambda c: (0, c, 0, 0)
        full = lambda shape: pl.BlockSpec(shape, lambda c: (0,) * len(shape))
        sems = ("arbitrary",)
    else:
        n_chunks = seq_len // CHUNK
        grid = (seqs_per_stream, n_chunks)
        row_map = lambda b, c: (0, b * n_chunks + c, 0)
        seq_map = lambda b, c: (0, b, 0, 0)
        full = lambda shape: pl.BlockSpec(shape, lambda b, c: (0,) * len(shape))
        sems = ("arbitrary", "arbitrary")

    def streamed(arr):
        return arr.reshape((ns, arr.shape[0] // ns) + arr.shape[1:])

    state_blocks = [(n_seq, CONV_W - 1, CONV_DIM), (n_seq, D_SSD, SSD_STATE), (n_seq, D_QK, GLA_DV)]
    in_arrays = [streamed(pa), streamed(pb)]
    in_specs = [pl.BlockSpec((ns, CHUNK, PA_COLS), row_map), pl.BlockSpec((ns, CHUNK, PB_COLS), row_map)]
    if is_sample:
        ssd0, conv0, gla0 = states
        in_arrays += [streamed(ssd0), streamed(conv0), streamed(gla0)]
        in_specs += [pl.BlockSpec((ns,) + state_blocks[i], seq_map) for i in (1, 0, 2)]
    for arr in list(params) + consts:
        in_arrays.append(arr)
        in_specs.append(full(arr.shape))

    out_shape = [jax.ShapeDtypeStruct((ns, rows_per_stream, D_MODEL), bf16)] + [
        jax.ShapeDtypeStruct((ns, seqs_per_stream) + blk[1:], f32) for blk in state_blocks]
    out_specs = [pl.BlockSpec((ns, CHUNK, D_MODEL), row_map)] + [
        pl.BlockSpec((ns,) + blk, seq_map) for blk in state_blocks]
    if is_sample:
        scratch = [pltpu.VMEM((ns, n_seq, SUBLANES + seq_rows, CONV_DIM), f32)]
    else:
        scratch = [pltpu.VMEM((ns, 2 * CHUNK, CONV_DIM), bf16),
                   pltpu.VMEM((ns, SSD_STATE, D_SSD), f32), pltpu.VMEM((ns, D_GLA, D_QK), f32)]
    mixed, nconv, nssd, ngla = pl.pallas_call(
        functools.partial(_mixer_body, seq_rows),
        grid=grid, in_specs=in_specs, out_specs=out_specs, out_shape=out_shape, scratch_shapes=scratch,
        compiler_params=pltpu.CompilerParams(dimension_semantics=sems, vmem_limit_bytes=VMEM_LIMIT_BYTES),
        name="mixer_sample" if is_sample else "mixer_prompt",
    )(*in_arrays)
    merged = lambda arr: arr.reshape((arr.shape[0] * arr.shape[1],) + arr.shape[2:])
    return merged(mixed), merged(nconv), merged(nssd), merged(ngla)


def _regroup_w_in(w_in):
    o_xbc = D_SSD
    o_dt = o_xbc + CONV_DIM
    o_q = o_dt + N_SSD_HEADS
    o_k = o_q + D_QK
    o_v = o_k + D_QK
    o_g = o_v + D_GLA
    o_lr = o_g + D_GLA
    dt_cols = w_in[:, o_dt:o_q]
    small = jnp.concatenate([dt_cols, w_in[:, o_lr:o_lr + GLA_RANK],
                             jnp.zeros((D_MODEL, PB_COLS - N_SSD_HEADS - GLA_RANK), w_in.dtype)], axis=1)
    return jnp.concatenate([w_in[:, :o_dt], w_in[:, o_q:o_lr], small], axis=1).astype(bf16)


def _row(vec):
    return vec.reshape(1, -1).astype(f32)


def _pad_lanes(vec, width):
    return jnp.concatenate([vec.astype(f32), jnp.zeros((width - vec.shape[0],), f32)]).reshape(1, width)


def _mixer_params(conv_w, conv_b, dt_bias, a_log, d_skip, g_ssd_norm, w_gk2, b_gk, g_gla_norm):
    rep = lambda vec: _row(jnp.repeat(vec, SSD_HEAD_DIM))
    wgk = jnp.zeros((PB_COLS, D_QK), f32).at[N_SSD_HEADS:N_SSD_HEADS + GLA_RANK, :].set(w_gk2.astype(f32))
    return [conv_w.astype(f32), _row(conv_b), rep(d_skip), _row(g_ssd_norm),
            _pad_lanes(dt_bias, PB_COLS), _pad_lanes(a_log, PB_COLS), wgk, _row(b_gk), _row(g_gla_norm)]


def kernel(x_prompt, x_sample, state_ssd, state_conv, state_gla, g_ffn1, w_ffn1_in, w_ffn1_out, g_mix, w_in, conv_w, conv_b, dt_bias, a_log, d_skip, g_ssd_norm, w_gk2, b_gk, g_gla_norm, w_out, g_ffn2, w_ffn2_in, w_ffn2_out, g_final):
    bp, lp, _ = x_prompt.shape
    bs, ls, _ = x_sample.shape
    assert w_in.shape[0] == 1, "single-layer step: the final norm is fused into the layer's last kernel"
    xp = x_prompt.reshape(bp * lp, D_MODEL)
    xs = x_sample.reshape(bs * ls, D_MODEL)
    x1p, x1s, pap, pas, pbp, pbs = _token_a(
        xp, xs, _row(g_ffn1[0]), w_ffn1_in[0].astype(bf16), w_ffn1_out[0].astype(bf16),
        _row(g_mix[0]), _regroup_w_in(w_in[0]))
    params = _mixer_params(conv_w[0], conv_b[0], dt_bias[0], a_log[0], d_skip[0], g_ssd_norm[0],
                           w_gk2[0], b_gk[0], g_gla_norm[0])
    mxp, p_conv, p_ssd, p_gla = _mixer(pap, pbp, params, None, bp, lp)
    s_states = (state_ssd[0].reshape(bs, D_SSD, SSD_STATE), state_conv[0], state_gla[0].reshape(bs, D_QK, GLA_DV))
    mxs, s_conv, s_ssd, s_gla = _mixer(pas, pbs, params, s_states, bs, ls)
    yp, ys = _token_b(x1p, x1s, mxp, mxs, w_out[0].astype(bf16), _row(g_ffn2[0]),
                      w_ffn2_in[0].astype(bf16), w_ffn2_out[0].astype(bf16), _row(g_final))
    return (yp.reshape(bp, lp, D_MODEL), ys.reshape(bs, ls, D_MODEL),
            p_ssd.reshape(1, bp, N_SSD_HEADS, SSD_HEAD_DIM, SSD_STATE), p_conv[None],
            p_gla.reshape(1, bp, N_GLA_HEADS, GLA_DK, GLA_DV),
            s_ssd.reshape(1, bs, N_SSD_HEADS, SSD_HEAD_DIM, SSD_STATE), s_conv[None],
            s_gla.reshape(1, bs, N_GLA_HEADS, GLA_DK, GLA_DV))
```

```python
import functools

import jax
import jax.numpy as jnp
import numpy as np
from jax import lax
from jax.experimental import pallas as pl
from jax.experimental.pallas import tpu as pltpu

f32 = jnp.float32
bf16 = jnp.bfloat16

D_MODEL = 1024
D_SSD = 512
SSD_HEAD_DIM = 64
N_SSD_HEADS = 8
SSD_STATE = 128
SSD_GROUPS = 2
CONV_W = 4
CONV_DIM = D_SSD + 2 * SSD_GROUPS * SSD_STATE
D_GLA = 512
N_GLA_HEADS = 4
GLA_DV = 128
GLA_DK = 64
GLA_RANK = 16
GLA_GATE_NORMALIZER = 16.0
D_FF = 2816
EPS = 1e-6
D_QK = N_GLA_HEADS * GLA_DK

SUBLANES = 8
LANES = 128
BF16_ROWS = 16
VMEM_LIMIT_BYTES = 60 * 1024 * 1024

CHUNK = 128
TOKEN_TILE = 256
TOKEN_B_TILE = 512

PA_COLS = D_SSD + CONV_DIM + 2 * D_QK + 2 * D_GLA
PB_COLS = LANES
OFF_Z, OFF_XBC, OFF_Q, OFF_K, OFF_V, OFF_G = 0, 512, 1536, 1792, 2048, 2560


def _dot(a, b):
    return jnp.dot(a, b, preferred_element_type=f32)


def _dot_nt(a, b):
    return lax.dot_general(a, b, (((1,), (1,)), ((), ())), preferred_element_type=f32)


def _split3(x):
    hi = x.astype(bf16)
    r1 = x - hi.astype(f32)
    mid = r1.astype(bf16)
    lo = (r1 - mid.astype(f32)).astype(bf16)
    return hi, mid, lo


def _sel_dot(sel3, x):
    return _dot(sel3, jnp.concatenate(_split3(x), axis=0))


def _tiled3(sel):
    return np.concatenate([sel] * 3, axis=1)


def _dot_x3(a, b):
    a_hi = a.astype(bf16)
    a_lo = (a - a_hi.astype(f32)).astype(bf16)
    b_hi = b.astype(bf16)
    b_lo = (b - b_hi.astype(f32)).astype(bf16)
    return _dot(a_hi, b_hi) + _dot(a_hi, b_lo) + _dot(a_lo, b_hi)


def _rms(x, g):
    return x * lax.rsqrt(jnp.mean(x * x, axis=-1, keepdims=True) + EPS) * g


def _swiglu_half_step(x, g, w_up, w_dn):
    h = _rms(x, g).astype(bf16)
    gu = _dot(h, w_up)
    act = (jax.nn.silu(gu[:, :D_FF]) * gu[:, D_FF:]).astype(bf16)
    return x + 0.5 * _dot(act, w_dn)


def _token_a_body(n_prompt, xp, xs, g1, wup, wdn, gmix, win, x1p, x1s, pap, pas, pbp, pbs):
    def compute(x_ref, x1_ref, pa_ref, pb_ref):
        x1 = _swiglu_half_step(x_ref[...], g1[...], wup[...], wdn[...])
        x1_ref[...] = x1
        pr = _dot(_rms(x1, gmix[...]).astype(bf16), win[...])
        pa_ref[...] = pr[:, :PA_COLS].astype(bf16)
        pb_ref[...] = pr[:, PA_COLS:]

    i = pl.program_id(0)
    pl.when(i < n_prompt)(lambda: compute(xp, x1p, pap, pbp))
    pl.when(i >= n_prompt)(lambda: compute(xs, x1s, pas, pbs))


def _token_b_body(n_prompt, x1p, x1s, mxp, mxs, wout, g2, wup, wdn, gfin, yp, ys):
    def compute(x_ref, m_ref, y_ref):
        x2 = x_ref[...] + _dot(m_ref[...], wout[...])
        x3 = _swiglu_half_step(x2, g2[...], wup[...], wdn[...])
        y_ref[...] = _rms(x3, gfin[...])

    i = pl.program_id(0)
    pl.when(i < n_prompt)(lambda: compute(x1p, mxp, yp))
    pl.when(i >= n_prompt)(lambda: compute(x1s, mxs, ys))


def _two_group_specs(n_prompt, cols, tile):
    prompt = pl.BlockSpec((tile, cols), lambda i: (jnp.minimum(i, n_prompt - 1), 0))
    sample = pl.BlockSpec((tile, cols), lambda i: (jnp.maximum(i - n_prompt, 0), 0))
    return prompt, sample


def _resident(shape):
    return pl.BlockSpec(shape, lambda i: (0,) * len(shape), pipeline_mode=pl.Buffered(1))


def _token_a(xp, xs, g1, wup, wdn, gmix, win):
    tp, ts = xp.shape[0], xs.shape[0]
    tile = TOKEN_TILE
    n_prompt, n_sample = tp // tile, ts // tile
    assert tp % tile == 0 and ts % tile == 0
    xin = _two_group_specs(n_prompt, D_MODEL, tile)
    pa = _two_group_specs(n_prompt, PA_COLS, tile)
    pb = _two_group_specs(n_prompt, PB_COLS, tile)
    return pl.pallas_call(
        functools.partial(_token_a_body, n_prompt),
        grid=(n_prompt + n_sample,),
        in_specs=[*xin, _resident(g1.shape), _resident(wup.shape), _resident(wdn.shape),
                  _resident(gmix.shape), _resident(win.shape)],
        out_specs=[*xin, *pa, *pb],
        out_shape=[jax.ShapeDtypeStruct((tp, D_MODEL), f32), jax.ShapeDtypeStruct((ts, D_MODEL), f32),
                   jax.ShapeDtypeStruct((tp, PA_COLS), bf16), jax.ShapeDtypeStruct((ts, PA_COLS), bf16),
                   jax.ShapeDtypeStruct((tp, PB_COLS), f32), jax.ShapeDtypeStruct((ts, PB_COLS), f32)],
        compiler_params=pltpu.CompilerParams(dimension_semantics=("arbitrary",), vmem_limit_bytes=VMEM_LIMIT_BYTES),
        name="token_a",
    )(xp, xs, g1, wup, wdn, gmix, win)


def _token_b(x1p, x1s, mxp, mxs, wout, g2, wup, wdn, gfin):
    tp, ts = x1p.shape[0], x1s.shape[0]
    tile = TOKEN_B_TILE
    assert tp % tile == 0 and ts % tile == 0
    n_prompt, n_sample = tp // tile, ts // tile
    xin = _two_group_specs(n_prompt, D_MODEL, tile)
    return pl.pallas_call(
        functools.partial(_token_b_body, n_prompt),
        grid=(n_prompt + n_sample,),
        in_specs=[*xin, *xin, _resident(wout.shape), _resident(g2.shape), _resident(wup.shape),
                  _resident(wdn.shape), _resident(gfin.shape)],
        out_specs=[*xin],
        out_shape=[jax.ShapeDtypeStruct((tp, D_MODEL), f32), jax.ShapeDtypeStruct((ts, D_MODEL), f32)],
        compiler_params=pltpu.CompilerParams(dimension_semantics=("arbitrary",), vmem_limit_bytes=VMEM_LIMIT_BYTES),
        name="token_b",
    )(x1p, x1s, mxp, mxs, wout, g2, wup, wdn, gfin)


def _level_halves(seq_rows):
    return [h for h in (1, 2, 4, 8, 16, 32, 64) if 2 * h <= seq_rows]


def _matmul_levels(seq_rows):
    return [h for h in _level_halves(seq_rows) if h < SUBLANES]


def _mixer_consts(seq_rows):
    t = np.arange(CHUNK)
    is_sample = seq_rows < CHUNK
    same = (t[:, None] // seq_rows) == (t[None, :] // seq_rows)
    tri = (same & (t[None, :] <= t[:, None])).astype(np.float32)
    last = ((t[:, None] // seq_rows) * seq_rows + seq_rows - 1 == t[None, :]).astype(np.float32)
    cum_gla, pair_mask = [tri], [np.eye(CHUNK, dtype=np.float32)]
    for h in _level_halves(seq_rows):
        ref_row = (t // (2 * h)) * (2 * h) + h
        if h in _matmul_levels(seq_rows):
            cum_gla.append((ref_row[:, None] == t[None, :]).astype(np.float32) @ tri)
        right = (t % (2 * h)) >= h
        blk = (t[:, None] // (2 * h)) == (t[None, :] // (2 * h))
        pair_mask.append((blk & right[:, None] & ~right[None, :]).astype(np.float32))
    cum_ssd = [tri]
    if is_sample:
        cum_gla.append(last @ tri)
        cum_ssd.append(last @ tri)
    hq = np.arange(D_QK) // GLA_DK
    hv = np.arange(D_GLA) // GLA_DV
    head_of_lane = np.arange(D_SSD) // SSD_HEAD_DIM
    expand = (np.arange(PB_COLS)[:, None] == head_of_lane[None, :]).astype(np.float32)
    consts = [jnp.asarray(_tiled3(np.concatenate(cum_ssd, 0)), bf16),
              jnp.asarray(_tiled3(np.concatenate(cum_gla, 0)), bf16),
              jnp.asarray(np.stack(pair_mask, 0), f32),
              jnp.asarray((hv[:, None] == hq[None, :]).astype(np.float32), f32),
              jnp.asarray(np.concatenate([expand] * 3, 0), bf16)]
    if not is_sample:
        shift = np.zeros(((CONV_W - 1) * CHUNK, 2 * CHUNK), np.float32)
        for k in range(1, CONV_W):
            shift[(k - 1) * CHUNK + t, CHUNK + t - k] = 1.0
        consts.append(jnp.asarray(shift, bf16))
    return consts


PROMPT_STREAMS = 2
SAMPLE_STREAMS = 1
N_SHARED_REFS = 14
EPILOGUE = "epilogue"


def _round_robin(streams):
    live = list(streams)
    waiting = []
    while live:
        for s in list(live):
            try:
                if next(s) == EPILOGUE:
                    live.remove(s)
                    waiting.append(s)
            except StopIteration:
                live.remove(s)
    for s in waiting:
        for _ in s:
            pass


def _mixer_body(seq_rows, *refs):
    is_sample = seq_rows < CHUNK
    n_streams = refs[0].shape[0]
    n_in = 5 if is_sample else 2
    n_shared = N_SHARED_REFS if is_sample else N_SHARED_REFS + 1
    ins, shared = refs[:n_in], refs[n_in:n_in + n_shared]
    outs = refs[n_in + n_shared:n_in + n_shared + 4]
    scratch = refs[n_in + n_shared + 4:]
    is_last = None
    if not is_sample:
        xext, st_ref, stbd_ref = scratch
        c_idx = pl.program_id(1)
        is_last = c_idx == pl.num_programs(1) - 1

        @pl.when(c_idx == 0)
        def _():
            xext[:, 0:CHUNK, :] = jnp.zeros((n_streams, CHUNK, CONV_DIM), bf16)
            st_ref[...] = jnp.zeros(st_ref.shape, f32)
            stbd_ref[...] = jnp.zeros(stbd_ref.shape, f32)

    at = lambda group, j: [r.at[j] for r in group]
    _round_robin([_mixer_phases(seq_rows, at(ins, j), shared, at(outs, j), at(scratch, j), is_last)
                  for j in range(n_streams)])


def _mixer_phases(seq_rows, ins, shared, outs, scratch, is_last):
    is_sample = seq_rows < CHUNK
    n_seq = CHUNK // seq_rows
    halves = _level_halves(seq_rows)
    n_lvl = len(halves)
    if is_sample:
        pa_ref, pb_ref, ssd0_ref, conv0_ref, gla0_ref = ins
        (cbuf,) = scratch
    else:
        pa_ref, pb_ref = ins
        xext, st_ref, stbd_ref = scratch
    (convw_ref, convb_ref, dsk_e_ref, gssd_ref, dtb_c_ref, alog_c_ref, wgk_ref, bgk_ref, ggla_ref,
     cum_ssd_ref, cum_gla_ref, pair_mask_ref, bdmask_ref, expand_ref) = shared[:N_SHARED_REFS]
    if not is_sample:
        shift_ref = shared[N_SHARED_REFS]
    mixed_ref, nconv_ref, nssd_ref, ngla_ref = outs

    C = CHUNK
    rows = lax.broadcasted_iota(jnp.int32, (C, 1), 0)
    lane_qk = lax.shift_right_logical(lax.broadcasted_iota(jnp.int32, (1, D_QK), 1), int(np.log2(GLA_DK)))

    z = pa_ref[:, OFF_Z:OFF_Z + D_SSD].astype(f32)
    q = pa_ref[:, OFF_Q:OFF_Q + D_QK].astype(f32) * (GLA_DK ** -0.5)
    k = pa_ref[:, OFF_K:OFF_K + D_QK].astype(f32)
    v_bf = pa_ref[:, OFF_V:OFF_V + D_GLA]
    g = pa_ref[:, OFF_G:OFF_G + D_GLA].astype(f32)

    conv = jnp.broadcast_to(convb_ref[...], (C, CONV_DIM))
    if is_sample:
        cbuf[:, SUBLANES - (CONV_W - 1):SUBLANES, :] = conv0_ref[...]
        cbuf[:, SUBLANES:SUBLANES + seq_rows, :] = (
            pa_ref[:, OFF_XBC:OFF_XBC + CONV_DIM].astype(f32).reshape(n_seq, seq_rows, CONV_DIM))
        for i in range(CONV_W):
            shift = CONV_W - 1 - i
            win = cbuf[:, SUBLANES - shift:SUBLANES - shift + seq_rows, :].reshape(C, CONV_DIM)
            conv = conv + win * convw_ref[i:i + 1, :]
        nconv_ref[...] = cbuf[:, SUBLANES + seq_rows - (CONV_W - 1):SUBLANES + seq_rows, :]
    else:
        xbc = pa_ref[:, OFF_XBC:OFF_XBC + CONV_DIM]
        xext[C:, :] = xbc
        taps = _dot(shift_ref[...], xext[...])
        for i in range(CONV_W - 1):
            shift = CONV_W - 1 - i
            conv = conv + taps[(shift - 1) * C:shift * C] * convw_ref[i:i + 1, :]
        conv = conv + xbc.astype(f32) * convw_ref[CONV_W - 1:CONV_W, :]
        xext[C - BF16_ROWS:C, :] = xext[2 * C - BF16_ROWS:, :]
    yield
    xc = jax.nn.silu(conv)
    xs = xc[:, :D_SSD]
    bm = xc[:, D_SSD:D_SSD + SSD_GROUPS * SSD_STATE].astype(bf16)
    cm = xc[:, D_SSD + SSD_GROUPS * SSD_STATE:].astype(bf16)

    small = pb_ref[...]
    dtp_c = jax.nn.softplus(small + dtb_c_ref[...])
    cums = _sel_dot(cum_ssd_ref[...], dtp_c * (-jnp.exp(alog_c_ref[...])))
    yield
    acum_c = cums[:C]
    acum_t = acum_c.T
    wide = _dot(jnp.concatenate(_split3(jnp.concatenate([cums, dtp_c], axis=0)), axis=1), expand_ref[...])
    yield
    acum_e, dtp_e = wide[:C], wide[-C:]
    acum_last = wide[C:2 * C] if is_sample else acum_e[C - 1:C, :]
    causal = cum_ssd_ref[0:C, 0:C].astype(f32) > 0
    xdt = xs * dtp_e
    xdt_bf = xdt.astype(bf16)
    lane = lax.broadcasted_iota(jnp.int32, (1, LANES), 1)
    left = lane < SSD_HEAD_DIM
    heads_per_group = N_SSD_HEADS // SSD_GROUPS
    gw = heads_per_group * SSD_HEAD_DIM
    y_parts = []
    for grp in range(SSD_GROUPS):
        cb = _dot_nt(cm[:, grp * SSD_STATE:(grp + 1) * SSD_STATE], bm[:, grp * SSD_STATE:(grp + 1) * SSD_STATE])
        for pair in range(heads_per_group // 2):
            sc = []
            for hh in range(2):
                h = grp * heads_per_group + pair * 2 + hh
                seg = acum_c[:, h:h + 1] - acum_t[h:h + 1, :]
                sc.append((cb * jnp.exp(jnp.where(causal, seg, -jnp.inf))).astype(bf16))
            lo = (grp * heads_per_group + pair * 2) * SSD_HEAD_DIM
            xp = xdt_bf[:, lo:lo + LANES]
            bd = jnp.concatenate([jnp.where(left, xp, 0), jnp.where(left, 0, xp)], axis=0)
            y_parts.append(_dot(jnp.concatenate(sc, axis=1), bd))
            yield
    y = jnp.concatenate(y_parts, axis=1)

    to_end = jnp.exp(jnp.minimum(acum_last - acum_e, 0.0))
    xs_end = (xdt * to_end).astype(bf16)
    e_acum = jnp.exp(acum_e)
    bm_t = bm.T

    def ssd_inter(st):
        st_bf = st.astype(bf16)
        return jnp.concatenate([_dot(cm[:, grp * SSD_STATE:(grp + 1) * SSD_STATE], st_bf[:, grp * gw:(grp + 1) * gw])
                                for grp in range(SSD_GROUPS)], axis=1) * e_acum

    def ssd_update(st, decay_row, xs_rows):
        upd = jnp.concatenate([_dot(bm_t[grp * SSD_STATE:(grp + 1) * SSD_STATE, :], xs_rows[:, grp * gw:(grp + 1) * gw])
                               for grp in range(SSD_GROUPS)], axis=1)
        return st * decay_row + upd

    gk = _dot_x3(small, wgk_ref[...]) + bgk_ref[...]
    log_a = jax.nn.log_sigmoid(gk) / GLA_GATE_NORMALIZER
    bsel = _sel_dot(cum_gla_ref[...], log_a)
    yield
    bcum = bsel[:C]
    n_mm = len(_matmul_levels(seq_rows))
    b_last = bsel[(n_mm + 1) * C:] if is_sample else bcum[C - 1:C, :]
    k_bf = k.astype(bf16)

    def b_ref(lvl, half):
        if lvl < n_mm:
            return bsel[(lvl + 1) * C:(lvl + 2) * C]
        return jnp.concatenate([jnp.broadcast_to(bcum[r + half:r + half + 1, :], (2 * half, D_QK))
                                for r in range(0, C, 2 * half)], axis=0)

    def head_rows(x_bf):
        return jnp.concatenate([jnp.where(lane_qk == h, x_bf, jnp.zeros_like(x_bf)) for h in range(N_GLA_HEADS)], axis=0)

    a = _dot_nt(head_rows(q.astype(bf16)), k_bf)
    yield
    m = pair_mask_ref[0]
    att = [a[h * C:(h + 1) * C] * m for h in range(N_GLA_HEADS)]
    for lvl, half in enumerate(halves):
        decay = jnp.exp(-jnp.abs(bcum - b_ref(lvl, half)))
        in_right = (rows & half) != 0
        u = (jnp.where(in_right, q, k) * decay).astype(bf16)
        a = _dot_nt(head_rows(u), u)
        yield
        m = pair_mask_ref[lvl + 1]
        for h in range(N_GLA_HEADS):
            att[h] = att[h] + a[h * C:(h + 1) * C] * m
    o = jnp.concatenate([_dot(att[h].astype(bf16), v_bf[:, h * GLA_DV:(h + 1) * GLA_DV])
                         for h in range(N_GLA_HEADS)], axis=1)
    yield

    q_in = (q * jnp.exp(bcum)).astype(bf16)
    kd = (k * jnp.exp(jnp.minimum(b_last - bcum, 0.0))).astype(bf16)
    v_t = v_bf.T
    bdmask = bdmask_ref[...]

    def gla_inter(stbd):
        return _dot_nt(q_in, stbd.astype(bf16))

    def gla_update(stbd, decay_row, kd_rows):
        return stbd * decay_row + bdmask * _dot(v_t, kd_rows)

    def gla_heads(stbd):
        acc = jnp.where(lane_qk == 0, stbd[0:GLA_DV, :], 0.0)
        for h in range(1, N_GLA_HEADS):
            acc = acc + jnp.where(lane_qk == h, stbd[h * GLA_DV:(h + 1) * GLA_DV, :], 0.0)
        return acc.T

    if not is_sample:
        st = st_ref[...]
        stbd = stbd_ref[...]
        y = y + ssd_inter(st)
        o = o + gla_inter(stbd)
        yield
        st_ref[...] = ssd_update(st, jnp.exp(acum_last), xs_end)
        stbd_ref[...] = gla_update(stbd, jnp.exp(b_last), kd)
        yield
    else:
        seq_of_row = lax.shift_right_logical(rows, int(np.log2(seq_rows)))
        y_inter = jnp.zeros((C, D_SSD), f32)
        o_inter = jnp.zeros((C, D_GLA), f32)
        for s in range(n_seq):
            mine = seq_of_row == s
            r0 = s * seq_rows
            st = ssd0_ref[s].T
            y_inter = y_inter + jnp.where(mine, ssd_inter(st), 0.0)
            nssd_ref[s] = ssd_update(st, jnp.exp(acum_last[r0:r0 + 1, :]), jnp.where(mine, xs_end, 0)).T
            g0 = gla0_ref[s].T
            stbd = jnp.concatenate([jnp.where(lane_qk == h, g0, 0.0) for h in range(N_GLA_HEADS)], axis=0)
            o_inter = o_inter + jnp.where(mine, gla_inter(stbd), 0.0)
            ngla_ref[s] = gla_heads(gla_update(stbd, jnp.exp(b_last[r0:r0 + 1, :]), jnp.where(mine, kd, 0)))
            yield
        y = y + y_inter
        o = o + o_inter

    y = (y + dsk_e_ref[...] * xs) * jax.nn.silu(z)
    gsz = D_SSD // SSD_GROUPS
    y = jnp.concatenate([_rms(y[:, i * gsz:(i + 1) * gsz], gssd_ref[:, i * gsz:(i + 1) * gsz])
                         for i in range(SSD_GROUPS)], axis=1)
    o = jnp.concatenate([_rms(o[:, h * GLA_DV:(h + 1) * GLA_DV], ggla_ref[...]) for h in range(N_GLA_HEADS)], axis=1)
    o = o * jax.nn.silu(g)
    mixed_ref[:, :D_SSD] = y.astype(bf16)
    mixed_ref[:, D_SSD:] = o.astype(bf16)

    if not is_sample:
        yield EPILOGUE

        @pl.when(is_last)
        def _():
            carried = xext[C - BF16_ROWS:C, :].astype(f32)
            nconv_ref[0] = carried[BF16_ROWS - (CONV_W - 1):, :]
            nssd_ref[0] = st_ref[...].T
            ngla_ref[0] = gla_heads(stbd_ref[...])


def _mixer(pa, pb, params, states, n_batch, seq_len):
    is_sample = states is not None
    seq_rows = seq_len if is_sample else CHUNK
    assert CHUNK % seq_rows == 0 and (is_sample or seq_len % CHUNK == 0)
    n_seq = CHUNK // seq_rows
    consts = _mixer_consts(seq_rows)
    ns = SAMPLE_STREAMS if is_sample else PROMPT_STREAMS
    assert n_batch % (ns * n_seq) == 0
    rows_per_stream = n_batch * seq_len // ns
    seqs_per_stream = n_batch // ns

    if is_sample:
        grid = (seqs_per_stream // n_seq,)
        row_map = lambda c: (0, c, 0)
        seq_map = lambda c: (0, c, 0, 0)
        full = lambda shape: pl.BlockSpec(shape, lambda c: (0,) * len(shape))
        sems = ("arbitrary",)
    else:
        n_chunks = seq_len // CHUNK
        grid = (seqs_per_stream, n_chunks)
        row_map = lambda b, c: (0, b * n_chunks + c, 0)
        seq_map = lambda b, c: (0, b, 0, 0)
        full = lambda shape: pl.BlockSpec(shape, lambda b, c: (0,) * len(shape))
        sems = ("arbitrary", "arbitrary")

    def streamed(arr):
        return arr.reshape((ns, arr.shape[0] // ns) + arr.shape[1:])

    state_blocks = [(n_seq, CONV_W - 1, CONV_DIM), (n_seq, D_SSD, SSD_STATE), (n_seq, D_QK, GLA_DV)]
    in_arrays = [streamed(pa), streamed(pb)]
    in_specs = [pl.BlockSpec((ns, CHUNK, PA_COLS), row_map), pl.BlockSpec((ns, CHUNK, PB_COLS), row_map)]
    if is_sample:
        ssd0, conv0, gla0 = states
        in_arrays += [streamed(ssd0), streamed(conv0), streamed(gla0)]
        in_specs += [pl.BlockSpec((ns,) + state_blocks[i], seq_map) for i in (1, 0, 2)]
    for arr in list(params) + consts:
        in_arrays.append(arr)
        in_specs.append(full(arr.shape))

    out_shape = [jax.ShapeDtypeStruct((ns, rows_per_stream, D_MODEL), bf16)] + [
        jax.ShapeDtypeStruct((ns, seqs_per_stream) + blk[1:], f32) for blk in state_blocks]
    out_specs = [pl.BlockSpec((ns, CHUNK, D_MODEL), row_map)] + [
        pl.BlockSpec((ns,) + blk, seq_map) for blk in state_blocks]
    if is_sample:
        scratch = [pltpu.VMEM((ns, n_seq, SUBLANES + seq_rows, CONV_DIM), f32)]
    else:
        scratch = [pltpu.VMEM((ns, 2 * CHUNK, CONV_DIM), bf16),
                   pltpu.VMEM((ns, SSD_STATE, D_SSD), f32), pltpu.VMEM((ns, D_GLA, D_QK), f32)]
    mixed, nconv, nssd, ngla = pl.pallas_call(
        functools.partial(_mixer_body, seq_rows),
        grid=grid, in_specs=in_specs, out_specs=out_specs, out_shape=out_shape, scratch_shapes=scratch,
        compiler_params=pltpu.CompilerParams(dimension_semantics=sems, vmem_limit_bytes=VMEM_LIMIT_BYTES),
        name="mixer_sample" if is_sample else "mixer_prompt",
    )(*in_arrays)
    merged = lambda arr: arr.reshape((arr.shape[0] * arr.shape[1],) + arr.shape[2:])
    return merged(mixed), merged(nconv), merged(nssd), merged(ngla)


def _regroup_w_in(w_in):
    o_xbc = D_SSD
    o_dt = o_xbc + CONV_DIM
    o_q = o_dt + N_SSD_HEADS
    o_k = o_q + D_QK
    o_v = o_k + D_QK
    o_g = o_v + D_GLA
    o_lr = o_g + D_GLA
    dt_cols = w_in[:, o_dt:o_q]
    small = jnp.concatenate([dt_cols, w_in[:, o_lr:o_lr + GLA_RANK],
                             jnp.zeros((D_MODEL, PB_COLS - N_SSD_HEADS - GLA_RANK), w_in.dtype)], axis=1)
    return jnp.concatenate([w_in[:, :o_dt], w_in[:, o_q:o_lr], small], axis=1).astype(bf16)


def _row(vec):
    return vec.reshape(1, -1).astype(f32)


def _pad_lanes(vec, width):
    return jnp.concatenate([vec.astype(f32), jnp.zeros((width - vec.shape[0],), f32)]).reshape(1, width)


def _mixer_params(conv_w, conv_b, dt_bias, a_log, d_skip, g_ssd_norm, w_gk2, b_gk, g_gla_norm):
    rep = lambda vec: _row(jnp.repeat(vec, SSD_HEAD_DIM))
    wgk = jnp.zeros((PB_COLS, D_QK), f32).at[N_SSD_HEADS:N_SSD_HEADS + GLA_RANK, :].set(w_gk2.astype(f32))
    return [conv_w.astype(f32), _row(conv_b), rep(d_skip), _row(g_ssd_norm),
            _pad_lanes(dt_bias, PB_COLS), _pad_lanes(a_log, PB_COLS), wgk, _row(b_gk), _row(g_gla_norm)]


def kernel(x_prompt, x_sample, state_ssd, state_conv, state_gla, g_ffn1, w_ffn1_in, w_ffn1_out, g_mix, w_in, conv_w, conv_b, dt_bias, a_log, d_skip, g_ssd_norm, w_gk2, b_gk, g_gla_norm, w_out, g_ffn2, w_ffn2_in, w_ffn2_out, g_final):
    bp, lp, _ = x_prompt.shape
    bs, ls, _ = x_sample.shape
    assert w_in.shape[0] == 1, "single-layer step: the final norm is fused into the layer's last kernel"
    xp = x_prompt.reshape(bp * lp, D_MODEL)
    xs = x_sample.reshape(bs * ls, D_MODEL)
    x1p, x1s, pap, pas, pbp, pbs = _token_a(
        xp, xs, _row(g_ffn1[0]), w_ffn1_in[0].astype(bf16), w_ffn1_out[0].astype(bf16),
        _row(g_mix[0]), _regroup_w_in(w_in[0]))
    params = _mixer_params(conv_w[0], conv_b[0], dt_bias[0], a_log[0], d_skip[0], g_ssd_norm[0],
                           w_gk2[0], b_gk[0], g_gla_norm[0])
    mxp, p_conv, p_ssd, p_gla = _mixer(pap, pbp, params, None, bp, lp)
    s_states = (state_ssd[0].reshape(bs, D_SSD, SSD_STATE), state_conv[0], state_gla[0].reshape(bs, D_QK, GLA_DV))
    mxs, s_conv, s_ssd, s_gla = _mixer(pas, pbs, params, s_states, bs, ls)
    yp, ys = _token_b(x1p, x1s, mxp, mxs, w_out[0].astype(bf16), _row(g_ffn2[0]),
                      w_ffn2_in[0].astype(bf16), w_ffn2_out[0].astype(bf16), _row(g_final))
    return (yp.reshape(bp, lp, D_MODEL), ys.reshape(bs, ls, D_MODEL),
            p_ssd.reshape(1, bp, N_SSD_HEADS, SSD_HEAD_DIM, SSD_STATE), p_conv[None],
            p_gla.reshape(1, bp, N_GLA_HEADS, GLA_DK, GLA_DV),
            s_ssd.reshape(1, bs, N_SSD_HEADS, SSD_HEAD_DIM, SSD_STATE), s_conv[None],
            s_gla.reshape(1, bs, N_GLA_HEADS, GLA_DK, GLA_DV))
```

```python
import functools

import jax
import jax.numpy as jnp
import numpy as np
from jax import lax
from jax.experimental import pallas as pl
from jax.experimental.pallas import tpu as pltpu

f32 = jnp.float32
bf16 = jnp.bfloat16

D_MODEL = 1024
D_SSD = 512
SSD_HEAD_DIM = 64
N_SSD_HEADS = 8
SSD_STATE = 128
SSD_GROUPS = 2
CONV_W = 4
CONV_DIM = D_SSD + 2 * SSD_GROUPS * SSD_STATE
D_GLA = 512
N_GLA_HEADS = 4
GLA_DV = 128
GLA_DK = 64
GLA_RANK = 16
GLA_GATE_NORMALIZER = 16.0
D_FF = 2816
EPS = 1e-6
D_QK = N_GLA_HEADS * GLA_DK

SUBLANES = 8
LANES = 128
BF16_ROWS = 16
VMEM_LIMIT_BYTES = 60 * 1024 * 1024

CHUNK = 128
TOKEN_TILE = 256
TOKEN_B_TILE = 512

PA_COLS = D_SSD + CONV_DIM + 2 * D_QK + 2 * D_GLA
PB_COLS = LANES
OFF_Z, OFF_XBC, OFF_Q, OFF_K, OFF_V, OFF_G = 0, 512, 1536, 1792, 2048, 2560


def _dot(a, b):
    return jnp.dot(a, b, preferred_element_type=f32)


def _dot_nt(a, b):
    return lax.dot_general(a, b, (((1,), (1,)), ((), ())), preferred_element_type=f32)


def _split3(x):
    hi = x.astype(bf16)
    r1 = x - hi.astype(f32)
    mid = r1.astype(bf16)
    lo = (r1 - mid.astype(f32)).astype(bf16)
    return hi, mid, lo


def _sel_dot(sel3, x):
    return _dot(sel3, jnp.concatenate(_split3(x), axis=0))


def _tiled3(sel):
    return np.concatenate([sel] * 3, axis=1)


def _dot_x3(a, b):
    a_hi = a.astype(bf16)
    a_lo = (a - a_hi.astype(f32)).astype(bf16)
    b_hi = b.astype(bf16)
    b_lo = (b - b_hi.astype(f32)).astype(bf16)
    return _dot(a_hi, b_hi) + _dot(a_hi, b_lo) + _dot(a_lo, b_hi)


def _rms(x, g):
    return x * lax.rsqrt(jnp.mean(x * x, axis=-1, keepdims=True) + EPS) * g


def _swiglu_half_step(x, g, w_up, w_dn):
    h = _rms(x, g).astype(bf16)
    gu = _dot(h, w_up)
    act = (jax.nn.silu(gu[:, :D_FF]) * gu[:, D_FF:]).astype(bf16)
    return x + 0.5 * _dot(act, w_dn)


N_LOAD = 16

W_IN_DT = D_SSD + CONV_DIM
W_IN_Q = W_IN_DT + N_SSD_HEADS
W_IN_LR = W_IN_Q + 2 * D_QK + 2 * D_GLA
W_IN_COLS = W_IN_LR + GLA_RANK


def _load_slab(step, slab_ref, dst_ref, regroup=None):
    rows = slab_ref.shape[0]
    assert rows % BF16_ROWS == 0
    val = slab_ref[...]
    if regroup is not None:
        val = regroup(val)
    dst_ref[pl.ds(pl.multiple_of(step * rows, BF16_ROWS), rows), :] = val.astype(bf16)


def _regroup_w_in(w):
    small = jnp.concatenate([w[:, W_IN_DT:W_IN_Q], w[:, W_IN_LR:W_IN_COLS],
                             jnp.zeros((w.shape[0], PB_COLS - N_SSD_HEADS - GLA_RANK), w.dtype)], axis=1)
    return jnp.concatenate([w[:, :W_IN_DT], w[:, W_IN_Q:W_IN_LR], small], axis=1)


def _token_a_body(n_prompt, xp, xs, g1, wup_f, wdn_f, gmix, win_f, x1p, x1s, pap, pas, pbp, pbs, wup, wdn, win):
    def compute(x_ref, x1_ref, pa_ref, pb_ref):
        x1 = _swiglu_half_step(x_ref[...], g1[...], wup[...], wdn[...])
        x1_ref[...] = x1
        pr = _dot(_rms(x1, gmix[...]).astype(bf16), win[...])
        pa_ref[...] = pr[:, :PA_COLS].astype(bf16)
        pb_ref[...] = pr[:, PA_COLS:]

    i = pl.program_id(0)

    @pl.when(i < N_LOAD)
    def _():
        _load_slab(i, wup_f, wup)
        _load_slab(i, wdn_f, wdn)
        _load_slab(i, win_f, win, _regroup_w_in)

    pl.when(jnp.logical_and(i >= N_LOAD, i < N_LOAD + n_prompt))(lambda: compute(xp, x1p, pap, pbp))
    pl.when(i >= N_LOAD + n_prompt)(lambda: compute(xs, x1s, pas, pbs))


def _token_b_body(n_prompt, x1p, x1s, mxp, mxs, wout_f, g2, wup_f, wdn_f, gfin, yp, ys, wout, wup, wdn):
    def compute(x_ref, m_ref, y_ref):
        x2 = x_ref[...] + _dot(m_ref[...], wout[...])
        x3 = _swiglu_half_step(x2, g2[...], wup[...], wdn[...])
        y_ref[...] = _rms(x3, gfin[...])

    i = pl.program_id(0)

    @pl.when(i < N_LOAD)
    def _():
        _load_slab(i, wout_f, wout)
        _load_slab(i, wup_f, wup)
        _load_slab(i, wdn_f, wdn)

    pl.when(jnp.logical_and(i >= N_LOAD, i < N_LOAD + n_prompt))(lambda: compute(x1p, mxp, yp))
    pl.when(i >= N_LOAD + n_prompt)(lambda: compute(x1s, mxs, ys))


def _two_group_specs(n_prompt, cols, tile):
    prompt = pl.BlockSpec((tile, cols), lambda i: (jnp.clip(i - N_LOAD, 0, n_prompt - 1), 0))
    sample = pl.BlockSpec((tile, cols), lambda i: (jnp.maximum(i - N_LOAD - n_prompt, 0), 0))
    return prompt, sample


def _whole(shape):
    return pl.BlockSpec(shape, lambda i: (0,) * len(shape), pipeline_mode=pl.Buffered(1))


def _slabs(weight):
    rows, cols = weight.shape
    assert rows % N_LOAD == 0
    return pl.BlockSpec((rows // N_LOAD, cols), lambda i: (jnp.minimum(i, N_LOAD - 1), 0))


def _token_a(xp, xs, g1, wup, wdn, gmix, win):
    tp, ts = xp.shape[0], xs.shape[0]
    tile = TOKEN_TILE
    n_prompt, n_sample = tp // tile, ts // tile
    assert tp % tile == 0 and ts % tile == 0 and win.shape[1] == W_IN_COLS
    xin = _two_group_specs(n_prompt, D_MODEL, tile)
    pa = _two_group_specs(n_prompt, PA_COLS, tile)
    pb = _two_group_specs(n_prompt, PB_COLS, tile)
    return pl.pallas_call(
        functools.partial(_token_a_body, n_prompt),
        grid=(N_LOAD + n_prompt + n_sample,),
        in_specs=[*xin, _whole(g1.shape), _slabs(wup), _slabs(wdn), _whole(gmix.shape), _slabs(win)],
        out_specs=[*xin, *pa, *pb],
        out_shape=[jax.ShapeDtypeStruct((tp, D_MODEL), f32), jax.ShapeDtypeStruct((ts, D_MODEL), f32),
                   jax.ShapeDtypeStruct((tp, PA_COLS), bf16), jax.ShapeDtypeStruct((ts, PA_COLS), bf16),
                   jax.ShapeDtypeStruct((tp, PB_COLS), f32), jax.ShapeDtypeStruct((ts, PB_COLS), f32)],
        scratch_shapes=[pltpu.VMEM(wup.shape, bf16), pltpu.VMEM(wdn.shape, bf16),
                        pltpu.VMEM((D_MODEL, PA_COLS + PB_COLS), bf16)],
        compiler_params=pltpu.CompilerParams(dimension_semantics=("arbitrary",), vmem_limit_bytes=VMEM_LIMIT_BYTES),
        name="token_a",
    )(xp, xs, g1, wup, wdn, gmix, win)


def _token_b(x1p, x1s, mxp, mxs, wout, g2, wup, wdn, gfin):
    tp, ts = x1p.shape[0], x1s.shape[0]
    tile = TOKEN_B_TILE
    assert tp % tile == 0 and ts % tile == 0
    n_prompt, n_sample = tp // tile, ts // tile
    xin = _two_group_specs(n_prompt, D_MODEL, tile)
    return pl.pallas_call(
        functools.partial(_token_b_body, n_prompt),
        grid=(N_LOAD + n_prompt + n_sample,),
        in_specs=[*xin, *xin, _slabs(wout), _whole(g2.shape), _slabs(wup), _slabs(wdn), _whole(gfin.shape)],
        out_specs=[*xin],
        out_shape=[jax.ShapeDtypeStruct((tp, D_MODEL), f32), jax.ShapeDtypeStruct((ts, D_MODEL), f32)],
        scratch_shapes=[pltpu.VMEM(wout.shape, bf16), pltpu.VMEM(wup.shape, bf16), pltpu.VMEM(wdn.shape, bf16)],
        compiler_params=pltpu.CompilerParams(dimension_semantics=("arbitrary",), vmem_limit_bytes=VMEM_LIMIT_BYTES),
        name="token_b",
    )(x1p, x1s, mxp, mxs, wout, g2, wup, wdn, gfin)


def _level_halves(seq_rows):
    return [h for h in (1, 2, 4, 8, 16, 32, 64) if 2 * h <= seq_rows]


def _matmul_levels(seq_rows):
    return [h for h in _level_halves(seq_rows) if h < SUBLANES]


def _mixer_consts(seq_rows):
    t = np.arange(CHUNK)
    is_sample = seq_rows < CHUNK
    same = (t[:, None] // seq_rows) == (t[None, :] // seq_rows)
    tri = (same & (t[None, :] <= t[:, None])).astype(np.float32)
    last = ((t[:, None] // seq_rows) * seq_rows + seq_rows - 1 == t[None, :]).astype(np.float32)
    cum_gla, pair_mask = [tri], [np.eye(CHUNK, dtype=np.float32)]
    for h in _level_halves(seq_rows):
        ref_row = (t // (2 * h)) * (2 * h) + h
        if h in _matmul_levels(seq_rows):
            cum_gla.append((ref_row[:, None] == t[None, :]).astype(np.float32) @ tri)
        right = (t % (2 * h)) >= h
        blk = (t[:, None] // (2 * h)) == (t[None, :] // (2 * h))
        pair_mask.append((blk & right[:, None] & ~right[None, :]).astype(np.float32))
    cum_ssd = [tri]
    if is_sample:
        cum_gla.append(last @ tri)
        cum_ssd.append(last @ tri)
    hq = np.arange(D_QK) // GLA_DK
    hv = np.arange(D_GLA) // GLA_DV
    head_of_lane = np.arange(D_SSD) // SSD_HEAD_DIM
    expand = (np.arange(PB_COLS)[:, None] == head_of_lane[None, :]).astype(np.float32)
    consts = [jnp.asarray(_tiled3(np.concatenate(cum_ssd, 0)), bf16),
              jnp.asarray(_tiled3(np.concatenate(cum_gla, 0)), bf16),
              jnp.asarray(np.stack(pair_mask, 0), f32),
              jnp.asarray((hv[:, None] == hq[None, :]).astype(np.float32), f32),
              jnp.asarray(np.concatenate([expand] * 3, 0), bf16)]
    if not is_sample:
        shift = np.zeros(((CONV_W - 1) * CHUNK, 2 * CHUNK), np.float32)
        for k in range(1, CONV_W):
            shift[(k - 1) * CHUNK + t, CHUNK + t - k] = 1.0
        consts.append(jnp.asarray(shift, bf16))
    return consts


PROMPT_STREAMS = 2
SAMPLE_STREAMS = 1
N_SHARED_REFS = 14
EPILOGUE = "epilogue"


def _round_robin(streams):
    live = list(streams)
    waiting = []
    while live:
        for s in list(live):
            try:
                if next(s) == EPILOGUE:
                    live.remove(s)
                    waiting.append(s)
            except StopIteration:
                live.remove(s)
    for s in waiting:
        for _ in s:
            pass


def _mixer_body(seq_rows, *refs):
    is_sample = seq_rows < CHUNK
    n_streams = refs[0].shape[0]
    n_in = 5 if is_sample else 2
    n_shared = N_SHARED_REFS if is_sample else N_SHARED_REFS + 1
    ins, shared = refs[:n_in], refs[n_in:n_in + n_shared]
    outs = refs[n_in + n_shared:n_in + n_shared + 4]
    scratch = refs[n_in + n_shared + 4:]
    is_last = None
    if not is_sample:
        xext, st_ref, stbd_ref = scratch
        c_idx = pl.program_id(1)
        is_last = c_idx == pl.num_programs(1) - 1

        @pl.when(c_idx == 0)
        def _():
            xext[:, 0:CHUNK, :] = jnp.zeros((n_streams, CHUNK, CONV_DIM), bf16)
            st_ref[...] = jnp.zeros(st_ref.shape, f32)
            stbd_ref[...] = jnp.zeros(stbd_ref.shape, f32)

    at = lambda group, j: [r.at[j] for r in group]
    _round_robin([_mixer_phases(seq_rows, at(ins, j), shared, at(outs, j), at(scratch, j), is_last)
                  for j in range(n_streams)])


def _mixer_phases(seq_rows, ins, shared, outs, scratch, is_last):
    is_sample = seq_rows < CHUNK
    n_seq = CHUNK // seq_rows
    halves = _level_halves(seq_rows)
    n_lvl = len(halves)
    if is_sample:
        pa_ref, pb_ref, ssd0_ref, conv0_ref, gla0_ref = ins
        (cbuf,) = scratch
    else:
        pa_ref, pb_ref = ins
        xext, st_ref, stbd_ref = scratch
    (convw_ref, convb_ref, dsk_e_ref, gssd_ref, dtb_c_ref, alog_c_ref, wgk_ref, bgk_ref, ggla_ref,
     cum_ssd_ref, cum_gla_ref, pair_mask_ref, bdmask_ref, expand_ref) = shared[:N_SHARED_REFS]
    if not is_sample:
        shift_ref = shared[N_SHARED_REFS]
    mixed_ref, nconv_ref, nssd_ref, ngla_ref = outs

    C = CHUNK
    rows = lax.broadcasted_iota(jnp.int32, (C, 1), 0)
    lane_qk = lax.shift_right_logical(lax.broadcasted_iota(jnp.int32, (1, D_QK), 1), int(np.log2(GLA_DK)))

    z = pa_ref[:, OFF_Z:OFF_Z + D_SSD].astype(f32)
    q = pa_ref[:, OFF_Q:OFF_Q + D_QK].astype(f32) * (GLA_DK ** -0.5)
    k = pa_ref[:, OFF_K:OFF_K + D_QK].astype(f32)
    v_bf = pa_ref[:, OFF_V:OFF_V + D_GLA]
    g = pa_ref[:, OFF_G:OFF_G + D_GLA].astype(f32)

    conv = jnp.broadcast_to(convb_ref[...], (C, CONV_DIM))
    if is_sample:
        cbuf[:, SUBLANES - (CONV_W - 1):SUBLANES, :] = conv0_ref[...]
        cbuf[:, SUBLANES:SUBLANES + seq_rows, :] = (
            pa_ref[:, OFF_XBC:OFF_XBC + CONV_DIM].astype(f32).reshape(n_seq, seq_rows, CONV_DIM))
        for i in range(CONV_W):
            shift = CONV_W - 1 - i
            win = cbuf[:, SUBLANES - shift:SUBLANES - shift + seq_rows, :].reshape(C, CONV_DIM)
            conv = conv + win * convw_ref[i:i + 1, :]
        nconv_ref[...] = cbuf[:, SUBLANES + seq_rows - (CONV_W - 1):SUBLANES + seq_rows, :]
    else:
        xbc = pa_ref[:, OFF_XBC:OFF_XBC + CONV_DIM]
        xext[C:, :] = xbc
        taps = _dot(shift_ref[...], xext[...])
        for i in range(CONV_W - 1):
            shift = CONV_W - 1 - i
            conv = conv + taps[(shift - 1) * C:shift * C] * convw_ref[i:i + 1, :]
        conv = conv + xbc.astype(f32) * convw_ref[CONV_W - 1:CONV_W, :]
        xext[C - BF16_ROWS:C, :] = xext[2 * C - BF16_ROWS:, :]
    yield
    xc = jax.nn.silu(conv)
    xs = xc[:, :D_SSD]
    bm = xc[:, D_SSD:D_SSD + SSD_GROUPS * SSD_STATE].astype(bf16)
    cm = xc[:, D_SSD + SSD_GROUPS * SSD_STATE:].astype(bf16)

    small = pb_ref[...]
    dtp_c = jax.nn.softplus(small + dtb_c_ref[...])
    cums = _sel_dot(cum_ssd_ref[...], dtp_c * (-jnp.exp(alog_c_ref[...])))
    yield
    acum_c = cums[:C]
    acum_t = acum_c.T
    wide = _dot(jnp.concatenate(_split3(jnp.concatenate([cums, dtp_c], axis=0)), axis=1), expand_ref[...])
    yield
    acum_e, dtp_e = wide[:C], wide[-C:]
    acum_last = wide[C:2 * C] if is_sample else acum_e[C - 1:C, :]
    causal = cum_ssd_ref[0:C, 0:C].astype(f32) > 0
    xdt = xs * dtp_e
    xdt_bf = xdt.astype(bf16)
    lane = lax.broadcasted_iota(jnp.int32, (1, LANES), 1)
    left = lane < SSD_HEAD_DIM
    heads_per_group = N_SSD_HEADS // SSD_GROUPS
    gw = heads_per_group * SSD_HEAD_DIM
    y_parts = []
    for grp in range(SSD_GROUPS):
        cb = _dot_nt(cm[:, grp * SSD_STATE:(grp + 1) * SSD_STATE], bm[:, grp * SSD_STATE:(grp + 1) * SSD_STATE])
        for pair in range(heads_per_group // 2):
            sc = []
            for hh in range(2):
                h = grp * heads_per_group + pair * 2 + hh
                seg = acum_c[:, h:h + 1] - acum_t[h:h + 1, :]
                sc.append((cb * jnp.exp(jnp.where(causal, seg, -jnp.inf))).astype(bf16))
            lo = (grp * heads_per_group + pair * 2) * SSD_HEAD_DIM
            xp = xdt_bf[:, lo:lo + LANES]
            bd = jnp.concatenate([jnp.where(left, xp, 0), jnp.where(left, 0, xp)], axis=0)
            y_parts.append(_dot(jnp.concatenate(sc, axis=1), bd))
            yield
    y = jnp.concatenate(y_parts, axis=1)

    to_end = jnp.exp(jnp.minimum(acum_last - acum_e, 0.0))
    xs_end = (xdt * to_end).astype(bf16)
    e_acum = jnp.exp(acum_e)
    bm_t = bm.T

    def ssd_inter(st):
        st_bf = st.astype(bf16)
        return jnp.concatenate([_dot(cm[:, grp * SSD_STATE:(grp + 1) * SSD_STATE], st_bf[:, grp * gw:(grp + 1) * gw])
                                for grp in range(SSD_GROUPS)], axis=1) * e_acum

    def ssd_update(st, decay_row, xs_rows):
        upd = jnp.concatenate([_dot(bm_t[grp * SSD_STATE:(grp + 1) * SSD_STATE, :], xs_rows[:, grp * gw:(grp + 1) * gw])
                               for grp in range(SSD_GROUPS)], axis=1)
        return st * decay_row + upd

    gk = _dot_x3(small, wgk_ref[...]) + bgk_ref[...]
    log_a = jax.nn.log_sigmoid(gk) / GLA_GATE_NORMALIZER
    bsel = _sel_dot(cum_gla_ref[...], log_a)
    yield
    bcum = bsel[:C]
    n_mm = len(_matmul_levels(seq_rows))
    b_last = bsel[(n_mm + 1) * C:] if is_sample else bcum[C - 1:C, :]
    k_bf = k.astype(bf16)

    def b_ref(lvl, half):
        if lvl < n_mm:
            return bsel[(lvl + 1) * C:(lvl + 2) * C]
        return jnp.concatenate([jnp.broadcast_to(bcum[r + half:r + half + 1, :], (2 * half, D_QK))
                                for r in range(0, C, 2 * half)], axis=0)

    def head_rows(x_bf):
        return jnp.concatenate([jnp.where(lane_qk == h, x_bf, jnp.zeros_like(x_bf)) for h in range(N_GLA_HEADS)], axis=0)

    a = _dot_nt(head_rows(q.astype(bf16)), k_bf)
    yield
    m = pair_mask_ref[0]
    att = [a[h * C:(h + 1) * C] * m for h in range(N_GLA_HEADS)]
    for lvl, half in enumerate(halves):
        decay = jnp.exp(-jnp.abs(bcum - b_ref(lvl, half)))
        in_right = (rows & half) != 0
        u = (jnp.where(in_right, q, k) * decay).astype(bf16)
        a = _dot_nt(head_rows(u), u)
        yield
        m = pair_mask_ref[lvl + 1]
        for h in range(N_GLA_HEADS):
            att[h] = att[h] + a[h * C:(h + 1) * C] * m
    o = jnp.concatenate([_dot(att[h].astype(bf16), v_bf[:, h * GLA_DV:(h + 1) * GLA_DV])
                         for h in range(N_GLA_HEADS)], axis=1)
    yield

    q_in = (q * jnp.exp(bcum)).astype(bf16)
    kd = (k * jnp.exp(jnp.minimum(b_last - bcum, 0.0))).astype(bf16)
    v_t = v_bf.T
    bdmask = bdmask_ref[...]

    def gla_inter(stbd):
        return _dot_nt(q_in, stbd.astype(bf16))

    def gla_update(stbd, decay_row, kd_rows):
        return stbd * decay_row + bdmask * _dot(v_t, kd_rows)

    def gla_heads(stbd):
        acc = jnp.where(lane_qk == 0, stbd[0:GLA_DV, :], 0.0)
        for h in range(1, N_GLA_HEADS):
            acc = acc + jnp.where(lane_qk == h, stbd[h * GLA_DV:(h + 1) * GLA_DV, :], 0.0)
        return acc.T

    if not is_sample:
        st = st_ref[...]
        stbd = stbd_ref[...]
        y = y + ssd_inter(st)
        o = o + gla_inter(stbd)
        yield
        st_ref[...] = ssd_update(st, jnp.exp(acum_last), xs_end)
        stbd_ref[...] = gla_update(stbd, jnp.exp(b_last), kd)
        yield
    else:
        seq_of_row = lax.shift_right_logical(rows, int(np.log2(seq_rows)))
        y_inter = jnp.zeros((C, D_SSD), f32)
        o_inter = jnp.zeros((C, D_GLA), f32)
        for s in range(n_seq):
            mine = seq_of_row == s
            r0 = s * seq_rows
            st = ssd0_ref[s].T
            y_inter = y_inter + jnp.where(mine, ssd_inter(st), 0.0)
            nssd_ref[s] = ssd_update(st, jnp.exp(acum_last[r0:r0 + 1, :]), jnp.where(mine, xs_end, 0)).T
            g0 = gla0_ref[s].T
            stbd = jnp.concatenate([jnp.where(lane_qk == h, g0, 0.0) for h in range(N_GLA_HEADS)], axis=0)
            o_inter = o_inter + jnp.where(mine, gla_inter(stbd), 0.0)
            ngla_ref[s] = gla_heads(gla_update(stbd, jnp.exp(b_last[r0:r0 + 1, :]), jnp.where(mine, kd, 0)))
            yield
        y = y + y_inter
        o = o + o_inter

    y = (y + dsk_e_ref[...] * xs) * jax.nn.silu(z)
    gsz = D_SSD // SSD_GROUPS
    y = jnp.concatenate([_rms(y[:, i * gsz:(i + 1) * gsz], gssd_ref[:, i * gsz:(i + 1) * gsz])
                         for i in range(SSD_GROUPS)], axis=1)
    o = jnp.concatenate([_rms(o[:, h * GLA_DV:(h + 1) * GLA_DV], ggla_ref[...]) for h in range(N_GLA_HEADS)], axis=1)
    o = o * jax.nn.silu(g)
    mixed_ref[:, :D_SSD] = y.astype(bf16)
    mixed_ref[:, D_SSD:] = o.astype(bf16)

    if not is_sample:
        yield EPILOGUE

        @pl.when(is_last)
        def _():
            carried = xext[C - BF16_ROWS:C, :].astype(f32)
            nconv_ref[0] = carried[BF16_ROWS - (CONV_W - 1):, :]
            nssd_ref[0] = st_ref[...].T
            ngla_ref[0] = gla_heads(stbd_ref[...])


def _mixer(pa, pb, params, states, n_batch, seq_len):
    is_sample = states is not None
    seq_rows = seq_len if is_sample else CHUNK
    assert CHUNK % seq_rows == 0 and (is_sample or seq_len % CHUNK == 0)
    n_seq = CHUNK // seq_rows
    consts = _mixer_consts(seq_rows)
    ns = SAMPLE_STREAMS if is_sample else PROMPT_STREAMS
    assert n_batch % (ns * n_seq) == 0
    rows_per_stream = n_batch * seq_len // ns
    seqs_per_stream = n_batch // ns

    if is_sample:
        grid = (seqs_per_stream // n_seq,)
        row_map = lambda c: (0, c, 0)
        seq_map = lambda c: (0, c, 0, 0)
        full = lambda shape: pl.BlockSpec(shape, lambda c: (0,) * len(shape))
        sems = ("arbitrary",)
    else:
        n_chunks = seq_len // CHUNK
        grid = (seqs_per_stream, n_chunks)
        row_map = lambda b, c: (0, b * n_chunks + c, 0)
        seq_map = lambda b, c: (0, b, 0, 0)
        full = lambda shape: pl.BlockSpec(shape, lambda b, c: (0,) * len(shape))
        sems = ("arbitrary", "arbitrary")

    def streamed(arr):
        return arr.reshape((ns, arr.shape[0] // ns) + arr.shape[1:])

    state_blocks = [(n_seq, CONV_W - 1, CONV_DIM), (n_seq, D_SSD, SSD_STATE), (n_seq, D_QK, GLA_DV)]
    in_arrays = [streamed(pa), streamed(pb)]
    in_specs = [pl.BlockSpec((ns, CHUNK, PA_COLS), row_map), pl.BlockSpec((ns, CHUNK, PB_COLS), row_map)]
    if is_sample:
        ssd0, conv0, gla0 = states
        in_arrays += [streamed(ssd0), streamed(conv0), streamed(gla0)]
        in_specs += [pl.BlockSpec((ns,) + state_blocks[i], seq_map) for i in (1, 0, 2)]
    for arr in list(params) + consts:
        in_arrays.append(arr)
        in_specs.append(full(arr.shape))

    out_shape = [jax.ShapeDtypeStruct((ns, rows_per_stream, D_MODEL), bf16)] + [
        jax.ShapeDtypeStruct((ns, seqs_per_stream) + blk[1:], f32) for blk in state_blocks]
    out_specs = [pl.BlockSpec((ns, CHUNK, D_MODEL), row_map)] + [
        pl.BlockSpec((ns,) + blk, seq_map) for blk in state_blocks]
    if is_sample:
        scratch = [pltpu.VMEM((ns, n_seq, SUBLANES + seq_rows, CONV_DIM), f32)]
    else:
        scratch = [pltpu.VMEM((ns, 2 * CHUNK, CONV_DIM), bf16),
                   pltpu.VMEM((ns, SSD_STATE, D_SSD), f32), pltpu.VMEM((ns, D_GLA, D_QK), f32)]
    mixed, nconv, nssd, ngla = pl.pallas_call(
        functools.partial(_mixer_body, seq_rows),
        grid=grid, in_specs=in_specs, out_specs=out_specs, out_shape=out_shape, scratch_shapes=scratch,
        compiler_params=pltpu.CompilerParams(dimension_semantics=sems, vmem_limit_bytes=VMEM_LIMIT_BYTES),
        name="mixer_sample" if is_sample else "mixer_prompt",
    )(*in_arrays)
    merged = lambda arr: arr.reshape((arr.shape[0] * arr.shape[1],) + arr.shape[2:])
    return merged(mixed), merged(nconv), merged(nssd), merged(ngla)


def _row(vec):
    return vec.reshape(1, -1).astype(f32)


def _pad_lanes(vec, width):
    return jnp.concatenate([vec.astype(f32), jnp.zeros((width - vec.shape[0],), f32)]).reshape(1, width)


def _mixer_params(conv_w, conv_b, dt_bias, a_log, d_skip, g_ssd_norm, w_gk2, b_gk, g_gla_norm):
    rep = lambda vec: _row(jnp.repeat(vec, SSD_HEAD_DIM))
    wgk = jnp.zeros((PB_COLS, D_QK), f32).at[N_SSD_HEADS:N_SSD_HEADS + GLA_RANK, :].set(w_gk2.astype(f32))
    return [conv_w.astype(f32), _row(conv_b), rep(d_skip), _row(g_ssd_norm),
            _pad_lanes(dt_bias, PB_COLS), _pad_lanes(a_log, PB_COLS), wgk, _row(b_gk), _row(g_gla_norm)]


def kernel(x_prompt, x_sample, state_ssd, state_conv, state_gla, g_ffn1, w_ffn1_in, w_ffn1_out, g_mix, w_in, conv_w, conv_b, dt_bias, a_log, d_skip, g_ssd_norm, w_gk2, b_gk, g_gla_norm, w_out, g_ffn2, w_ffn2_in, w_ffn2_out, g_final):
    bp, lp, _ = x_prompt.shape
    bs, ls, _ = x_sample.shape
    assert w_in.shape[0] == 1, "single-layer step: the final norm is fused into the layer's last kernel"
    xp = x_prompt.reshape(bp * lp, D_MODEL)
    xs = x_sample.reshape(bs * ls, D_MODEL)
    x1p, x1s, pap, pas, pbp, pbs = _token_a(
        xp, xs, _row(g_ffn1[0]), w_ffn1_in[0].astype(f32), w_ffn1_out[0].astype(f32),
        _row(g_mix[0]), w_in[0].astype(f32))
    params = _mixer_params(conv_w[0], conv_b[0], dt_bias[0], a_log[0], d_skip[0], g_ssd_norm[0],
                           w_gk2[0], b_gk[0], g_gla_norm[0])
    mxp, p_conv, p_ssd, p_gla = _mixer(pap, pbp, params, None, bp, lp)
    s_states = (state_ssd[0].reshape(bs, D_SSD, SSD_STATE), state_conv[0], state_gla[0].reshape(bs, D_QK, GLA_DV))
    mxs, s_conv, s_ssd, s_gla = _mixer(pas, pbs, params, s_states, bs, ls)
    yp, ys = _token_b(x1p, x1s, mxp, mxs, w_out[0].astype(f32), _row(g_ffn2[0]),
                      w_ffn2_in[0].astype(f32), w_ffn2_out[0].astype(f32), _row(g_final))
    return (yp.reshape(bp, lp, D_MODEL), ys.reshape(bs, ls, D_MODEL),
            p_ssd.reshape(1, bp, N_SSD_HEADS, SSD_HEAD_DIM, SSD_STATE), p_conv[None],
            p_gla.reshape(1, bp, N_GLA_HEADS, GLA_DK, GLA_DV),
            s_ssd.reshape(1, bs, N_SSD_HEADS, SSD_HEAD_DIM, SSD_STATE), s_conv[None],
            s_gla.reshape(1, bs, N_GLA_HEADS, GLA_DK, GLA_DV))
```

```python
import functools

import jax
import jax.numpy as jnp
import numpy as np
from jax import lax
from jax.experimental import pallas as pl
from jax.experimental.pallas import tpu as pltpu

f32 = jnp.float32
bf16 = jnp.bfloat16

D_MODEL = 1024
D_SSD = 512
SSD_HEAD_DIM = 64
N_SSD_HEADS = 8
SSD_STATE = 128
SSD_GROUPS = 2
CONV_W = 4
CONV_DIM = D_SSD + 2 * SSD_GROUPS * SSD_STATE
D_GLA = 512
N_GLA_HEADS = 4
GLA_DV = 128
GLA_DK = 64
GLA_RANK = 16
GLA_GATE_NORMALIZER = 16.0
D_FF = 2816
EPS = 1e-6
D_QK = N_GLA_HEADS * GLA_DK
LOG2E = float(np.log2(np.e))

SUBLANES = 8
LANES = 128
BF16_ROWS = 16
VMEM_LIMIT_BYTES = 60 * 1024 * 1024

CHUNK = 128
TOKEN_TILE = 256
TOKEN_B_TILE = 512

PA_COLS = D_SSD + CONV_DIM + 2 * D_QK + 2 * D_GLA
PB_COLS = LANES
OFF_Z, OFF_XBC, OFF_Q, OFF_K, OFF_V, OFF_G = 0, 512, 1536, 1792, 2048, 2560


def _dot(a, b):
    return jnp.dot(a, b, preferred_element_type=f32)


def _dot_nt(a, b):
    return lax.dot_general(a, b, (((1,), (1,)), ((), ())), preferred_element_type=f32)


def _split3(x):
    hi = x.astype(bf16)
    r1 = x - hi.astype(f32)
    mid = r1.astype(bf16)
    lo = (r1 - mid.astype(f32)).astype(bf16)
    return hi, mid, lo


def _sel_dot(sel3, x):
    return _dot(sel3, jnp.concatenate(_split3(x), axis=0))


def _tiled3(sel):
    return np.concatenate([sel] * 3, axis=1)


def _dot_x3(a, b):
    a_hi = a.astype(bf16)
    a_lo = (a - a_hi.astype(f32)).astype(bf16)
    b_hi = b.astype(bf16)
    b_lo = (b - b_hi.astype(f32)).astype(bf16)
    return _dot(a_hi, b_hi) + _dot(a_hi, b_lo) + _dot(a_lo, b_hi)


def _rms(x, g):
    return x * lax.rsqrt(jnp.mean(x * x, axis=-1, keepdims=True) + EPS) * g


def _swiglu_half_step(x, g, w_up, w_dn):
    h = _rms(x, g).astype(bf16)
    gu = _dot(h, w_up)
    act = (jax.nn.silu(gu[:, :D_FF]) * gu[:, D_FF:]).astype(bf16)
    return x + 0.5 * _dot(act, w_dn)


N_LOAD = 16

W_IN_DT = D_SSD + CONV_DIM
W_IN_Q = W_IN_DT + N_SSD_HEADS
W_IN_LR = W_IN_Q + 2 * D_QK + 2 * D_GLA
W_IN_COLS = W_IN_LR + GLA_RANK


def _load_slab(step, slab_ref, dst_ref, regroup=None):
    rows = slab_ref.shape[0]
    assert rows % BF16_ROWS == 0
    val = slab_ref[...]
    if regroup is not None:
        val = regroup(val)
    dst_ref[pl.ds(pl.multiple_of(step * rows, BF16_ROWS), rows), :] = val.astype(bf16)


def _regroup_w_in(w):
    small = jnp.concatenate([w[:, W_IN_DT:W_IN_Q], w[:, W_IN_LR:W_IN_COLS],
                             jnp.zeros((w.shape[0], PB_COLS - N_SSD_HEADS - GLA_RANK), w.dtype)], axis=1)
    return jnp.concatenate([w[:, :W_IN_DT], w[:, W_IN_Q:W_IN_LR], small], axis=1)


ROW_STREAMS = 2


def _token_a_phases(rows, x_ref, x1_ref, pa_ref, pb_ref, g1, wup, wdn, gmix, win):
    x = x_ref[rows, :]
    gu = _dot(_rms(x, g1[...]).astype(bf16), wup[...])
    yield
    act = (jax.nn.silu(gu[:, :D_FF]) * gu[:, D_FF:]).astype(bf16)
    x1 = x + 0.5 * _dot(act, wdn[...])
    yield
    x1_ref[rows, :] = x1
    pr = _dot(_rms(x1, gmix[...]).astype(bf16), win[...])
    yield
    pa_ref[rows, :] = pr[:, :PA_COLS].astype(bf16)
    pb_ref[rows, :] = pr[:, PA_COLS:]


def _token_a_body(n_prompt, xp, xs, g1, wup_f, wdn_f, gmix, win_f, x1p, x1s, pap, pas, pbp, pbs, wup, wdn, win):
    def compute(x_ref, x1_ref, pa_ref, pb_ref):
        part = x_ref.shape[0] // ROW_STREAMS
        _round_robin([_token_a_phases(pl.ds(j * part, part), x_ref, x1_ref, pa_ref, pb_ref, g1, wup, wdn, gmix, win)
                      for j in range(ROW_STREAMS)])

    i = pl.program_id(0)

    @pl.when(i < N_LOAD)
    def _():
        _load_slab(i, wup_f, wup)
        _load_slab(i, wdn_f, wdn)
        _load_slab(i, win_f, win, _regroup_w_in)

    pl.when(jnp.logical_and(i >= N_LOAD, i < N_LOAD + n_prompt))(lambda: compute(xp, x1p, pap, pbp))
    pl.when(i >= N_LOAD + n_prompt)(lambda: compute(xs, x1s, pas, pbs))


def _token_b_phases(rows, x_ref, m_ref, y_ref, wout, g2, wup, wdn, gfin):
    x2 = x_ref[rows, :] + _dot(m_ref[rows, :], wout[...])
    yield
    gu = _dot(_rms(x2, g2[...]).astype(bf16), wup[...])
    yield
    act = (jax.nn.silu(gu[:, :D_FF]) * gu[:, D_FF:]).astype(bf16)
    x3 = x2 + 0.5 * _dot(act, wdn[...])
    yield
    y_ref[rows, :] = _rms(x3, gfin[...])


def _token_b_body(n_prompt, x1p, x1s, mxp, mxs, wout_f, g2, wup_f, wdn_f, gfin, yp, ys, wout, wup, wdn):
    def compute(x_ref, m_ref, y_ref):
        part = x_ref.shape[0] // ROW_STREAMS
        _round_robin([_token_b_phases(pl.ds(j * part, part), x_ref, m_ref, y_ref, wout, g2, wup, wdn, gfin)
                      for j in range(ROW_STREAMS)])

    i = pl.program_id(0)

    @pl.when(i < N_LOAD)
    def _():
        _load_slab(i, wout_f, wout)
        _load_slab(i, wup_f, wup)
        _load_slab(i, wdn_f, wdn)

    pl.when(jnp.logical_and(i >= N_LOAD, i < N_LOAD + n_prompt))(lambda: compute(x1p, mxp, yp))
    pl.when(i >= N_LOAD + n_prompt)(lambda: compute(x1s, mxs, ys))


def _two_group_specs(n_prompt, cols, tile):
    prompt = pl.BlockSpec((tile, cols), lambda i: (jnp.clip(i - N_LOAD, 0, n_prompt - 1), 0))
    sample = pl.BlockSpec((tile, cols), lambda i: (jnp.maximum(i - N_LOAD - n_prompt, 0), 0))
    return prompt, sample


def _whole(shape):
    return pl.BlockSpec(shape, lambda i: (0,) * len(shape), pipeline_mode=pl.Buffered(1))


def _slabs(weight):
    _, rows, cols = weight.shape
    assert rows % N_LOAD == 0
    return pl.BlockSpec((None, rows // N_LOAD, cols), lambda i: (0, jnp.minimum(i, N_LOAD - 1), 0))


def _token_a(xp, xs, g1, wup, wdn, gmix, win):
    tp, ts = xp.shape[0], xs.shape[0]
    tile = TOKEN_TILE
    n_prompt, n_sample = tp // tile, ts // tile
    assert tp % tile == 0 and ts % tile == 0 and win.shape[2] == W_IN_COLS
    xin = _two_group_specs(n_prompt, D_MODEL, tile)
    pa = _two_group_specs(n_prompt, PA_COLS, tile)
    pb = _two_group_specs(n_prompt, PB_COLS, tile)
    return pl.pallas_call(
        functools.partial(_token_a_body, n_prompt),
        grid=(N_LOAD + n_prompt + n_sample,),
        in_specs=[*xin, _whole(g1.shape), _slabs(wup), _slabs(wdn), _whole(gmix.shape), _slabs(win)],
        out_specs=[*xin, *pa, *pb],
        out_shape=[jax.ShapeDtypeStruct((tp, D_MODEL), f32), jax.ShapeDtypeStruct((ts, D_MODEL), f32),
                   jax.ShapeDtypeStruct((tp, PA_COLS), bf16), jax.ShapeDtypeStruct((ts, PA_COLS), bf16),
                   jax.ShapeDtypeStruct((tp, PB_COLS), f32), jax.ShapeDtypeStruct((ts, PB_COLS), f32)],
        scratch_shapes=[pltpu.VMEM(wup.shape[1:], bf16), pltpu.VMEM(wdn.shape[1:], bf16),
                        pltpu.VMEM((D_MODEL, PA_COLS + PB_COLS), bf16)],
        compiler_params=pltpu.CompilerParams(dimension_semantics=("arbitrary",), vmem_limit_bytes=VMEM_LIMIT_BYTES),
        name="token_a",
    )(xp, xs, g1, wup, wdn, gmix, win)


def _token_b(x1p, x1s, mxp, mxs, wout, g2, wup, wdn, gfin):
    tp, ts = x1p.shape[0], x1s.shape[0]
    tile = TOKEN_B_TILE
    assert tp % tile == 0 and ts % tile == 0
    n_prompt, n_sample = tp // tile, ts // tile
    xin = _two_group_specs(n_prompt, D_MODEL, tile)
    return pl.pallas_call(
        functools.partial(_token_b_body, n_prompt),
        grid=(N_LOAD + n_prompt + n_sample,),
        in_specs=[*xin, *xin, _slabs(wout), _whole(g2.shape), _slabs(wup), _slabs(wdn), _whole(gfin.shape)],
        out_specs=[*xin],
        out_shape=[jax.ShapeDtypeStruct((tp, D_MODEL), f32), jax.ShapeDtypeStruct((ts, D_MODEL), f32)],
        scratch_shapes=[pltpu.VMEM(w.shape[1:], bf16) for w in (wout, wup, wdn)],
        compiler_params=pltpu.CompilerParams(dimension_semantics=("arbitrary",), vmem_limit_bytes=VMEM_LIMIT_BYTES),
        name="token_b",
    )(x1p, x1s, mxp, mxs, wout, g2, wup, wdn, gfin)


def _level_halves(seq_rows):
    return [h for h in (1, 2, 4, 8, 16, 32, 64) if 2 * h <= seq_rows]


def _matmul_levels(seq_rows):
    return [h for h in _level_halves(seq_rows) if h < SUBLANES]


def _mixer_consts(seq_rows):
    t = np.arange(CHUNK)
    is_sample = seq_rows < CHUNK
    same = (t[:, None] // seq_rows) == (t[None, :] // seq_rows)
    tri = (same & (t[None, :] <= t[:, None])).astype(np.float32)
    last = ((t[:, None] // seq_rows) * seq_rows + seq_rows - 1 == t[None, :]).astype(np.float32)
    cum_gla, pair_mask = [tri], [np.eye(CHUNK, dtype=np.float32)]
    for h in _level_halves(seq_rows):
        ref_row = (t // (2 * h)) * (2 * h) + h
        if h in _matmul_levels(seq_rows):
            cum_gla.append((ref_row[:, None] == t[None, :]).astype(np.float32) @ tri)
        right = (t % (2 * h)) >= h
        blk = (t[:, None] // (2 * h)) == (t[None, :] // (2 * h))
        pair_mask.append((blk & right[:, None] & ~right[None, :]).astype(np.float32))
    cum_ssd = [tri]
    if is_sample:
        cum_gla.append(last @ tri)
        cum_ssd.append(last @ tri)
    hq = np.arange(D_QK) // GLA_DK
    hv = np.arange(D_GLA) // GLA_DV
    head_of_lane = np.arange(D_SSD) // SSD_HEAD_DIM
    expand = (np.arange(PB_COLS)[:, None] == head_of_lane[None, :]).astype(np.float32)
    consts = [jnp.asarray(_tiled3(np.concatenate(cum_ssd, 0)), bf16),
              jnp.asarray(_tiled3(np.concatenate(cum_gla, 0)), bf16),
              jnp.asarray(np.stack(pair_mask, 0), f32),
              jnp.asarray((hv[:, None] == hq[None, :]).astype(np.float32), f32),
              jnp.asarray(np.concatenate([expand] * 3, 0), bf16)]
    if not is_sample:
        shift = np.zeros(((CONV_W - 1) * CHUNK, 2 * CHUNK), np.float32)
        for k in range(1, CONV_W):
            shift[(k - 1) * CHUNK + t, CHUNK + t - k] = 1.0
        consts.append(jnp.asarray(shift, bf16))
    return consts


PROMPT_STREAMS = 2
SAMPLE_STREAMS = 1
N_SHARED_REFS = 14
EPILOGUE = "epilogue"


def _round_robin(streams):
    live = list(streams)
    waiting = []
    while live:
        for s in list(live):
            try:
                if next(s) == EPILOGUE:
                    live.remove(s)
                    waiting.append(s)
            except StopIteration:
                live.remove(s)
    for s in waiting:
        for _ in s:
            pass


def _mixer_body(seq_rows, *refs):
    is_sample = seq_rows < CHUNK
    n_streams = refs[0].shape[0]
    n_in = 5 if is_sample else 2
    n_shared = N_SHARED_REFS if is_sample else N_SHARED_REFS + 1
    ins, shared = refs[:n_in], refs[n_in:n_in + n_shared]
    outs = refs[n_in + n_shared:n_in + n_shared + 4]
    scratch = refs[n_in + n_shared + 4:]
    is_last = None
    if not is_sample:
        xext, st_ref, stbd_ref = scratch
        c_idx = pl.program_id(1)
        is_last = c_idx == pl.num_programs(1) - 1

        @pl.when(c_idx == 0)
        def _():
            xext[:, 0:CHUNK, :] = jnp.zeros((n_streams, CHUNK, CONV_DIM), bf16)
            st_ref[...] = jnp.zeros(st_ref.shape, f32)
            stbd_ref[...] = jnp.zeros(stbd_ref.shape, f32)

    at = lambda group, j: [r.at[j] for r in group]
    _round_robin([_mixer_phases(seq_rows, at(ins, j), shared, at(outs, j), at(scratch, j), is_last)
                  for j in range(n_streams)])


def _mixer_phases(seq_rows, ins, shared, outs, scratch, is_last):
    is_sample = seq_rows < CHUNK
    n_seq = CHUNK // seq_rows
    halves = _level_halves(seq_rows)
    n_lvl = len(halves)
    if is_sample:
        pa_ref, pb_ref, ssd0_ref, conv0_ref, gla0_ref = ins
        (cbuf,) = scratch
    else:
        pa_ref, pb_ref = ins
        xext, st_ref, stbd_ref = scratch
    (convw_ref, convb_ref, dsk_e_ref, gssd_ref, dtb_c_ref, alog_c_ref, wgk_ref, bgk_ref, ggla_ref,
     cum_ssd_ref, cum_gla_ref, pair_mask_ref, bdmask_ref, expand_ref) = shared[:N_SHARED_REFS]
    if not is_sample:
        shift_ref = shared[N_SHARED_REFS]
    mixed_ref, nconv_ref, nssd_ref, ngla_ref = outs

    C = CHUNK
    rows = lax.broadcasted_iota(jnp.int32, (C, 1), 0)
    lane_qk = lax.shift_right_logical(lax.broadcasted_iota(jnp.int32, (1, D_QK), 1), int(np.log2(GLA_DK)))

    z = pa_ref[:, OFF_Z:OFF_Z + D_SSD].astype(f32)
    q = pa_ref[:, OFF_Q:OFF_Q + D_QK].astype(f32) * (GLA_DK ** -0.5)
    k = pa_ref[:, OFF_K:OFF_K + D_QK].astype(f32)
    v_bf = pa_ref[:, OFF_V:OFF_V + D_GLA]
    g = pa_ref[:, OFF_G:OFF_G + D_GLA].astype(f32)

    conv = jnp.broadcast_to(convb_ref[...], (C, CONV_DIM))
    if is_sample:
        cbuf[:, SUBLANES - (CONV_W - 1):SUBLANES, :] = conv0_ref[...]
        cbuf[:, SUBLANES:SUBLANES + seq_rows, :] = (
            pa_ref[:, OFF_XBC:OFF_XBC + CONV_DIM].astype(f32).reshape(n_seq, seq_rows, CONV_DIM))
        for i in range(CONV_W):
            shift = CONV_W - 1 - i
            win = cbuf[:, SUBLANES - shift:SUBLANES - shift + seq_rows, :].reshape(C, CONV_DIM)
            conv = conv + win * convw_ref[i:i + 1, :]
        nconv_ref[...] = cbuf[:, SUBLANES + seq_rows - (CONV_W - 1):SUBLANES + seq_rows, :]
    else:
        xbc = pa_ref[:, OFF_XBC:OFF_XBC + CONV_DIM]
        xext[C:, :] = xbc
        taps = _dot(shift_ref[...], xext[...])
        for i in range(CONV_W - 1):
            shift = CONV_W - 1 - i
            conv = conv + taps[(shift - 1) * C:shift * C] * convw_ref[i:i + 1, :]
        conv = conv + xbc.astype(f32) * convw_ref[CONV_W - 1:CONV_W, :]
        xext[C - BF16_ROWS:C, :] = xext[2 * C - BF16_ROWS:, :]
    yield
    xc = jax.nn.silu(conv)
    xs = xc[:, :D_SSD]
    bm = xc[:, D_SSD:D_SSD + SSD_GROUPS * SSD_STATE].astype(bf16)
    cm = xc[:, D_SSD + SSD_GROUPS * SSD_STATE:].astype(bf16)

    small = pb_ref[...]
    dtp_c = jax.nn.softplus(small + dtb_c_ref[...])
    cums = _sel_dot(cum_ssd_ref[...], dtp_c * (-LOG2E * jnp.exp(alog_c_ref[...])))
    yield
    acum_c = cums[:C]
    acum_t = acum_c.T
    wide = _dot(jnp.concatenate(_split3(jnp.concatenate([cums, dtp_c], axis=0)), axis=1), expand_ref[...])
    yield
    acum_e, dtp_e = wide[:C], wide[-C:]
    acum_last = wide[C:2 * C] if is_sample else acum_e[C - 1:C, :]
    causal = cum_ssd_ref[0:C, 0:C].astype(f32) > 0
    xdt = xs * dtp_e
    xdt_bf = xdt.astype(bf16)
    lane = lax.broadcasted_iota(jnp.int32, (1, LANES), 1)
    left = lane < SSD_HEAD_DIM
    heads_per_group = N_SSD_HEADS // SSD_GROUPS
    gw = heads_per_group * SSD_HEAD_DIM
    y_parts = []
    for grp in range(SSD_GROUPS):
        cb = _dot_nt(cm[:, grp * SSD_STATE:(grp + 1) * SSD_STATE], bm[:, grp * SSD_STATE:(grp + 1) * SSD_STATE])
        for pair in range(heads_per_group // 2):
            sc = []
            for hh in range(2):
                h = grp * heads_per_group + pair * 2 + hh
                seg = acum_c[:, h:h + 1] - acum_t[h:h + 1, :]
                sc.append((cb * jnp.exp2(jnp.where(causal, seg, -jnp.inf))).astype(bf16))
            lo = (grp * heads_per_group + pair * 2) * SSD_HEAD_DIM
            xp = xdt_bf[:, lo:lo + LANES]
            bd = jnp.concatenate([jnp.where(left, xp, 0), jnp.where(left, 0, xp)], axis=0)
            y_parts.append(_dot(jnp.concatenate(sc, axis=1), bd))
            yield
    y = jnp.concatenate(y_parts, axis=1)

    to_end = jnp.exp2(jnp.minimum(acum_last - acum_e, 0.0))
    xs_end = (xdt * to_end).astype(bf16)
    e_acum = jnp.exp2(acum_e)
    bm_t = bm.T

    def ssd_inter(st):
        st_bf = st.astype(bf16)
        return jnp.concatenate([_dot(cm[:, grp * SSD_STATE:(grp + 1) * SSD_STATE], st_bf[:, grp * gw:(grp + 1) * gw])
                                for grp in range(SSD_GROUPS)], axis=1) * e_acum

    def ssd_update(st, decay_row, xs_rows):
        upd = jnp.concatenate([_dot(bm_t[grp * SSD_STATE:(grp + 1) * SSD_STATE, :], xs_rows[:, grp * gw:(grp + 1) * gw])
                               for grp in range(SSD_GROUPS)], axis=1)
        return st * decay_row + upd

    gk = _dot_x3(small, wgk_ref[...]) + bgk_ref[...]
    log_a = jax.nn.log_sigmoid(gk) * (LOG2E / GLA_GATE_NORMALIZER)
    bsel = _sel_dot(cum_gla_ref[...], log_a)
    yield
    bcum = bsel[:C]
    n_mm = len(_matmul_levels(seq_rows))
    b_last = bsel[(n_mm + 1) * C:] if is_sample else bcum[C - 1:C, :]
    k_bf = k.astype(bf16)

    def b_ref(lvl, half):
        if lvl < n_mm:
            return bsel[(lvl + 1) * C:(lvl + 2) * C]
        return jnp.concatenate([jnp.broadcast_to(bcum[r + half:r + half + 1, :], (2 * half, D_QK))
                                for r in range(0, C, 2 * half)], axis=0)

    def head_rows(x_bf):
        return jnp.concatenate([jnp.where(lane_qk == h, x_bf, jnp.zeros_like(x_bf)) for h in range(N_GLA_HEADS)], axis=0)

    a = _dot_nt(head_rows(q.astype(bf16)), k_bf)
    yield
    m = pair_mask_ref[0]
    att = [a[h * C:(h + 1) * C] * m for h in range(N_GLA_HEADS)]
    for lvl, half in enumerate(halves):
        decay = jnp.exp2(-jnp.abs(bcum - b_ref(lvl, half)))
        in_right = (rows & half) != 0
        u = (jnp.where(in_right, q, k) * decay).astype(bf16)
        a = _dot_nt(head_rows(u), u)
        yield
        m = pair_mask_ref[lvl + 1]
        for h in range(N_GLA_HEADS):
            att[h] = att[h] + a[h * C:(h + 1) * C] * m
    o = jnp.concatenate([_dot(att[h].astype(bf16), v_bf[:, h * GLA_DV:(h + 1) * GLA_DV])
                         for h in range(N_GLA_HEADS)], axis=1)
    yield

    q_in = (q * jnp.exp2(bcum)).astype(bf16)
    kd = (k * jnp.exp2(jnp.minimum(b_last - bcum, 0.0))).astype(bf16)
    v_t = v_bf.T
    bdmask = bdmask_ref[...]

    def gla_inter(stbd):
        return _dot_nt(q_in, stbd.astype(bf16))

    def gla_update(stbd, decay_row, kd_rows):
        return stbd * decay_row + bdmask * _dot(v_t, kd_rows)

    def gla_heads(stbd):
        acc = jnp.where(lane_qk == 0, stbd[0:GLA_DV, :], 0.0)
        for h in range(1, N_GLA_HEADS):
            acc = acc + jnp.where(lane_qk == h, stbd[h * GLA_DV:(h + 1) * GLA_DV, :], 0.0)
        return acc.T

    if not is_sample:
        st = st_ref[...]
        stbd = stbd_ref[...]
        y = y + ssd_inter(st)
        o = o + gla_inter(stbd)
        yield
        st_ref[...] = ssd_update(st, jnp.exp2(acum_last), xs_end)
        stbd_ref[...] = gla_update(stbd, jnp.exp2(b_last), kd)
        yield
    else:
        seq_of_row = lax.shift_right_logical(rows, int(np.log2(seq_rows)))
        y_inter = jnp.zeros((C, D_SSD), f32)
        o_inter = jnp.zeros((C, D_GLA), f32)
        for s in range(n_seq):
            mine = seq_of_row == s
            r0 = s * seq_rows
            st = ssd0_ref[s].T
            y_inter = y_inter + jnp.where(mine, ssd_inter(st), 0.0)
            nssd_ref[s] = ssd_update(st, jnp.exp2(acum_last[r0:r0 + 1, :]), jnp.where(mine, xs_end, 0)).T
            g0 = gla0_ref[s].T
            stbd = jnp.concatenate([jnp.where(lane_qk == h, g0, 0.0) for h in range(N_GLA_HEADS)], axis=0)
            o_inter = o_inter + jnp.where(mine, gla_inter(stbd), 0.0)
            ngla_ref[s] = gla_heads(gla_update(stbd, jnp.exp2(b_last[r0:r0 + 1, :]), jnp.where(mine, kd, 0)))
            yield
        y = y + y_inter
        o = o + o_inter

    y = (y + dsk_e_ref[...] * xs) * jax.nn.silu(z)
    gsz = D_SSD // SSD_GROUPS
    y = jnp.concatenate([_rms(y[:, i * gsz:(i + 1) * gsz], gssd_ref[:, i * gsz:(i + 1) * gsz])
                         for i in range(SSD_GROUPS)], axis=1)
    o = jnp.concatenate([_rms(o[:, h * GLA_DV:(h + 1) * GLA_DV], ggla_ref[...]) for h in range(N_GLA_HEADS)], axis=1)
    o = o * jax.nn.silu(g)
    mixed_ref[:, :D_SSD] = y.astype(bf16)
    mixed_ref[:, D_SSD:] = o.astype(bf16)

    if not is_sample:
        yield EPILOGUE

        @pl.when(is_last)
        def _():
            carried = xext[C - BF16_ROWS:C, :].astype(f32)
            nconv_ref[0] = carried[BF16_ROWS - (CONV_W - 1):, :]
            nssd_ref[0] = st_ref[...].T
            ngla_ref[0] = gla_heads(stbd_ref[...])


def _mixer(pa, pb, params, states, n_batch, seq_len):
    is_sample = states is not None
    seq_rows = seq_len if is_sample else CHUNK
    assert CHUNK % seq_rows == 0 and (is_sample or seq_len % CHUNK == 0)
    n_seq = CHUNK // seq_rows
    consts = _mixer_consts(seq_rows)
    ns = SAMPLE_STREAMS if is_sample else PROMPT_STREAMS
    assert n_batch % (ns * n_seq) == 0
    rows_per_stream = n_batch * seq_len // ns
    seqs_per_stream = n_batch // ns

    if is_sample:
        grid = (seqs_per_stream // n_seq,)
        row_map = lambda c: (0, c, 0)
        seq_map = lambda c: (0, c, 0, 0)
        full = lambda shape: pl.BlockSpec(shape, lambda c: (0,) * len(shape))
        sems = ("arbitrary",)
    else:
        n_chunks = seq_len // CHUNK
        grid = (seqs_per_stream, n_chunks)
        row_map = lambda b, c: (0, b * n_chunks + c, 0)
        seq_map = lambda b, c: (0, b, 0, 0)
        full = lambda shape: pl.BlockSpec(shape, lambda b, c: (0,) * len(shape))
        sems = ("arbitrary", "arbitrary")

    def streamed(arr):
        return arr.reshape((ns, arr.shape[0] // ns) + arr.shape[1:])

    state_blocks = [(n_seq, CONV_W - 1, CONV_DIM), (n_seq, D_SSD, SSD_STATE), (n_seq, D_QK, GLA_DV)]
    in_arrays = [streamed(pa), streamed(pb)]
    in_specs = [pl.BlockSpec((ns, CHUNK, PA_COLS), row_map), pl.BlockSpec((ns, CHUNK, PB_COLS), row_map)]
    if is_sample:
        ssd0, conv0, gla0 = states
        in_arrays += [streamed(ssd0), streamed(conv0), streamed(gla0)]
        in_specs += [pl.BlockSpec((ns,) + state_blocks[i], seq_map) for i in (1, 0, 2)]
    for arr in list(params) + consts:
        in_arrays.append(arr)
        in_specs.append(full(arr.shape))

    out_shape = [jax.ShapeDtypeStruct((ns, rows_per_stream, D_MODEL), bf16)] + [
        jax.ShapeDtypeStruct((ns, seqs_per_stream) + blk[1:], f32) for blk in state_blocks]
    out_specs = [pl.BlockSpec((ns, CHUNK, D_MODEL), row_map)] + [
        pl.BlockSpec((ns,) + blk, seq_map) for blk in state_blocks]
    if is_sample:
        scratch = [pltpu.VMEM((ns, n_seq, SUBLANES + seq_rows, CONV_DIM), f32)]
    else:
        scratch = [pltpu.VMEM((ns, 2 * CHUNK, CONV_DIM), bf16),
                   pltpu.VMEM((ns, SSD_STATE, D_SSD), f32), pltpu.VMEM((ns, D_GLA, D_QK), f32)]
    mixed, nconv, nssd, ngla = pl.pallas_call(
        functools.partial(_mixer_body, seq_rows),
        grid=grid, in_specs=in_specs, out_specs=out_specs, out_shape=out_shape, scratch_shapes=scratch,
        compiler_params=pltpu.CompilerParams(dimension_semantics=sems, vmem_limit_bytes=VMEM_LIMIT_BYTES),
        name="mixer_sample" if is_sample else "mixer_prompt",
    )(*in_arrays)
    merged = lambda arr: arr.reshape((arr.shape[0] * arr.shape[1],) + arr.shape[2:])
    return merged(mixed), merged(nconv), merged(nssd), merged(ngla)


def _row(vec):
    return vec.reshape(1, -1).astype(f32)


def _pad_lanes(vec, width):
    return jnp.concatenate([vec.astype(f32), jnp.zeros((width - vec.shape[0],), f32)]).reshape(1, width)


def _mixer_params(conv_w, conv_b, dt_bias, a_log, d_skip, g_ssd_norm, w_gk2, b_gk, g_gla_norm):
    rep = lambda vec: _row(jnp.repeat(vec, SSD_HEAD_DIM))
    wgk = jnp.zeros((PB_COLS, D_QK), f32).at[N_SSD_HEADS:N_SSD_HEADS + GLA_RANK, :].set(w_gk2.astype(f32))
    return [conv_w.astype(f32), _row(conv_b), rep(d_skip), _row(g_ssd_norm),
            _pad_lanes(dt_bias, PB_COLS), _pad_lanes(a_log, PB_COLS), wgk, _row(b_gk), _row(g_gla_norm)]


def kernel(x_prompt, x_sample, state_ssd, state_conv, state_gla, g_ffn1, w_ffn1_in, w_ffn1_out, g_mix, w_in, conv_w, conv_b, dt_bias, a_log, d_skip, g_ssd_norm, w_gk2, b_gk, g_gla_norm, w_out, g_ffn2, w_ffn2_in, w_ffn2_out, g_final):
    bp, lp, _ = x_prompt.shape
    bs, ls, _ = x_sample.shape
    assert w_in.shape[0] == 1, "single-layer step: the final norm is fused into the layer's last kernel"
    xp = x_prompt.reshape(bp * lp, D_MODEL)
    xs = x_sample.reshape(bs * ls, D_MODEL)
    x1p, x1s, pap, pas, pbp, pbs = _token_a(
        xp, xs, _row(g_ffn1[0]), w_ffn1_in.astype(f32), w_ffn1_out.astype(f32), _row(g_mix[0]), w_in.astype(f32))
    params = _mixer_params(conv_w[0], conv_b[0], dt_bias[0], a_log[0], d_skip[0], g_ssd_norm[0],
                           w_gk2[0], b_gk[0], g_gla_norm[0])
    mxp, p_conv, p_ssd, p_gla = _mixer(pap, pbp, params, None, bp, lp)
    s_states = (state_ssd[0].reshape(bs, D_SSD, SSD_STATE), state_conv[0], state_gla[0].reshape(bs, D_QK, GLA_DV))
    mxs, s_conv, s_ssd, s_gla = _mixer(pas, pbs, params, s_states, bs, ls)
    yp, ys = _token_b(x1p, x1s, mxp, mxs, w_out.astype(f32), _row(g_ffn2[0]),
                      w_ffn2_in.astype(f32), w_ffn2_out.astype(f32), _row(g_final))
    return (yp.reshape(bp, lp, D_MODEL), ys.reshape(bs, ls, D_MODEL),
            p_ssd.reshape(1, bp, N_SSD_HEADS, SSD_HEAD_DIM, SSD_STATE), p_conv[None],
            p_gla.reshape(1, bp, N_GLA_HEADS, GLA_DK, GLA_DV),
            s_ssd.reshape(1, bs, N_SSD_HEADS, SSD_HEAD_DIM, SSD_STATE), s_conv[None],
            s_gla.reshape(1, bs, N_GLA_HEADS, GLA_DK, GLA_DV))
```

```python
import functools

import jax
import jax.numpy as jnp
import numpy as np
from jax import lax
from jax.experimental import pallas as pl
from jax.experimental.pallas import tpu as pltpu

f32 = jnp.float32
bf16 = jnp.bfloat16

D_MODEL = 1024
D_SSD = 512
SSD_HEAD_DIM = 64
N_SSD_HEADS = 8
SSD_STATE = 128
SSD_GROUPS = 2
CONV_W = 4
CONV_DIM = D_SSD + 2 * SSD_GROUPS * SSD_STATE
D_GLA = 512
N_GLA_HEADS = 4
GLA_DV = 128
GLA_DK = 64
GLA_RANK = 16
GLA_GATE_NORMALIZER = 16.0
D_FF = 2816
EPS = 1e-6
D_QK = N_GLA_HEADS * GLA_DK
LOG2E = float(np.log2(np.e))

SUBLANES = 8
LANES = 128
BF16_ROWS = 16
VMEM_LIMIT_BYTES = 60 * 1024 * 1024

CHUNK = 128
TOKEN_TILE = 256
TOKEN_B_TILE = 512

PA_COLS = D_SSD + CONV_DIM + 2 * D_QK + 2 * D_GLA
PB_COLS = LANES
OFF_Z, OFF_XBC, OFF_Q, OFF_K, OFF_V, OFF_G = 0, 512, 1536, 1792, 2048, 2560


def _dot(a, b):
    return jnp.dot(a, b, preferred_element_type=f32)


def _dot_nt(a, b):
    return lax.dot_general(a, b, (((1,), (1,)), ((), ())), preferred_element_type=f32)


def _split3(x):
    hi = x.astype(bf16)
    r1 = x - hi.astype(f32)
    mid = r1.astype(bf16)
    lo = (r1 - mid.astype(f32)).astype(bf16)
    return hi, mid, lo


def _sel_dot(sel3, x):
    return _dot(sel3, jnp.concatenate(_split3(x), axis=0))


def _tiled3(sel):
    return np.concatenate([sel] * 3, axis=1)


def _dot_x3(a, b):
    a_hi = a.astype(bf16)
    a_lo = (a - a_hi.astype(f32)).astype(bf16)
    b_hi = b.astype(bf16)
    b_lo = (b - b_hi.astype(f32)).astype(bf16)
    return _dot(a_hi, b_hi) + _dot(a_hi, b_lo) + _dot(a_lo, b_hi)


def _rms(x, g):
    return x * lax.rsqrt(jnp.mean(x * x, axis=-1, keepdims=True) + EPS) * g


N_LOAD = 16

W_IN_DT = D_SSD + CONV_DIM
W_IN_Q = W_IN_DT + N_SSD_HEADS
W_IN_LR = W_IN_Q + 2 * D_QK + 2 * D_GLA
W_IN_COLS = W_IN_LR + GLA_RANK


def _load_slab(step, slab_ref, dst_ref, regroup=None):
    rows = slab_ref.shape[0]
    assert rows % BF16_ROWS == 0
    val = slab_ref[...]
    if regroup is not None:
        val = regroup(val)
    dst_ref[pl.ds(pl.multiple_of(step * rows, BF16_ROWS), rows), :] = val.astype(bf16)


def _regroup_w_in(w):
    small = jnp.concatenate([w[:, W_IN_DT:W_IN_Q], w[:, W_IN_LR:W_IN_COLS],
                             jnp.zeros((w.shape[0], PB_COLS - N_SSD_HEADS - GLA_RANK), w.dtype)], axis=1)
    return jnp.concatenate([w[:, :W_IN_DT], w[:, W_IN_Q:W_IN_LR], small], axis=1)


ROW_STREAMS = 2


def _token_a_phases(rows, x_ref, x1_ref, pa_ref, pb_ref, g1, wup, wdn, gmix, win):
    x = x_ref[rows, :]
    gu = _dot(_rms(x, g1[...]).astype(bf16), wup[...])
    yield
    act = (jax.nn.silu(gu[:, :D_FF]) * gu[:, D_FF:]).astype(bf16)
    x1 = x + 0.5 * _dot(act, wdn[...])
    yield
    x1_ref[rows, :] = x1
    pr = _dot(_rms(x1, gmix[...]).astype(bf16), win[...])
    yield
    pa_ref[rows, :] = pr[:, :PA_COLS].astype(bf16)
    pb_ref[rows, :] = pr[:, PA_COLS:]


def _token_a_body(n_prompt, xp, xs, g1, wup_f, wdn_f, gmix, win_f, x1p, x1s, pap, pas, pbp, pbs, wup, wdn, win):
    def compute(x_ref, x1_ref, pa_ref, pb_ref):
        part = x_ref.shape[0] // ROW_STREAMS
        _round_robin([_token_a_phases(pl.ds(j * part, part), x_ref, x1_ref, pa_ref, pb_ref, g1, wup, wdn, gmix, win)
                      for j in range(ROW_STREAMS)])

    i = pl.program_id(0)

    @pl.when(i < N_LOAD)
    def _():
        _load_slab(i, wup_f, wup)
        _load_slab(i, wdn_f, wdn)
        _load_slab(i, win_f, win, _regroup_w_in)

    pl.when(jnp.logical_and(i >= N_LOAD, i < N_LOAD + n_prompt))(lambda: compute(xp, x1p, pap, pbp))
    pl.when(i >= N_LOAD + n_prompt)(lambda: compute(xs, x1s, pas, pbs))


def _token_b_phases(rows, x_ref, m_ref, y_ref, wout, g2, wup, wdn, gfin):
    x2 = x_ref[rows, :] + _dot(m_ref[rows, :], wout[...])
    yield
    gu = _dot(_rms(x2, g2[...]).astype(bf16), wup[...])
    yield
    act = (jax.nn.silu(gu[:, :D_FF]) * gu[:, D_FF:]).astype(bf16)
    x3 = x2 + 0.5 * _dot(act, wdn[...])
    yield
    y_ref[rows, :] = _rms(x3, gfin[...])


def _token_b_body(n_prompt, x1p, x1s, mxp, mxs, wout_f, g2, wup_f, wdn_f, gfin, yp, ys, wout, wup, wdn):
    def compute(x_ref, m_ref, y_ref):
        part = x_ref.shape[0] // ROW_STREAMS
        _round_robin([_token_b_phases(pl.ds(j * part, part), x_ref, m_ref, y_ref, wout, g2, wup, wdn, gfin)
                      for j in range(ROW_STREAMS)])

    i = pl.program_id(0)

    @pl.when(i < N_LOAD)
    def _():
        _load_slab(i, wout_f, wout)
        _load_slab(i, wup_f, wup)
        _load_slab(i, wdn_f, wdn)

    pl.when(jnp.logical_and(i >= N_LOAD, i < N_LOAD + n_prompt))(lambda: compute(x1p, mxp, yp))
    pl.when(i >= N_LOAD + n_prompt)(lambda: compute(x1s, mxs, ys))


def _two_group_specs(n_prompt, cols, tile):
    prompt = pl.BlockSpec((tile, cols), lambda i: (jnp.clip(i - N_LOAD, 0, n_prompt - 1), 0))
    sample = pl.BlockSpec((tile, cols), lambda i: (jnp.maximum(i - N_LOAD - n_prompt, 0), 0))
    return prompt, sample


def _whole(shape):
    return pl.BlockSpec(shape, lambda i: (0,) * len(shape), pipeline_mode=pl.Buffered(1))


def _slabs(weight):
    _, rows, cols = weight.shape
    assert rows % N_LOAD == 0
    return pl.BlockSpec((None, rows // N_LOAD, cols), lambda i: (0, jnp.minimum(i, N_LOAD - 1), 0))


def _token_a(xp, xs, g1, wup, wdn, gmix, win):
    tp, ts = xp.shape[0], xs.shape[0]
    tile = TOKEN_TILE
    n_prompt, n_sample = tp // tile, ts // tile
    assert tp % tile == 0 and ts % tile == 0 and win.shape[2] == W_IN_COLS
    xin = _two_group_specs(n_prompt, D_MODEL, tile)
    pa = _two_group_specs(n_prompt, PA_COLS, tile)
    pb = _two_group_specs(n_prompt, PB_COLS, tile)
    return pl.pallas_call(
        functools.partial(_token_a_body, n_prompt),
        grid=(N_LOAD + n_prompt + n_sample,),
        in_specs=[*xin, _whole(g1.shape), _slabs(wup), _slabs(wdn), _whole(gmix.shape), _slabs(win)],
        out_specs=[*xin, *pa, *pb],
        out_shape=[jax.ShapeDtypeStruct((tp, D_MODEL), f32), jax.ShapeDtypeStruct((ts, D_MODEL), f32),
                   jax.ShapeDtypeStruct((tp, PA_COLS), bf16), jax.ShapeDtypeStruct((ts, PA_COLS), bf16),
                   jax.ShapeDtypeStruct((tp, PB_COLS), f32), jax.ShapeDtypeStruct((ts, PB_COLS), f32)],
        scratch_shapes=[pltpu.VMEM(wup.shape[1:], bf16), pltpu.VMEM(wdn.shape[1:], bf16),
                        pltpu.VMEM((D_MODEL, PA_COLS + PB_COLS), bf16)],
        compiler_params=pltpu.CompilerParams(dimension_semantics=("arbitrary",), vmem_limit_bytes=VMEM_LIMIT_BYTES),
        name="token_a",
    )(xp, xs, g1, wup, wdn, gmix, win)


def _token_b(x1p, x1s, mxp, mxs, wout, g2, wup, wdn, gfin):
    tp, ts = x1p.shape[0], x1s.shape[0]
    tile = TOKEN_B_TILE
    assert tp % tile == 0 and ts % tile == 0
    n_prompt, n_sample = tp // tile, ts // tile
    xin = _two_group_specs(n_prompt, D_MODEL, tile)
    return pl.pallas_call(
        functools.partial(_token_b_body, n_prompt),
        grid=(N_LOAD + n_prompt + n_sample,),
        in_specs=[*xin, *xin, _slabs(wout), _whole(g2.shape), _slabs(wup), _slabs(wdn), _whole(gfin.shape)],
        out_specs=[*xin],
        out_shape=[jax.ShapeDtypeStruct((tp, D_MODEL), f32), jax.ShapeDtypeStruct((ts, D_MODEL), f32)],
        scratch_shapes=[pltpu.VMEM(w.shape[1:], bf16) for w in (wout, wup, wdn)],
        compiler_params=pltpu.CompilerParams(dimension_semantics=("arbitrary",), vmem_limit_bytes=VMEM_LIMIT_BYTES),
        name="token_b",
    )(x1p, x1s, mxp, mxs, wout, g2, wup, wdn, gfin)


def _level_halves(seq_rows):
    return [h for h in (1, 2, 4, 8, 16, 32, 64) if 2 * h <= seq_rows]


def _matmul_levels(seq_rows):
    return [h for h in _level_halves(seq_rows) if h < SUBLANES]


def _mixer_consts(seq_rows):
    t = np.arange(CHUNK)
    is_sample = seq_rows < CHUNK
    same = (t[:, None] // seq_rows) == (t[None, :] // seq_rows)
    tri = (same & (t[None, :] <= t[:, None])).astype(np.float32)
    last = ((t[:, None] // seq_rows) * seq_rows + seq_rows - 1 == t[None, :]).astype(np.float32)
    cum_gla, pair_mask = [tri], [np.eye(CHUNK, dtype=np.float32)]
    for h in _level_halves(seq_rows):
        ref_row = (t // (2 * h)) * (2 * h) + h
        if h in _matmul_levels(seq_rows):
            cum_gla.append((ref_row[:, None] == t[None, :]).astype(np.float32) @ tri)
        right = (t % (2 * h)) >= h
        blk = (t[:, None] // (2 * h)) == (t[None, :] // (2 * h))
        pair_mask.append((blk & right[:, None] & ~right[None, :]).astype(np.float32))
    cum_ssd = [tri]
    if is_sample:
        cum_gla.append(last @ tri)
        cum_ssd.append(last @ tri)
    hq = np.arange(D_QK) // GLA_DK
    hv = np.arange(D_GLA) // GLA_DV
    head_of_lane = np.arange(D_SSD) // SSD_HEAD_DIM
    expand = (np.arange(PB_COLS)[:, None] == head_of_lane[None, :]).astype(np.float32)
    consts = [jnp.asarray(_tiled3(np.concatenate(cum_ssd, 0)), bf16),
              jnp.asarray(_tiled3(np.concatenate(cum_gla, 0)), bf16),
              jnp.asarray(np.stack(pair_mask, 0), f32),
              jnp.asarray((hv[:, None] == hq[None, :]).astype(np.float32), f32),
              jnp.asarray(np.concatenate([expand] * 3, 0), bf16)]
    if not is_sample:
        shift = np.zeros(((CONV_W - 1) * CHUNK, 2 * CHUNK), np.float32)
        for k in range(1, CONV_W):
            shift[(k - 1) * CHUNK + t, CHUNK + t - k] = 1.0
        consts.append(jnp.asarray(shift, bf16))
    return consts


PROMPT_STREAMS = 2
SAMPLE_STREAMS = 1
N_SHARED_REFS = 14
EPILOGUE = "epilogue"


def _round_robin(streams):
    live = list(streams)
    waiting = []
    while live:
        for s in list(live):
            try:
                if next(s) == EPILOGUE:
                    live.remove(s)
                    waiting.append(s)
            except StopIteration:
                live.remove(s)
    for s in waiting:
        for _ in s:
            pass


def _mixer_body(seq_rows, *refs):
    is_sample = seq_rows < CHUNK
    n_streams = refs[0].shape[0]
    n_in = 5 if is_sample else 2
    n_shared = N_SHARED_REFS if is_sample else N_SHARED_REFS + 1
    ins, shared = refs[:n_in], refs[n_in:n_in + n_shared]
    outs = refs[n_in + n_shared:n_in + n_shared + 4]
    scratch = refs[n_in + n_shared + 4:]
    is_last = None
    if not is_sample:
        xext, st_ref, stbd_ref = scratch
        c_idx = pl.program_id(1)
        is_last = c_idx == pl.num_programs(1) - 1

        @pl.when(c_idx == 0)
        def _():
            xext[:, 0:CHUNK, :] = jnp.zeros((n_streams, CHUNK, CONV_DIM), bf16)
            st_ref[...] = jnp.zeros(st_ref.shape, f32)
            stbd_ref[...] = jnp.zeros(stbd_ref.shape, f32)

    at = lambda group, j: [r.at[j] for r in group]
    _round_robin([_mixer_phases(seq_rows, at(ins, j), shared, at(outs, j), at(scratch, j), is_last)
                  for j in range(n_streams)])


def _mixer_phases(seq_rows, ins, shared, outs, scratch, is_last):
    is_sample = seq_rows < CHUNK
    n_seq = CHUNK // seq_rows
    halves = _level_halves(seq_rows)
    n_lvl = len(halves)
    if is_sample:
        pa_ref, pb_ref, ssd0_ref, conv0_ref, gla0_ref = ins
        (cbuf,) = scratch
    else:
        pa_ref, pb_ref = ins
        xext, st_ref, stbd_ref = scratch
    (convw_ref, convb_ref, dsk_e_ref, gssd_ref, dtb_c_ref, alog_c_ref, wgk_ref, bgk_ref, ggla_ref,
     cum_ssd_ref, cum_gla_ref, pair_mask_ref, bdmask_ref, expand_ref) = shared[:N_SHARED_REFS]
    if not is_sample:
        shift_ref = shared[N_SHARED_REFS]
    mixed_ref, nconv_ref, nssd_ref, ngla_ref = outs

    C = CHUNK
    rows = lax.broadcasted_iota(jnp.int32, (C, 1), 0)
    lane_qk = lax.shift_right_logical(lax.broadcasted_iota(jnp.int32, (1, D_QK), 1), int(np.log2(GLA_DK)))

    z = pa_ref[:, OFF_Z:OFF_Z + D_SSD].astype(f32)
    q = pa_ref[:, OFF_Q:OFF_Q + D_QK].astype(f32) * (GLA_DK ** -0.5)
    k = pa_ref[:, OFF_K:OFF_K + D_QK].astype(f32)
    v_bf = pa_ref[:, OFF_V:OFF_V + D_GLA]
    g = pa_ref[:, OFF_G:OFF_G + D_GLA].astype(f32)

    conv = jnp.broadcast_to(convb_ref[...], (C, CONV_DIM))
    if is_sample:
        cbuf[:, SUBLANES - (CONV_W - 1):SUBLANES, :] = conv0_ref[...]
        cbuf[:, SUBLANES:SUBLANES + seq_rows, :] = (
            pa_ref[:, OFF_XBC:OFF_XBC + CONV_DIM].astype(f32).reshape(n_seq, seq_rows, CONV_DIM))
        for i in range(CONV_W):
            shift = CONV_W - 1 - i
            win = cbuf[:, SUBLANES - shift:SUBLANES - shift + seq_rows, :].reshape(C, CONV_DIM)
            conv = conv + win * convw_ref[i:i + 1, :]
        nconv_ref[...] = cbuf[:, SUBLANES + seq_rows - (CONV_W - 1):SUBLANES + seq_rows, :]
    else:
        xbc = pa_ref[:, OFF_XBC:OFF_XBC + CONV_DIM]
        xext[C:, :] = xbc
        taps = _dot(shift_ref[...], xext[...])
        for i in range(CONV_W - 1):
            shift = CONV_W - 1 - i
            conv = conv + taps[(shift - 1) * C:shift * C] * convw_ref[i:i + 1, :]
        conv = conv + xbc.astype(f32) * convw_ref[CONV_W - 1:CONV_W, :]
        xext[C - BF16_ROWS:C, :] = xext[2 * C - BF16_ROWS:, :]
    yield
    xc = jax.nn.silu(conv)
    xs = xc[:, :D_SSD]
    bm = xc[:, D_SSD:D_SSD + SSD_GROUPS * SSD_STATE].astype(bf16)
    cm = xc[:, D_SSD + SSD_GROUPS * SSD_STATE:].astype(bf16)

    small = pb_ref[...]
    dtp_c = jax.nn.softplus(small + dtb_c_ref[...])
    cums = _sel_dot(cum_ssd_ref[...], dtp_c * (-LOG2E * jnp.exp(alog_c_ref[...])))
    yield
    acum_c = cums[:C]
    acum_t = acum_c.T
    wide = _dot(jnp.concatenate(_split3(jnp.concatenate([cums, dtp_c], axis=0)), axis=1), expand_ref[...])
    yield
    acum_e, dtp_e = wide[:C], wide[-C:]
    acum_last = wide[C:2 * C] if is_sample else acum_e[C - 1:C, :]
    causal = cum_ssd_ref[0:C, 0:C].astype(f32) > 0
    xdt = xs * dtp_e
    xdt_bf = xdt.astype(bf16)
    lane = lax.broadcasted_iota(jnp.int32, (1, LANES), 1)
    left = lane < SSD_HEAD_DIM
    heads_per_group = N_SSD_HEADS // SSD_GROUPS
    gw = heads_per_group * SSD_HEAD_DIM
    y_parts = []
    for grp in range(SSD_GROUPS):
        cb = _dot_nt(cm[:, grp * SSD_STATE:(grp + 1) * SSD_STATE], bm[:, grp * SSD_STATE:(grp + 1) * SSD_STATE])
        for pair in range(heads_per_group // 2):
            sc = []
            for hh in range(2):
                h = grp * heads_per_group + pair * 2 + hh
                seg = acum_c[:, h:h + 1] - acum_t[h:h + 1, :]
                sc.append((cb * jnp.exp2(jnp.where(causal, seg, -jnp.inf))).astype(bf16))
            lo = (grp * heads_per_group + pair * 2) * SSD_HEAD_DIM
            xp = xdt_bf[:, lo:lo + LANES]
            bd = jnp.concatenate([jnp.where(left, xp, 0), jnp.where(left, 0, xp)], axis=0)
            y_parts.append(_dot(jnp.concatenate(sc, axis=1), bd))
            yield
    y = jnp.concatenate(y_parts, axis=1)

    to_end = jnp.exp2(jnp.minimum(acum_last - acum_e, 0.0))
    xs_end = (xdt * to_end).astype(bf16)
    e_acum = jnp.exp2(acum_e)
    bm_t = bm.T

    def ssd_inter(st):
        st_bf = st.astype(bf16)
        return jnp.concatenate([_dot(cm[:, grp * SSD_STATE:(grp + 1) * SSD_STATE], st_bf[:, grp * gw:(grp + 1) * gw])
                                for grp in range(SSD_GROUPS)], axis=1) * e_acum

    def ssd_update(st, decay_row, xs_rows):
        upd = jnp.concatenate([_dot(bm_t[grp * SSD_STATE:(grp + 1) * SSD_STATE, :], xs_rows[:, grp * gw:(grp + 1) * gw])
                               for grp in range(SSD_GROUPS)], axis=1)
        return st * decay_row + upd

    gk = _dot_x3(small, wgk_ref[...]) + bgk_ref[...]
    log_a = jax.nn.log_sigmoid(gk) * (LOG2E / GLA_GATE_NORMALIZER)
    bsel = _sel_dot(cum_gla_ref[...], log_a)
    yield
    bcum = bsel[:C]
    n_mm = len(_matmul_levels(seq_rows))
    b_last = bsel[(n_mm + 1) * C:] if is_sample else bcum[C - 1:C, :]
    k_bf = k.astype(bf16)

    def b_ref(lvl, half):
        if lvl < n_mm:
            return bsel[(lvl + 1) * C:(lvl + 2) * C]
        return jnp.concatenate([jnp.broadcast_to(bcum[r + half:r + half + 1, :], (2 * half, D_QK))
                                for r in range(0, C, 2 * half)], axis=0)

    def head_rows(x_bf):
        return jnp.concatenate([jnp.where(lane_qk == h, x_bf, jnp.zeros_like(x_bf)) for h in range(N_GLA_HEADS)], axis=0)

    a = _dot_nt(head_rows(q.astype(bf16)), k_bf)
    yield
    m = pair_mask_ref[0]
    att = [a[h * C:(h + 1) * C] * m for h in range(N_GLA_HEADS)]
    for lvl, half in enumerate(halves):
        decay = jnp.exp2(-jnp.abs(bcum - b_ref(lvl, half)))
        in_right = (rows & half) != 0
        u = jnp.where(in_right, q, k) * decay
        u_bf = u.astype(bf16)
        m = pair_mask_ref[lvl + 1]
        if half < SUBLANES:
            a = _dot_nt(head_rows(u_bf), u_bf)
            yield
            for h in range(N_GLA_HEADS):
                att[h] = att[h] + a[h * C:(h + 1) * C] * m
        else:
            starts = range(half, C, 2 * half)
            pick = lambda x: jnp.concatenate([x[r:r + half] for r in starts], axis=0)
            a = _dot_nt(head_rows(pick(u).astype(bf16)), u_bf)
            yield
            m_right = pick(m)
            for h in range(N_GLA_HEADS):
                upd = a[h * (C // 2):(h + 1) * (C // 2)] * m_right
                parts = []
                for i, r in enumerate(starts):
                    parts += [att[h][r - half:r], att[h][r:r + half] + upd[i * half:(i + 1) * half]]
                att[h] = jnp.concatenate(parts, axis=0)
    o = jnp.concatenate([_dot(att[h].astype(bf16), v_bf[:, h * GLA_DV:(h + 1) * GLA_DV])
                         for h in range(N_GLA_HEADS)], axis=1)
    yield

    q_in = (q * jnp.exp2(bcum)).astype(bf16)
    kd = (k * jnp.exp2(jnp.minimum(b_last - bcum, 0.0))).astype(bf16)
    v_t = v_bf.T
    bdmask = bdmask_ref[...]

    def gla_inter(stbd):
        return _dot_nt(q_in, stbd.astype(bf16))

    def gla_update(stbd, decay_row, kd_rows):
        return stbd * decay_row + bdmask * _dot(v_t, kd_rows)

    def gla_heads(stbd):
        acc = jnp.where(lane_qk == 0, stbd[0:GLA_DV, :], 0.0)
        for h in range(1, N_GLA_HEADS):
            acc = acc + jnp.where(lane_qk == h, stbd[h * GLA_DV:(h + 1) * GLA_DV, :], 0.0)
        return acc.T

    if not is_sample:
        st = st_ref[...]
        stbd = stbd_ref[...]
        y = y + ssd_inter(st)
        o = o + gla_inter(stbd)
        yield
        st_ref[...] = ssd_update(st, jnp.exp2(acum_last), xs_end)
        stbd_ref[...] = gla_update(stbd, jnp.exp2(b_last), kd)
        yield
    else:
        seq_of_row = lax.shift_right_logical(rows, int(np.log2(seq_rows)))
        cm_f32 = xc[:, D_SSD + SSD_GROUPS * SSD_STATE:]
        q_in_f32 = q * jnp.exp2(bcum)
        y_rows, o_rows = [], []
        transposed = (ssd0_ref[0].T, gla0_ref[0].T)
        for s in range(n_seq):
            st, g0 = transposed
            if s + 1 < n_seq:
                transposed = (ssd0_ref[s + 1].T, gla0_ref[s + 1].T)
            mine = seq_of_row == s
            r0 = s * seq_rows
            st_bf = st.astype(bf16)
            c_rows = cm_f32[r0:r0 + seq_rows].astype(bf16)
            y_rows.append(jnp.concatenate(
                [_dot(c_rows[:, grp * SSD_STATE:(grp + 1) * SSD_STATE], st_bf[:, grp * gw:(grp + 1) * gw])
                 for grp in range(SSD_GROUPS)], axis=1))
            nssd_ref[s] = ssd_update(st, jnp.exp2(acum_last[r0:r0 + 1, :]), jnp.where(mine, xs_end, 0)).T
            stbd = jnp.concatenate([jnp.where(lane_qk == h, g0, 0.0) for h in range(N_GLA_HEADS)], axis=0)
            o_rows.append(_dot_nt(q_in_f32[r0:r0 + seq_rows].astype(bf16), stbd.astype(bf16)))
            ngla_ref[s] = gla_heads(gla_update(stbd, jnp.exp2(b_last[r0:r0 + 1, :]), jnp.where(mine, kd, 0)))
            yield
        y = y + jnp.concatenate(y_rows, axis=0) * e_acum
        o = o + jnp.concatenate(o_rows, axis=0)

    y = (y + dsk_e_ref[...] * xs) * jax.nn.silu(z)
    gsz = D_SSD // SSD_GROUPS
    y = jnp.concatenate([_rms(y[:, i * gsz:(i + 1) * gsz], gssd_ref[:, i * gsz:(i + 1) * gsz])
                         for i in range(SSD_GROUPS)], axis=1)
    o = jnp.concatenate([_rms(o[:, h * GLA_DV:(h + 1) * GLA_DV], ggla_ref[...]) for h in range(N_GLA_HEADS)], axis=1)
    o = o * jax.nn.silu(g)
    mixed_ref[:, :D_SSD] = y.astype(bf16)
    mixed_ref[:, D_SSD:] = o.astype(bf16)

    if not is_sample:
        yield EPILOGUE

        @pl.when(is_last)
        def _():
            carried = xext[C - BF16_ROWS:C, :].astype(f32)
            nconv_ref[0] = carried[BF16_ROWS - (CONV_W - 1):, :]
            nssd_ref[0] = st_ref[...].T
            ngla_ref[0] = gla_heads(stbd_ref[...])


def _mixer(pa, pb, params, states, n_batch, seq_len):
    is_sample = states is not None
    seq_rows = seq_len if is_sample else CHUNK
    assert CHUNK % seq_rows == 0 and (is_sample or seq_len % CHUNK == 0)
    n_seq = CHUNK // seq_rows
    consts = _mixer_consts(seq_rows)
    ns = SAMPLE_STREAMS if is_sample else PROMPT_STREAMS
    assert n_batch % (ns * n_seq) == 0
    rows_per_stream = n_batch * seq_len // ns
    seqs_per_stream = n_batch // ns

    if is_sample:
        grid = (seqs_per_stream // n_seq,)
        row_map = lambda c: (0, c, 0)
        seq_map = lambda c: (0, c, 0, 0)
        full = lambda shape: pl.BlockSpec(shape, lambda c: (0,) * len(shape))
        sems = ("arbitrary",)
    else:
        n_chunks = seq_len // CHUNK
        grid = (seqs_per_stream, n_chunks)
        row_map = lambda b, c: (0, b * n_chunks + c, 0)
        seq_map = lambda b, c: (0, b, 0, 0)
        full = lambda shape: pl.BlockSpec(shape, lambda b, c: (0,) * len(shape))
        sems = ("arbitrary", "arbitrary")

    def streamed(arr):
        return arr.reshape((ns, arr.shape[0] // ns) + arr.shape[1:])

    state_blocks = [(n_seq, CONV_W - 1, CONV_DIM), (n_seq, D_SSD, SSD_STATE), (n_seq, D_QK, GLA_DV)]
    in_arrays = [streamed(pa), streamed(pb)]
    in_specs = [pl.BlockSpec((ns, CHUNK, PA_COLS), row_map), pl.BlockSpec((ns, CHUNK, PB_COLS), row_map)]
    if is_sample:
        ssd0, conv0, gla0 = states
        in_arrays += [streamed(ssd0), streamed(conv0), streamed(gla0)]
        in_specs += [pl.BlockSpec((ns,) + state_blocks[i], seq_map) for i in (1, 0, 2)]
    for arr in list(params) + consts:
        in_arrays.append(arr)
        in_specs.append(full(arr.shape))

    out_shape = [jax.ShapeDtypeStruct((ns, rows_per_stream, D_MODEL), bf16)] + [
        jax.ShapeDtypeStruct((ns, seqs_per_stream) + blk[1:], f32) for blk in state_blocks]
    out_specs = [pl.BlockSpec((ns, CHUNK, D_MODEL), row_map)] + [
        pl.BlockSpec((ns,) + blk, seq_map) for blk in state_blocks]
    if is_sample:
        scratch = [pltpu.VMEM((ns, n_seq, SUBLANES + seq_rows, CONV_DIM), f32)]
    else:
        scratch = [pltpu.VMEM((ns, 2 * CHUNK, CONV_DIM), bf16),
                   pltpu.VMEM((ns, SSD_STATE, D_SSD), f32), pltpu.VMEM((ns, D_GLA, D_QK), f32)]
    mixed, nconv, nssd, ngla = pl.pallas_call(
        functools.partial(_mixer_body, seq_rows),
        grid=grid, in_specs=in_specs, out_specs=out_specs, out_shape=out_shape, scratch_shapes=scratch,
        compiler_params=pltpu.CompilerParams(dimension_semantics=sems, vmem_limit_bytes=VMEM_LIMIT_BYTES),
        name="mixer_sample" if is_sample else "mixer_prompt",
    )(*in_arrays)
    merged = lambda arr: arr.reshape((arr.shape[0] * arr.shape[1],) + arr.shape[2:])
    return merged(mixed), merged(nconv), merged(nssd), merged(ngla)


def _row(vec):
    return vec.reshape(1, -1).astype(f32)


def _pad_lanes(vec, width):
    return jnp.concatenate([vec.astype(f32), jnp.zeros((width - vec.shape[0],), f32)]).reshape(1, width)


def _mixer_params(conv_w, conv_b, dt_bias, a_log, d_skip, g_ssd_norm, w_gk2, b_gk, g_gla_norm):
    rep = lambda vec: _row(jnp.repeat(vec, SSD_HEAD_DIM))
    wgk = jnp.zeros((PB_COLS, D_QK), f32).at[N_SSD_HEADS:N_SSD_HEADS + GLA_RANK, :].set(w_gk2.astype(f32))
    return [conv_w.astype(f32), _row(conv_b), rep(d_skip), _row(g_ssd_norm),
            _pad_lanes(dt_bias, PB_COLS), _pad_lanes(a_log, PB_COLS), wgk, _row(b_gk), _row(g_gla_norm)]


def kernel(x_prompt, x_sample, state_ssd, state_conv, state_gla, g_ffn1, w_ffn1_in, w_ffn1_out, g_mix, w_in, conv_w, conv_b, dt_bias, a_log, d_skip, g_ssd_norm, w_gk2, b_gk, g_gla_norm, w_out, g_ffn2, w_ffn2_in, w_ffn2_out, g_final):
    bp, lp, _ = x_prompt.shape
    bs, ls, _ = x_sample.shape
    assert w_in.shape[0] == 1, "single-layer step: the final norm is fused into the layer's last kernel"
    xp = x_prompt.reshape(bp * lp, D_MODEL)
    xs = x_sample.reshape(bs * ls, D_MODEL)
    x1p, x1s, pap, pas, pbp, pbs = _token_a(
        xp, xs, _row(g_ffn1[0]), w_ffn1_in.astype(f32), w_ffn1_out.astype(f32), _row(g_mix[0]), w_in.astype(f32))
    params = _mixer_params(conv_w[0], conv_b[0], dt_bias[0], a_log[0], d_skip[0], g_ssd_norm[0],
                           w_gk2[0], b_gk[0], g_gla_norm[0])
    mxp, p_conv, p_ssd, p_gla = _mixer(pap, pbp, params, None, bp, lp)
    s_states = (state_ssd[0].reshape(bs, D_SSD, SSD_STATE), state_conv[0], state_gla[0].reshape(bs, D_QK, GLA_DV))
    mxs, s_conv, s_ssd, s_gla = _mixer(pas, pbs, params, s_states, bs, ls)
    yp, ys = _token_b(x1p, x1s, mxp, mxs, w_out.astype(f32), _row(g_ffn2[0]),
                      w_ffn2_in.astype(f32), w_ffn2_out.astype(f32), _row(g_final))
    return (yp.reshape(bp, lp, D_MODEL), ys.reshape(bs, ls, D_MODEL),
            p_ssd.reshape(1, bp, N_SSD_HEADS, SSD_HEAD_DIM, SSD_STATE), p_conv[None],
            p_gla.reshape(1, bp, N_GLA_HEADS, GLA_DK, GLA_DV),
            s_ssd.reshape(1, bs, N_SSD_HEADS, SSD_HEAD_DIM, SSD_STATE), s_conv[None],
            s_gla.reshape(1, bs, N_GLA_HEADS, GLA_DK, GLA_DV))
```

```python
import functools

import jax
import jax.numpy as jnp
import numpy as np
from jax import lax
from jax.experimental import pallas as pl
from jax.experimental.pallas import tpu as pltpu

f32 = jnp.float32
bf16 = jnp.bfloat16

D_MODEL = 1024
D_SSD = 512
SSD_HEAD_DIM = 64
N_SSD_HEADS = 8
SSD_STATE = 128
SSD_GROUPS = 2
CONV_W = 4
CONV_DIM = D_SSD + 2 * SSD_GROUPS * SSD_STATE
D_GLA = 512
N_GLA_HEADS = 4
GLA_DV = 128
GLA_DK = 64
GLA_RANK = 16
GLA_GATE_NORMALIZER = 16.0
D_FF = 2816
EPS = 1e-6
D_QK = N_GLA_HEADS * GLA_DK
LOG2E = float(np.log2(np.e))

SUBLANES = 8
LANES = 128
BF16_ROWS = 16
VMEM_LIMIT_BYTES = 60 * 1024 * 1024

CHUNK = 128
TOKEN_TILE = 256
TOKEN_B_TILE = 512

PA_COLS = D_SSD + CONV_DIM + 2 * D_QK + 2 * D_GLA
PB_COLS = LANES
OFF_Z, OFF_XBC, OFF_Q, OFF_K, OFF_V, OFF_G = 0, 512, 1536, 1792, 2048, 2560


def _dot(a, b):
    return jnp.dot(a, b, preferred_element_type=f32)


def _dot_nt(a, b):
    return lax.dot_general(a, b, (((1,), (1,)), ((), ())), preferred_element_type=f32)


def _split3(x):
    hi = x.astype(bf16)
    r1 = x - hi.astype(f32)
    mid = r1.astype(bf16)
    lo = (r1 - mid.astype(f32)).astype(bf16)
    return hi, mid, lo


def _sel_dot(sel3, x):
    return _dot(sel3, jnp.concatenate(_split3(x), axis=0))


def _tiled3(sel):
    return np.concatenate([sel] * 3, axis=1)


def _dot_x3(a, b):
    a_hi = a.astype(bf16)
    a_lo = (a - a_hi.astype(f32)).astype(bf16)
    b_hi = b.astype(bf16)
    b_lo = (b - b_hi.astype(f32)).astype(bf16)
    return _dot(a_hi, b_hi) + _dot(a_hi, b_lo) + _dot(a_lo, b_hi)


def _rms(x, g):
    return x * lax.rsqrt(jnp.mean(x * x, axis=-1, keepdims=True) + EPS) * g


N_LOAD = 16

W_IN_DT = D_SSD + CONV_DIM
W_IN_Q = W_IN_DT + N_SSD_HEADS
W_IN_LR = W_IN_Q + 2 * D_QK + 2 * D_GLA
W_IN_COLS = W_IN_LR + GLA_RANK


def _load_slab(step, slab_ref, dst_ref, regroup=None):
    rows = slab_ref.shape[0]
    assert rows % BF16_ROWS == 0
    val = slab_ref[...]
    if regroup is not None:
        val = regroup(val)
    dst_ref[pl.ds(pl.multiple_of(step * rows, BF16_ROWS), rows), :] = val.astype(bf16)


def _regroup_w_in(w):
    small = jnp.concatenate([w[:, W_IN_DT:W_IN_Q], w[:, W_IN_LR:W_IN_COLS],
                             jnp.zeros((w.shape[0], PB_COLS - N_SSD_HEADS - GLA_RANK), w.dtype)], axis=1)
    return jnp.concatenate([w[:, :W_IN_DT], w[:, W_IN_Q:W_IN_LR], small], axis=1)


ROW_STREAMS = 2


def _token_a_phases(rows, x_ref, x1_ref, pa_ref, pb_ref, g1, wup, wdn, gmix, win):
    x = x_ref[rows, :]
    gu = _dot(_rms(x, g1[...]).astype(bf16), wup[...])
    yield
    act = (jax.nn.silu(gu[:, :D_FF]) * gu[:, D_FF:]).astype(bf16)
    x1 = x + 0.5 * _dot(act, wdn[...])
    yield
    x1_ref[rows, :] = x1
    pr = _dot(_rms(x1, gmix[...]).astype(bf16), win[...])
    yield
    pa_ref[rows, :] = pr[:, :PA_COLS].astype(bf16)
    pb_ref[rows, :] = pr[:, PA_COLS:]


def _token_a_body(n_prompt, xp, xs, g1, wup_f, wdn_f, gmix, win_f, x1p, x1s, pap, pas, pbp, pbs, wup, wdn, win):
    def compute(x_ref, x1_ref, pa_ref, pb_ref):
        part = x_ref.shape[0] // ROW_STREAMS
        _round_robin([_token_a_phases(pl.ds(j * part, part), x_ref, x1_ref, pa_ref, pb_ref, g1, wup, wdn, gmix, win)
                      for j in range(ROW_STREAMS)])

    i = pl.program_id(0)

    @pl.when(i < N_LOAD)
    def _():
        _load_slab(i, wup_f, wup)
        _load_slab(i, wdn_f, wdn)
        _load_slab(i, win_f, win, _regroup_w_in)

    pl.when(jnp.logical_and(i >= N_LOAD, i < N_LOAD + n_prompt))(lambda: compute(xp, x1p, pap, pbp))
    pl.when(i >= N_LOAD + n_prompt)(lambda: compute(xs, x1s, pas, pbs))


def _token_b_phases(rows, x_ref, m_ref, y_ref, wout, g2, wup, wdn, gfin):
    x2 = x_ref[rows, :] + _dot(m_ref[rows, :], wout[...])
    yield
    gu = _dot(_rms(x2, g2[...]).astype(bf16), wup[...])
    yield
    act = (jax.nn.silu(gu[:, :D_FF]) * gu[:, D_FF:]).astype(bf16)
    x3 = x2 + 0.5 * _dot(act, wdn[...])
    yield
    y_ref[rows, :] = _rms(x3, gfin[...])


def _token_b_body(n_prompt, x1p, x1s, mxp, mxs, wout_f, g2, wup_f, wdn_f, gfin, yp, ys, wout, wup, wdn):
    def compute(x_ref, m_ref, y_ref):
        part = x_ref.shape[0] // ROW_STREAMS
        _round_robin([_token_b_phases(pl.ds(j * part, part), x_ref, m_ref, y_ref, wout, g2, wup, wdn, gfin)
                      for j in range(ROW_STREAMS)])

    i = pl.program_id(0)

    @pl.when(i < N_LOAD)
    def _():
        _load_slab(i, wout_f, wout)
        _load_slab(i, wup_f, wup)
        _load_slab(i, wdn_f, wdn)

    pl.when(jnp.logical_and(i >= N_LOAD, i < N_LOAD + n_prompt))(lambda: compute(x1p, mxp, yp))
    pl.when(i >= N_LOAD + n_prompt)(lambda: compute(x1s, mxs, ys))


def _two_group_specs(n_prompt, cols, tile):
    prompt = pl.BlockSpec((tile, cols), lambda i: (jnp.clip(i - N_LOAD, 0, n_prompt - 1), 0))
    sample = pl.BlockSpec((tile, cols), lambda i: (jnp.maximum(i - N_LOAD - n_prompt, 0), 0))
    return prompt, sample


def _whole(shape):
    return pl.BlockSpec(shape, lambda i: (0,) * len(shape), pipeline_mode=pl.Buffered(1))


def _slabs(weight):
    _, rows, cols = weight.shape
    assert rows % N_LOAD == 0
    return pl.BlockSpec((None, rows // N_LOAD, cols), lambda i: (0, jnp.minimum(i, N_LOAD - 1), 0))


def _token_a(xp, xs, g1, wup, wdn, gmix, win):
    tp, ts = xp.shape[0], xs.shape[0]
    tile = TOKEN_TILE
    n_prompt, n_sample = tp // tile, ts // tile
    assert tp % tile == 0 and ts % tile == 0 and win.shape[2] == W_IN_COLS
    xin = _two_group_specs(n_prompt, D_MODEL, tile)
    pa = _two_group_specs(n_prompt, PA_COLS, tile)
    pb = _two_group_specs(n_prompt, PB_COLS, tile)
    return pl.pallas_call(
        functools.partial(_token_a_body, n_prompt),
        grid=(N_LOAD + n_prompt + n_sample,),
        in_specs=[*xin, _whole(g1.shape), _slabs(wup), _slabs(wdn), _whole(gmix.shape), _slabs(win)],
        out_specs=[*xin, *pa, *pb],
        out_shape=[jax.ShapeDtypeStruct((tp, D_MODEL), f32), jax.ShapeDtypeStruct((ts, D_MODEL), f32),
                   jax.ShapeDtypeStruct((tp, PA_COLS), bf16), jax.ShapeDtypeStruct((ts, PA_COLS), bf16),
                   jax.ShapeDtypeStruct((tp, PB_COLS), f32), jax.ShapeDtypeStruct((ts, PB_COLS), f32)],
        scratch_shapes=[pltpu.VMEM(wup.shape[1:], bf16), pltpu.VMEM(wdn.shape[1:], bf16),
                        pltpu.VMEM((D_MODEL, PA_COLS + PB_COLS), bf16)],
        compiler_params=pltpu.CompilerParams(dimension_semantics=("arbitrary",), vmem_limit_bytes=VMEM_LIMIT_BYTES),
        name="token_a",
    )(xp, xs, g1, wup, wdn, gmix, win)


def _token_b(x1p, x1s, mxp, mxs, wout, g2, wup, wdn, gfin):
    tp, ts = x1p.shape[0], x1s.shape[0]
    tile = TOKEN_B_TILE
    assert tp % tile == 0 and ts % tile == 0
    n_prompt, n_sample = tp // tile, ts // tile
    xin = _two_group_specs(n_prompt, D_MODEL, tile)
    return pl.pallas_call(
        functools.partial(_token_b_body, n_prompt),
        grid=(N_LOAD + n_prompt + n_sample,),
        in_specs=[*xin, *xin, _slabs(wout), _whole(g2.shape), _slabs(wup), _slabs(wdn), _whole(gfin.shape)],
        out_specs=[*xin],
        out_shape=[jax.ShapeDtypeStruct((tp, D_MODEL), f32), jax.ShapeDtypeStruct((ts, D_MODEL), f32)],
        scratch_shapes=[pltpu.VMEM(w.shape[1:], bf16) for w in (wout, wup, wdn)],
        compiler_params=pltpu.CompilerParams(dimension_semantics=("arbitrary",), vmem_limit_bytes=VMEM_LIMIT_BYTES),
        name="token_b",
    )(x1p, x1s, mxp, mxs, wout, g2, wup, wdn, gfin)


def _level_halves(seq_rows):
    return [h for h in (1, 2, 4, 8, 16, 32, 64) if 2 * h <= seq_rows]


def _matmul_levels(seq_rows):
    return [h for h in _level_halves(seq_rows) if h < SUBLANES]


def _mixer_consts(seq_rows):
    t = np.arange(CHUNK)
    is_sample = seq_rows < CHUNK
    same = (t[:, None] // seq_rows) == (t[None, :] // seq_rows)
    tri = (same & (t[None, :] <= t[:, None])).astype(np.float32)
    last = ((t[:, None] // seq_rows) * seq_rows + seq_rows - 1 == t[None, :]).astype(np.float32)
    cum_gla, pair_mask = [tri], [np.eye(CHUNK, dtype=np.float32)]
    for h in _level_halves(seq_rows):
        ref_row = (t // (2 * h)) * (2 * h) + h
        if h in _matmul_levels(seq_rows):
            cum_gla.append((ref_row[:, None] == t[None, :]).astype(np.float32) @ tri)
        right = (t % (2 * h)) >= h
        blk = (t[:, None] // (2 * h)) == (t[None, :] // (2 * h))
        pair_mask.append((blk & right[:, None] & ~right[None, :]).astype(np.float32))
    cum_ssd = [tri]
    if is_sample:
        cum_gla.append(last @ tri)
        cum_ssd.append(last @ tri)
    hq = np.arange(D_QK) // GLA_DK
    hv = np.arange(D_GLA) // GLA_DV
    head_of_lane = np.arange(D_SSD) // SSD_HEAD_DIM
    expand = (np.arange(PB_COLS)[:, None] == head_of_lane[None, :]).astype(np.float32)
    consts = [jnp.asarray(_tiled3(np.concatenate(cum_ssd, 0)), bf16),
              jnp.asarray(_tiled3(np.concatenate(cum_gla, 0)), bf16),
              jnp.asarray(np.stack(pair_mask, 0), f32),
              jnp.asarray((hv[:, None] == hq[None, :]).astype(np.float32), f32),
              jnp.asarray(np.concatenate([expand] * 3, 0), bf16)]
    if not is_sample:
        shift = np.zeros(((CONV_W - 1) * CHUNK, 2 * CHUNK), np.float32)
        for k in range(1, CONV_W):
            shift[(k - 1) * CHUNK + t, CHUNK + t - k] = 1.0
        consts.append(jnp.asarray(shift, bf16))
    return consts


PROMPT_STREAMS = 4
SAMPLE_STREAMS = 1
N_SHARED_REFS = 14
EPILOGUE = "epilogue"


def _round_robin(streams):
    live = list(streams)
    waiting = []
    while live:
        for s in list(live):
            try:
                if next(s) == EPILOGUE:
                    live.remove(s)
                    waiting.append(s)
            except StopIteration:
                live.remove(s)
    for s in waiting:
        for _ in s:
            pass


def _mixer_body(seq_rows, *refs):
    is_sample = seq_rows < CHUNK
    n_streams = refs[0].shape[0]
    n_in = 5 if is_sample else 2
    n_shared = N_SHARED_REFS if is_sample else N_SHARED_REFS + 1
    ins, shared = refs[:n_in], refs[n_in:n_in + n_shared]
    outs = refs[n_in + n_shared:n_in + n_shared + 4]
    scratch = refs[n_in + n_shared + 4:]
    is_last = None
    if not is_sample:
        xext, st_ref, stbd_ref = scratch
        c_idx = pl.program_id(1)
        is_last = c_idx == pl.num_programs(1) - 1

        @pl.when(c_idx == 0)
        def _():
            xext[:, 0:CHUNK, :] = jnp.zeros((n_streams, CHUNK, CONV_DIM), bf16)
            st_ref[...] = jnp.zeros(st_ref.shape, f32)
            stbd_ref[...] = jnp.zeros(stbd_ref.shape, f32)

    at = lambda group, j: [r.at[j] for r in group]
    _round_robin([_mixer_phases(seq_rows, at(ins, j), shared, at(outs, j), at(scratch, j), is_last)
                  for j in range(n_streams)])


def _mixer_phases(seq_rows, ins, shared, outs, scratch, is_last):
    is_sample = seq_rows < CHUNK
    n_seq = CHUNK // seq_rows
    halves = _level_halves(seq_rows)
    n_lvl = len(halves)
    if is_sample:
        pa_ref, pb_ref, ssd0_ref, conv0_ref, gla0_ref = ins
        (cbuf,) = scratch
    else:
        pa_ref, pb_ref = ins
        xext, st_ref, stbd_ref = scratch
    (convw_ref, convb_ref, dsk_e_ref, gssd_ref, dtb_c_ref, alog_c_ref, wgk_ref, bgk_ref, ggla_ref,
     cum_ssd_ref, cum_gla_ref, pair_mask_ref, bdmask_ref, expand_ref) = shared[:N_SHARED_REFS]
    if not is_sample:
        shift_ref = shared[N_SHARED_REFS]
    mixed_ref, nconv_ref, nssd_ref, ngla_ref = outs

    C = CHUNK
    rows = lax.broadcasted_iota(jnp.int32, (C, 1), 0)
    lane_qk = lax.shift_right_logical(lax.broadcasted_iota(jnp.int32, (1, D_QK), 1), int(np.log2(GLA_DK)))

    z = pa_ref[:, OFF_Z:OFF_Z + D_SSD].astype(f32)
    q = pa_ref[:, OFF_Q:OFF_Q + D_QK].astype(f32) * (GLA_DK ** -0.5)
    k = pa_ref[:, OFF_K:OFF_K + D_QK].astype(f32)
    v_bf = pa_ref[:, OFF_V:OFF_V + D_GLA]
    g = pa_ref[:, OFF_G:OFF_G + D_GLA].astype(f32)

    conv = jnp.broadcast_to(convb_ref[...], (C, CONV_DIM))
    if is_sample:
        for tap in range(CONV_W - 1):
            cbuf[:, SUBLANES - (CONV_W - 1) + tap, :] = conv0_ref[tap]
        cbuf[:, SUBLANES:SUBLANES + seq_rows, :] = (
            pa_ref[:, OFF_XBC:OFF_XBC + CONV_DIM].astype(f32).reshape(n_seq, seq_rows, CONV_DIM))
        for i in range(CONV_W):
            shift = CONV_W - 1 - i
            win = cbuf[:, SUBLANES - shift:SUBLANES - shift + seq_rows, :].reshape(C, CONV_DIM)
            conv = conv + win * convw_ref[i:i + 1, :]
        for tap in range(CONV_W - 1):
            nconv_ref[tap] = cbuf[:, SUBLANES + seq_rows - (CONV_W - 1) + tap, :]
    else:
        xbc = pa_ref[:, OFF_XBC:OFF_XBC + CONV_DIM]
        xext[C:, :] = xbc
        taps = _dot(shift_ref[...], xext[...])
        for i in range(CONV_W - 1):
            shift = CONV_W - 1 - i
            conv = conv + taps[(shift - 1) * C:shift * C] * convw_ref[i:i + 1, :]
        conv = conv + xbc.astype(f32) * convw_ref[CONV_W - 1:CONV_W, :]
        xext[C - BF16_ROWS:C, :] = xext[2 * C - BF16_ROWS:, :]
    yield
    xc = jax.nn.silu(conv)
    xs = xc[:, :D_SSD]
    bm = xc[:, D_SSD:D_SSD + SSD_GROUPS * SSD_STATE].astype(bf16)
    cm = xc[:, D_SSD + SSD_GROUPS * SSD_STATE:].astype(bf16)

    small = pb_ref[...]
    dtp_c = jax.nn.softplus(small + dtb_c_ref[...])
    cums = _sel_dot(cum_ssd_ref[...], dtp_c * (-LOG2E * jnp.exp(alog_c_ref[...])))
    yield
    acum_c = cums[:C]
    acum_t = acum_c.T
    wide = _dot(jnp.concatenate(_split3(jnp.concatenate([cums, dtp_c], axis=0)), axis=1), expand_ref[...])
    yield
    acum_e, dtp_e = wide[:C], wide[-C:]
    acum_last = wide[C:2 * C] if is_sample else acum_e[C - 1:C, :]
    causal = cum_ssd_ref[0:C, 0:C].astype(f32) > 0
    xdt = xs * dtp_e
    xdt_bf = xdt.astype(bf16)
    lane = lax.broadcasted_iota(jnp.int32, (1, LANES), 1)
    left = lane < SSD_HEAD_DIM
    heads_per_group = N_SSD_HEADS // SSD_GROUPS
    gw = heads_per_group * SSD_HEAD_DIM
    y_parts = []
    for grp in range(SSD_GROUPS):
        cb = _dot_nt(cm[:, grp * SSD_STATE:(grp + 1) * SSD_STATE], bm[:, grp * SSD_STATE:(grp + 1) * SSD_STATE])
        for pair in range(heads_per_group // 2):
            sc = []
            for hh in range(2):
                h = grp * heads_per_group + pair * 2 + hh
                seg = acum_c[:, h:h + 1] - acum_t[h:h + 1, :]
                sc.append((cb * jnp.exp2(jnp.where(causal, seg, -jnp.inf))).astype(bf16))
            lo = (grp * heads_per_group + pair * 2) * SSD_HEAD_DIM
            xp = xdt_bf[:, lo:lo + LANES]
            bd = jnp.concatenate([jnp.where(left, xp, 0), jnp.where(left, 0, xp)], axis=0)
            y_parts.append(_dot(jnp.concatenate(sc, axis=1), bd))
            yield
    y = jnp.concatenate(y_parts, axis=1)

    to_end = jnp.exp2(jnp.minimum(acum_last - acum_e, 0.0))
    xs_end = (xdt * to_end).astype(bf16)
    e_acum = jnp.exp2(acum_e)
    bm_t = bm.T

    def ssd_inter(st):
        st_bf = st.astype(bf16)
        return jnp.concatenate([_dot(cm[:, grp * SSD_STATE:(grp + 1) * SSD_STATE], st_bf[:, grp * gw:(grp + 1) * gw])
                                for grp in range(SSD_GROUPS)], axis=1) * e_acum

    def ssd_update(st, decay_row, xs_rows):
        upd = jnp.concatenate([_dot(bm_t[grp * SSD_STATE:(grp + 1) * SSD_STATE, :], xs_rows[:, grp * gw:(grp + 1) * gw])
                               for grp in range(SSD_GROUPS)], axis=1)
        return st * decay_row + upd

    gk = _dot_x3(small, wgk_ref[...]) + bgk_ref[...]
    log_a = jax.nn.log_sigmoid(gk) * (LOG2E / GLA_GATE_NORMALIZER)
    bsel = _sel_dot(cum_gla_ref[...], log_a)
    yield
    bcum = bsel[:C]
    n_mm = len(_matmul_levels(seq_rows))
    b_last = bsel[(n_mm + 1) * C:] if is_sample else bcum[C - 1:C, :]
    k_bf = k.astype(bf16)

    def b_ref(lvl, half):
        if lvl < n_mm:
            return bsel[(lvl + 1) * C:(lvl + 2) * C]
        return jnp.concatenate([jnp.broadcast_to(bcum[r + half:r + half + 1, :], (2 * half, D_QK))
                                for r in range(0, C, 2 * half)], axis=0)

    def head_rows(x_bf):
        return jnp.concatenate([jnp.where(lane_qk == h, x_bf, jnp.zeros_like(x_bf)) for h in range(N_GLA_HEADS)], axis=0)

    a = _dot_nt(head_rows(q.astype(bf16)), k_bf)
    yield
    m = pair_mask_ref[0]
    att = [a[h * C:(h + 1) * C] * m for h in range(N_GLA_HEADS)]
    for lvl, half in enumerate(halves):
        decay = jnp.exp2(-jnp.abs(bcum - b_ref(lvl, half)))
        in_right = (rows & half) != 0
        u = jnp.where(in_right, q, k) * decay
        u_bf = u.astype(bf16)
        m = pair_mask_ref[lvl + 1]
        if half < SUBLANES:
            a = _dot_nt(head_rows(u_bf), u_bf)
            yield
            for h in range(N_GLA_HEADS):
                att[h] = att[h] + a[h * C:(h + 1) * C] * m
        else:
            starts = range(half, C, 2 * half)
            pick = lambda x: jnp.concatenate([x[r:r + half] for r in starts], axis=0)
            a = _dot_nt(head_rows(pick(u).astype(bf16)), u_bf)
            yield
            m_right = pick(m)
            for h in range(N_GLA_HEADS):
                upd = a[h * (C // 2):(h + 1) * (C // 2)] * m_right
                parts = []
                for i, r in enumerate(starts):
                    parts += [att[h][r - half:r], att[h][r:r + half] + upd[i * half:(i + 1) * half]]
                att[h] = jnp.concatenate(parts, axis=0)
    o = jnp.concatenate([_dot(att[h].astype(bf16), v_bf[:, h * GLA_DV:(h + 1) * GLA_DV])
                         for h in range(N_GLA_HEADS)], axis=1)
    yield

    q_in = (q * jnp.exp2(bcum)).astype(bf16)
    kd = (k * jnp.exp2(jnp.minimum(b_last - bcum, 0.0))).astype(bf16)
    v_t = v_bf.T
    bdmask = bdmask_ref[...]

    def gla_inter(stbd):
        return _dot_nt(q_in, stbd.astype(bf16))

    def gla_update(stbd, decay_row, kd_rows):
        return stbd * decay_row + bdmask * _dot(v_t, kd_rows)

    def gla_heads(stbd):
        acc = jnp.where(lane_qk == 0, stbd[0:GLA_DV, :], 0.0)
        for h in range(1, N_GLA_HEADS):
            acc = acc + jnp.where(lane_qk == h, stbd[h * GLA_DV:(h + 1) * GLA_DV, :], 0.0)
        return acc.T

    if not is_sample:
        st = st_ref[...]
        stbd = stbd_ref[...]
        y = y + ssd_inter(st)
        o = o + gla_inter(stbd)
        yield
        st_ref[...] = ssd_update(st, jnp.exp2(acum_last), xs_end)
        stbd_ref[...] = gla_update(stbd, jnp.exp2(b_last), kd)
        yield
    else:
        seq_of_row = lax.shift_right_logical(rows, int(np.log2(seq_rows)))
        cm_f32 = xc[:, D_SSD + SSD_GROUPS * SSD_STATE:]
        q_in_f32 = q * jnp.exp2(bcum)
        y_rows, o_rows = [], []
        transposed = (ssd0_ref[0].T, gla0_ref[0].T)
        for s in range(n_seq):
            st, g0 = transposed
            if s + 1 < n_seq:
                transposed = (ssd0_ref[s + 1].T, gla0_ref[s + 1].T)
            mine = seq_of_row == s
            r0 = s * seq_rows
            st_bf = st.astype(bf16)
            c_rows = cm_f32[r0:r0 + seq_rows].astype(bf16)
            y_rows.append(jnp.concatenate(
                [_dot(c_rows[:, grp * SSD_STATE:(grp + 1) * SSD_STATE], st_bf[:, grp * gw:(grp + 1) * gw])
                 for grp in range(SSD_GROUPS)], axis=1))
            nssd_ref[s] = ssd_update(st, jnp.exp2(acum_last[r0:r0 + 1, :]), jnp.where(mine, xs_end, 0)).T
            stbd = jnp.concatenate([jnp.where(lane_qk == h, g0, 0.0) for h in range(N_GLA_HEADS)], axis=0)
            o_rows.append(_dot_nt(q_in_f32[r0:r0 + seq_rows].astype(bf16), stbd.astype(bf16)))
            ngla_ref[s] = gla_heads(gla_update(stbd, jnp.exp2(b_last[r0:r0 + 1, :]), jnp.where(mine, kd, 0)))
            yield
        y = y + jnp.concatenate(y_rows, axis=0) * e_acum
        o = o + jnp.concatenate(o_rows, axis=0)

    y = (y + dsk_e_ref[...] * xs) * jax.nn.silu(z)
    gsz = D_SSD // SSD_GROUPS
    y = jnp.concatenate([_rms(y[:, i * gsz:(i + 1) * gsz], gssd_ref[:, i * gsz:(i + 1) * gsz])
                         for i in range(SSD_GROUPS)], axis=1)
    o = jnp.concatenate([_rms(o[:, h * GLA_DV:(h + 1) * GLA_DV], ggla_ref[...]) for h in range(N_GLA_HEADS)], axis=1)
    o = o * jax.nn.silu(g)
    mixed_ref[:, :D_SSD] = y.astype(bf16)
    mixed_ref[:, D_SSD:] = o.astype(bf16)

    if not is_sample:
        yield EPILOGUE

        @pl.when(is_last)
        def _():
            carried = xext[C - BF16_ROWS:C, :].astype(f32)
            nconv_ref[0] = carried[BF16_ROWS - (CONV_W - 1):, :]
            nssd_ref[0] = st_ref[...].T
            ngla_ref[0] = gla_heads(stbd_ref[...])


def _mixer(pa, pb, params, states, n_batch, seq_len):
    is_sample = states is not None
    seq_rows = seq_len if is_sample else CHUNK
    assert CHUNK % seq_rows == 0 and (is_sample or seq_len % CHUNK == 0)
    n_seq = CHUNK // seq_rows
    consts = _mixer_consts(seq_rows)
    ns = SAMPLE_STREAMS if is_sample else PROMPT_STREAMS
    assert n_batch % (ns * n_seq) == 0
    rows_per_stream = n_batch * seq_len // ns
    seqs_per_stream = n_batch // ns

    if is_sample:
        grid = (seqs_per_stream // n_seq,)
        row_map = lambda c: (0, c, 0)
        seq_map = lambda c: (0, c, 0, 0)
        full = lambda shape: pl.BlockSpec(shape, lambda c: (0,) * len(shape))
        sems = ("arbitrary",)
    else:
        n_chunks = seq_len // CHUNK
        grid = (seqs_per_stream, n_chunks)
        row_map = lambda b, c: (0, b * n_chunks + c, 0)
        seq_map = lambda b, c: (0, b, 0, 0)
        full = lambda shape: pl.BlockSpec(shape, lambda b, c: (0,) * len(shape))
        sems = ("arbitrary", "arbitrary")

    def streamed(arr):
        return arr.reshape((ns, arr.shape[0] // ns) + arr.shape[1:])

    taps = CONV_W - 1
    if is_sample:
        conv_spec = pl.BlockSpec((ns, taps, n_seq, CONV_DIM), lambda *g: (0, 0, seq_map(*g)[1], 0))
        conv_shape = (ns, taps, seqs_per_stream, CONV_DIM)
    else:
        conv_spec = pl.BlockSpec((ns, n_seq, taps, CONV_DIM), seq_map)
        conv_shape = (ns, seqs_per_stream, taps, CONV_DIM)
    state_blocks = [(n_seq, D_SSD, SSD_STATE), (n_seq, D_QK, GLA_DV)]
    in_arrays = [streamed(pa), streamed(pb)]
    in_specs = [pl.BlockSpec((ns, CHUNK, PA_COLS), row_map), pl.BlockSpec((ns, CHUNK, PB_COLS), row_map)]
    if is_sample:
        ssd0, conv0, gla0 = states
        conv0_t = jnp.transpose(streamed(conv0), (0, 2, 1, 3))
        in_arrays += [streamed(ssd0), conv0_t, streamed(gla0)]
        in_specs += [pl.BlockSpec((ns,) + state_blocks[0], seq_map), conv_spec,
                     pl.BlockSpec((ns,) + state_blocks[1], seq_map)]
    for arr in list(params) + consts:
        in_arrays.append(arr)
        in_specs.append(full(arr.shape))

    out_shape = [jax.ShapeDtypeStruct((ns, rows_per_stream, D_MODEL), bf16),
                 jax.ShapeDtypeStruct(conv_shape, f32)] + [
        jax.ShapeDtypeStruct((ns, seqs_per_stream) + blk[1:], f32) for blk in state_blocks]
    out_specs = [pl.BlockSpec((ns, CHUNK, D_MODEL), row_map), conv_spec] + [
        pl.BlockSpec((ns,) + blk, seq_map) for blk in state_blocks]
    if is_sample:
        scratch = [pltpu.VMEM((ns, n_seq, SUBLANES + seq_rows, CONV_DIM), f32)]
    else:
        scratch = [pltpu.VMEM((ns, 2 * CHUNK, CONV_DIM), bf16),
                   pltpu.VMEM((ns, SSD_STATE, D_SSD), f32), pltpu.VMEM((ns, D_GLA, D_QK), f32)]
    mixed, nconv, nssd, ngla = pl.pallas_call(
        functools.partial(_mixer_body, seq_rows),
        grid=grid, in_specs=in_specs, out_specs=out_specs, out_shape=out_shape, scratch_shapes=scratch,
        compiler_params=pltpu.CompilerParams(dimension_semantics=sems, vmem_limit_bytes=VMEM_LIMIT_BYTES),
        name="mixer_sample" if is_sample else "mixer_prompt",
    )(*in_arrays)
    merged = lambda arr: arr.reshape((arr.shape[0] * arr.shape[1],) + arr.shape[2:])
    if is_sample:
        nconv = jnp.transpose(nconv, (0, 2, 1, 3))
    return merged(mixed), merged(nconv), merged(nssd), merged(ngla)


def _row(vec):
    return vec.reshape(1, -1).astype(f32)


def _pad_lanes(vec, width):
    return jnp.concatenate([vec.astype(f32), jnp.zeros((width - vec.shape[0],), f32)]).reshape(1, width)


def _mixer_params(conv_w, conv_b, dt_bias, a_log, d_skip, g_ssd_norm, w_gk2, b_gk, g_gla_norm):
    rep = lambda vec: _row(jnp.repeat(vec, SSD_HEAD_DIM))
    wgk = jnp.zeros((PB_COLS, D_QK), f32).at[N_SSD_HEADS:N_SSD_HEADS + GLA_RANK, :].set(w_gk2.astype(f32))
    return [conv_w.astype(f32), _row(conv_b), rep(d_skip), _row(g_ssd_norm),
            _pad_lanes(dt_bias, PB_COLS), _pad_lanes(a_log, PB_COLS), wgk, _row(b_gk), _row(g_gla_norm)]


def kernel(x_prompt, x_sample, state_ssd, state_conv, state_gla, g_ffn1, w_ffn1_in, w_ffn1_out, g_mix, w_in, conv_w, conv_b, dt_bias, a_log, d_skip, g_ssd_norm, w_gk2, b_gk, g_gla_norm, w_out, g_ffn2, w_ffn2_in, w_ffn2_out, g_final):
    bp, lp, _ = x_prompt.shape
    bs, ls, _ = x_sample.shape
    assert w_in.shape[0] == 1, "single-layer step: the final norm is fused into the layer's last kernel"
    xp = x_prompt.reshape(bp * lp, D_MODEL)
    xs = x_sample.reshape(bs * ls, D_MODEL)
    x1p, x1s, pap, pas, pbp, pbs = _token_a(
        xp, xs, _row(g_ffn1[0]), w_ffn1_in.astype(f32), w_ffn1_out.astype(f32), _row(g_mix[0]), w_in.astype(f32))
    params = _mixer_params(conv_w[0], conv_b[0], dt_bias[0], a_log[0], d_skip[0], g_ssd_norm[0],
                           w_gk2[0], b_gk[0], g_gla_norm[0])
    mxp, p_conv, p_ssd, p_gla = _mixer(pap, pbp, params, None, bp, lp)
    s_states = (state_ssd[0].reshape(bs, D_SSD, SSD_STATE), state_conv[0], state_gla[0].reshape(bs, D_QK, GLA_DV))
    mxs, s_conv, s_ssd, s_gla = _mixer(pas, pbs, params, s_states, bs, ls)
    yp, ys = _token_b(x1p, x1s, mxp, mxs, w_out.astype(f32), _row(g_ffn2[0]),
                      w_ffn2_in.astype(f32), w_ffn2_out.astype(f32), _row(g_final))
    return (yp.reshape(bp, lp, D_MODEL), ys.reshape(bs, ls, D_MODEL),
            p_ssd.reshape(1, bp, N_SSD_HEADS, SSD_HEAD_DIM, SSD_STATE), p_conv[None],
            p_gla.reshape(1, bp, N_GLA_HEADS, GLA_DK, GLA_DV),
            s_ssd.reshape(1, bs, N_SSD_HEADS, SSD_HEAD_DIM, SSD_STATE), s_conv[None],
            s_gla.reshape(1, bs, N_GLA_HEADS, GLA_DK, GLA_DV))
```

```python
import functools

import jax
import jax.numpy as jnp
import numpy as np
from jax import lax
from jax.experimental import pallas as pl
from jax.experimental.pallas import tpu as pltpu

f32 = jnp.float32
bf16 = jnp.bfloat16

D_MODEL = 1024
D_SSD = 512
SSD_HEAD_DIM = 64
N_SSD_HEADS = 8
SSD_STATE = 128
SSD_GROUPS = 2
CONV_W = 4
CONV_DIM = D_SSD + 2 * SSD_GROUPS * SSD_STATE
D_GLA = 512
N_GLA_HEADS = 4
GLA_DV = 128
GLA_DK = 64
GLA_RANK = 16
GLA_GATE_NORMALIZER = 16.0
D_FF = 2816
EPS = 1e-6
D_QK = N_GLA_HEADS * GLA_DK
LOG2E = float(np.log2(np.e))

SUBLANES = 8
LANES = 128
BF16_ROWS = 16
VMEM_LIMIT_BYTES = 60 * 1024 * 1024

CHUNK = 128
TOKEN_TILE = 256
TOKEN_B_TILE = 512

PA_COLS = D_SSD + CONV_DIM + 2 * D_QK + 2 * D_GLA
PB_COLS = LANES
OFF_Z, OFF_XBC, OFF_Q, OFF_K, OFF_V, OFF_G = 0, 512, 1536, 1792, 2048, 2560


def _dot(a, b):
    return jnp.dot(a, b, preferred_element_type=f32)


def _dot_nt(a, b):
    return lax.dot_general(a, b, (((1,), (1,)), ((), ())), preferred_element_type=f32)


def _split3(x):
    hi = x.astype(bf16)
    r1 = x - hi.astype(f32)
    mid = r1.astype(bf16)
    lo = (r1 - mid.astype(f32)).astype(bf16)
    return hi, mid, lo


def _sel_dot(sel3, x):
    return _dot(sel3, jnp.concatenate(_split3(x), axis=0))


def _tiled3(sel):
    return np.concatenate([sel] * 3, axis=1)


def _split2(x):
    hi = x.astype(bf16)
    return hi, (x - hi.astype(f32)).astype(bf16)


def _dot_x3(a, b_stack):
    a_hi, a_lo = _split2(a)
    return _dot(jnp.concatenate([a_hi, a_hi, a_lo], axis=1), b_stack)


def _stack_x3(b):
    b_hi, b_lo = _split2(b)
    return jnp.concatenate([b_hi, b_lo, b_hi], axis=0)


def _rms(x, g):
    return x * lax.rsqrt(jnp.mean(x * x, axis=-1, keepdims=True) + EPS) * g


N_LOAD = 16

W_IN_DT = D_SSD + CONV_DIM
W_IN_Q = W_IN_DT + N_SSD_HEADS
W_IN_LR = W_IN_Q + 2 * D_QK + 2 * D_GLA
W_IN_COLS = W_IN_LR + GLA_RANK


def _load_slab(step, slab_ref, dst_ref, regroup=None):
    rows = slab_ref.shape[0]
    assert rows % BF16_ROWS == 0
    val = slab_ref[...]
    if regroup is not None:
        val = regroup(val)
    dst_ref[pl.ds(pl.multiple_of(step * rows, BF16_ROWS), rows), :] = val.astype(bf16)


def _regroup_w_in(w):
    small = jnp.concatenate([w[:, W_IN_DT:W_IN_Q], w[:, W_IN_LR:W_IN_COLS],
                             jnp.zeros((w.shape[0], PB_COLS - N_SSD_HEADS - GLA_RANK), w.dtype)], axis=1)
    return jnp.concatenate([w[:, :W_IN_DT], w[:, W_IN_Q:W_IN_LR], small], axis=1)


ROW_STREAMS = 2


def _token_a_phases(rows, x_ref, x1_ref, pa_ref, pb_ref, g1, wup, wdn, gmix, win):
    x = x_ref[rows, :]
    gu = _dot(_rms(x, g1[...]).astype(bf16), wup[...])
    yield
    act = (jax.nn.silu(gu[:, :D_FF]) * gu[:, D_FF:]).astype(bf16)
    x1 = x + 0.5 * _dot(act, wdn[...])
    yield
    x1_ref[rows, :] = x1
    pr = _dot(_rms(x1, gmix[...]).astype(bf16), win[...])
    yield
    pa_ref[rows, :] = pr[:, :PA_COLS].astype(bf16)
    pb_ref[rows, :] = pr[:, PA_COLS:]


def _token_a_body(n_prompt, xp, xs, g1, wup_f, wdn_f, gmix, win_f, x1p, x1s, pap, pas, pbp, pbs, wup, wdn, win):
    def compute(x_ref, x1_ref, pa_ref, pb_ref):
        part = x_ref.shape[0] // ROW_STREAMS
        _round_robin([_token_a_phases(pl.ds(j * part, part), x_ref, x1_ref, pa_ref, pb_ref, g1, wup, wdn, gmix, win)
                      for j in range(ROW_STREAMS)])

    i = pl.program_id(0)

    @pl.when(i < N_LOAD)
    def _():
        _load_slab(i, wup_f, wup)
        _load_slab(i, wdn_f, wdn)
        _load_slab(i, win_f, win, _regroup_w_in)

    pl.when(jnp.logical_and(i >= N_LOAD, i < N_LOAD + n_prompt))(lambda: compute(xp, x1p, pap, pbp))
    pl.when(i >= N_LOAD + n_prompt)(lambda: compute(xs, x1s, pas, pbs))


def _token_b_phases(rows, x_ref, m_ref, y_ref, wout, g2, wup, wdn, gfin):
    x2 = x_ref[rows, :] + _dot(m_ref[rows, :], wout[...])
    yield
    gu = _dot(_rms(x2, g2[...]).astype(bf16), wup[...])
    yield
    act = (jax.nn.silu(gu[:, :D_FF]) * gu[:, D_FF:]).astype(bf16)
    x3 = x2 + 0.5 * _dot(act, wdn[...])
    yield
    y_ref[rows, :] = _rms(x3, gfin[...])


def _token_b_body(n_prompt, x1p, x1s, mxp, mxs, wout, g2, wup, wdn, gfin, yp, ys):
    def compute(x_ref, m_ref, y_ref):
        part = x_ref.shape[0] // ROW_STREAMS
        _round_robin([_token_b_phases(pl.ds(j * part, part), x_ref, m_ref, y_ref, wout, g2, wup, wdn, gfin)
                      for j in range(ROW_STREAMS)])

    i = pl.program_id(0)
    pl.when(i < n_prompt)(lambda: compute(x1p, mxp, yp))
    pl.when(i >= n_prompt)(lambda: compute(x1s, mxs, ys))


def _two_group_specs(n_prompt, cols, tile, first=0):
    prompt = pl.BlockSpec((tile, cols), lambda i: (jnp.clip(i - first, 0, n_prompt - 1), 0))
    sample = pl.BlockSpec((tile, cols), lambda i: (jnp.maximum(i - first - n_prompt, 0), 0))
    return prompt, sample


def _whole(shape):
    return pl.BlockSpec(shape, lambda i: (0,) * len(shape), pipeline_mode=pl.Buffered(1))


def _slabs(weight):
    _, rows, cols = weight.shape
    assert rows % N_LOAD == 0
    return pl.BlockSpec((None, rows // N_LOAD, cols), lambda i: (0, jnp.minimum(i, N_LOAD - 1), 0))


def _token_a(xp, xs, g1, wup, wdn, gmix, win):
    tp, ts = xp.shape[0], xs.shape[0]
    tile = TOKEN_TILE
    n_prompt, n_sample = tp // tile, ts // tile
    assert tp % tile == 0 and ts % tile == 0 and win.shape[2] == W_IN_COLS
    xin = _two_group_specs(n_prompt, D_MODEL, tile, N_LOAD)
    pa = _two_group_specs(n_prompt, PA_COLS, tile, N_LOAD)
    pb = _two_group_specs(n_prompt, PB_COLS, tile, N_LOAD)
    return pl.pallas_call(
        functools.partial(_token_a_body, n_prompt),
        grid=(N_LOAD + n_prompt + n_sample,),
        in_specs=[*xin, _whole(g1.shape), _slabs(wup), _slabs(wdn), _whole(gmix.shape), _slabs(win)],
        out_specs=[*xin, *pa, *pb],
        out_shape=[jax.ShapeDtypeStruct((tp, D_MODEL), f32), jax.ShapeDtypeStruct((ts, D_MODEL), f32),
                   jax.ShapeDtypeStruct((tp, PA_COLS), bf16), jax.ShapeDtypeStruct((ts, PA_COLS), bf16),
                   jax.ShapeDtypeStruct((tp, PB_COLS), f32), jax.ShapeDtypeStruct((ts, PB_COLS), f32)],
        scratch_shapes=[pltpu.VMEM(wup.shape[1:], bf16), pltpu.VMEM(wdn.shape[1:], bf16),
                        pltpu.VMEM((D_MODEL, PA_COLS + PB_COLS), bf16)],
        compiler_params=pltpu.CompilerParams(dimension_semantics=("arbitrary",), vmem_limit_bytes=VMEM_LIMIT_BYTES),
        name="token_a",
    )(xp, xs, g1, wup, wdn, gmix, win)


def _token_b(x1p, x1s, mxp, mxs, wout, g2, wup, wdn, gfin):
    tp, ts = x1p.shape[0], x1s.shape[0]
    tile = TOKEN_B_TILE
    assert tp % tile == 0 and ts % tile == 0
    n_prompt, n_sample = tp // tile, ts // tile
    xin = _two_group_specs(n_prompt, D_MODEL, tile)
    return pl.pallas_call(
        functools.partial(_token_b_body, n_prompt),
        grid=(n_prompt + n_sample,),
        in_specs=[*xin, *xin, _whole(wout.shape), _whole(g2.shape), _whole(wup.shape), _whole(wdn.shape),
                  _whole(gfin.shape)],
        out_specs=[*xin],
        out_shape=[jax.ShapeDtypeStruct((tp, D_MODEL), f32), jax.ShapeDtypeStruct((ts, D_MODEL), f32)],
        compiler_params=pltpu.CompilerParams(dimension_semantics=("arbitrary",), vmem_limit_bytes=VMEM_LIMIT_BYTES),
        name="token_b",
    )(x1p, x1s, mxp, mxs, wout, g2, wup, wdn, gfin)


def _level_halves(seq_rows):
    return [h for h in (1, 2, 4, 8, 16, 32, 64) if 2 * h <= seq_rows]


def _matmul_levels(seq_rows):
    return [h for h in _level_halves(seq_rows) if h < SUBLANES]


def _mixer_consts(seq_rows):
    t = np.arange(CHUNK)
    is_sample = seq_rows < CHUNK
    same = (t[:, None] // seq_rows) == (t[None, :] // seq_rows)
    tri = (same & (t[None, :] <= t[:, None])).astype(np.float32)
    last = ((t[:, None] // seq_rows) * seq_rows + seq_rows - 1 == t[None, :]).astype(np.float32)
    cum_gla, pair_mask = [tri], [np.eye(CHUNK, dtype=np.float32)]
    for h in _level_halves(seq_rows):
        ref_row = (t // (2 * h)) * (2 * h) + h
        if h in _matmul_levels(seq_rows):
            cum_gla.append((ref_row[:, None] == t[None, :]).astype(np.float32) @ tri)
        right = (t % (2 * h)) >= h
        blk = (t[:, None] // (2 * h)) == (t[None, :] // (2 * h))
        pair_mask.append((blk & right[:, None] & ~right[None, :]).astype(np.float32))
    cum_ssd = [tri]
    if is_sample:
        cum_gla.append(last @ tri)
        cum_ssd.append(last @ tri)
    hq = np.arange(D_QK) // GLA_DK
    hv = np.arange(D_GLA) // GLA_DV
    head_of_lane = np.arange(D_SSD) // SSD_HEAD_DIM
    expand = (np.arange(PB_COLS)[:, None] == head_of_lane[None, :]).astype(np.float32)
    consts = [jnp.asarray(_tiled3(np.concatenate(cum_ssd, 0)), bf16),
              jnp.asarray(_tiled3(np.concatenate(cum_gla, 0)), bf16),
              jnp.asarray(np.stack(pair_mask, 0), f32),
              jnp.asarray((hv[:, None] == hq[None, :]).astype(np.float32), f32),
              jnp.asarray(np.concatenate([expand] * 3, 0), bf16)]
    if not is_sample:
        shift = np.zeros(((CONV_W - 1) * CHUNK, 2 * CHUNK), np.float32)
        for k in range(1, CONV_W):
            shift[(k - 1) * CHUNK + t, CHUNK + t - k] = 1.0
        consts.append(jnp.asarray(shift, bf16))
    return consts


PROMPT_STREAMS = 4
SAMPLE_STREAMS = 1
N_SHARED_REFS = 14
EPILOGUE = "epilogue"


def _round_robin(streams):
    live = list(streams)
    waiting = []
    while live:
        for s in list(live):
            try:
                if next(s) == EPILOGUE:
                    live.remove(s)
                    waiting.append(s)
            except StopIteration:
                live.remove(s)
    for s in waiting:
        for _ in s:
            pass


def _mixer_body(seq_rows, n_cast, *refs):
    is_sample = seq_rows < CHUNK
    n_streams = refs[0].shape[0]
    n_in = 5 if is_sample else 2
    n_shared = N_SHARED_REFS if is_sample else N_SHARED_REFS + 1
    ins, shared = refs[:n_in], refs[n_in:n_in + n_shared]
    cast_in = refs[n_in + n_shared:n_in + n_shared + n_cast]
    outs = refs[n_in + n_shared + n_cast:n_in + n_shared + n_cast + 4]
    cast_out = refs[n_in + n_shared + n_cast + 4:n_in + n_shared + 2 * n_cast + 4]
    scratch = refs[n_in + n_shared + 2 * n_cast + 4:]
    is_last = None
    if not is_sample:
        xext, st_ref, stbd_ref = scratch
        c_idx = pl.program_id(1)
        is_last = c_idx == pl.num_programs(1) - 1

        @pl.when(pl.program_id(0) * pl.num_programs(1) + c_idx < N_LOAD)
        def _():
            for src, dst in zip(cast_in, cast_out):
                dst[...] = src[...].astype(bf16)

        @pl.when(c_idx == 0)
        def _():
            xext[:, 0:CHUNK, :] = jnp.zeros((n_streams, CHUNK, CONV_DIM), bf16)
            st_ref[...] = jnp.zeros(st_ref.shape, f32)
            stbd_ref[...] = jnp.zeros(stbd_ref.shape, f32)

    at = lambda group, j: [r.at[j] for r in group]
    _round_robin([_mixer_phases(seq_rows, at(ins, j), shared, at(outs, j), at(scratch, j), is_last)
                  for j in range(n_streams)])


def _mixer_phases(seq_rows, ins, shared, outs, scratch, is_last):
    is_sample = seq_rows < CHUNK
    n_seq = CHUNK // seq_rows
    halves = _level_halves(seq_rows)
    n_lvl = len(halves)
    if is_sample:
        pa_ref, pb_ref, ssd0_ref, conv0_ref, gla0_ref = ins
        (cbuf,) = scratch
    else:
        pa_ref, pb_ref = ins
        xext, st_ref, stbd_ref = scratch
    (convw_ref, convb_ref, dsk_e_ref, gssd_ref, dtb_c_ref, alog_c_ref, wgk_ref, bgk_ref, ggla_ref,
     cum_ssd_ref, cum_gla_ref, pair_mask_ref, bdmask_ref, expand_ref) = shared[:N_SHARED_REFS]
    if not is_sample:
        shift_ref = shared[N_SHARED_REFS]
    mixed_ref, nconv_ref, nssd_ref, ngla_ref = outs

    C = CHUNK
    rows = lax.broadcasted_iota(jnp.int32, (C, 1), 0)
    lane_qk = lax.shift_right_logical(lax.broadcasted_iota(jnp.int32, (1, D_QK), 1), int(np.log2(GLA_DK)))

    z = pa_ref[:, OFF_Z:OFF_Z + D_SSD].astype(f32)
    q = pa_ref[:, OFF_Q:OFF_Q + D_QK].astype(f32) * (GLA_DK ** -0.5)
    k = pa_ref[:, OFF_K:OFF_K + D_QK].astype(f32)
    v_bf = pa_ref[:, OFF_V:OFF_V + D_GLA]
    g = pa_ref[:, OFF_G:OFF_G + D_GLA].astype(f32)

    conv = jnp.broadcast_to(convb_ref[...], (C, CONV_DIM))
    if is_sample:
        for tap in range(CONV_W - 1):
            cbuf[:, SUBLANES - (CONV_W - 1) + tap, :] = conv0_ref[tap]
        cbuf[:, SUBLANES:SUBLANES + seq_rows, :] = (
            pa_ref[:, OFF_XBC:OFF_XBC + CONV_DIM].astype(f32).reshape(n_seq, seq_rows, CONV_DIM))
        for i in range(CONV_W):
            shift = CONV_W - 1 - i
            win = cbuf[:, SUBLANES - shift:SUBLANES - shift + seq_rows, :].reshape(C, CONV_DIM)
            conv = conv + win * convw_ref[i:i + 1, :]
        for tap in range(CONV_W - 1):
            nconv_ref[tap] = cbuf[:, SUBLANES + seq_rows - (CONV_W - 1) + tap, :]
    else:
        xbc = pa_ref[:, OFF_XBC:OFF_XBC + CONV_DIM]
        xext[C:, :] = xbc
        taps = _dot(shift_ref[...], xext[...])
        for i in range(CONV_W - 1):
            shift = CONV_W - 1 - i
            conv = conv + taps[(shift - 1) * C:shift * C] * convw_ref[i:i + 1, :]
        conv = conv + xbc.astype(f32) * convw_ref[CONV_W - 1:CONV_W, :]
        xext[C - BF16_ROWS:C, :] = xext[2 * C - BF16_ROWS:, :]
    yield
    xc = jax.nn.silu(conv)
    xs = xc[:, :D_SSD]
    bm = xc[:, D_SSD:D_SSD + SSD_GROUPS * SSD_STATE].astype(bf16)
    cm = xc[:, D_SSD + SSD_GROUPS * SSD_STATE:].astype(bf16)

    small = pb_ref[...]
    dtp_c = jax.nn.softplus(small + dtb_c_ref[...])
    cums = _sel_dot(cum_ssd_ref[...], dtp_c * (-LOG2E * jnp.exp(alog_c_ref[...])))
    yield
    acum_c = cums[:C]
    acum_t = acum_c.T
    wide = _dot(jnp.concatenate(_split3(jnp.concatenate([cums, dtp_c], axis=0)), axis=1), expand_ref[...])
    yield
    acum_e, dtp_e = wide[:C], wide[-C:]
    acum_last = wide[C:2 * C] if is_sample else acum_e[C - 1:C, :]
    causal = cum_ssd_ref[0:C, 0:C].astype(f32) > 0
    xdt = xs * dtp_e
    xdt_bf = xdt.astype(bf16)
    lane = lax.broadcasted_iota(jnp.int32, (1, LANES), 1)
    left = lane < SSD_HEAD_DIM
    heads_per_group = N_SSD_HEADS // SSD_GROUPS
    gw = heads_per_group * SSD_HEAD_DIM
    y_parts = []
    for grp in range(SSD_GROUPS):
        cb = _dot_nt(cm[:, grp * SSD_STATE:(grp + 1) * SSD_STATE], bm[:, grp * SSD_STATE:(grp + 1) * SSD_STATE])
        for pair in range(heads_per_group // 2):
            sc = []
            for hh in range(2):
                h = grp * heads_per_group + pair * 2 + hh
                seg = acum_c[:, h:h + 1] - acum_t[h:h + 1, :]
                sc.append((cb * jnp.exp2(jnp.where(causal, seg, -jnp.inf))).astype(bf16))
            lo = (grp * heads_per_group + pair * 2) * SSD_HEAD_DIM
            xp = xdt_bf[:, lo:lo + LANES]
            bd = jnp.concatenate([jnp.where(left, xp, 0), jnp.where(left, 0, xp)], axis=0)
            y_parts.append(_dot(jnp.concatenate(sc, axis=1), bd))
            yield
    y = jnp.concatenate(y_parts, axis=1)

    to_end = jnp.exp2(jnp.minimum(acum_last - acum_e, 0.0))
    xs_end = (xdt * to_end).astype(bf16)
    e_acum = jnp.exp2(acum_e)
    bm_t = bm.T

    def ssd_inter(st):
        st_bf = st.astype(bf16)
        return jnp.concatenate([_dot(cm[:, grp * SSD_STATE:(grp + 1) * SSD_STATE], st_bf[:, grp * gw:(grp + 1) * gw])
                                for grp in range(SSD_GROUPS)], axis=1) * e_acum

    def ssd_update(st, decay_row, xs_rows):
        upd = jnp.concatenate([_dot(bm_t[grp * SSD_STATE:(grp + 1) * SSD_STATE, :], xs_rows[:, grp * gw:(grp + 1) * gw])
                               for grp in range(SSD_GROUPS)], axis=1)
        return st * decay_row + upd

    gk = _dot_x3(small, wgk_ref[...]) + bgk_ref[...]
    log_a = jax.nn.log_sigmoid(gk) * (LOG2E / GLA_GATE_NORMALIZER)
    bsel = _sel_dot(cum_gla_ref[...], log_a)
    yield
    bcum = bsel[:C]
    n_mm = len(_matmul_levels(seq_rows))
    b_last = bsel[(n_mm + 1) * C:] if is_sample else bcum[C - 1:C, :]
    k_bf = k.astype(bf16)

    def b_ref(lvl, half):
        if lvl < n_mm:
            return bsel[(lvl + 1) * C:(lvl + 2) * C]
        return jnp.concatenate([jnp.broadcast_to(bcum[r + half:r + half + 1, :], (2 * half, D_QK))
                                for r in range(0, C, 2 * half)], axis=0)

    def head_rows(x_bf):
        return jnp.concatenate([jnp.where(lane_qk == h, x_bf, jnp.zeros_like(x_bf)) for h in range(N_GLA_HEADS)], axis=0)

    a = _dot_nt(head_rows(q.astype(bf16)), k_bf)
    yield
    m = pair_mask_ref[0]
    att = [a[h * C:(h + 1) * C] * m for h in range(N_GLA_HEADS)]
    for lvl, half in enumerate(halves):
        decay = jnp.exp2(-jnp.abs(bcum - b_ref(lvl, half)))
        in_right = (rows & half) != 0
        u = jnp.where(in_right, q, k) * decay
        u_bf = u.astype(bf16)
        m = pair_mask_ref[lvl + 1]
        if half < SUBLANES:
            a = _dot_nt(head_rows(u_bf), u_bf)
            yield
            for h in range(N_GLA_HEADS):
                att[h] = att[h] + a[h * C:(h + 1) * C] * m
        else:
            starts = range(half, C, 2 * half)
            pick = lambda x: jnp.concatenate([x[r:r + half] for r in starts], axis=0)
            a = _dot_nt(head_rows(pick(u).astype(bf16)), u_bf)
            yield
            m_right = pick(m)
            for h in range(N_GLA_HEADS):
                upd = a[h * (C // 2):(h + 1) * (C // 2)] * m_right
                parts = []
                for i, r in enumerate(starts):
                    parts += [att[h][r - half:r], att[h][r:r + half] + upd[i * half:(i + 1) * half]]
                att[h] = jnp.concatenate(parts, axis=0)
    o = jnp.concatenate([_dot(att[h].astype(bf16), v_bf[:, h * GLA_DV:(h + 1) * GLA_DV])
                         for h in range(N_GLA_HEADS)], axis=1)
    yield

    q_in = (q * jnp.exp2(bcum)).astype(bf16)
    kd = (k * jnp.exp2(jnp.minimum(b_last - bcum, 0.0))).astype(bf16)
    v_t = v_bf.T
    bdmask = bdmask_ref[...]

    def gla_inter(stbd):
        return _dot_nt(q_in, stbd.astype(bf16))

    def gla_update(stbd, decay_row, kd_rows):
        return stbd * decay_row + bdmask * _dot(v_t, kd_rows)

    def gla_heads(stbd):
        acc = jnp.where(lane_qk == 0, stbd[0:GLA_DV, :], 0.0)
        for h in range(1, N_GLA_HEADS):
            acc = acc + jnp.where(lane_qk == h, stbd[h * GLA_DV:(h + 1) * GLA_DV, :], 0.0)
        return acc.T

    if not is_sample:
        st = st_ref[...]
        stbd = stbd_ref[...]
        y = y + ssd_inter(st)
        o = o + gla_inter(stbd)
        yield
        st_ref[...] = ssd_update(st, jnp.exp2(acum_last), xs_end)
        stbd_ref[...] = gla_update(stbd, jnp.exp2(b_last), kd)
        yield
    else:
        seq_of_row = lax.shift_right_logical(rows, int(np.log2(seq_rows)))
        cm_f32 = xc[:, D_SSD + SSD_GROUPS * SSD_STATE:]
        q_in_f32 = q * jnp.exp2(bcum)
        y_rows, o_rows = [], []
        transposed = (ssd0_ref[0].T, gla0_ref[0].T)
        for s in range(n_seq):
            st, g0 = transposed
            if s + 1 < n_seq:
                transposed = (ssd0_ref[s + 1].T, gla0_ref[s + 1].T)
            mine = seq_of_row == s
            r0 = s * seq_rows
            st_bf = st.astype(bf16)
            c_rows = cm_f32[r0:r0 + seq_rows].astype(bf16)
            y_rows.append(jnp.concatenate(
                [_dot(c_rows[:, grp * SSD_STATE:(grp + 1) * SSD_STATE], st_bf[:, grp * gw:(grp + 1) * gw])
                 for grp in range(SSD_GROUPS)], axis=1))
            nssd_ref[s] = ssd_update(st, jnp.exp2(acum_last[r0:r0 + 1, :]), jnp.where(mine, xs_end, 0)).T
            stbd = jnp.concatenate([jnp.where(lane_qk == h, g0, 0.0) for h in range(N_GLA_HEADS)], axis=0)
            o_rows.append(_dot_nt(q_in_f32[r0:r0 + seq_rows].astype(bf16), stbd.astype(bf16)))
            ngla_ref[s] = gla_heads(gla_update(stbd, jnp.exp2(b_last[r0:r0 + 1, :]), jnp.where(mine, kd, 0)))
            yield
        y = y + jnp.concatenate(y_rows, axis=0) * e_acum
        o = o + jnp.concatenate(o_rows, axis=0)

    y = (y + dsk_e_ref[...] * xs) * jax.nn.silu(z)
    gsz = D_SSD // SSD_GROUPS
    y = jnp.concatenate([_rms(y[:, i * gsz:(i + 1) * gsz], gssd_ref[:, i * gsz:(i + 1) * gsz])
                         for i in range(SSD_GROUPS)], axis=1)
    o = jnp.concatenate([_rms(o[:, h * GLA_DV:(h + 1) * GLA_DV], ggla_ref[...]) for h in range(N_GLA_HEADS)], axis=1)
    o = o * jax.nn.silu(g)
    mixed_ref[:, :D_SSD] = y.astype(bf16)
    mixed_ref[:, D_SSD:] = o.astype(bf16)

    if not is_sample:
        yield EPILOGUE

        @pl.when(is_last)
        def _():
            carried = xext[C - BF16_ROWS:C, :].astype(f32)
            nconv_ref[0] = carried[BF16_ROWS - (CONV_W - 1):, :]
            nssd_ref[0] = st_ref[...].T
            ngla_ref[0] = gla_heads(stbd_ref[...])


def _mixer(pa, pb, params, states, n_batch, seq_len, cast_weights=()):
    is_sample = states is not None
    assert not (is_sample and cast_weights)
    seq_rows = seq_len if is_sample else CHUNK
    assert CHUNK % seq_rows == 0 and (is_sample or seq_len % CHUNK == 0)
    n_seq = CHUNK // seq_rows
    consts = _mixer_consts(seq_rows)
    ns = SAMPLE_STREAMS if is_sample else PROMPT_STREAMS
    assert n_batch % (ns * n_seq) == 0
    rows_per_stream = n_batch * seq_len // ns
    seqs_per_stream = n_batch // ns

    if is_sample:
        grid = (seqs_per_stream // n_seq,)
        row_map = lambda c: (0, c, 0)
        seq_map = lambda c: (0, c, 0, 0)
        full = lambda shape: pl.BlockSpec(shape, lambda c: (0,) * len(shape))
        sems = ("arbitrary",)
    else:
        n_chunks = seq_len // CHUNK
        grid = (seqs_per_stream, n_chunks)
        row_map = lambda b, c: (0, b * n_chunks + c, 0)
        seq_map = lambda b, c: (0, b, 0, 0)
        full = lambda shape: pl.BlockSpec(shape, lambda b, c: (0,) * len(shape))
        sems = ("arbitrary", "arbitrary")

    def streamed(arr):
        return arr.reshape((ns, arr.shape[0] // ns) + arr.shape[1:])

    taps = CONV_W - 1
    if is_sample:
        conv_spec = pl.BlockSpec((ns, taps, n_seq, CONV_DIM), lambda *g: (0, 0, seq_map(*g)[1], 0))
        conv_shape = (ns, taps, seqs_per_stream, CONV_DIM)
    else:
        conv_spec = pl.BlockSpec((ns, n_seq, taps, CONV_DIM), seq_map)
        conv_shape = (ns, seqs_per_stream, taps, CONV_DIM)
    state_blocks = [(n_seq, D_SSD, SSD_STATE), (n_seq, D_QK, GLA_DV)]
    in_arrays = [streamed(pa), streamed(pb)]
    in_specs = [pl.BlockSpec((ns, CHUNK, PA_COLS), row_map), pl.BlockSpec((ns, CHUNK, PB_COLS), row_map)]
    if is_sample:
        ssd0, conv0, gla0 = states
        conv0_t = jnp.transpose(streamed(conv0), (0, 2, 1, 3))
        in_arrays += [streamed(ssd0), conv0_t, streamed(gla0)]
        in_specs += [pl.BlockSpec((ns,) + state_blocks[0], seq_map), conv_spec,
                     pl.BlockSpec((ns,) + state_blocks[1], seq_map)]
    for arr in list(params) + consts:
        in_arrays.append(arr)
        in_specs.append(full(arr.shape))

    out_shape = [jax.ShapeDtypeStruct((ns, rows_per_stream, D_MODEL), bf16),
                 jax.ShapeDtypeStruct(conv_shape, f32)] + [
        jax.ShapeDtypeStruct((ns, seqs_per_stream) + blk[1:], f32) for blk in state_blocks]
    out_specs = [pl.BlockSpec((ns, CHUNK, D_MODEL), row_map), conv_spec] + [
        pl.BlockSpec((ns,) + blk, seq_map) for blk in state_blocks]
    for w in cast_weights:
        _, w_rows, w_cols = w.shape
        slab = w_rows // N_LOAD
        assert w_rows % N_LOAD == 0 and slab % BF16_ROWS == 0 and grid[0] * grid[1] >= N_LOAD
        step = lambda b, c: jnp.minimum(b * n_chunks + c, N_LOAD - 1)
        in_arrays.append(w)
        in_specs.append(pl.BlockSpec((None, slab, w_cols), lambda b, c: (0, step(b, c), 0)))
        out_shape.append(jax.ShapeDtypeStruct((w_rows, w_cols), bf16))
        out_specs.append(pl.BlockSpec((slab, w_cols), lambda b, c: (step(b, c), 0)))
    if is_sample:
        scratch = [pltpu.VMEM((ns, n_seq, SUBLANES + seq_rows, CONV_DIM), f32)]
    else:
        scratch = [pltpu.VMEM((ns, 2 * CHUNK, CONV_DIM), bf16),
                   pltpu.VMEM((ns, SSD_STATE, D_SSD), f32), pltpu.VMEM((ns, D_GLA, D_QK), f32)]
    mixed, nconv, nssd, ngla, *cast = pl.pallas_call(
        functools.partial(_mixer_body, seq_rows, len(cast_weights)),
        grid=grid, in_specs=in_specs, out_specs=out_specs, out_shape=out_shape, scratch_shapes=scratch,
        compiler_params=pltpu.CompilerParams(dimension_semantics=sems, vmem_limit_bytes=VMEM_LIMIT_BYTES),
        name="mixer_sample" if is_sample else "mixer_prompt",
    )(*in_arrays)
    merged = lambda arr: arr.reshape((arr.shape[0] * arr.shape[1],) + arr.shape[2:])
    if is_sample:
        nconv = jnp.transpose(nconv, (0, 2, 1, 3))
    return merged(mixed), merged(nconv), merged(nssd), merged(ngla), cast


def _row(vec):
    return vec.reshape(1, -1).astype(f32)


def _pad_lanes(vec, width):
    return jnp.concatenate([vec.astype(f32), jnp.zeros((width - vec.shape[0],), f32)]).reshape(1, width)


def _mixer_params(conv_w, conv_b, dt_bias, a_log, d_skip, g_ssd_norm, w_gk2, b_gk, g_gla_norm):
    rep = lambda vec: _row(jnp.repeat(vec, SSD_HEAD_DIM))
    wgk = jnp.zeros((PB_COLS, D_QK), f32).at[N_SSD_HEADS:N_SSD_HEADS + GLA_RANK, :].set(w_gk2.astype(f32))
    return [conv_w.astype(f32), _row(conv_b), rep(d_skip), _row(g_ssd_norm),
            _pad_lanes(dt_bias, PB_COLS), _pad_lanes(a_log, PB_COLS), _stack_x3(wgk), _row(b_gk), _row(g_gla_norm)]


def kernel(x_prompt, x_sample, state_ssd, state_conv, state_gla, g_ffn1, w_ffn1_in, w_ffn1_out, g_mix, w_in, conv_w, conv_b, dt_bias, a_log, d_skip, g_ssd_norm, w_gk2, b_gk, g_gla_norm, w_out, g_ffn2, w_ffn2_in, w_ffn2_out, g_final):
    bp, lp, _ = x_prompt.shape
    bs, ls, _ = x_sample.shape
    assert w_in.shape[0] == 1, "single-layer step: the final norm is fused into the layer's last kernel"
    xp = x_prompt.reshape(bp * lp, D_MODEL)
    xs = x_sample.reshape(bs * ls, D_MODEL)
    x1p, x1s, pap, pas, pbp, pbs = _token_a(
        xp, xs, _row(g_ffn1[0]), w_ffn1_in.astype(f32), w_ffn1_out.astype(f32), _row(g_mix[0]), w_in.astype(f32))
    params = _mixer_params(conv_w[0], conv_b[0], dt_bias[0], a_log[0], d_skip[0], g_ssd_norm[0],
                           w_gk2[0], b_gk[0], g_gla_norm[0])
    mxp, p_conv, p_ssd, p_gla, (w_out_bf, w_up2_bf, w_dn2_bf) = _mixer(
        pap, pbp, params, None, bp, lp,
        cast_weights=(w_out.astype(f32), w_ffn2_in.astype(f32), w_ffn2_out.astype(f32)))
    s_states = (state_ssd[0].reshape(bs, D_SSD, SSD_STATE), state_conv[0], state_gla[0].reshape(bs, D_QK, GLA_DV))
    mxs, s_conv, s_ssd, s_gla, _ = _mixer(pas, pbs, params, s_states, bs, ls)
    yp, ys = _token_b(x1p, x1s, mxp, mxs, w_out_bf, _row(g_ffn2[0]), w_up2_bf, w_dn2_bf, _row(g_final))
    return (yp.reshape(bp, lp, D_MODEL), ys.reshape(bs, ls, D_MODEL),
            p_ssd.reshape(1, bp, N_SSD_HEADS, SSD_HEAD_DIM, SSD_STATE), p_conv[None],
            p_gla.reshape(1, bp, N_GLA_HEADS, GLA_DK, GLA_DV),
            s_ssd.reshape(1, bs, N_SSD_HEADS, SSD_HEAD_DIM, SSD_STATE), s_conv[None],
            s_gla.reshape(1, bs, N_GLA_HEADS, GLA_DK, GLA_DV))
```

```python
import functools

import jax
import jax.numpy as jnp
import numpy as np
from jax import lax
from jax.experimental import pallas as pl
from jax.experimental.pallas import tpu as pltpu

f32 = jnp.float32
bf16 = jnp.bfloat16

D_MODEL = 1024
D_SSD = 512
SSD_HEAD_DIM = 64
N_SSD_HEADS = 8
SSD_STATE = 128
SSD_GROUPS = 2
CONV_W = 4
CONV_DIM = D_SSD + 2 * SSD_GROUPS * SSD_STATE
D_GLA = 512
N_GLA_HEADS = 4
GLA_DV = 128
GLA_DK = 64
GLA_RANK = 16
GLA_GATE_NORMALIZER = 16.0
D_FF = 2816
EPS = 1e-6
D_QK = N_GLA_HEADS * GLA_DK
LOG2E = float(np.log2(np.e))

SUBLANES = 8
LANES = 128
BF16_ROWS = 16
VMEM_LIMIT_BYTES = 60 * 1024 * 1024

CHUNK = 128
TOKEN_TILE = 256
TOKEN_B_TILE = 512

PA_COLS = D_SSD + CONV_DIM + 2 * D_QK + 2 * D_GLA
PB_COLS = LANES
OFF_Z, OFF_XBC, OFF_Q, OFF_K, OFF_V, OFF_G = 0, 512, 1536, 1792, 2048, 2560


def _dot(a, b):
    return jnp.dot(a, b, preferred_element_type=f32)


def _dot_nt(a, b):
    return lax.dot_general(a, b, (((1,), (1,)), ((), ())), preferred_element_type=f32)


def _split3(x):
    hi = x.astype(bf16)
    r1 = x - hi.astype(f32)
    mid = r1.astype(bf16)
    lo = (r1 - mid.astype(f32)).astype(bf16)
    return hi, mid, lo


def _sel_dot(sel3, x):
    return _dot(sel3, jnp.concatenate(_split3(x), axis=0))


def _tiled3(sel):
    return np.concatenate([sel] * 3, axis=1)


def _split2(x):
    hi = x.astype(bf16)
    return hi, (x - hi.astype(f32)).astype(bf16)


def _dot_x3(a, b_stack):
    a_hi, a_lo = _split2(a)
    return _dot(jnp.concatenate([a_hi, a_hi, a_lo], axis=1), b_stack)


def _stack_x3(b):
    b_hi, b_lo = _split2(b)
    return jnp.concatenate([b_hi, b_lo, b_hi], axis=0)


def _rms(x, g):
    return x * lax.rsqrt(jnp.mean(x * x, axis=-1, keepdims=True) + EPS) * g


N_LOAD = 16

W_IN_DT = D_SSD + CONV_DIM
W_IN_Q = W_IN_DT + N_SSD_HEADS
W_IN_LR = W_IN_Q + 2 * D_QK + 2 * D_GLA
W_IN_COLS = W_IN_LR + GLA_RANK


def _load_slab(step, slab_ref, dst_ref, regroup=None):
    rows = slab_ref.shape[0]
    assert rows % BF16_ROWS == 0
    val = slab_ref[...]
    if regroup is not None:
        val = regroup(val)
    dst_ref[pl.ds(pl.multiple_of(step * rows, BF16_ROWS), rows), :] = val.astype(bf16)


def _regroup_w_in(w):
    small = jnp.concatenate([w[:, W_IN_DT:W_IN_Q], w[:, W_IN_LR:W_IN_COLS],
                             jnp.zeros((w.shape[0], PB_COLS - N_SSD_HEADS - GLA_RANK), w.dtype)], axis=1)
    return jnp.concatenate([w[:, :W_IN_DT], w[:, W_IN_Q:W_IN_LR], small], axis=1)


ROW_STREAMS = 2


def _token_a_phases(rows, x_ref, x1_ref, pa_ref, pb_ref, g1, wup, wdn, gmix, win):
    x = x_ref[rows, :]
    gu = _dot(_rms(x, g1[...]).astype(bf16), wup[...])
    yield
    act = (jax.nn.silu(gu[:, :D_FF]) * gu[:, D_FF:]).astype(bf16)
    x1 = x + 0.5 * _dot(act, wdn[...])
    yield
    x1_ref[rows, :] = x1
    pr = _dot(_rms(x1, gmix[...]).astype(bf16), win[...])
    yield
    pa_ref[rows, :] = pr[:, :PA_COLS].astype(bf16)
    pb_ref[rows, :] = pr[:, PA_COLS:]


def _token_a_body(n_prompt, xp, xs, g1, wup_f, wdn_f, gmix, win_f, x1p, x1s, pap, pas, pbp, pbs, wup, wdn, win):
    def compute(x_ref, x1_ref, pa_ref, pb_ref):
        part = x_ref.shape[0] // ROW_STREAMS
        _round_robin([_token_a_phases(pl.ds(j * part, part), x_ref, x1_ref, pa_ref, pb_ref, g1, wup, wdn, gmix, win)
                      for j in range(ROW_STREAMS)])

    i = pl.program_id(0)

    @pl.when(i < N_LOAD)
    def _():
        _load_slab(i, wup_f, wup)
        _load_slab(i, wdn_f, wdn)
        _load_slab(i, win_f, win, _regroup_w_in)

    pl.when(jnp.logical_and(i >= N_LOAD, i < N_LOAD + n_prompt))(lambda: compute(xp, x1p, pap, pbp))
    pl.when(i >= N_LOAD + n_prompt)(lambda: compute(xs, x1s, pas, pbs))


def _token_b_phases(rows, x_ref, m_ref, y_ref, wout, g2, wup, wdn, gfin):
    x2 = x_ref[rows, :] + _dot(m_ref[rows, :], wout[...])
    yield
    gu = _dot(_rms(x2, g2[...]).astype(bf16), wup[...])
    yield
    act = (jax.nn.silu(gu[:, :D_FF]) * gu[:, D_FF:]).astype(bf16)
    x3 = x2 + 0.5 * _dot(act, wdn[...])
    yield
    y_ref[rows, :] = _rms(x3, gfin[...])


def _token_b_body(n_prompt, x1p, x1s, mxp, mxs, wout, g2, wup, wdn, gfin, yp, ys):
    def compute(x_ref, m_ref, y_ref):
        part = x_ref.shape[0] // ROW_STREAMS
        _round_robin([_token_b_phases(pl.ds(j * part, part), x_ref, m_ref, y_ref, wout, g2, wup, wdn, gfin)
                      for j in range(ROW_STREAMS)])

    i = pl.program_id(0)
    pl.when(i < n_prompt)(lambda: compute(x1p, mxp, yp))
    pl.when(i >= n_prompt)(lambda: compute(x1s, mxs, ys))


def _two_group_specs(n_prompt, cols, tile, first=0):
    prompt = pl.BlockSpec((tile, cols), lambda i: (jnp.clip(i - first, 0, n_prompt - 1), 0))
    sample = pl.BlockSpec((tile, cols), lambda i: (jnp.maximum(i - first - n_prompt, 0), 0))
    return prompt, sample


def _whole(shape):
    return pl.BlockSpec(shape, lambda i: (0,) * len(shape), pipeline_mode=pl.Buffered(1))


def _slabs(weight):
    _, rows, cols = weight.shape
    assert rows % N_LOAD == 0
    return pl.BlockSpec((None, rows // N_LOAD, cols), lambda i: (0, jnp.minimum(i, N_LOAD - 1), 0))


def _token_a(xp, xs, g1, wup, wdn, gmix, win):
    tp, ts = xp.shape[0], xs.shape[0]
    tile = TOKEN_TILE
    n_prompt, n_sample = tp // tile, ts // tile
    assert tp % tile == 0 and ts % tile == 0 and win.shape[2] == W_IN_COLS
    xin = _two_group_specs(n_prompt, D_MODEL, tile, N_LOAD)
    pa = _two_group_specs(n_prompt, PA_COLS, tile, N_LOAD)
    pb = _two_group_specs(n_prompt, PB_COLS, tile, N_LOAD)
    return pl.pallas_call(
        functools.partial(_token_a_body, n_prompt),
        grid=(N_LOAD + n_prompt + n_sample,),
        in_specs=[*xin, _whole(g1.shape), _slabs(wup), _slabs(wdn), _whole(gmix.shape), _slabs(win)],
        out_specs=[*xin, *pa, *pb],
        out_shape=[jax.ShapeDtypeStruct((tp, D_MODEL), f32), jax.ShapeDtypeStruct((ts, D_MODEL), f32),
                   jax.ShapeDtypeStruct((tp, PA_COLS), bf16), jax.ShapeDtypeStruct((ts, PA_COLS), bf16),
                   jax.ShapeDtypeStruct((tp, PB_COLS), f32), jax.ShapeDtypeStruct((ts, PB_COLS), f32)],
        scratch_shapes=[pltpu.VMEM(wup.shape[1:], bf16), pltpu.VMEM(wdn.shape[1:], bf16),
                        pltpu.VMEM((D_MODEL, PA_COLS + PB_COLS), bf16)],
        compiler_params=pltpu.CompilerParams(dimension_semantics=("arbitrary",), vmem_limit_bytes=VMEM_LIMIT_BYTES),
        name="token_a",
    )(xp, xs, g1, wup, wdn, gmix, win)


def _token_b(x1p, x1s, mxp, mxs, wout, g2, wup, wdn, gfin):
    tp, ts = x1p.shape[0], x1s.shape[0]
    tile = TOKEN_B_TILE
    assert tp % tile == 0 and ts % tile == 0
    n_prompt, n_sample = tp // tile, ts // tile
    xin = _two_group_specs(n_prompt, D_MODEL, tile)
    return pl.pallas_call(
        functools.partial(_token_b_body, n_prompt),
        grid=(n_prompt + n_sample,),
        in_specs=[*xin, *xin, _whole(wout.shape), _whole(g2.shape), _whole(wup.shape), _whole(wdn.shape),
                  _whole(gfin.shape)],
        out_specs=[*xin],
        out_shape=[jax.ShapeDtypeStruct((tp, D_MODEL), f32), jax.ShapeDtypeStruct((ts, D_MODEL), f32)],
        compiler_params=pltpu.CompilerParams(dimension_semantics=("arbitrary",), vmem_limit_bytes=VMEM_LIMIT_BYTES),
        name="token_b",
    )(x1p, x1s, mxp, mxs, wout, g2, wup, wdn, gfin)


def _level_halves(seq_rows):
    return [h for h in (1, 2, 4, 8, 16, 32, 64) if 2 * h <= seq_rows]


def _matmul_levels(seq_rows):
    return [h for h in _level_halves(seq_rows) if h < SUBLANES]


def _mixer_consts(seq_rows):
    t = np.arange(CHUNK)
    is_sample = seq_rows < CHUNK
    same = (t[:, None] // seq_rows) == (t[None, :] // seq_rows)
    tri = (same & (t[None, :] <= t[:, None])).astype(np.float32)
    last = ((t[:, None] // seq_rows) * seq_rows + seq_rows - 1 == t[None, :]).astype(np.float32)
    cum_gla, pair_mask = [tri], [np.eye(CHUNK, dtype=np.float32)]
    for h in _level_halves(seq_rows):
        ref_row = (t // (2 * h)) * (2 * h) + h
        if h in _matmul_levels(seq_rows):
            cum_gla.append((ref_row[:, None] == t[None, :]).astype(np.float32) @ tri)
        right = (t % (2 * h)) >= h
        blk = (t[:, None] // (2 * h)) == (t[None, :] // (2 * h))
        pair_mask.append((blk & right[:, None] & ~right[None, :]).astype(np.float32))
    cum_ssd = [tri]
    if is_sample:
        cum_gla.append(last @ tri)
        cum_ssd.append(last @ tri)
    hq = np.arange(D_QK) // GLA_DK
    hv = np.arange(D_GLA) // GLA_DV
    head_of_lane = np.arange(D_SSD) // SSD_HEAD_DIM
    expand = (np.arange(PB_COLS)[:, None] == head_of_lane[None, :]).astype(np.float32)
    consts = [jnp.asarray(_tiled3(np.concatenate(cum_ssd, 0)), bf16),
              jnp.asarray(_tiled3(np.concatenate(cum_gla, 0)), bf16),
              jnp.asarray(np.stack(pair_mask, 0), f32),
              jnp.asarray((hv[:, None] == hq[None, :]).astype(np.float32), f32),
              jnp.asarray(np.concatenate([expand] * 3, 0), bf16)]
    if not is_sample:
        shift = np.zeros(((CONV_W - 1) * CHUNK, 2 * CHUNK), np.float32)
        for k in range(1, CONV_W):
            shift[(k - 1) * CHUNK + t, CHUNK + t - k] = 1.0
        consts.append(jnp.asarray(shift, bf16))
    return consts


PROMPT_STREAMS = 8
SAMPLE_STREAMS = 1
N_SHARED_REFS = 14
EPILOGUE = "epilogue"


def _round_robin(streams):
    live = list(streams)
    waiting = []
    while live:
        for s in list(live):
            try:
                if next(s) == EPILOGUE:
                    live.remove(s)
                    waiting.append(s)
            except StopIteration:
                live.remove(s)
    for s in waiting:
        for _ in s:
            pass


def _mixer_body(seq_rows, n_cast, *refs):
    is_sample = seq_rows < CHUNK
    n_streams = refs[0].shape[0]
    n_in = 5 if is_sample else 2
    n_shared = N_SHARED_REFS if is_sample else N_SHARED_REFS + 1
    ins, shared = refs[:n_in], refs[n_in:n_in + n_shared]
    cast_in = refs[n_in + n_shared:n_in + n_shared + n_cast]
    outs = refs[n_in + n_shared + n_cast:n_in + n_shared + n_cast + 4]
    cast_out = refs[n_in + n_shared + n_cast + 4:n_in + n_shared + 2 * n_cast + 4]
    scratch = refs[n_in + n_shared + 2 * n_cast + 4:]
    is_last = None
    if not is_sample:
        xext, st_ref, stbd_ref = scratch
        c_idx = pl.program_id(1)
        is_last = c_idx == pl.num_programs(1) - 1

        @pl.when(pl.program_id(0) * pl.num_programs(1) + c_idx < N_LOAD)
        def _():
            for src, dst in zip(cast_in, cast_out):
                dst[...] = src[...].astype(bf16)

        @pl.when(c_idx == 0)
        def _():
            xext[:, 0:CHUNK, :] = jnp.zeros((n_streams, CHUNK, CONV_DIM), bf16)
            st_ref[...] = jnp.zeros(st_ref.shape, f32)
            stbd_ref[...] = jnp.zeros(stbd_ref.shape, f32)

    at = lambda group, j: [r.at[j] for r in group]
    _round_robin([_mixer_phases(seq_rows, at(ins, j), shared, at(outs, j), at(scratch, j), is_last)
                  for j in range(n_streams)])


def _mixer_phases(seq_rows, ins, shared, outs, scratch, is_last):
    is_sample = seq_rows < CHUNK
    n_seq = CHUNK // seq_rows
    halves = _level_halves(seq_rows)
    n_lvl = len(halves)
    if is_sample:
        pa_ref, pb_ref, ssd0_ref, conv0_ref, gla0_ref = ins
        (cbuf,) = scratch
    else:
        pa_ref, pb_ref = ins
        xext, st_ref, stbd_ref = scratch
    (convw_ref, convb_ref, dsk_e_ref, gssd_ref, dtb_c_ref, alog_c_ref, wgk_ref, bgk_ref, ggla_ref,
     cum_ssd_ref, cum_gla_ref, pair_mask_ref, bdmask_ref, expand_ref) = shared[:N_SHARED_REFS]
    if not is_sample:
        shift_ref = shared[N_SHARED_REFS]
    mixed_ref, nconv_ref, nssd_ref, ngla_ref = outs

    C = CHUNK
    rows = lax.broadcasted_iota(jnp.int32, (C, 1), 0)
    lane_qk = lax.shift_right_logical(lax.broadcasted_iota(jnp.int32, (1, D_QK), 1), int(np.log2(GLA_DK)))


    conv = jnp.broadcast_to(convb_ref[...], (C, CONV_DIM))
    if is_sample:
        for tap in range(CONV_W - 1):
            cbuf[:, SUBLANES - (CONV_W - 1) + tap, :] = conv0_ref[tap]
        cbuf[:, SUBLANES:SUBLANES + seq_rows, :] = (
            pa_ref[:, OFF_XBC:OFF_XBC + CONV_DIM].astype(f32).reshape(n_seq, seq_rows, CONV_DIM))
        for i in range(CONV_W):
            shift = CONV_W - 1 - i
            win = cbuf[:, SUBLANES - shift:SUBLANES - shift + seq_rows, :].reshape(C, CONV_DIM)
            conv = conv + win * convw_ref[i:i + 1, :]
        for tap in range(CONV_W - 1):
            nconv_ref[tap] = cbuf[:, SUBLANES + seq_rows - (CONV_W - 1) + tap, :]
    else:
        xbc = pa_ref[:, OFF_XBC:OFF_XBC + CONV_DIM]
        xext[C:, :] = xbc
        taps = _dot(shift_ref[...], xext[...])
        for i in range(CONV_W - 1):
            shift = CONV_W - 1 - i
            conv = conv + taps[(shift - 1) * C:shift * C] * convw_ref[i:i + 1, :]
        conv = conv + xbc.astype(f32) * convw_ref[CONV_W - 1:CONV_W, :]
        xext[C - BF16_ROWS:C, :] = xext[2 * C - BF16_ROWS:, :]
    yield
    xc = jax.nn.silu(conv)
    xs = xc[:, :D_SSD]
    bm = xc[:, D_SSD:D_SSD + SSD_GROUPS * SSD_STATE].astype(bf16)
    cm = xc[:, D_SSD + SSD_GROUPS * SSD_STATE:].astype(bf16)

    small = pb_ref[...]
    dtp_c = jax.nn.softplus(small + dtb_c_ref[...])
    cums = _sel_dot(cum_ssd_ref[...], dtp_c * (-LOG2E * jnp.exp(alog_c_ref[...])))
    yield
    acum_c = cums[:C]
    acum_t = acum_c.T
    wide = _dot(jnp.concatenate(_split3(jnp.concatenate([cums, dtp_c], axis=0)), axis=1), expand_ref[...])
    yield
    acum_e, dtp_e = wide[:C], wide[-C:]
    acum_last = wide[C:2 * C] if is_sample else acum_e[C - 1:C, :]
    causal = cum_ssd_ref[0:C, 0:C].astype(f32) > 0
    xdt = xs * dtp_e
    xdt_bf = xdt.astype(bf16)
    lane = lax.broadcasted_iota(jnp.int32, (1, LANES), 1)
    left = lane < SSD_HEAD_DIM
    heads_per_group = N_SSD_HEADS // SSD_GROUPS
    gw = heads_per_group * SSD_HEAD_DIM
    y_parts = []
    for grp in range(SSD_GROUPS):
        cb = _dot_nt(cm[:, grp * SSD_STATE:(grp + 1) * SSD_STATE], bm[:, grp * SSD_STATE:(grp + 1) * SSD_STATE])
        for pair in range(heads_per_group // 2):
            sc = []
            for hh in range(2):
                h = grp * heads_per_group + pair * 2 + hh
                seg = acum_c[:, h:h + 1] - acum_t[h:h + 1, :]
                sc.append((cb * jnp.exp2(jnp.where(causal, seg, -jnp.inf))).astype(bf16))
            lo = (grp * heads_per_group + pair * 2) * SSD_HEAD_DIM
            xp = xdt_bf[:, lo:lo + LANES]
            bd = jnp.concatenate([jnp.where(left, xp, 0), jnp.where(left, 0, xp)], axis=0)
            y_parts.append(_dot(jnp.concatenate(sc, axis=1), bd))
            yield
    y = jnp.concatenate(y_parts, axis=1)

    to_end = jnp.exp2(jnp.minimum(acum_last - acum_e, 0.0))
    xs_end = (xdt * to_end).astype(bf16)
    e_acum = jnp.exp2(acum_e)
    bm_t = bm.T

    def ssd_inter(st):
        st_bf = st.astype(bf16)
        return jnp.concatenate([_dot(cm[:, grp * SSD_STATE:(grp + 1) * SSD_STATE], st_bf[:, grp * gw:(grp + 1) * gw])
                                for grp in range(SSD_GROUPS)], axis=1) * e_acum

    def ssd_update(st, decay_row, xs_rows):
        upd = jnp.concatenate([_dot(bm_t[grp * SSD_STATE:(grp + 1) * SSD_STATE, :], xs_rows[:, grp * gw:(grp + 1) * gw])
                               for grp in range(SSD_GROUPS)], axis=1)
        return st * decay_row + upd

    q = pa_ref[:, OFF_Q:OFF_Q + D_QK].astype(f32) * (GLA_DK ** -0.5)
    k = pa_ref[:, OFF_K:OFF_K + D_QK].astype(f32)
    v_bf = pa_ref[:, OFF_V:OFF_V + D_GLA]
    gk = _dot_x3(small, wgk_ref[...]) + bgk_ref[...]
    log_a = jax.nn.log_sigmoid(gk) * (LOG2E / GLA_GATE_NORMALIZER)
    bsel = _sel_dot(cum_gla_ref[...], log_a)
    yield
    bcum = bsel[:C]
    n_mm = len(_matmul_levels(seq_rows))
    b_last = bsel[(n_mm + 1) * C:] if is_sample else bcum[C - 1:C, :]
    k_bf = k.astype(bf16)

    def b_ref(lvl, half):
        if lvl < n_mm:
            return bsel[(lvl + 1) * C:(lvl + 2) * C]
        return jnp.concatenate([jnp.broadcast_to(bcum[r + half:r + half + 1, :], (2 * half, D_QK))
                                for r in range(0, C, 2 * half)], axis=0)

    def head_rows(x_bf):
        return jnp.concatenate([jnp.where(lane_qk == h, x_bf, jnp.zeros_like(x_bf)) for h in range(N_GLA_HEADS)], axis=0)

    a = _dot_nt(head_rows(q.astype(bf16)), k_bf)
    yield
    m = pair_mask_ref[0]
    att = [a[h * C:(h + 1) * C] * m for h in range(N_GLA_HEADS)]
    for lvl, half in enumerate(halves):
        decay = jnp.exp2(-jnp.abs(bcum - b_ref(lvl, half)))
        in_right = (rows & half) != 0
        u = jnp.where(in_right, q, k) * decay
        u_bf = u.astype(bf16)
        m = pair_mask_ref[lvl + 1]
        if half < SUBLANES:
            a = _dot_nt(head_rows(u_bf), u_bf)
            yield
            for h in range(N_GLA_HEADS):
                att[h] = att[h] + a[h * C:(h + 1) * C] * m
        else:
            starts = range(half, C, 2 * half)
            pick = lambda x: jnp.concatenate([x[r:r + half] for r in starts], axis=0)
            a = _dot_nt(head_rows(pick(u).astype(bf16)), u_bf)
            yield
            m_right = pick(m)
            for h in range(N_GLA_HEADS):
                upd = a[h * (C // 2):(h + 1) * (C // 2)] * m_right
                parts = []
                for i, r in enumerate(starts):
                    parts += [att[h][r - half:r], att[h][r:r + half] + upd[i * half:(i + 1) * half]]
                att[h] = jnp.concatenate(parts, axis=0)
    o = jnp.concatenate([_dot(att[h].astype(bf16), v_bf[:, h * GLA_DV:(h + 1) * GLA_DV])
                         for h in range(N_GLA_HEADS)], axis=1)
    yield

    q_in = (q * jnp.exp2(bcum)).astype(bf16)
    kd = (k * jnp.exp2(jnp.minimum(b_last - bcum, 0.0))).astype(bf16)
    v_t = v_bf.T
    bdmask = bdmask_ref[...]

    def gla_inter(stbd):
        return _dot_nt(q_in, stbd.astype(bf16))

    def gla_update(stbd, decay_row, kd_rows):
        return stbd * decay_row + bdmask * _dot(v_t, kd_rows)

    def gla_heads(stbd):
        acc = jnp.where(lane_qk == 0, stbd[0:GLA_DV, :], 0.0)
        for h in range(1, N_GLA_HEADS):
            acc = acc + jnp.where(lane_qk == h, stbd[h * GLA_DV:(h + 1) * GLA_DV, :], 0.0)
        return acc.T

    if not is_sample:
        st = st_ref[...]
        stbd = stbd_ref[...]
        y = y + ssd_inter(st)
        o = o + gla_inter(stbd)
        yield
        st_ref[...] = ssd_update(st, jnp.exp2(acum_last), xs_end)
        stbd_ref[...] = gla_update(stbd, jnp.exp2(b_last), kd)
        yield
    else:
        seq_of_row = lax.shift_right_logical(rows, int(np.log2(seq_rows)))
        cm_f32 = xc[:, D_SSD + SSD_GROUPS * SSD_STATE:]
        q_in_f32 = q * jnp.exp2(bcum)
        y_rows, o_rows = [], []
        transposed = (ssd0_ref[0].T, gla0_ref[0].T)
        for s in range(n_seq):
            st, g0 = transposed
            if s + 1 < n_seq:
                transposed = (ssd0_ref[s + 1].T, gla0_ref[s + 1].T)
            mine = seq_of_row == s
            r0 = s * seq_rows
            st_bf = st.astype(bf16)
            c_rows = cm_f32[r0:r0 + seq_rows].astype(bf16)
            y_rows.append(jnp.concatenate(
                [_dot(c_rows[:, grp * SSD_STATE:(grp + 1) * SSD_STATE], st_bf[:, grp * gw:(grp + 1) * gw])
                 for grp in range(SSD_GROUPS)], axis=1))
            nssd_ref[s] = ssd_update(st, jnp.exp2(acum_last[r0:r0 + 1, :]), jnp.where(mine, xs_end, 0)).T
            stbd = jnp.concatenate([jnp.where(lane_qk == h, g0, 0.0) for h in range(N_GLA_HEADS)], axis=0)
            o_rows.append(_dot_nt(q_in_f32[r0:r0 + seq_rows].astype(bf16), stbd.astype(bf16)))
            ngla_ref[s] = gla_heads(gla_update(stbd, jnp.exp2(b_last[r0:r0 + 1, :]), jnp.where(mine, kd, 0)))
            yield
        y = y + jnp.concatenate(y_rows, axis=0) * e_acum
        o = o + jnp.concatenate(o_rows, axis=0)

    z = pa_ref[:, OFF_Z:OFF_Z + D_SSD].astype(f32)
    g = pa_ref[:, OFF_G:OFF_G + D_GLA].astype(f32)
    y = (y + dsk_e_ref[...] * xs) * jax.nn.silu(z)
    gsz = D_SSD // SSD_GROUPS
    y = jnp.concatenate([_rms(y[:, i * gsz:(i + 1) * gsz], gssd_ref[:, i * gsz:(i + 1) * gsz])
                         for i in range(SSD_GROUPS)], axis=1)
    o = jnp.concatenate([_rms(o[:, h * GLA_DV:(h + 1) * GLA_DV], ggla_ref[...]) for h in range(N_GLA_HEADS)], axis=1)
    o = o * jax.nn.silu(g)
    mixed_ref[:, :D_SSD] = y.astype(bf16)
    mixed_ref[:, D_SSD:] = o.astype(bf16)

    if not is_sample:
        yield EPILOGUE

        @pl.when(is_last)
        def _():
            carried = xext[C - BF16_ROWS:C, :].astype(f32)
            nconv_ref[0] = carried[BF16_ROWS - (CONV_W - 1):, :]
            nssd_ref[0] = st_ref[...].T
            ngla_ref[0] = gla_heads(stbd_ref[...])


def _mixer(pa, pb, params, states, n_batch, seq_len, cast_weights=()):
    is_sample = states is not None
    assert not (is_sample and cast_weights)
    seq_rows = seq_len if is_sample else CHUNK
    assert CHUNK % seq_rows == 0 and (is_sample or seq_len % CHUNK == 0)
    n_seq = CHUNK // seq_rows
    consts = _mixer_consts(seq_rows)
    ns = SAMPLE_STREAMS if is_sample else PROMPT_STREAMS
    assert n_batch % (ns * n_seq) == 0
    rows_per_stream = n_batch * seq_len // ns
    seqs_per_stream = n_batch // ns

    if is_sample:
        grid = (seqs_per_stream // n_seq,)
        row_map = lambda c: (0, c, 0)
        seq_map = lambda c: (0, c, 0, 0)
        full = lambda shape: pl.BlockSpec(shape, lambda c: (0,) * len(shape))
        sems = ("arbitrary",)
    else:
        n_chunks = seq_len // CHUNK
        grid = (seqs_per_stream, n_chunks)
        row_map = lambda b, c: (0, b * n_chunks + c, 0)
        seq_map = lambda b, c: (0, b, 0, 0)
        full = lambda shape: pl.BlockSpec(shape, lambda b, c: (0,) * len(shape))
        sems = ("arbitrary", "arbitrary")

    def streamed(arr):
        return arr.reshape((ns, arr.shape[0] // ns) + arr.shape[1:])

    taps = CONV_W - 1
    if is_sample:
        conv_spec = pl.BlockSpec((ns, taps, n_seq, CONV_DIM), lambda *g: (0, 0, seq_map(*g)[1], 0))
        conv_shape = (ns, taps, seqs_per_stream, CONV_DIM)
    else:
        conv_spec = pl.BlockSpec((ns, n_seq, taps, CONV_DIM), seq_map)
        conv_shape = (ns, seqs_per_stream, taps, CONV_DIM)
    state_blocks = [(n_seq, D_SSD, SSD_STATE), (n_seq, D_QK, GLA_DV)]
    in_arrays = [streamed(pa), streamed(pb)]
    in_specs = [pl.BlockSpec((ns, CHUNK, PA_COLS), row_map), pl.BlockSpec((ns, CHUNK, PB_COLS), row_map)]
    if is_sample:
        ssd0, conv0, gla0 = states
        conv0_t = jnp.transpose(streamed(conv0), (0, 2, 1, 3))
        in_arrays += [streamed(ssd0), conv0_t, streamed(gla0)]
        in_specs += [pl.BlockSpec((ns,) + state_blocks[0], seq_map), conv_spec,
                     pl.BlockSpec((ns,) + state_blocks[1], seq_map)]
    for arr in list(params) + consts:
        in_arrays.append(arr)
        in_specs.append(full(arr.shape))

    out_shape = [jax.ShapeDtypeStruct((ns, rows_per_stream, D_MODEL), bf16),
                 jax.ShapeDtypeStruct(conv_shape, f32)] + [
        jax.ShapeDtypeStruct((ns, seqs_per_stream) + blk[1:], f32) for blk in state_blocks]
    out_specs = [pl.BlockSpec((ns, CHUNK, D_MODEL), row_map), conv_spec] + [
        pl.BlockSpec((ns,) + blk, seq_map) for blk in state_blocks]
    for w in cast_weights:
        _, w_rows, w_cols = w.shape
        slab = w_rows // N_LOAD
        assert w_rows % N_LOAD == 0 and slab % BF16_ROWS == 0 and grid[0] * grid[1] >= N_LOAD
        step = lambda b, c: jnp.minimum(b * n_chunks + c, N_LOAD - 1)
        in_arrays.append(w)
        in_specs.append(pl.BlockSpec((None, slab, w_cols), lambda b, c: (0, step(b, c), 0)))
        out_shape.append(jax.ShapeDtypeStruct((w_rows, w_cols), bf16))
        out_specs.append(pl.BlockSpec((slab, w_cols), lambda b, c: (step(b, c), 0)))
    if is_sample:
        scratch = [pltpu.VMEM((ns, n_seq, SUBLANES + seq_rows, CONV_DIM), f32)]
    else:
        scratch = [pltpu.VMEM((ns, 2 * CHUNK, CONV_DIM), bf16),
                   pltpu.VMEM((ns, SSD_STATE, D_SSD), f32), pltpu.VMEM((ns, D_GLA, D_QK), f32)]
    mixed, nconv, nssd, ngla, *cast = pl.pallas_call(
        functools.partial(_mixer_body, seq_rows, len(cast_weights)),
        grid=grid, in_specs=in_specs, out_specs=out_specs, out_shape=out_shape, scratch_shapes=scratch,
        compiler_params=pltpu.CompilerParams(dimension_semantics=sems, vmem_limit_bytes=VMEM_LIMIT_BYTES),
        name="mixer_sample" if is_sample else "mixer_prompt",
    )(*in_arrays)
    merged = lambda arr: arr.reshape((arr.shape[0] * arr.shape[1],) + arr.shape[2:])
    if is_sample:
        nconv = jnp.transpose(nconv, (0, 2, 1, 3))
    return merged(mixed), merged(nconv), merged(nssd), merged(ngla), cast


def _row(vec):
    return vec.reshape(1, -1).astype(f32)


def _pad_lanes(vec, width):
    return jnp.concatenate([vec.astype(f32), jnp.zeros((width - vec.shape[0],), f32)]).reshape(1, width)


def _mixer_params(conv_w, conv_b, dt_bias, a_log, d_skip, g_ssd_norm, w_gk2, b_gk, g_gla_norm):
    rep = lambda vec: _row(jnp.repeat(vec, SSD_HEAD_DIM))
    wgk = jnp.zeros((PB_COLS, D_QK), f32).at[N_SSD_HEADS:N_SSD_HEADS + GLA_RANK, :].set(w_gk2.astype(f32))
    return [conv_w.astype(f32), _row(conv_b), rep(d_skip), _row(g_ssd_norm),
            _pad_lanes(dt_bias, PB_COLS), _pad_lanes(a_log, PB_COLS), _stack_x3(wgk), _row(b_gk), _row(g_gla_norm)]


def kernel(x_prompt, x_sample, state_ssd, state_conv, state_gla, g_ffn1, w_ffn1_in, w_ffn1_out, g_mix, w_in, conv_w, conv_b, dt_bias, a_log, d_skip, g_ssd_norm, w_gk2, b_gk, g_gla_norm, w_out, g_ffn2, w_ffn2_in, w_ffn2_out, g_final):
    bp, lp, _ = x_prompt.shape
    bs, ls, _ = x_sample.shape
    assert w_in.shape[0] == 1, "single-layer step: the final norm is fused into the layer's last kernel"
    xp = x_prompt.reshape(bp * lp, D_MODEL)
    xs = x_sample.reshape(bs * ls, D_MODEL)
    x1p, x1s, pap, pas, pbp, pbs = _token_a(
        xp, xs, _row(g_ffn1[0]), w_ffn1_in.astype(f32), w_ffn1_out.astype(f32), _row(g_mix[0]), w_in.astype(f32))
    params = _mixer_params(conv_w[0], conv_b[0], dt_bias[0], a_log[0], d_skip[0], g_ssd_norm[0],
                           w_gk2[0], b_gk[0], g_gla_norm[0])
    mxp, p_conv, p_ssd, p_gla, (w_out_bf, w_up2_bf, w_dn2_bf) = _mixer(
        pap, pbp, params, None, bp, lp,
        cast_weights=(w_out.astype(f32), w_ffn2_in.astype(f32), w_ffn2_out.astype(f32)))
    s_states = (state_ssd[0].reshape(bs, D_SSD, SSD_STATE), state_conv[0], state_gla[0].reshape(bs, D_QK, GLA_DV))
    mxs, s_conv, s_ssd, s_gla, _ = _mixer(pas, pbs, params, s_states, bs, ls)
    yp, ys = _token_b(x1p, x1s, mxp, mxs, w_out_bf, _row(g_ffn2[0]), w_up2_bf, w_dn2_bf, _row(g_final))
    return (yp.reshape(bp, lp, D_MODEL), ys.reshape(bs, ls, D_MODEL),
            p_ssd.reshape(1, bp, N_SSD_HEADS, SSD_HEAD_DIM, SSD_STATE), p_conv[None],
            p_gla.reshape(1, bp, N_GLA_HEADS, GLA_DK, GLA_DV),
            s_ssd.reshape(1, bs, N_SSD_HEADS, SSD_HEAD_DIM, SSD_STATE), s_conv[None],
            s_gla.reshape(1, bs, N_GLA_HEADS, GLA_DK, GLA_DV))
```

```python
import functools

import jax
import jax.numpy as jnp
import numpy as np
from jax import lax
from jax.experimental import pallas as pl
from jax.experimental.pallas import tpu as pltpu

f32 = jnp.float32
bf16 = jnp.bfloat16

D_MODEL = 1024
D_SSD = 512
SSD_HEAD_DIM = 64
N_SSD_HEADS = 8
SSD_STATE = 128
SSD_GROUPS = 2
CONV_W = 4
CONV_DIM = D_SSD + 2 * SSD_GROUPS * SSD_STATE
D_GLA = 512
N_GLA_HEADS = 4
GLA_DV = 128
GLA_DK = 64
GLA_RANK = 16
GLA_GATE_NORMALIZER = 16.0
D_FF = 2816
EPS = 1e-6
D_QK = N_GLA_HEADS * GLA_DK
LOG2E = float(np.log2(np.e))

SUBLANES = 8
LANES = 128
BF16_ROWS = 16
VMEM_LIMIT_BYTES = 60 * 1024 * 1024

CHUNK = 128
TOKEN_TILE = 256
TOKEN_B_TILE = 512

PA_COLS = D_SSD + CONV_DIM + 2 * D_QK + 2 * D_GLA
PB_COLS = LANES
OFF_Z, OFF_XBC, OFF_Q, OFF_K, OFF_V, OFF_G = 0, 512, 1536, 1792, 2048, 2560


def _dot(a, b):
    return jnp.dot(a, b, preferred_element_type=f32)


def _dot_nt(a, b):
    return lax.dot_general(a, b, (((1,), (1,)), ((), ())), preferred_element_type=f32)


def _split3(x):
    hi = x.astype(bf16)
    r1 = x - hi.astype(f32)
    mid = r1.astype(bf16)
    lo = (r1 - mid.astype(f32)).astype(bf16)
    return hi, mid, lo


def _sel_dot(sel3, x):
    return _dot(sel3, jnp.concatenate(_split3(x), axis=0))


def _tiled3(sel):
    return np.concatenate([sel] * 3, axis=1)


def _split2(x):
    hi = x.astype(bf16)
    return hi, (x - hi.astype(f32)).astype(bf16)


def _dot_x3(a, b_stack):
    a_hi, a_lo = _split2(a)
    return _dot(jnp.concatenate([a_hi, a_hi, a_lo], axis=1), b_stack)


def _stack_x3(b):
    b_hi, b_lo = _split2(b)
    return jnp.concatenate([b_hi, b_lo, b_hi], axis=0)


def _rms(x, g):
    return x * lax.rsqrt(jnp.mean(x * x, axis=-1, keepdims=True) + EPS) * g


N_LOAD = 16

W_IN_DT = D_SSD + CONV_DIM
W_IN_Q = W_IN_DT + N_SSD_HEADS
W_IN_LR = W_IN_Q + 2 * D_QK + 2 * D_GLA
W_IN_COLS = W_IN_LR + GLA_RANK


def _load_slab(step, slab_ref, dst_ref, regroup=None):
    rows = slab_ref.shape[0]
    assert rows % BF16_ROWS == 0
    val = slab_ref[...]
    if regroup is not None:
        val = regroup(val)
    dst_ref[pl.ds(pl.multiple_of(step * rows, BF16_ROWS), rows), :] = val.astype(bf16)


def _regroup_w_in(w):
    small = jnp.concatenate([w[:, W_IN_DT:W_IN_Q], w[:, W_IN_LR:W_IN_COLS],
                             jnp.zeros((w.shape[0], PB_COLS - N_SSD_HEADS - GLA_RANK), w.dtype)], axis=1)
    return jnp.concatenate([w[:, :W_IN_DT], w[:, W_IN_Q:W_IN_LR], small], axis=1)


ROW_STREAMS = 2


def _token_a_phases(rows, x_ref, x1_ref, pa_ref, pb_ref, g1, wup, wdn, gmix, win):
    x = x_ref[rows, :]
    gu = _dot(_rms(x, g1[...]).astype(bf16), wup[...])
    yield
    act = (jax.nn.silu(gu[:, :D_FF]) * gu[:, D_FF:]).astype(bf16)
    x1 = x + 0.5 * _dot(act, wdn[...])
    yield
    x1_ref[rows, :] = x1
    pr = _dot(_rms(x1, gmix[...]).astype(bf16), win[...])
    yield
    pa_ref[rows, :] = pr[:, :PA_COLS].astype(bf16)
    pb_ref[rows, :] = pr[:, PA_COLS:]


def _token_a_body(n_prompt, xp, xs, g1, wup_f, wdn_f, gmix, win_f, x1p, x1s, pap, pas, pbp, pbs, wup, wdn, win):
    def compute(x_ref, x1_ref, pa_ref, pb_ref):
        part = x_ref.shape[0] // ROW_STREAMS
        _round_robin([_token_a_phases(pl.ds(j * part, part), x_ref, x1_ref, pa_ref, pb_ref, g1, wup, wdn, gmix, win)
                      for j in range(ROW_STREAMS)])

    i = pl.program_id(0)

    @pl.when(i < N_LOAD)
    def _():
        _load_slab(i, wup_f, wup)
        _load_slab(i, wdn_f, wdn)
        _load_slab(i, win_f, win, _regroup_w_in)

    pl.when(jnp.logical_and(i >= N_LOAD, i < N_LOAD + n_prompt))(lambda: compute(xp, x1p, pap, pbp))
    pl.when(i >= N_LOAD + n_prompt)(lambda: compute(xs, x1s, pas, pbs))


def _token_b_phases(rows, x_ref, m_ref, y_ref, wout, g2, wup, wdn, gfin):
    x2 = x_ref[rows, :] + _dot(m_ref[rows, :], wout[...])
    yield
    gu = _dot(_rms(x2, g2[...]).astype(bf16), wup[...])
    yield
    act = (jax.nn.silu(gu[:, :D_FF]) * gu[:, D_FF:]).astype(bf16)
    x3 = x2 + 0.5 * _dot(act, wdn[...])
    yield
    y_ref[rows, :] = _rms(x3, gfin[...])


def _token_b_body(n_prompt, x1p, x1s, mxp, mxs, wout, g2, wup, wdn, gfin, yp, ys):
    def compute(x_ref, m_ref, y_ref):
        part = x_ref.shape[0] // ROW_STREAMS
        _round_robin([_token_b_phases(pl.ds(j * part, part), x_ref, m_ref, y_ref, wout, g2, wup, wdn, gfin)
                      for j in range(ROW_STREAMS)])

    i = pl.program_id(0)
    pl.when(i < n_prompt)(lambda: compute(x1p, mxp, yp))
    pl.when(i >= n_prompt)(lambda: compute(x1s, mxs, ys))


def _two_group_specs(n_prompt, cols, tile, first=0):
    prompt = pl.BlockSpec((tile, cols), lambda i: (jnp.clip(i - first, 0, n_prompt - 1), 0))
    sample = pl.BlockSpec((tile, cols), lambda i: (jnp.maximum(i - first - n_prompt, 0), 0))
    return prompt, sample


def _whole(shape):
    return pl.BlockSpec(shape, lambda i: (0,) * len(shape), pipeline_mode=pl.Buffered(1))


def _slabs(weight):
    _, rows, cols = weight.shape
    assert rows % N_LOAD == 0
    return pl.BlockSpec((None, rows // N_LOAD, cols), lambda i: (0, jnp.minimum(i, N_LOAD - 1), 0))


def _token_a(xp, xs, g1, wup, wdn, gmix, win):
    tp, ts = xp.shape[0], xs.shape[0]
    tile = TOKEN_TILE
    n_prompt, n_sample = tp // tile, ts // tile
    assert tp % tile == 0 and ts % tile == 0 and win.shape[2] == W_IN_COLS
    xin = _two_group_specs(n_prompt, D_MODEL, tile, N_LOAD)
    pa = _two_group_specs(n_prompt, PA_COLS, tile, N_LOAD)
    pb = _two_group_specs(n_prompt, PB_COLS, tile, N_LOAD)
    return pl.pallas_call(
        functools.partial(_token_a_body, n_prompt),
        grid=(N_LOAD + n_prompt + n_sample,),
        in_specs=[*xin, _whole(g1.shape), _slabs(wup), _slabs(wdn), _whole(gmix.shape), _slabs(win)],
        out_specs=[*xin, *pa, *pb],
        out_shape=[jax.ShapeDtypeStruct((tp, D_MODEL), f32), jax.ShapeDtypeStruct((ts, D_MODEL), f32),
                   jax.ShapeDtypeStruct((tp, PA_COLS), bf16), jax.ShapeDtypeStruct((ts, PA_COLS), bf16),
                   jax.ShapeDtypeStruct((tp, PB_COLS), f32), jax.ShapeDtypeStruct((ts, PB_COLS), f32)],
        scratch_shapes=[pltpu.VMEM(wup.shape[1:], bf16), pltpu.VMEM(wdn.shape[1:], bf16),
                        pltpu.VMEM((D_MODEL, PA_COLS + PB_COLS), bf16)],
        compiler_params=pltpu.CompilerParams(dimension_semantics=("arbitrary",), vmem_limit_bytes=VMEM_LIMIT_BYTES),
        name="token_a",
    )(xp, xs, g1, wup, wdn, gmix, win)


def _token_b(x1p, x1s, mxp, mxs, wout, g2, wup, wdn, gfin):
    tp, ts = x1p.shape[0], x1s.shape[0]
    tile = TOKEN_B_TILE
    assert tp % tile == 0 and ts % tile == 0
    n_prompt, n_sample = tp // tile, ts // tile
    xin = _two_group_specs(n_prompt, D_MODEL, tile)
    return pl.pallas_call(
        functools.partial(_token_b_body, n_prompt),
        grid=(n_prompt + n_sample,),
        in_specs=[*xin, *xin, _whole(wout.shape), _whole(g2.shape), _whole(wup.shape), _whole(wdn.shape),
                  _whole(gfin.shape)],
        out_specs=[*xin],
        out_shape=[jax.ShapeDtypeStruct((tp, D_MODEL), f32), jax.ShapeDtypeStruct((ts, D_MODEL), f32)],
        compiler_params=pltpu.CompilerParams(dimension_semantics=("arbitrary",), vmem_limit_bytes=VMEM_LIMIT_BYTES),
        name="token_b",
    )(x1p, x1s, mxp, mxs, wout, g2, wup, wdn, gfin)


def _level_halves(seq_rows):
    return [h for h in (1, 2, 4, 8, 16, 32, 64) if 2 * h <= seq_rows]


def _matmul_levels(seq_rows):
    return [h for h in _level_halves(seq_rows) if h < SUBLANES]


def _mixer_consts(seq_rows):
    t = np.arange(CHUNK)
    is_sample = seq_rows < CHUNK
    same = (t[:, None] // seq_rows) == (t[None, :] // seq_rows)
    tri = (same & (t[None, :] <= t[:, None])).astype(np.float32)
    last = ((t[:, None] // seq_rows) * seq_rows + seq_rows - 1 == t[None, :]).astype(np.float32)
    cum_gla, pair_mask = [tri], [np.eye(CHUNK, dtype=np.float32)]
    for h in _level_halves(seq_rows):
        ref_row = (t // (2 * h)) * (2 * h) + h
        if h in _matmul_levels(seq_rows):
            cum_gla.append((ref_row[:, None] == t[None, :]).astype(np.float32) @ tri)
        right = (t % (2 * h)) >= h
        blk = (t[:, None] // (2 * h)) == (t[None, :] // (2 * h))
        pair_mask.append((blk & right[:, None] & ~right[None, :]).astype(np.float32))
    cum_ssd = [tri]
    if is_sample:
        cum_gla.append(last @ tri)
        cum_ssd.append(last @ tri)
    hq = np.arange(D_QK) // GLA_DK
    hv = np.arange(D_GLA) // GLA_DV
    head_of_lane = np.arange(D_SSD) // SSD_HEAD_DIM
    expand = (np.arange(PB_COLS)[:, None] == head_of_lane[None, :]).astype(np.float32)
    consts = [jnp.asarray(_tiled3(np.concatenate(cum_ssd, 0)), bf16),
              jnp.asarray(_tiled3(np.concatenate(cum_gla, 0)), bf16),
              jnp.asarray(np.stack(pair_mask, 0), f32),
              jnp.asarray((hv[:, None] == hq[None, :]).astype(np.float32), f32),
              jnp.asarray(np.concatenate([expand] * 3, 0), bf16)]
    if not is_sample:
        shift = np.zeros(((CONV_W - 1) * CHUNK, 2 * CHUNK), np.float32)
        for k in range(1, CONV_W):
            shift[(k - 1) * CHUNK + t, CHUNK + t - k] = 1.0
        consts.append(jnp.asarray(shift, bf16))
    return consts


PROMPT_STREAMS = 8
SAMPLE_STREAMS = 1
N_SHARED_REFS = 14
EPILOGUE = "epilogue"


def _round_robin(streams):
    live = list(streams)
    waiting = []
    while live:
        for s in list(live):
            try:
                if next(s) == EPILOGUE:
                    live.remove(s)
                    waiting.append(s)
            except StopIteration:
                live.remove(s)
    for s in waiting:
        for _ in s:
            pass


def _mixer_body(seq_rows, n_cast, *refs):
    is_sample = seq_rows < CHUNK
    n_streams = refs[0].shape[0]
    n_in = 5 if is_sample else 2
    n_shared = N_SHARED_REFS if is_sample else N_SHARED_REFS + 1
    ins, shared = refs[:n_in], refs[n_in:n_in + n_shared]
    cast_in = refs[n_in + n_shared:n_in + n_shared + n_cast]
    outs = refs[n_in + n_shared + n_cast:n_in + n_shared + n_cast + 4]
    cast_out = refs[n_in + n_shared + n_cast + 4:n_in + n_shared + 2 * n_cast + 4]
    scratch = refs[n_in + n_shared + 2 * n_cast + 4:]
    is_last = None
    if not is_sample:
        xext, st_ref, stbd_ref = scratch
        c_idx = pl.program_id(1)
        is_last = c_idx == pl.num_programs(1) - 1

        @pl.when(pl.program_id(0) * pl.num_programs(1) + c_idx < N_LOAD)
        def _():
            for src, dst in zip(cast_in, cast_out):
                dst[...] = src[...].astype(bf16)

        @pl.when(c_idx == 0)
        def _():
            xext[:, 0:CHUNK, :] = jnp.zeros((n_streams, CHUNK, CONV_DIM), bf16)
            st_ref[...] = jnp.zeros(st_ref.shape, f32)
            stbd_ref[...] = jnp.zeros(stbd_ref.shape, f32)

    at = lambda group, j: [r.at[j] for r in group]
    _round_robin([_mixer_phases(seq_rows, at(ins, j), shared, at(outs, j), at(scratch, j), is_last)
                  for j in range(n_streams)])


def _mixer_phases(seq_rows, ins, shared, outs, scratch, is_last):
    is_sample = seq_rows < CHUNK
    n_seq = CHUNK // seq_rows
    halves = _level_halves(seq_rows)
    n_lvl = len(halves)
    if is_sample:
        pa_ref, pb_ref, ssd0_ref, conv0_ref, gla0_ref = ins
        (cbuf,) = scratch
    else:
        pa_ref, pb_ref = ins
        xext, st_ref, stbd_ref = scratch
    (convw_ref, convb_ref, dsk_e_ref, gssd_ref, dtb_c_ref, alog_c_ref, wgk_ref, bgk_ref, ggla_ref,
     cum_ssd_ref, cum_gla_ref, pair_mask_ref, bdmask_ref, expand_ref) = shared[:N_SHARED_REFS]
    if not is_sample:
        shift_ref = shared[N_SHARED_REFS]
    mixed_ref, nconv_ref, nssd_ref, ngla_ref = outs

    C = CHUNK
    rows = lax.broadcasted_iota(jnp.int32, (C, 1), 0)
    lane_qk = lax.shift_right_logical(lax.broadcasted_iota(jnp.int32, (1, D_QK), 1), int(np.log2(GLA_DK)))


    conv = jnp.broadcast_to(convb_ref[...], (C, CONV_DIM))
    if is_sample:
        for tap in range(CONV_W - 1):
            cbuf[:, SUBLANES - (CONV_W - 1) + tap, :] = conv0_ref[tap]
        cbuf[:, SUBLANES:SUBLANES + seq_rows, :] = (
            pa_ref[:, OFF_XBC:OFF_XBC + CONV_DIM].astype(f32).reshape(n_seq, seq_rows, CONV_DIM))
        for i in range(CONV_W):
            shift = CONV_W - 1 - i
            win = cbuf[:, SUBLANES - shift:SUBLANES - shift + seq_rows, :].reshape(C, CONV_DIM)
            conv = conv + win * convw_ref[i:i + 1, :]
        for tap in range(CONV_W - 1):
            nconv_ref[tap] = cbuf[:, SUBLANES + seq_rows - (CONV_W - 1) + tap, :]
    else:
        xbc = pa_ref[:, OFF_XBC:OFF_XBC + CONV_DIM]
        xext[C:, :] = xbc
        taps = _dot(shift_ref[...], xext[...])
        for i in range(CONV_W - 1):
            shift = CONV_W - 1 - i
            conv = conv + taps[(shift - 1) * C:shift * C] * convw_ref[i:i + 1, :]
        conv = conv + xbc.astype(f32) * convw_ref[CONV_W - 1:CONV_W, :]
        xext[C - BF16_ROWS:C, :] = xext[2 * C - BF16_ROWS:, :]
    yield
    xc = jax.nn.silu(conv)
    xs = xc[:, :D_SSD]
    bm = xc[:, D_SSD:D_SSD + SSD_GROUPS * SSD_STATE].astype(bf16)
    cm = xc[:, D_SSD + SSD_GROUPS * SSD_STATE:].astype(bf16)

    small = pb_ref[...]
    dtp_c = jax.nn.softplus(small + dtb_c_ref[...])
    cums = _sel_dot(cum_ssd_ref[...], dtp_c * (-LOG2E * jnp.exp(alog_c_ref[...])))
    yield
    acum_c = cums[:C]
    acum_t = acum_c.T
    wide = _dot(jnp.concatenate(_split3(jnp.concatenate([cums, dtp_c], axis=0)), axis=1), expand_ref[...])
    yield
    acum_e, dtp_e = wide[:C], wide[-C:]
    acum_last = wide[C:2 * C] if is_sample else acum_e[C - 1:C, :]
    causal = cum_ssd_ref[0:C, 0:C].astype(f32) > 0
    xdt = xs * dtp_e
    xdt_bf = xdt.astype(bf16)
    lane = lax.broadcasted_iota(jnp.int32, (1, LANES), 1)
    left = lane < SSD_HEAD_DIM
    heads_per_group = N_SSD_HEADS // SSD_GROUPS
    gw = heads_per_group * SSD_HEAD_DIM
    y_parts = []
    for grp in range(SSD_GROUPS):
        cb = _dot_nt(cm[:, grp * SSD_STATE:(grp + 1) * SSD_STATE], bm[:, grp * SSD_STATE:(grp + 1) * SSD_STATE])
        for pair in range(heads_per_group // 2):
            sc = []
            for hh in range(2):
                h = grp * heads_per_group + pair * 2 + hh
                seg = acum_c[:, h:h + 1] - acum_t[h:h + 1, :]
                sc.append((cb * jnp.exp2(jnp.where(causal, seg, -jnp.inf))).astype(bf16))
            lo = (grp * heads_per_group + pair * 2) * SSD_HEAD_DIM
            xp = xdt_bf[:, lo:lo + LANES]
            bd = jnp.concatenate([jnp.where(left, xp, 0), jnp.where(left, 0, xp)], axis=0)
            y_parts.append(_dot(jnp.concatenate(sc, axis=1), bd))
            yield
    y = jnp.concatenate(y_parts, axis=1)

    to_end = jnp.exp2(jnp.minimum(acum_last - acum_e, 0.0))
    xs_end = (xdt * to_end).astype(bf16)
    e_acum = jnp.exp2(acum_e)
    bm_t = bm.T

    def ssd_inter(st):
        st_bf = st.astype(bf16)
        return jnp.concatenate([_dot(cm[:, grp * SSD_STATE:(grp + 1) * SSD_STATE], st_bf[:, grp * gw:(grp + 1) * gw])
                                for grp in range(SSD_GROUPS)], axis=1) * e_acum

    def ssd_update(st, decay_row, xs_rows):
        upd = jnp.concatenate([_dot(bm_t[grp * SSD_STATE:(grp + 1) * SSD_STATE, :], xs_rows[:, grp * gw:(grp + 1) * gw])
                               for grp in range(SSD_GROUPS)], axis=1)
        return st * decay_row + upd

    q = pa_ref[:, OFF_Q:OFF_Q + D_QK].astype(f32) * (GLA_DK ** -0.5)
    k = pa_ref[:, OFF_K:OFF_K + D_QK].astype(f32)
    v_bf = pa_ref[:, OFF_V:OFF_V + D_GLA]
    gk = _dot_x3(small, wgk_ref[...]) + bgk_ref[...]
    log_a = jax.nn.log_sigmoid(gk) * (LOG2E / GLA_GATE_NORMALIZER)
    bsel = _sel_dot(cum_gla_ref[...], log_a)
    yield
    bcum = bsel[:C]
    n_mm = len(_matmul_levels(seq_rows))
    b_last = bsel[(n_mm + 1) * C:] if is_sample else bcum[C - 1:C, :]
    k_bf = k.astype(bf16)

    def b_ref(lvl, half):
        if lvl < n_mm:
            return bsel[(lvl + 1) * C:(lvl + 2) * C]
        return jnp.concatenate([jnp.broadcast_to(bcum[r + half:r + half + 1, :], (2 * half, D_QK))
                                for r in range(0, C, 2 * half)], axis=0)

    def head_rows(x_bf):
        return jnp.concatenate([jnp.where(lane_qk == h, x_bf, jnp.zeros_like(x_bf)) for h in range(N_GLA_HEADS)], axis=0)

    a = _dot_nt(head_rows(q.astype(bf16)), k_bf)
    yield
    m = pair_mask_ref[0]
    att = [a[h * C:(h + 1) * C] * m for h in range(N_GLA_HEADS)]
    for lvl, half in enumerate(halves):
        decay = jnp.exp2(-jnp.abs(bcum - b_ref(lvl, half)))
        in_right = (rows & half) != 0
        u = jnp.where(in_right, q, k) * decay
        u_bf = u.astype(bf16)
        m = pair_mask_ref[lvl + 1]
        if half < SUBLANES:
            a = _dot_nt(head_rows(u_bf), u_bf)
            yield
            for h in range(N_GLA_HEADS):
                att[h] = att[h] + a[h * C:(h + 1) * C] * m
        else:
            starts = range(half, C, 2 * half)
            pick = lambda x: jnp.concatenate([x[r:r + half] for r in starts], axis=0)
            a = _dot_nt(head_rows(pick(u).astype(bf16)), u_bf)
            yield
            m_right = pick(m)
            for h in range(N_GLA_HEADS):
                upd = a[h * (C // 2):(h + 1) * (C // 2)] * m_right
                parts = []
                for i, r in enumerate(starts):
                    parts += [att[h][r - half:r], att[h][r:r + half] + upd[i * half:(i + 1) * half]]
                att[h] = jnp.concatenate(parts, axis=0)
    o = jnp.concatenate([_dot(att[h].astype(bf16), v_bf[:, h * GLA_DV:(h + 1) * GLA_DV])
                         for h in range(N_GLA_HEADS)], axis=1)
    yield

    q_in = (q * jnp.exp2(bcum)).astype(bf16)
    kd = (k * jnp.exp2(jnp.minimum(b_last - bcum, 0.0))).astype(bf16)
    v_t = v_bf.T
    bdmask = bdmask_ref[...]

    def gla_inter(stbd):
        return _dot_nt(q_in, stbd.astype(bf16))

    def gla_update(stbd, decay_row, kd_rows):
        return stbd * decay_row + bdmask * _dot(v_t, kd_rows)

    def gla_heads(stbd):
        acc = jnp.where(lane_qk == 0, stbd[0:GLA_DV, :], 0.0)
        for h in range(1, N_GLA_HEADS):
            acc = acc + jnp.where(lane_qk == h, stbd[h * GLA_DV:(h + 1) * GLA_DV, :], 0.0)
        return acc.T

    if not is_sample:
        st = st_ref[...]
        stbd = stbd_ref[...]
        y = y + ssd_inter(st)
        o = o + gla_inter(stbd)
        yield
        st_ref[...] = ssd_update(st, jnp.exp2(acum_last), xs_end)
        stbd_ref[...] = gla_update(stbd, jnp.exp2(b_last), kd)
        yield
    else:
        seq_of_row = lax.shift_right_logical(rows, int(np.log2(seq_rows)))
        bm_f32 = xc[:, D_SSD:D_SSD + SSD_GROUPS * SSD_STATE]
        cm_f32 = xc[:, D_SSD + SSD_GROUPS * SSD_STATE:]
        xs_end_f32 = xdt * to_end
        e_last_c = jnp.exp2(cums[C:])
        q_in_f32 = q * jnp.exp2(bcum)
        y_rows, o_rows = [], []
        g0_next = gla0_ref[0].T
        for s in range(n_seq):
            g0 = g0_next
            if s + 1 < n_seq:
                g0_next = gla0_ref[s + 1].T
            mine = seq_of_row == s
            r0 = s * seq_rows
            own = slice(r0, r0 + seq_rows)
            st = ssd0_ref[s]
            c_rows = cm_f32[own].astype(bf16)
            b_rows = bm_f32[own].astype(bf16)
            x_rows = xs_end_f32[own].astype(bf16)
            y_grp, st_new = [], []
            for grp in range(SSD_GROUPS):
                st_g = st[grp * gw:(grp + 1) * gw]
                y_grp.append(_dot_nt(c_rows[:, grp * SSD_STATE:(grp + 1) * SSD_STATE], st_g.astype(bf16)))
                upd = lax.dot_general(x_rows[:, grp * gw:(grp + 1) * gw], b_rows[:, grp * SSD_STATE:(grp + 1) * SSD_STATE],
                                      (((0,), (0,)), ((), ())), preferred_element_type=f32)
                for hh in range(heads_per_group):
                    h = grp * heads_per_group + hh
                    rows_h = slice(hh * SSD_HEAD_DIM, (hh + 1) * SSD_HEAD_DIM)
                    decay_h = jnp.broadcast_to(e_last_c[r0:r0 + 1, h:h + 1], (SSD_HEAD_DIM, SSD_STATE))
                    st_new.append(st_g[rows_h] * decay_h + upd[rows_h])
            y_rows.append(jnp.concatenate(y_grp, axis=1))
            nssd_ref[s] = jnp.concatenate(st_new, axis=0)
            stbd = jnp.concatenate([jnp.where(lane_qk == h, g0, 0.0) for h in range(N_GLA_HEADS)], axis=0)
            o_rows.append(_dot_nt(q_in_f32[r0:r0 + seq_rows].astype(bf16), stbd.astype(bf16)))
            ngla_ref[s] = gla_heads(gla_update(stbd, jnp.exp2(b_last[r0:r0 + 1, :]), jnp.where(mine, kd, 0)))
            yield
        y = y + jnp.concatenate(y_rows, axis=0) * e_acum
        o = o + jnp.concatenate(o_rows, axis=0)

    z = pa_ref[:, OFF_Z:OFF_Z + D_SSD].astype(f32)
    g = pa_ref[:, OFF_G:OFF_G + D_GLA].astype(f32)
    y = (y + dsk_e_ref[...] * xs) * jax.nn.silu(z)
    gsz = D_SSD // SSD_GROUPS
    y = jnp.concatenate([_rms(y[:, i * gsz:(i + 1) * gsz], gssd_ref[:, i * gsz:(i + 1) * gsz])
                         for i in range(SSD_GROUPS)], axis=1)
    o = jnp.concatenate([_rms(o[:, h * GLA_DV:(h + 1) * GLA_DV], ggla_ref[...]) for h in range(N_GLA_HEADS)], axis=1)
    o = o * jax.nn.silu(g)
    mixed_ref[:, :D_SSD] = y.astype(bf16)
    mixed_ref[:, D_SSD:] = o.astype(bf16)

    if not is_sample:
        yield EPILOGUE

        @pl.when(is_last)
        def _():
            carried = xext[C - BF16_ROWS:C, :].astype(f32)
            nconv_ref[0] = carried[BF16_ROWS - (CONV_W - 1):, :]
            nssd_ref[0] = st_ref[...].T
            ngla_ref[0] = gla_heads(stbd_ref[...])


def _mixer(pa, pb, params, states, n_batch, seq_len, cast_weights=()):
    is_sample = states is not None
    assert not (is_sample and cast_weights)
    seq_rows = seq_len if is_sample else CHUNK
    assert CHUNK % seq_rows == 0 and (is_sample or seq_len % CHUNK == 0)
    n_seq = CHUNK // seq_rows
    consts = _mixer_consts(seq_rows)
    ns = SAMPLE_STREAMS if is_sample else PROMPT_STREAMS
    assert n_batch % (ns * n_seq) == 0
    rows_per_stream = n_batch * seq_len // ns
    seqs_per_stream = n_batch // ns

    if is_sample:
        grid = (seqs_per_stream // n_seq,)
        row_map = lambda c: (0, c, 0)
        seq_map = lambda c: (0, c, 0, 0)
        full = lambda shape: pl.BlockSpec(shape, lambda c: (0,) * len(shape))
        sems = ("arbitrary",)
    else:
        n_chunks = seq_len // CHUNK
        grid = (seqs_per_stream, n_chunks)
        row_map = lambda b, c: (0, b * n_chunks + c, 0)
        seq_map = lambda b, c: (0, b, 0, 0)
        full = lambda shape: pl.BlockSpec(shape, lambda b, c: (0,) * len(shape))
        sems = ("arbitrary", "arbitrary")

    def streamed(arr):
        return arr.reshape((ns, arr.shape[0] // ns) + arr.shape[1:])

    taps = CONV_W - 1
    if is_sample:
        conv_spec = pl.BlockSpec((ns, taps, n_seq, CONV_DIM), lambda *g: (0, 0, seq_map(*g)[1], 0))
        conv_shape = (ns, taps, seqs_per_stream, CONV_DIM)
    else:
        conv_spec = pl.BlockSpec((ns, n_seq, taps, CONV_DIM), seq_map)
        conv_shape = (ns, seqs_per_stream, taps, CONV_DIM)
    state_blocks = [(n_seq, D_SSD, SSD_STATE), (n_seq, D_QK, GLA_DV)]
    in_arrays = [streamed(pa), streamed(pb)]
    in_specs = [pl.BlockSpec((ns, CHUNK, PA_COLS), row_map), pl.BlockSpec((ns, CHUNK, PB_COLS), row_map)]
    if is_sample:
        ssd0, conv0, gla0 = states
        conv0_t = jnp.transpose(streamed(conv0), (0, 2, 1, 3))
        in_arrays += [streamed(ssd0), conv0_t, streamed(gla0)]
        in_specs += [pl.BlockSpec((ns,) + state_blocks[0], seq_map), conv_spec,
                     pl.BlockSpec((ns,) + state_blocks[1], seq_map)]
    for arr in list(params) + consts:
        in_arrays.append(arr)
        in_specs.append(full(arr.shape))

    out_shape = [jax.ShapeDtypeStruct((ns, rows_per_stream, D_MODEL), bf16),
                 jax.ShapeDtypeStruct(conv_shape, f32)] + [
        jax.ShapeDtypeStruct((ns, seqs_per_stream) + blk[1:], f32) for blk in state_blocks]
    out_specs = [pl.BlockSpec((ns, CHUNK, D_MODEL), row_map), conv_spec] + [
        pl.BlockSpec((ns,) + blk, seq_map) for blk in state_blocks]
    for w in cast_weights:
        _, w_rows, w_cols = w.shape
        slab = w_rows // N_LOAD
        assert w_rows % N_LOAD == 0 and slab % BF16_ROWS == 0 and grid[0] * grid[1] >= N_LOAD
        step = lambda b, c: jnp.minimum(b * n_chunks + c, N_LOAD - 1)
        in_arrays.append(w)
        in_specs.append(pl.BlockSpec((None, slab, w_cols), lambda b, c: (0, step(b, c), 0)))
        out_shape.append(jax.ShapeDtypeStruct((w_rows, w_cols), bf16))
        out_specs.append(pl.BlockSpec((slab, w_cols), lambda b, c: (step(b, c), 0)))
    if is_sample:
        scratch = [pltpu.VMEM((ns, n_seq, SUBLANES + seq_rows, CONV_DIM), f32)]
    else:
        scratch = [pltpu.VMEM((ns, 2 * CHUNK, CONV_DIM), bf16),
                   pltpu.VMEM((ns, SSD_STATE, D_SSD), f32), pltpu.VMEM((ns, D_GLA, D_QK), f32)]
    mixed, nconv, nssd, ngla, *cast = pl.pallas_call(
        functools.partial(_mixer_body, seq_rows, len(cast_weights)),
        grid=grid, in_specs=in_specs, out_specs=out_specs, out_shape=out_shape, scratch_shapes=scratch,
        compiler_params=pltpu.CompilerParams(dimension_semantics=sems, vmem_limit_bytes=VMEM_LIMIT_BYTES),
        name="mixer_sample" if is_sample else "mixer_prompt",
    )(*in_arrays)
    merged = lambda arr: arr.reshape((arr.shape[0] * arr.shape[1],) + arr.shape[2:])
    if is_sample:
        nconv = jnp.transpose(nconv, (0, 2, 1, 3))
    return merged(mixed), merged(nconv), merged(nssd), merged(ngla), cast


def _row(vec):
    return vec.reshape(1, -1).astype(f32)


def _pad_lanes(vec, width):
    return jnp.concatenate([vec.astype(f32), jnp.zeros((width - vec.shape[0],), f32)]).reshape(1, width)


def _mixer_params(conv_w, conv_b, dt_bias, a_log, d_skip, g_ssd_norm, w_gk2, b_gk, g_gla_norm):
    rep = lambda vec: _row(jnp.repeat(vec, SSD_HEAD_DIM))
    wgk = jnp.zeros((PB_COLS, D_QK), f32).at[N_SSD_HEADS:N_SSD_HEADS + GLA_RANK, :].set(w_gk2.astype(f32))
    return [conv_w.astype(f32), _row(conv_b), rep(d_skip), _row(g_ssd_norm),
            _pad_lanes(dt_bias, PB_COLS), _pad_lanes(a_log, PB_COLS), _stack_x3(wgk), _row(b_gk), _row(g_gla_norm)]


def kernel(x_prompt, x_sample, state_ssd, state_conv, state_gla, g_ffn1, w_ffn1_in, w_ffn1_out, g_mix, w_in, conv_w, conv_b, dt_bias, a_log, d_skip, g_ssd_norm, w_gk2, b_gk, g_gla_norm, w_out, g_ffn2, w_ffn2_in, w_ffn2_out, g_final):
    bp, lp, _ = x_prompt.shape
    bs, ls, _ = x_sample.shape
    assert w_in.shape[0] == 1, "single-layer step: the final norm is fused into the layer's last kernel"
    xp = x_prompt.reshape(bp * lp, D_MODEL)
    xs = x_sample.reshape(bs * ls, D_MODEL)
    x1p, x1s, pap, pas, pbp, pbs = _token_a(
        xp, xs, _row(g_ffn1[0]), w_ffn1_in.astype(f32), w_ffn1_out.astype(f32), _row(g_mix[0]), w_in.astype(f32))
    params = _mixer_params(conv_w[0], conv_b[0], dt_bias[0], a_log[0], d_skip[0], g_ssd_norm[0],
                           w_gk2[0], b_gk[0], g_gla_norm[0])
    mxp, p_conv, p_ssd, p_gla, (w_out_bf, w_up2_bf, w_dn2_bf) = _mixer(
        pap, pbp, params, None, bp, lp,
        cast_weights=(w_out.astype(f32), w_ffn2_in.astype(f32), w_ffn2_out.astype(f32)))
    s_states = (state_ssd[0].reshape(bs, D_SSD, SSD_STATE), state_conv[0], state_gla[0].reshape(bs, D_QK, GLA_DV))
    mxs, s_conv, s_ssd, s_gla, _ = _mixer(pas, pbs, params, s_states, bs, ls)
    yp, ys = _token_b(x1p, x1s, mxp, mxs, w_out_bf, _row(g_ffn2[0]), w_up2_bf, w_dn2_bf, _row(g_final))
    return (yp.reshape(bp, lp, D_MODEL), ys.reshape(bs, ls, D_MODEL),
            p_ssd.reshape(1, bp, N_SSD_HEADS, SSD_HEAD_DIM, SSD_STATE), p_conv[None],
            p_gla.reshape(1, bp, N_GLA_HEADS, GLA_DK, GLA_DV),
            s_ssd.reshape(1, bs, N_SSD_HEADS, SSD_HEAD_DIM, SSD_STATE), s_conv[None],
            s_gla.reshape(1, bs, N_GLA_HEADS, GLA_DK, GLA_DV))
```

```python
import functools

import jax
import jax.numpy as jnp
import numpy as np
from jax import lax
from jax.experimental import pallas as pl
from jax.experimental.pallas import tpu as pltpu

f32 = jnp.float32
bf16 = jnp.bfloat16

D_MODEL = 1024
D_SSD = 512
SSD_HEAD_DIM = 64
N_SSD_HEADS = 8
SSD_STATE = 128
SSD_GROUPS = 2
CONV_W = 4
CONV_DIM = D_SSD + 2 * SSD_GROUPS * SSD_STATE
D_GLA = 512
N_GLA_HEADS = 4
GLA_DV = 128
GLA_DK = 64
GLA_RANK = 16
GLA_GATE_NORMALIZER = 16.0
D_FF = 2816
EPS = 1e-6
D_QK = N_GLA_HEADS * GLA_DK
LOG2E = float(np.log2(np.e))

SUBLANES = 8
LANES = 128
BF16_ROWS = 16
VMEM_LIMIT_BYTES = 60 * 1024 * 1024

CHUNK = 128
TOKEN_TILE = 256
TOKEN_B_TILE = 512

PA_COLS = D_SSD + CONV_DIM + 2 * D_QK + 2 * D_GLA
PB_COLS = LANES
OFF_Z, OFF_XBC, OFF_Q, OFF_K, OFF_V, OFF_G = 0, 512, 1536, 1792, 2048, 2560


def _dot(a, b):
    return jnp.dot(a, b, preferred_element_type=f32)


def _dot_nt(a, b):
    return lax.dot_general(a, b, (((1,), (1,)), ((), ())), preferred_element_type=f32)


def _split3(x):
    hi = x.astype(bf16)
    r1 = x - hi.astype(f32)
    mid = r1.astype(bf16)
    lo = (r1 - mid.astype(f32)).astype(bf16)
    return hi, mid, lo


def _sel_dot(sel3, x):
    return _dot(sel3, jnp.concatenate(_split3(x), axis=0))


def _tiled3(sel):
    return np.concatenate([sel] * 3, axis=1)


def _split2(x):
    hi = x.astype(bf16)
    return hi, (x - hi.astype(f32)).astype(bf16)


def _dot_x3(a, b_stack):
    a_hi, a_lo = _split2(a)
    return _dot(jnp.concatenate([a_hi, a_hi, a_lo], axis=1), b_stack)


def _stack_x3(b):
    b_hi, b_lo = _split2(b)
    return jnp.concatenate([b_hi, b_lo, b_hi], axis=0)


def _rms(x, g):
    return x * lax.rsqrt(jnp.mean(x * x, axis=-1, keepdims=True) + EPS) * g


N_LOAD = 16

W_IN_DT = D_SSD + CONV_DIM
W_IN_Q = W_IN_DT + N_SSD_HEADS
W_IN_LR = W_IN_Q + 2 * D_QK + 2 * D_GLA
W_IN_COLS = W_IN_LR + GLA_RANK


def _load_slab(step, slab_ref, dst_ref, regroup=None):
    rows = slab_ref.shape[0]
    assert rows % BF16_ROWS == 0
    val = slab_ref[...]
    if regroup is not None:
        val = regroup(val)
    dst_ref[pl.ds(pl.multiple_of(step * rows, BF16_ROWS), rows), :] = val.astype(bf16)


def _regroup_w_in(w):
    small = jnp.concatenate([w[:, W_IN_DT:W_IN_Q], w[:, W_IN_LR:W_IN_COLS],
                             jnp.zeros((w.shape[0], PB_COLS - N_SSD_HEADS - GLA_RANK), w.dtype)], axis=1)
    return jnp.concatenate([w[:, :W_IN_DT], w[:, W_IN_Q:W_IN_LR], small], axis=1)


ROW_STREAMS = 2


def _token_a_phases(row0, n_rows, x_ref, x1_ref, pa_ref, pb_ref, g1, wup, wdn, gmix, win, under_first_matmul=None,
                    keep_xbc=None):
    rows = pl.ds(row0, n_rows)
    x = x_ref[rows, :]
    gu = _dot(_rms(x, g1[...]).astype(bf16), wup[...])
    if under_first_matmul is not None:
        under_first_matmul()
    yield
    act = (jax.nn.silu(gu[:, :D_FF]) * gu[:, D_FF:]).astype(bf16)
    x1 = x + 0.5 * _dot(act, wdn[...])
    yield
    x1_ref[rows, :] = x1
    pr = _dot(_rms(x1, gmix[...]).astype(bf16), win[...])
    yield
    pa_ref[rows, :] = pr[:, :PA_COLS].astype(bf16)
    pb_ref[rows, :] = pr[:, PA_COLS:]
    if keep_xbc is not None:
        keep_xbc[pl.ds(SUBLANES + row0, n_rows), :] = pr[:, OFF_XBC:OFF_XBC + CONV_DIM]


def _token_a_body(n_prompt, tiles_per_seq, xp, xs, g1, wup_f, wdn_f, gmix, win_f, convw, convb,
                  x1p, x1s, pap, pas, pbp, pbs, xc, nconv, wup, wdn, win, cbuf):
    i = pl.program_id(0)
    tile = xp.shape[0]
    part = tile // ROW_STREAMS

    def conv_previous_tile():
        t = lax.rem(i - N_LOAD - 1, tiles_per_seq)
        cbuf[0:SUBLANES, :] = jnp.where(t == 0, 0.0, cbuf[0:SUBLANES, :])
        acc = jnp.broadcast_to(convb[...], (tile, CONV_DIM))
        for tap in range(CONV_W):
            shift = CONV_W - 1 - tap
            acc = acc + cbuf[SUBLANES - shift:SUBLANES - shift + tile, :] * convw[tap:tap + 1, :]
        xc[...] = jax.nn.silu(acc).astype(bf16)
        nconv[0] = cbuf[SUBLANES + tile - (CONV_W - 1):SUBLANES + tile, :]
        cbuf[0:SUBLANES, :] = cbuf[tile:tile + SUBLANES, :]

    def streams(x_ref, x1_ref, pa_ref, pb_ref, pending, keep_xbc):
        _round_robin([_token_a_phases(j * part, part, x_ref, x1_ref, pa_ref, pb_ref, g1, wup, wdn, gmix, win,
                                      pending if j == 0 else None, keep_xbc) for j in range(ROW_STREAMS)])

    @pl.when(i < N_LOAD)
    def _():
        _load_slab(i, wup_f, wup)
        _load_slab(i, wdn_f, wdn)
        _load_slab(i, win_f, win, _regroup_w_in)

    @pl.when(i == 0)
    def _():
        cbuf[...] = jnp.zeros(cbuf.shape, f32)

    first_sample = N_LOAD + n_prompt
    pl.when(jnp.logical_and(i >= N_LOAD, i < first_sample))(
        lambda: streams(xp, x1p, pap, pbp, conv_previous_tile, cbuf))
    pl.when(i == first_sample)(lambda: streams(xs, x1s, pas, pbs, conv_previous_tile, None))
    pl.when(i > first_sample)(lambda: streams(xs, x1s, pas, pbs, None, None))


def _token_b_phases(rows, x_ref, m_ref, y_ref, wout, g2, wup, wdn, gfin):
    x2 = x_ref[rows, :] + _dot(m_ref[rows, :], wout[...])
    yield
    gu = _dot(_rms(x2, g2[...]).astype(bf16), wup[...])
    yield
    act = (jax.nn.silu(gu[:, :D_FF]) * gu[:, D_FF:]).astype(bf16)
    x3 = x2 + 0.5 * _dot(act, wdn[...])
    yield
    y_ref[rows, :] = _rms(x3, gfin[...])


def _token_b_body(n_prompt, x1p, x1s, mxp, mxs, wout, g2, wup, wdn, gfin, yp, ys):
    def compute(x_ref, m_ref, y_ref):
        part = x_ref.shape[0] // ROW_STREAMS
        _round_robin([_token_b_phases(pl.ds(j * part, part), x_ref, m_ref, y_ref, wout, g2, wup, wdn, gfin)
                      for j in range(ROW_STREAMS)])

    i = pl.program_id(0)
    pl.when(i < n_prompt)(lambda: compute(x1p, mxp, yp))
    pl.when(i >= n_prompt)(lambda: compute(x1s, mxs, ys))


def _two_group_specs(n_prompt, cols, tile, first=0):
    prompt = pl.BlockSpec((tile, cols), lambda i: (jnp.clip(i - first, 0, n_prompt - 1), 0))
    sample = pl.BlockSpec((tile, cols), lambda i: (jnp.maximum(i - first - n_prompt, 0), 0))
    return prompt, sample


def _whole(shape):
    return pl.BlockSpec(shape, lambda i: (0,) * len(shape), pipeline_mode=pl.Buffered(1))


def _slabs(weight):
    _, rows, cols = weight.shape
    assert rows % N_LOAD == 0
    return pl.BlockSpec((None, rows // N_LOAD, cols), lambda i: (0, jnp.minimum(i, N_LOAD - 1), 0))


def _token_a(xp, xs, g1, wup, wdn, gmix, win, conv_w, conv_b, prompt_len):
    tp, ts = xp.shape[0], xs.shape[0]
    tile = TOKEN_TILE
    n_prompt, n_sample = tp // tile, ts // tile
    assert tp % tile == 0 and ts % tile == 0 and win.shape[2] == W_IN_COLS and prompt_len % tile == 0
    tiles_per_seq = prompt_len // tile
    n_seq = tp // prompt_len
    xin = _two_group_specs(n_prompt, D_MODEL, tile, N_LOAD)
    pa = _two_group_specs(n_prompt, PA_COLS, tile, N_LOAD)
    pb = _two_group_specs(n_prompt, PB_COLS, tile, N_LOAD)
    conv_tile = lambda i: jnp.clip(i - N_LOAD - 1, 0, n_prompt - 1)
    return pl.pallas_call(
        functools.partial(_token_a_body, n_prompt, tiles_per_seq),
        grid=(N_LOAD + n_prompt + n_sample,),
        in_specs=[*xin, _whole(g1.shape), _slabs(wup), _slabs(wdn), _whole(gmix.shape), _slabs(win),
                  _whole(conv_w.shape), _whole(conv_b.shape)],
        out_specs=[*xin, *pa, *pb, pl.BlockSpec((tile, CONV_DIM), lambda i: (conv_tile(i), 0)),
                   pl.BlockSpec((1, CONV_W - 1, CONV_DIM), lambda i: (conv_tile(i) // tiles_per_seq, 0, 0))],
        out_shape=[jax.ShapeDtypeStruct((tp, D_MODEL), f32), jax.ShapeDtypeStruct((ts, D_MODEL), f32),
                   jax.ShapeDtypeStruct((tp, PA_COLS), bf16), jax.ShapeDtypeStruct((ts, PA_COLS), bf16),
                   jax.ShapeDtypeStruct((tp, PB_COLS), f32), jax.ShapeDtypeStruct((ts, PB_COLS), f32),
                   jax.ShapeDtypeStruct((tp, CONV_DIM), bf16),
                   jax.ShapeDtypeStruct((n_seq, CONV_W - 1, CONV_DIM), f32)],
        scratch_shapes=[pltpu.VMEM(wup.shape[1:], bf16), pltpu.VMEM(wdn.shape[1:], bf16),
                        pltpu.VMEM((D_MODEL, PA_COLS + PB_COLS), bf16),
                        pltpu.VMEM((SUBLANES + tile, CONV_DIM), f32)],
        compiler_params=pltpu.CompilerParams(dimension_semantics=("arbitrary",), vmem_limit_bytes=VMEM_LIMIT_BYTES),
        name="token_a",
    )(xp, xs, g1, wup, wdn, gmix, win, conv_w, conv_b)


def _token_b(x1p, x1s, mxp, mxs, wout, g2, wup, wdn, gfin):
    tp, ts = x1p.shape[0], x1s.shape[0]
    tile = TOKEN_B_TILE
    assert tp % tile == 0 and ts % tile == 0
    n_prompt, n_sample = tp // tile, ts // tile
    xin = _two_group_specs(n_prompt, D_MODEL, tile)
    return pl.pallas_call(
        functools.partial(_token_b_body, n_prompt),
        grid=(n_prompt + n_sample,),
        in_specs=[*xin, *xin, _whole(wout.shape), _whole(g2.shape), _whole(wup.shape), _whole(wdn.shape),
                  _whole(gfin.shape)],
        out_specs=[*xin],
        out_shape=[jax.ShapeDtypeStruct((tp, D_MODEL), f32), jax.ShapeDtypeStruct((ts, D_MODEL), f32)],
        compiler_params=pltpu.CompilerParams(dimension_semantics=("arbitrary",), vmem_limit_bytes=VMEM_LIMIT_BYTES),
        name="token_b",
    )(x1p, x1s, mxp, mxs, wout, g2, wup, wdn, gfin)


def _level_halves(seq_rows):
    return [h for h in (1, 2, 4, 8, 16, 32, 64) if 2 * h <= seq_rows]


def _matmul_levels(seq_rows):
    return [h for h in _level_halves(seq_rows) if h < SUBLANES]


def _mixer_consts(seq_rows):
    t = np.arange(CHUNK)
    is_sample = seq_rows < CHUNK
    same = (t[:, None] // seq_rows) == (t[None, :] // seq_rows)
    tri = (same & (t[None, :] <= t[:, None])).astype(np.float32)
    last = ((t[:, None] // seq_rows) * seq_rows + seq_rows - 1 == t[None, :]).astype(np.float32)
    cum_gla, pair_mask = [tri], [np.eye(CHUNK, dtype=np.float32)]
    for h in _level_halves(seq_rows):
        ref_row = (t // (2 * h)) * (2 * h) + h
        if h in _matmul_levels(seq_rows):
            cum_gla.append((ref_row[:, None] == t[None, :]).astype(np.float32) @ tri)
        right = (t % (2 * h)) >= h
        blk = (t[:, None] // (2 * h)) == (t[None, :] // (2 * h))
        pair_mask.append((blk & right[:, None] & ~right[None, :]).astype(np.float32))
    cum_ssd = [tri]
    if is_sample:
        cum_gla.append(last @ tri)
        cum_ssd.append(last @ tri)
    hq = np.arange(D_QK) // GLA_DK
    hv = np.arange(D_GLA) // GLA_DV
    head_of_lane = np.arange(D_SSD) // SSD_HEAD_DIM
    expand = (np.arange(PB_COLS)[:, None] == head_of_lane[None, :]).astype(np.float32)
    consts = [jnp.asarray(_tiled3(np.concatenate(cum_ssd, 0)), bf16),
              jnp.asarray(_tiled3(np.concatenate(cum_gla, 0)), bf16),
              jnp.asarray(np.stack(pair_mask, 0), f32),
              jnp.asarray((hv[:, None] == hq[None, :]).astype(np.float32), f32),
              jnp.asarray(np.concatenate([expand] * 3, 0), bf16)]
    return consts


PROMPT_STREAMS = 8
SAMPLE_STREAMS = 1
N_SHARED_REFS = 14
EPILOGUE = "epilogue"


def _round_robin(streams):
    live = list(streams)
    waiting = []
    while live:
        for s in list(live):
            try:
                if next(s) == EPILOGUE:
                    live.remove(s)
                    waiting.append(s)
            except StopIteration:
                live.remove(s)
    for s in waiting:
        for _ in s:
            pass


def _mixer_body(seq_rows, n_cast, *refs):
    is_sample = seq_rows < CHUNK
    n_streams = refs[0].shape[0]
    n_in = 5 if is_sample else 3
    n_out = 4 if is_sample else 3
    n_shared = N_SHARED_REFS
    ins, shared = refs[:n_in], refs[n_in:n_in + n_shared]
    cast_in = refs[n_in + n_shared:n_in + n_shared + n_cast]
    outs = refs[n_in + n_shared + n_cast:n_in + n_shared + n_cast + n_out]
    cast_out = refs[n_in + n_shared + n_cast + n_out:n_in + n_shared + 2 * n_cast + n_out]
    scratch = refs[n_in + n_shared + 2 * n_cast + n_out:]
    is_last = None
    if not is_sample:
        st_ref, stbd_ref = scratch
        c_idx = pl.program_id(1)
        is_last = c_idx == pl.num_programs(1) - 1

        @pl.when(pl.program_id(0) * pl.num_programs(1) + c_idx < N_LOAD)
        def _():
            for src, dst in zip(cast_in, cast_out):
                dst[...] = src[...].astype(bf16)

        @pl.when(c_idx == 0)
        def _():
            st_ref[...] = jnp.zeros(st_ref.shape, f32)
            stbd_ref[...] = jnp.zeros(stbd_ref.shape, f32)

    at = lambda group, j: [r.at[j] for r in group]
    _round_robin([_mixer_phases(seq_rows, at(ins, j), shared, at(outs, j), at(scratch, j), is_last)
                  for j in range(n_streams)])


def _mixer_phases(seq_rows, ins, shared, outs, scratch, is_last):
    is_sample = seq_rows < CHUNK
    n_seq = CHUNK // seq_rows
    halves = _level_halves(seq_rows)
    n_lvl = len(halves)
    if is_sample:
        pa_ref, pb_ref, ssd0_ref, conv0_ref, gla0_ref = ins
        (cbuf,) = scratch
        mixed_ref, nconv_ref, nssd_ref, ngla_ref = outs
    else:
        pa_ref, pb_ref, xc_ref = ins
        st_ref, stbd_ref = scratch
        mixed_ref, nssd_ref, ngla_ref = outs
    (convw_ref, convb_ref, dsk_e_ref, gssd_ref, dtb_c_ref, alog_c_ref, wgk_ref, bgk_ref, ggla_ref,
     cum_ssd_ref, cum_gla_ref, pair_mask_ref, bdmask_ref, expand_ref) = shared

    C = CHUNK
    rows = lax.broadcasted_iota(jnp.int32, (C, 1), 0)
    lane_qk = lax.shift_right_logical(lax.broadcasted_iota(jnp.int32, (1, D_QK), 1), int(np.log2(GLA_DK)))

    if is_sample:
        conv = jnp.broadcast_to(convb_ref[...], (C, CONV_DIM))
        for tap in range(CONV_W - 1):
            cbuf[:, SUBLANES - (CONV_W - 1) + tap, :] = conv0_ref[tap]
        cbuf[:, SUBLANES:SUBLANES + seq_rows, :] = (
            pa_ref[:, OFF_XBC:OFF_XBC + CONV_DIM].astype(f32).reshape(n_seq, seq_rows, CONV_DIM))
        for i in range(CONV_W):
            shift = CONV_W - 1 - i
            win = cbuf[:, SUBLANES - shift:SUBLANES - shift + seq_rows, :].reshape(C, CONV_DIM)
            conv = conv + win * convw_ref[i:i + 1, :]
        for tap in range(CONV_W - 1):
            nconv_ref[tap] = cbuf[:, SUBLANES + seq_rows - (CONV_W - 1) + tap, :]
        xc = jax.nn.silu(conv)
    else:
        xc = xc_ref[...]
    xs = xc[:, :D_SSD].astype(f32)
    bm = xc[:, D_SSD:D_SSD + SSD_GROUPS * SSD_STATE].astype(bf16)
    cm = xc[:, D_SSD + SSD_GROUPS * SSD_STATE:].astype(bf16)

    small = pb_ref[...]
    dtp_c = jax.nn.softplus(small + dtb_c_ref[...])
    cums = _sel_dot(cum_ssd_ref[...], dtp_c * (-LOG2E * jnp.exp(alog_c_ref[...])))
    yield
    acum_c = cums[:C]
    acum_t = acum_c.T
    wide = _dot(jnp.concatenate(_split3(jnp.concatenate([cums, dtp_c], axis=0)), axis=1), expand_ref[...])
    yield
    acum_e, dtp_e = wide[:C], wide[-C:]
    acum_last = wide[C:2 * C] if is_sample else acum_e[C - 1:C, :]
    causal = cum_ssd_ref[0:C, 0:C].astype(f32) > 0
    xdt = xs * dtp_e
    xdt_bf = xdt.astype(bf16)
    lane = lax.broadcasted_iota(jnp.int32, (1, LANES), 1)
    left = lane < SSD_HEAD_DIM
    heads_per_group = N_SSD_HEADS // SSD_GROUPS
    gw = heads_per_group * SSD_HEAD_DIM
    y_parts = []
    for grp in range(SSD_GROUPS):
        cb = _dot_nt(cm[:, grp * SSD_STATE:(grp + 1) * SSD_STATE], bm[:, grp * SSD_STATE:(grp + 1) * SSD_STATE])
        for pair in range(heads_per_group // 2):
            sc = []
            for hh in range(2):
                h = grp * heads_per_group + pair * 2 + hh
                seg = acum_c[:, h:h + 1] - acum_t[h:h + 1, :]
                sc.append((cb * jnp.exp2(jnp.where(causal, seg, -jnp.inf))).astype(bf16))
            lo = (grp * heads_per_group + pair * 2) * SSD_HEAD_DIM
            xp = xdt_bf[:, lo:lo + LANES]
            bd = jnp.concatenate([jnp.where(left, xp, 0), jnp.where(left, 0, xp)], axis=0)
            y_parts.append(_dot(jnp.concatenate(sc, axis=1), bd))
            yield
    y = jnp.concatenate(y_parts, axis=1)

    to_end = jnp.exp2(jnp.minimum(acum_last - acum_e, 0.0))
    xs_end = (xdt * to_end).astype(bf16)
    e_acum = jnp.exp2(acum_e)
    bm_t = bm.T

    def ssd_inter(st):
        st_bf = st.astype(bf16)
        return jnp.concatenate([_dot(cm[:, grp * SSD_STATE:(grp + 1) * SSD_STATE], st_bf[:, grp * gw:(grp + 1) * gw])
                                for grp in range(SSD_GROUPS)], axis=1) * e_acum

    def ssd_update(st, decay_row, xs_rows):
        upd = jnp.concatenate([_dot(bm_t[grp * SSD_STATE:(grp + 1) * SSD_STATE, :], xs_rows[:, grp * gw:(grp + 1) * gw])
                               for grp in range(SSD_GROUPS)], axis=1)
        return st * decay_row + upd

    q = pa_ref[:, OFF_Q:OFF_Q + D_QK].astype(f32) * (GLA_DK ** -0.5)
    k = pa_ref[:, OFF_K:OFF_K + D_QK].astype(f32)
    v_bf = pa_ref[:, OFF_V:OFF_V + D_GLA]
    gk = _dot_x3(small, wgk_ref[...]) + bgk_ref[...]
    log_a = jax.nn.log_sigmoid(gk) * (LOG2E / GLA_GATE_NORMALIZER)
    bsel = _sel_dot(cum_gla_ref[...], log_a)
    yield
    bcum = bsel[:C]
    n_mm = len(_matmul_levels(seq_rows))
    b_last = bsel[(n_mm + 1) * C:] if is_sample else bcum[C - 1:C, :]
    k_bf = k.astype(bf16)

    def b_ref(lvl, half):
        if lvl < n_mm:
            return bsel[(lvl + 1) * C:(lvl + 2) * C]
        return jnp.concatenate([jnp.broadcast_to(bcum[r + half:r + half + 1, :], (2 * half, D_QK))
                                for r in range(0, C, 2 * half)], axis=0)

    def head_rows(x_bf):
        return jnp.concatenate([jnp.where(lane_qk == h, x_bf, jnp.zeros_like(x_bf)) for h in range(N_GLA_HEADS)], axis=0)

    a = _dot_nt(head_rows(q.astype(bf16)), k_bf)
    yield
    m = pair_mask_ref[0]
    att = [a[h * C:(h + 1) * C] * m for h in range(N_GLA_HEADS)]
    for lvl, half in enumerate(halves):
        decay = jnp.exp2(-jnp.abs(bcum - b_ref(lvl, half)))
        in_right = (rows & half) != 0
        u = jnp.where(in_right, q, k) * decay
        u_bf = u.astype(bf16)
        m = pair_mask_ref[lvl + 1]
        if half < SUBLANES:
            a = _dot_nt(head_rows(u_bf), u_bf)
            yield
            for h in range(N_GLA_HEADS):
                att[h] = att[h] + a[h * C:(h + 1) * C] * m
        else:
            starts = range(half, C, 2 * half)
            pick = lambda x: jnp.concatenate([x[r:r + half] for r in starts], axis=0)
            a = _dot_nt(head_rows(pick(u).astype(bf16)), u_bf)
            yield
            m_right = pick(m)
            for h in range(N_GLA_HEADS):
                upd = a[h * (C // 2):(h + 1) * (C // 2)] * m_right
                parts = []
                for i, r in enumerate(starts):
                    parts += [att[h][r - half:r], att[h][r:r + half] + upd[i * half:(i + 1) * half]]
                att[h] = jnp.concatenate(parts, axis=0)
    o = jnp.concatenate([_dot(att[h].astype(bf16), v_bf[:, h * GLA_DV:(h + 1) * GLA_DV])
                         for h in range(N_GLA_HEADS)], axis=1)
    yield

    q_in = (q * jnp.exp2(bcum)).astype(bf16)
    kd = (k * jnp.exp2(jnp.minimum(b_last - bcum, 0.0))).astype(bf16)
    v_t = v_bf.T
    bdmask = bdmask_ref[...]

    def gla_inter(stbd):
        return _dot_nt(q_in, stbd.astype(bf16))

    def gla_update(stbd, decay_row, kd_rows):
        return stbd * decay_row + bdmask * _dot(v_t, kd_rows)

    def gla_heads(stbd):
        acc = jnp.where(lane_qk == 0, stbd[0:GLA_DV, :], 0.0)
        for h in range(1, N_GLA_HEADS):
            acc = acc + jnp.where(lane_qk == h, stbd[h * GLA_DV:(h + 1) * GLA_DV, :], 0.0)
        return acc.T

    if not is_sample:
        st = st_ref[...]
        stbd = stbd_ref[...]
        y = y + ssd_inter(st)
        o = o + gla_inter(stbd)
        yield
        st_ref[...] = ssd_update(st, jnp.exp2(acum_last), xs_end)
        stbd_ref[...] = gla_update(stbd, jnp.exp2(b_last), kd)
        yield
    else:
        seq_of_row = lax.shift_right_logical(rows, int(np.log2(seq_rows)))
        bm_f32 = xc[:, D_SSD:D_SSD + SSD_GROUPS * SSD_STATE]
        cm_f32 = xc[:, D_SSD + SSD_GROUPS * SSD_STATE:]
        xs_end_f32 = xdt * to_end
        e_last_c = jnp.exp2(cums[C:])
        q_in_f32 = q * jnp.exp2(bcum)
        y_rows, o_rows = [], []
        g0_next = gla0_ref[0].T
        for s in range(n_seq):
            g0 = g0_next
            if s + 1 < n_seq:
                g0_next = gla0_ref[s + 1].T
            mine = seq_of_row == s
            r0 = s * seq_rows
            own = slice(r0, r0 + seq_rows)
            st = ssd0_ref[s]
            c_rows = cm_f32[own].astype(bf16)
            b_rows = bm_f32[own].astype(bf16)
            x_rows = xs_end_f32[own].astype(bf16)
            y_grp, st_new = [], []
            for grp in range(SSD_GROUPS):
                st_g = st[grp * gw:(grp + 1) * gw]
                y_grp.append(_dot_nt(c_rows[:, grp * SSD_STATE:(grp + 1) * SSD_STATE], st_g.astype(bf16)))
                upd = lax.dot_general(x_rows[:, grp * gw:(grp + 1) * gw], b_rows[:, grp * SSD_STATE:(grp + 1) * SSD_STATE],
                                      (((0,), (0,)), ((), ())), preferred_element_type=f32)
                for hh in range(heads_per_group):
                    h = grp * heads_per_group + hh
                    rows_h = slice(hh * SSD_HEAD_DIM, (hh + 1) * SSD_HEAD_DIM)
                    decay_h = jnp.broadcast_to(e_last_c[r0:r0 + 1, h:h + 1], (SSD_HEAD_DIM, SSD_STATE))
                    st_new.append(st_g[rows_h] * decay_h + upd[rows_h])
            y_rows.append(jnp.concatenate(y_grp, axis=1))
            nssd_ref[s] = jnp.concatenate(st_new, axis=0)
            stbd = jnp.concatenate([jnp.where(lane_qk == h, g0, 0.0) for h in range(N_GLA_HEADS)], axis=0)
            o_rows.append(_dot_nt(q_in_f32[r0:r0 + seq_rows].astype(bf16), stbd.astype(bf16)))
            ngla_ref[s] = gla_heads(gla_update(stbd, jnp.exp2(b_last[r0:r0 + 1, :]), jnp.where(mine, kd, 0)))
            yield
        y = y + jnp.concatenate(y_rows, axis=0) * e_acum
        o = o + jnp.concatenate(o_rows, axis=0)

    z = pa_ref[:, OFF_Z:OFF_Z + D_SSD].astype(f32)
    g = pa_ref[:, OFF_G:OFF_G + D_GLA].astype(f32)
    y = (y + dsk_e_ref[...] * xs) * jax.nn.silu(z)
    gsz = D_SSD // SSD_GROUPS
    y = jnp.concatenate([_rms(y[:, i * gsz:(i + 1) * gsz], gssd_ref[:, i * gsz:(i + 1) * gsz])
                         for i in range(SSD_GROUPS)], axis=1)
    o = jnp.concatenate([_rms(o[:, h * GLA_DV:(h + 1) * GLA_DV], ggla_ref[...]) for h in range(N_GLA_HEADS)], axis=1)
    o = o * jax.nn.silu(g)
    mixed_ref[:, :D_SSD] = y.astype(bf16)
    mixed_ref[:, D_SSD:] = o.astype(bf16)

    if not is_sample:
        yield EPILOGUE

        @pl.when(is_last)
        def _():
            nssd_ref[0] = st_ref[...].T
            ngla_ref[0] = gla_heads(stbd_ref[...])


def _mixer(pa, pb, params, states, n_batch, seq_len, xc=None, cast_weights=()):
    is_sample = states is not None
    assert not (is_sample and cast_weights) and is_sample == (xc is None)
    seq_rows = seq_len if is_sample else CHUNK
    assert CHUNK % seq_rows == 0 and (is_sample or seq_len % CHUNK == 0)
    n_seq = CHUNK // seq_rows
    consts = _mixer_consts(seq_rows)
    ns = SAMPLE_STREAMS if is_sample else PROMPT_STREAMS
    assert n_batch % (ns * n_seq) == 0
    rows_per_stream = n_batch * seq_len // ns
    seqs_per_stream = n_batch // ns

    if is_sample:
        grid = (seqs_per_stream // n_seq,)
        row_map = lambda c: (0, c, 0)
        seq_map = lambda c: (0, c, 0, 0)
        full = lambda shape: pl.BlockSpec(shape, lambda c: (0,) * len(shape))
        sems = ("arbitrary",)
    else:
        n_chunks = seq_len // CHUNK
        grid = (seqs_per_stream, n_chunks)
        row_map = lambda b, c: (0, b * n_chunks + c, 0)
        seq_map = lambda b, c: (0, b, 0, 0)
        full = lambda shape: pl.BlockSpec(shape, lambda b, c: (0,) * len(shape))
        sems = ("arbitrary", "arbitrary")

    def streamed(arr):
        return arr.reshape((ns, arr.shape[0] // ns) + arr.shape[1:])

    taps = CONV_W - 1
    conv_spec = pl.BlockSpec((ns, taps, n_seq, CONV_DIM), lambda *g: (0, 0, seq_map(*g)[1], 0))
    state_blocks = [(n_seq, D_SSD, SSD_STATE), (n_seq, D_QK, GLA_DV)]
    in_arrays = [streamed(pa), streamed(pb)]
    in_specs = [pl.BlockSpec((ns, CHUNK, PA_COLS), row_map), pl.BlockSpec((ns, CHUNK, PB_COLS), row_map)]
    if is_sample:
        ssd0, conv0, gla0 = states
        conv0_t = jnp.transpose(streamed(conv0), (0, 2, 1, 3))
        in_arrays += [streamed(ssd0), conv0_t, streamed(gla0)]
        in_specs += [pl.BlockSpec((ns,) + state_blocks[0], seq_map), conv_spec,
                     pl.BlockSpec((ns,) + state_blocks[1], seq_map)]
    else:
        in_arrays.append(streamed(xc))
        in_specs.append(pl.BlockSpec((ns, CHUNK, CONV_DIM), row_map))
    for arr in list(params) + consts:
        in_arrays.append(arr)
        in_specs.append(full(arr.shape))

    out_shape = [jax.ShapeDtypeStruct((ns, rows_per_stream, D_MODEL), bf16)]
    out_specs = [pl.BlockSpec((ns, CHUNK, D_MODEL), row_map)]
    if is_sample:
        out_shape.append(jax.ShapeDtypeStruct((ns, taps, seqs_per_stream, CONV_DIM), f32))
        out_specs.append(conv_spec)
    out_shape += [jax.ShapeDtypeStruct((ns, seqs_per_stream) + blk[1:], f32) for blk in state_blocks]
    out_specs += [pl.BlockSpec((ns,) + blk, seq_map) for blk in state_blocks]
    for w in cast_weights:
        _, w_rows, w_cols = w.shape
        slab = w_rows // N_LOAD
        assert w_rows % N_LOAD == 0 and slab % BF16_ROWS == 0 and grid[0] * grid[1] >= N_LOAD
        step = lambda b, c: jnp.minimum(b * n_chunks + c, N_LOAD - 1)
        in_arrays.append(w)
        in_specs.append(pl.BlockSpec((None, slab, w_cols), lambda b, c: (0, step(b, c), 0)))
        out_shape.append(jax.ShapeDtypeStruct((w_rows, w_cols), bf16))
        out_specs.append(pl.BlockSpec((slab, w_cols), lambda b, c: (step(b, c), 0)))
    if is_sample:
        scratch = [pltpu.VMEM((ns, n_seq, SUBLANES + seq_rows, CONV_DIM), f32)]
    else:
        scratch = [pltpu.VMEM((ns, SSD_STATE, D_SSD), f32), pltpu.VMEM((ns, D_GLA, D_QK), f32)]
    results = pl.pallas_call(
        functools.partial(_mixer_body, seq_rows, len(cast_weights)),
        grid=grid, in_specs=in_specs, out_specs=out_specs, out_shape=out_shape, scratch_shapes=scratch,
        compiler_params=pltpu.CompilerParams(dimension_semantics=sems, vmem_limit_bytes=VMEM_LIMIT_BYTES),
        name="mixer_sample" if is_sample else "mixer_prompt",
    )(*in_arrays)
    merged = lambda arr: arr.reshape((arr.shape[0] * arr.shape[1],) + arr.shape[2:])
    if is_sample:
        mixed, nconv, nssd, ngla = results
        return merged(mixed), merged(jnp.transpose(nconv, (0, 2, 1, 3))), merged(nssd), merged(ngla)
    mixed, nssd, ngla, *cast = results
    return merged(mixed), merged(nssd), merged(ngla), cast


def _row(vec):
    return vec.reshape(1, -1).astype(f32)


def _pad_lanes(vec, width):
    return jnp.concatenate([vec.astype(f32), jnp.zeros((width - vec.shape[0],), f32)]).reshape(1, width)


def _mixer_params(conv_w, conv_b, dt_bias, a_log, d_skip, g_ssd_norm, w_gk2, b_gk, g_gla_norm):
    rep = lambda vec: _row(jnp.repeat(vec, SSD_HEAD_DIM))
    wgk = jnp.zeros((PB_COLS, D_QK), f32).at[N_SSD_HEADS:N_SSD_HEADS + GLA_RANK, :].set(w_gk2.astype(f32))
    return [conv_w.astype(f32), _row(conv_b), rep(d_skip), _row(g_ssd_norm),
            _pad_lanes(dt_bias, PB_COLS), _pad_lanes(a_log, PB_COLS), _stack_x3(wgk), _row(b_gk), _row(g_gla_norm)]


def kernel(x_prompt, x_sample, state_ssd, state_conv, state_gla, g_ffn1, w_ffn1_in, w_ffn1_out, g_mix, w_in, conv_w, conv_b, dt_bias, a_log, d_skip, g_ssd_norm, w_gk2, b_gk, g_gla_norm, w_out, g_ffn2, w_ffn2_in, w_ffn2_out, g_final):
    bp, lp, _ = x_prompt.shape
    bs, ls, _ = x_sample.shape
    assert w_in.shape[0] == 1, "single-layer step: the final norm is fused into the layer's last kernel"
    xp = x_prompt.reshape(bp * lp, D_MODEL)
    xs = x_sample.reshape(bs * ls, D_MODEL)
    params = _mixer_params(conv_w[0], conv_b[0], dt_bias[0], a_log[0], d_skip[0], g_ssd_norm[0],
                           w_gk2[0], b_gk[0], g_gla_norm[0])
    x1p, x1s, pap, pas, pbp, pbs, xcp, p_conv = _token_a(
        xp, xs, _row(g_ffn1[0]), w_ffn1_in.astype(f32), w_ffn1_out.astype(f32), _row(g_mix[0]), w_in.astype(f32),
        params[0], params[1], lp)
    mxp, p_ssd, p_gla, (w_out_bf, w_up2_bf, w_dn2_bf) = _mixer(
        pap, pbp, params, None, bp, lp, xc=xcp,
        cast_weights=(w_out.astype(f32), w_ffn2_in.astype(f32), w_ffn2_out.astype(f32)))
    s_states = (state_ssd[0].reshape(bs, D_SSD, SSD_STATE), state_conv[0], state_gla[0].reshape(bs, D_QK, GLA_DV))
    mxs, s_conv, s_ssd, s_gla = _mixer(pas, pbs, params, s_states, bs, ls)
    yp, ys = _token_b(x1p, x1s, mxp, mxs, w_out_bf, _row(g_ffn2[0]), w_up2_bf, w_dn2_bf, _row(g_final))
    return (yp.reshape(bp, lp, D_MODEL), ys.reshape(bs, ls, D_MODEL),
            p_ssd.reshape(1, bp, N_SSD_HEADS, SSD_HEAD_DIM, SSD_STATE), p_conv[None],
            p_gla.reshape(1, bp, N_GLA_HEADS, GLA_DK, GLA_DV),
            s_ssd.reshape(1, bs, N_SSD_HEADS, SSD_HEAD_DIM, SSD_STATE), s_conv[None],
            s_gla.reshape(1, bs, N_GLA_HEADS, GLA_DK, GLA_DV))
```

```python
import functools

import jax
import jax.numpy as jnp
import numpy as np
from jax import lax
from jax.experimental import pallas as pl
from jax.experimental.pallas import tpu as pltpu

f32 = jnp.float32
bf16 = jnp.bfloat16

D_MODEL = 1024
D_SSD = 512
SSD_HEAD_DIM = 64
N_SSD_HEADS = 8
SSD_STATE = 128
SSD_GROUPS = 2
CONV_W = 4
CONV_DIM = D_SSD + 2 * SSD_GROUPS * SSD_STATE
D_GLA = 512
N_GLA_HEADS = 4
GLA_DV = 128
GLA_DK = 64
GLA_RANK = 16
GLA_GATE_NORMALIZER = 16.0
D_FF = 2816
EPS = 1e-6
D_QK = N_GLA_HEADS * GLA_DK
LOG2E = float(np.log2(np.e))

SUBLANES = 8
LANES = 128
BF16_ROWS = 16
VMEM_LIMIT_BYTES = 60 * 1024 * 1024

CHUNK = 128
TOKEN_TILE = 256
TOKEN_B_TILE = 512

PA_COLS = D_SSD + CONV_DIM + 2 * D_QK + 2 * D_GLA
PB_COLS = LANES
OFF_Z, OFF_XBC, OFF_Q, OFF_K, OFF_V, OFF_G = 0, 512, 1536, 1792, 2048, 2560


def _dot(a, b):
    return jnp.dot(a, b, preferred_element_type=f32)


def _dot_nt(a, b):
    return lax.dot_general(a, b, (((1,), (1,)), ((), ())), preferred_element_type=f32)


def _split3(x):
    hi = x.astype(bf16)
    r1 = x - hi.astype(f32)
    mid = r1.astype(bf16)
    lo = (r1 - mid.astype(f32)).astype(bf16)
    return hi, mid, lo


def _sel_dot(sel3, x):
    return _dot(sel3, jnp.concatenate(_split3(x), axis=0))


def _tiled3(sel):
    return np.concatenate([sel] * 3, axis=1)


def _split2(x):
    hi = x.astype(bf16)
    return hi, (x - hi.astype(f32)).astype(bf16)


def _dot_x3(a, b_stack):
    a_hi, a_lo = _split2(a)
    return _dot(jnp.concatenate([a_hi, a_hi, a_lo], axis=1), b_stack)


def _stack_x3(b):
    b_hi, b_lo = _split2(b)
    return jnp.concatenate([b_hi, b_lo, b_hi], axis=0)


def _rms(x, g):
    return x * lax.rsqrt(jnp.mean(x * x, axis=-1, keepdims=True) + EPS) * g


N_LOAD = 16

W_IN_DT = D_SSD + CONV_DIM
W_IN_Q = W_IN_DT + N_SSD_HEADS
W_IN_LR = W_IN_Q + 2 * D_QK + 2 * D_GLA
W_IN_COLS = W_IN_LR + GLA_RANK


def _load_slab(step, slab_ref, dst_ref, regroup=None):
    rows = slab_ref.shape[0]
    assert rows % BF16_ROWS == 0
    val = slab_ref[...]
    if regroup is not None:
        val = regroup(val)
    dst_ref[pl.ds(pl.multiple_of(step * rows, BF16_ROWS), rows), :] = val.astype(bf16)


def _regroup_w_in(w):
    small = jnp.concatenate([w[:, W_IN_DT:W_IN_Q], w[:, W_IN_LR:W_IN_COLS],
                             jnp.zeros((w.shape[0], PB_COLS - N_SSD_HEADS - GLA_RANK), w.dtype)], axis=1)
    return jnp.concatenate([w[:, :W_IN_DT], w[:, W_IN_Q:W_IN_LR], small], axis=1)


ROW_STREAMS = 2


def _token_a_phases(row0, n_rows, x_ref, x1_ref, pa_ref, pb_ref, g1, wup, wdn, gmix, win, under_first_matmul=None,
                    keep_xbc=None):
    rows = pl.ds(row0, n_rows)
    x = x_ref[rows, :]
    gu = _dot(_rms(x, g1[...]).astype(bf16), wup[...])
    if under_first_matmul is not None:
        under_first_matmul()
    yield
    act = (jax.nn.silu(gu[:, :D_FF]) * gu[:, D_FF:]).astype(bf16)
    x1 = x + 0.5 * _dot(act, wdn[...])
    yield
    x1_ref[rows, :] = x1
    pr = _dot(_rms(x1, gmix[...]).astype(bf16), win[...])
    yield
    pa_ref[rows, :] = pr[:, :PA_COLS].astype(bf16)
    pb_ref[rows, :] = pr[:, PA_COLS:]
    if keep_xbc is not None:
        keep_xbc[pl.ds(SUBLANES + row0, n_rows), :] = pr[:, OFF_XBC:OFF_XBC + CONV_DIM]


def _token_a_body(n_prompt, tiles_per_seq, xp, xs, g1, wup_f, wdn_f, gmix, win_f, convw, convb,
                  x1p, x1s, pap, pas, pbp, pbs, xc, nconv, wup, wdn, win, cbuf):
    i = pl.program_id(0)
    tile = xp.shape[0]
    part = tile // ROW_STREAMS

    def conv_previous_tile():
        t = lax.rem(i - N_LOAD - 1, tiles_per_seq)
        cbuf[0:SUBLANES, :] = jnp.where(t == 0, 0.0, cbuf[0:SUBLANES, :])
        ext = cbuf[...]
        acc = ext * convw[0:1, :]
        for tap in range(1, CONV_W):
            acc = pltpu.roll(acc, 1, axis=0) + ext * convw[tap:tap + 1, :]
        xc[...] = jax.nn.silu(acc[SUBLANES:, :] + convb[...]).astype(bf16)
        nconv[0] = cbuf[SUBLANES + tile - (CONV_W - 1):SUBLANES + tile, :]
        cbuf[0:SUBLANES, :] = cbuf[tile:tile + SUBLANES, :]

    def streams(x_ref, x1_ref, pa_ref, pb_ref, pending, keep_xbc):
        _round_robin([_token_a_phases(j * part, part, x_ref, x1_ref, pa_ref, pb_ref, g1, wup, wdn, gmix, win,
                                      pending if j == 0 else None, keep_xbc) for j in range(ROW_STREAMS)])

    @pl.when(i < N_LOAD)
    def _():
        _load_slab(i, wup_f, wup)
        _load_slab(i, wdn_f, wdn)
        _load_slab(i, win_f, win, _regroup_w_in)

    @pl.when(i == 0)
    def _():
        cbuf[...] = jnp.zeros(cbuf.shape, f32)

    first_sample = N_LOAD + n_prompt
    pl.when(jnp.logical_and(i >= N_LOAD, i < first_sample))(
        lambda: streams(xp, x1p, pap, pbp, conv_previous_tile, cbuf))
    pl.when(i == first_sample)(lambda: streams(xs, x1s, pas, pbs, conv_previous_tile, None))
    pl.when(i > first_sample)(lambda: streams(xs, x1s, pas, pbs, None, None))


def _token_b_phases(rows, x_ref, m_ref, y_ref, wout, g2, wup, wdn, gfin):
    x2 = x_ref[rows, :] + _dot(m_ref[rows, :], wout[...])
    yield
    gu = _dot(_rms(x2, g2[...]).astype(bf16), wup[...])
    yield
    act = (jax.nn.silu(gu[:, :D_FF]) * gu[:, D_FF:]).astype(bf16)
    x3 = x2 + 0.5 * _dot(act, wdn[...])
    yield
    y_ref[rows, :] = _rms(x3, gfin[...])


def _token_b_body(n_prompt, x1p, x1s, mxp, mxs, wout, g2, wup, wdn, gfin, yp, ys):
    def compute(x_ref, m_ref, y_ref):
        part = x_ref.shape[0] // ROW_STREAMS
        _round_robin([_token_b_phases(pl.ds(j * part, part), x_ref, m_ref, y_ref, wout, g2, wup, wdn, gfin)
                      for j in range(ROW_STREAMS)])

    i = pl.program_id(0)
    pl.when(i < n_prompt)(lambda: compute(x1p, mxp, yp))
    pl.when(i >= n_prompt)(lambda: compute(x1s, mxs, ys))


def _two_group_specs(n_prompt, cols, tile, first=0):
    prompt = pl.BlockSpec((tile, cols), lambda i: (jnp.clip(i - first, 0, n_prompt - 1), 0))
    sample = pl.BlockSpec((tile, cols), lambda i: (jnp.maximum(i - first - n_prompt, 0), 0))
    return prompt, sample


def _whole(shape):
    return pl.BlockSpec(shape, lambda i: (0,) * len(shape), pipeline_mode=pl.Buffered(1))


def _slabs(weight):
    _, rows, cols = weight.shape
    assert rows % N_LOAD == 0
    return pl.BlockSpec((None, rows // N_LOAD, cols), lambda i: (0, jnp.minimum(i, N_LOAD - 1), 0))


def _token_a(xp, xs, g1, wup, wdn, gmix, win, conv_w, conv_b, prompt_len):
    tp, ts = xp.shape[0], xs.shape[0]
    tile = TOKEN_TILE
    n_prompt, n_sample = tp // tile, ts // tile
    assert tp % tile == 0 and ts % tile == 0 and win.shape[2] == W_IN_COLS and prompt_len % tile == 0
    tiles_per_seq = prompt_len // tile
    n_seq = tp // prompt_len
    xin = _two_group_specs(n_prompt, D_MODEL, tile, N_LOAD)
    pa = _two_group_specs(n_prompt, PA_COLS, tile, N_LOAD)
    pb = _two_group_specs(n_prompt, PB_COLS, tile, N_LOAD)
    conv_tile = lambda i: jnp.clip(i - N_LOAD - 1, 0, n_prompt - 1)
    return pl.pallas_call(
        functools.partial(_token_a_body, n_prompt, tiles_per_seq),
        grid=(N_LOAD + n_prompt + n_sample,),
        in_specs=[*xin, _whole(g1.shape), _slabs(wup), _slabs(wdn), _whole(gmix.shape), _slabs(win),
                  _whole(conv_w.shape), _whole(conv_b.shape)],
        out_specs=[*xin, *pa, *pb, pl.BlockSpec((tile, CONV_DIM), lambda i: (conv_tile(i), 0)),
                   pl.BlockSpec((1, CONV_W - 1, CONV_DIM), lambda i: (conv_tile(i) // tiles_per_seq, 0, 0))],
        out_shape=[jax.ShapeDtypeStruct((tp, D_MODEL), f32), jax.ShapeDtypeStruct((ts, D_MODEL), f32),
                   jax.ShapeDtypeStruct((tp, PA_COLS), bf16), jax.ShapeDtypeStruct((ts, PA_COLS), bf16),
                   jax.ShapeDtypeStruct((tp, PB_COLS), f32), jax.ShapeDtypeStruct((ts, PB_COLS), f32),
                   jax.ShapeDtypeStruct((tp, CONV_DIM), bf16),
                   jax.ShapeDtypeStruct((n_seq, CONV_W - 1, CONV_DIM), f32)],
        scratch_shapes=[pltpu.VMEM(wup.shape[1:], bf16), pltpu.VMEM(wdn.shape[1:], bf16),
                        pltpu.VMEM((D_MODEL, PA_COLS + PB_COLS), bf16),
                        pltpu.VMEM((SUBLANES + tile, CONV_DIM), f32)],
        compiler_params=pltpu.CompilerParams(dimension_semantics=("arbitrary",), vmem_limit_bytes=VMEM_LIMIT_BYTES),
        name="token_a",
    )(xp, xs, g1, wup, wdn, gmix, win, conv_w, conv_b)


def _token_b(x1p, x1s, mxp, mxs, wout, g2, wup, wdn, gfin):
    tp, ts = x1p.shape[0], x1s.shape[0]
    tile = TOKEN_B_TILE
    assert tp % tile == 0 and ts % tile == 0
    n_prompt, n_sample = tp // tile, ts // tile
    xin = _two_group_specs(n_prompt, D_MODEL, tile)
    return pl.pallas_call(
        functools.partial(_token_b_body, n_prompt),
        grid=(n_prompt + n_sample,),
        in_specs=[*xin, *xin, _whole(wout.shape), _whole(g2.shape), _whole(wup.shape), _whole(wdn.shape),
                  _whole(gfin.shape)],
        out_specs=[*xin],
        out_shape=[jax.ShapeDtypeStruct((tp, D_MODEL), f32), jax.ShapeDtypeStruct((ts, D_MODEL), f32)],
        compiler_params=pltpu.CompilerParams(dimension_semantics=("arbitrary",), vmem_limit_bytes=VMEM_LIMIT_BYTES),
        name="token_b",
    )(x1p, x1s, mxp, mxs, wout, g2, wup, wdn, gfin)


def _level_halves(seq_rows):
    return [h for h in (1, 2, 4, 8, 16, 32, 64) if 2 * h <= seq_rows]


def _matmul_levels(seq_rows):
    return [h for h in _level_halves(seq_rows) if h < SUBLANES]


def _mixer_consts(seq_rows):
    t = np.arange(CHUNK)
    is_sample = seq_rows < CHUNK
    same = (t[:, None] // seq_rows) == (t[None, :] // seq_rows)
    tri = (same & (t[None, :] <= t[:, None])).astype(np.float32)
    last = ((t[:, None] // seq_rows) * seq_rows + seq_rows - 1 == t[None, :]).astype(np.float32)
    cum_gla, pair_mask = [tri], [np.eye(CHUNK, dtype=np.float32)]
    for h in _level_halves(seq_rows):
        ref_row = (t // (2 * h)) * (2 * h) + h
        if h in _matmul_levels(seq_rows):
            cum_gla.append((ref_row[:, None] == t[None, :]).astype(np.float32) @ tri)
        right = (t % (2 * h)) >= h
        blk = (t[:, None] // (2 * h)) == (t[None, :] // (2 * h))
        pair_mask.append((blk & right[:, None] & ~right[None, :]).astype(np.float32))
    cum_ssd = [tri]
    if is_sample:
        cum_gla.append(last @ tri)
        cum_ssd.append(last @ tri)
    hq = np.arange(D_QK) // GLA_DK
    hv = np.arange(D_GLA) // GLA_DV
    head_of_lane = np.arange(D_SSD) // SSD_HEAD_DIM
    expand = (np.arange(PB_COLS)[:, None] == head_of_lane[None, :]).astype(np.float32)
    consts = [jnp.asarray(_tiled3(np.concatenate(cum_ssd, 0)), bf16),
              jnp.asarray(_tiled3(np.concatenate(cum_gla, 0)), bf16),
              jnp.asarray(np.stack(pair_mask, 0), f32),
              jnp.asarray((hv[:, None] == hq[None, :]).astype(np.float32), f32),
              jnp.asarray(np.concatenate([expand] * 3, 0), bf16)]
    return consts


PROMPT_STREAMS = 8
SAMPLE_STREAMS = 1
N_SHARED_REFS = 14
EPILOGUE = "epilogue"


def _round_robin(streams):
    live = list(streams)
    waiting = []
    while live:
        for s in list(live):
            try:
                if next(s) == EPILOGUE:
                    live.remove(s)
                    waiting.append(s)
            except StopIteration:
                live.remove(s)
    for s in waiting:
        for _ in s:
            pass


def _mixer_body(seq_rows, n_cast, *refs):
    is_sample = seq_rows < CHUNK
    n_streams = refs[0].shape[0]
    n_in = 5 if is_sample else 3
    n_out = 4 if is_sample else 3
    n_shared = N_SHARED_REFS
    ins, shared = refs[:n_in], refs[n_in:n_in + n_shared]
    cast_in = refs[n_in + n_shared:n_in + n_shared + n_cast]
    outs = refs[n_in + n_shared + n_cast:n_in + n_shared + n_cast + n_out]
    cast_out = refs[n_in + n_shared + n_cast + n_out:n_in + n_shared + 2 * n_cast + n_out]
    scratch = refs[n_in + n_shared + 2 * n_cast + n_out:]
    is_last = None
    if not is_sample:
        st_ref, stbd_ref = scratch
        c_idx = pl.program_id(1)
        is_last = c_idx == pl.num_programs(1) - 1

        @pl.when(pl.program_id(0) * pl.num_programs(1) + c_idx < N_LOAD)
        def _():
            for src, dst in zip(cast_in, cast_out):
                dst[...] = src[...].astype(bf16)

        @pl.when(c_idx == 0)
        def _():
            st_ref[...] = jnp.zeros(st_ref.shape, f32)
            stbd_ref[...] = jnp.zeros(stbd_ref.shape, f32)

    at = lambda group, j: [r.at[j] for r in group]
    _round_robin([_mixer_phases(seq_rows, at(ins, j), shared, at(outs, j), at(scratch, j), is_last)
                  for j in range(n_streams)])


def _mixer_phases(seq_rows, ins, shared, outs, scratch, is_last):
    is_sample = seq_rows < CHUNK
    n_seq = CHUNK // seq_rows
    halves = _level_halves(seq_rows)
    n_lvl = len(halves)
    if is_sample:
        pa_ref, pb_ref, ssd0_ref, conv0_ref, gla0_ref = ins
        (cbuf,) = scratch
        mixed_ref, nconv_ref, nssd_ref, ngla_ref = outs
    else:
        pa_ref, pb_ref, xc_ref = ins
        st_ref, stbd_ref = scratch
        mixed_ref, nssd_ref, ngla_ref = outs
    (convw_ref, convb_ref, dsk_e_ref, gssd_ref, dtb_c_ref, alog_c_ref, wgk_ref, bgk_ref, ggla_ref,
     cum_ssd_ref, cum_gla_ref, pair_mask_ref, bdmask_ref, expand_ref) = shared

    C = CHUNK
    rows = lax.broadcasted_iota(jnp.int32, (C, 1), 0)
    lane_qk = lax.shift_right_logical(lax.broadcasted_iota(jnp.int32, (1, D_QK), 1), int(np.log2(GLA_DK)))

    if is_sample:
        conv = jnp.broadcast_to(convb_ref[...], (C, CONV_DIM))
        for tap in range(CONV_W - 1):
            cbuf[:, SUBLANES - (CONV_W - 1) + tap, :] = conv0_ref[tap]
        cbuf[:, SUBLANES:SUBLANES + seq_rows, :] = (
            pa_ref[:, OFF_XBC:OFF_XBC + CONV_DIM].astype(f32).reshape(n_seq, seq_rows, CONV_DIM))
        for i in range(CONV_W):
            shift = CONV_W - 1 - i
            win = cbuf[:, SUBLANES - shift:SUBLANES - shift + seq_rows, :].reshape(C, CONV_DIM)
            conv = conv + win * convw_ref[i:i + 1, :]
        for tap in range(CONV_W - 1):
            nconv_ref[tap] = cbuf[:, SUBLANES + seq_rows - (CONV_W - 1) + tap, :]
        xc = jax.nn.silu(conv)
    else:
        xc = xc_ref[...]
    xs = xc[:, :D_SSD].astype(f32)
    bm = xc[:, D_SSD:D_SSD + SSD_GROUPS * SSD_STATE].astype(bf16)
    cm = xc[:, D_SSD + SSD_GROUPS * SSD_STATE:].astype(bf16)

    small = pb_ref[...]
    dtp_c = jax.nn.softplus(small + dtb_c_ref[...])
    cums = _sel_dot(cum_ssd_ref[...], dtp_c * (-LOG2E * jnp.exp(alog_c_ref[...])))
    yield
    acum_c = cums[:C]
    acum_t = acum_c.T
    wide = _dot(jnp.concatenate(_split3(jnp.concatenate([cums, dtp_c], axis=0)), axis=1), expand_ref[...])
    yield
    acum_e, dtp_e = wide[:C], wide[-C:]
    acum_last = wide[C:2 * C] if is_sample else acum_e[C - 1:C, :]
    causal = cum_ssd_ref[0:C, 0:C].astype(f32) > 0
    xdt = xs * dtp_e
    xdt_bf = xdt.astype(bf16)
    lane = lax.broadcasted_iota(jnp.int32, (1, LANES), 1)
    left = lane < SSD_HEAD_DIM
    heads_per_group = N_SSD_HEADS // SSD_GROUPS
    gw = heads_per_group * SSD_HEAD_DIM
    y_parts = []
    for grp in range(SSD_GROUPS):
        cb = _dot_nt(cm[:, grp * SSD_STATE:(grp + 1) * SSD_STATE], bm[:, grp * SSD_STATE:(grp + 1) * SSD_STATE])
        for pair in range(heads_per_group // 2):
            sc = []
            for hh in range(2):
                h = grp * heads_per_group + pair * 2 + hh
                seg = acum_c[:, h:h + 1] - acum_t[h:h + 1, :]
                sc.append((cb * jnp.exp2(jnp.where(causal, seg, -jnp.inf))).astype(bf16))
            lo = (grp * heads_per_group + pair * 2) * SSD_HEAD_DIM
            xp = xdt_bf[:, lo:lo + LANES]
            bd = jnp.concatenate([jnp.where(left, xp, 0), jnp.where(left, 0, xp)], axis=0)
            y_parts.append(_dot(jnp.concatenate(sc, axis=1), bd))
            yield
    y = jnp.concatenate(y_parts, axis=1)

    to_end = jnp.exp2(jnp.minimum(acum_last - acum_e, 0.0))
    xs_end = (xdt * to_end).astype(bf16)
    e_acum = jnp.exp2(acum_e)
    bm_t = bm.T

    def ssd_inter(st):
        st_bf = st.astype(bf16)
        return jnp.concatenate([_dot(cm[:, grp * SSD_STATE:(grp + 1) * SSD_STATE], st_bf[:, grp * gw:(grp + 1) * gw])
                                for grp in range(SSD_GROUPS)], axis=1) * e_acum

    def ssd_update(st, decay_row, xs_rows):
        upd = jnp.concatenate([_dot(bm_t[grp * SSD_STATE:(grp + 1) * SSD_STATE, :], xs_rows[:, grp * gw:(grp + 1) * gw])
                               for grp in range(SSD_GROUPS)], axis=1)
        return st * decay_row + upd

    q = pa_ref[:, OFF_Q:OFF_Q + D_QK].astype(f32) * (GLA_DK ** -0.5)
    k = pa_ref[:, OFF_K:OFF_K + D_QK].astype(f32)
    v_bf = pa_ref[:, OFF_V:OFF_V + D_GLA]
    gk = _dot_x3(small, wgk_ref[...]) + bgk_ref[...]
    log_a = jax.nn.log_sigmoid(gk) * (LOG2E / GLA_GATE_NORMALIZER)
    bsel = _sel_dot(cum_gla_ref[...], log_a)
    yield
    bcum = bsel[:C]
    n_mm = len(_matmul_levels(seq_rows))
    b_last = bsel[(n_mm + 1) * C:] if is_sample else bcum[C - 1:C, :]
    k_bf = k.astype(bf16)

    def b_ref(lvl, half):
        if lvl < n_mm:
            return bsel[(lvl + 1) * C:(lvl + 2) * C]
        return jnp.concatenate([jnp.broadcast_to(bcum[r + half:r + half + 1, :], (2 * half, D_QK))
                                for r in range(0, C, 2 * half)], axis=0)

    def head_rows(x_bf):
        return jnp.concatenate([jnp.where(lane_qk == h, x_bf, jnp.zeros_like(x_bf)) for h in range(N_GLA_HEADS)], axis=0)

    a = _dot_nt(head_rows(q.astype(bf16)), k_bf)
    yield
    m = pair_mask_ref[0]
    att = [a[h * C:(h + 1) * C] * m for h in range(N_GLA_HEADS)]
    for lvl, half in enumerate(halves):
        decay = jnp.exp2(-jnp.abs(bcum - b_ref(lvl, half)))
        in_right = (rows & half) != 0
        u = jnp.where(in_right, q, k) * decay
        u_bf = u.astype(bf16)
        m = pair_mask_ref[lvl + 1]
        if half < SUBLANES:
            a = _dot_nt(head_rows(u_bf), u_bf)
            yield
            for h in range(N_GLA_HEADS):
                att[h] = att[h] + a[h * C:(h + 1) * C] * m
        else:
            starts = range(half, C, 2 * half)
            pick = lambda x: jnp.concatenate([x[r:r + half] for r in starts], axis=0)
            a = _dot_nt(head_rows(pick(u).astype(bf16)), u_bf)
            yield
            m_right = pick(m)
            for h in range(N_GLA_HEADS):
                upd = a[h * (C // 2):(h + 1) * (C // 2)] * m_right
                parts = []
                for i, r in enumerate(starts):
                    parts += [att[h][r - half:r], att[h][r:r + half] + upd[i * half:(i + 1) * half]]
                att[h] = jnp.concatenate(parts, axis=0)
    o = jnp.concatenate([_dot(att[h].astype(bf16), v_bf[:, h * GLA_DV:(h + 1) * GLA_DV])
                         for h in range(N_GLA_HEADS)], axis=1)
    yield

    q_in = (q * jnp.exp2(bcum)).astype(bf16)
    kd = (k * jnp.exp2(jnp.minimum(b_last - bcum, 0.0))).astype(bf16)
    v_t = v_bf.T
    bdmask = bdmask_ref[...]

    def gla_inter(stbd):
        return _dot_nt(q_in, stbd.astype(bf16))

    def gla_update(stbd, decay_row, kd_rows):
        return stbd * decay_row + bdmask * _dot(v_t, kd_rows)

    def gla_heads(stbd):
        acc = jnp.where(lane_qk == 0, stbd[0:GLA_DV, :], 0.0)
        for h in range(1, N_GLA_HEADS):
            acc = acc + jnp.where(lane_qk == h, stbd[h * GLA_DV:(h + 1) * GLA_DV, :], 0.0)
        return acc.T

    if not is_sample:
        st = st_ref[...]
        stbd = stbd_ref[...]
        y = y + ssd_inter(st)
        o = o + gla_inter(stbd)
        yield
        st_ref[...] = ssd_update(st, jnp.exp2(acum_last), xs_end)
        stbd_ref[...] = gla_update(stbd, jnp.exp2(b_last), kd)
        yield
    else:
        seq_of_row = lax.shift_right_logical(rows, int(np.log2(seq_rows)))
        bm_f32 = xc[:, D_SSD:D_SSD + SSD_GROUPS * SSD_STATE]
        cm_f32 = xc[:, D_SSD + SSD_GROUPS * SSD_STATE:]
        xs_end_f32 = xdt * to_end
        e_last_c = jnp.exp2(cums[C:])
        q_in_f32 = q * jnp.exp2(bcum)
        y_rows, o_rows = [], []
        g0_next = gla0_ref[0].T
        for s in range(n_seq):
            g0 = g0_next
            if s + 1 < n_seq:
                g0_next = gla0_ref[s + 1].T
            mine = seq_of_row == s
            r0 = s * seq_rows
            own = slice(r0, r0 + seq_rows)
            st = ssd0_ref[s]
            c_rows = cm_f32[own].astype(bf16)
            b_rows = bm_f32[own].astype(bf16)
            x_rows = xs_end_f32[own].astype(bf16)
            y_grp, st_new = [], []
            for grp in range(SSD_GROUPS):
                st_g = st[grp * gw:(grp + 1) * gw]
                y_grp.append(_dot_nt(c_rows[:, grp * SSD_STATE:(grp + 1) * SSD_STATE], st_g.astype(bf16)))
                upd = lax.dot_general(x_rows[:, grp * gw:(grp + 1) * gw], b_rows[:, grp * SSD_STATE:(grp + 1) * SSD_STATE],
                                      (((0,), (0,)), ((), ())), preferred_element_type=f32)
                for hh in range(heads_per_group):
                    h = grp * heads_per_group + hh
                    rows_h = slice(hh * SSD_HEAD_DIM, (hh + 1) * SSD_HEAD_DIM)
                    decay_h = jnp.broadcast_to(e_last_c[r0:r0 + 1, h:h + 1], (SSD_HEAD_DIM, SSD_STATE))
                    st_new.append(st_g[rows_h] * decay_h + upd[rows_h])
            y_rows.append(jnp.concatenate(y_grp, axis=1))
            nssd_ref[s] = jnp.concatenate(st_new, axis=0)
            stbd = jnp.concatenate([jnp.where(lane_qk == h, g0, 0.0) for h in range(N_GLA_HEADS)], axis=0)
            o_rows.append(_dot_nt(q_in_f32[r0:r0 + seq_rows].astype(bf16), stbd.astype(bf16)))
            ngla_ref[s] = gla_heads(gla_update(stbd, jnp.exp2(b_last[r0:r0 + 1, :]), jnp.where(mine, kd, 0)))
            yield
        y = y + jnp.concatenate(y_rows, axis=0) * e_acum
        o = o + jnp.concatenate(o_rows, axis=0)

    z = pa_ref[:, OFF_Z:OFF_Z + D_SSD].astype(f32)
    g = pa_ref[:, OFF_G:OFF_G + D_GLA].astype(f32)
    y = (y + dsk_e_ref[...] * xs) * jax.nn.silu(z)
    gsz = D_SSD // SSD_GROUPS
    y = jnp.concatenate([_rms(y[:, i * gsz:(i + 1) * gsz], gssd_ref[:, i * gsz:(i + 1) * gsz])
                         for i in range(SSD_GROUPS)], axis=1)
    o = jnp.concatenate([_rms(o[:, h * GLA_DV:(h + 1) * GLA_DV], ggla_ref[...]) for h in range(N_GLA_HEADS)], axis=1)
    o = o * jax.nn.silu(g)
    mixed_ref[:, :D_SSD] = y.astype(bf16)
    mixed_ref[:, D_SSD:] = o.astype(bf16)

    if not is_sample:
        yield EPILOGUE

        @pl.when(is_last)
        def _():
            nssd_ref[0] = st_ref[...].T
            ngla_ref[0] = gla_heads(stbd_ref[...])


def _mixer(pa, pb, params, states, n_batch, seq_len, xc=None, cast_weights=()):
    is_sample = states is not None
    assert not (is_sample and cast_weights) and is_sample == (xc is None)
    seq_rows = seq_len if is_sample else CHUNK
    assert CHUNK % seq_rows == 0 and (is_sample or seq_len % CHUNK == 0)
    n_seq = CHUNK // seq_rows
    consts = _mixer_consts(seq_rows)
    ns = SAMPLE_STREAMS if is_sample else PROMPT_STREAMS
    assert n_batch % (ns * n_seq) == 0
    rows_per_stream = n_batch * seq_len // ns
    seqs_per_stream = n_batch // ns

    if is_sample:
        grid = (seqs_per_stream // n_seq,)
        row_map = lambda c: (0, c, 0)
        seq_map = lambda c: (0, c, 0, 0)
        full = lambda shape: pl.BlockSpec(shape, lambda c: (0,) * len(shape))
        sems = ("arbitrary",)
    else:
        n_chunks = seq_len // CHUNK
        grid = (seqs_per_stream, n_chunks)
        row_map = lambda b, c: (0, b * n_chunks + c, 0)
        seq_map = lambda b, c: (0, b, 0, 0)
        full = lambda shape: pl.BlockSpec(shape, lambda b, c: (0,) * len(shape))
        sems = ("arbitrary", "arbitrary")

    def streamed(arr):
        return arr.reshape((ns, arr.shape[0] // ns) + arr.shape[1:])

    taps = CONV_W - 1
    conv_spec = pl.BlockSpec((ns, taps, n_seq, CONV_DIM), lambda *g: (0, 0, seq_map(*g)[1], 0))
    state_blocks = [(n_seq, D_SSD, SSD_STATE), (n_seq, D_QK, GLA_DV)]
    in_arrays = [streamed(pa), streamed(pb)]
    in_specs = [pl.BlockSpec((ns, CHUNK, PA_COLS), row_map), pl.BlockSpec((ns, CHUNK, PB_COLS), row_map)]
    if is_sample:
        ssd0, conv0, gla0 = states
        conv0_t = jnp.transpose(streamed(conv0), (0, 2, 1, 3))
        in_arrays += [streamed(ssd0), conv0_t, streamed(gla0)]
        in_specs += [pl.BlockSpec((ns,) + state_blocks[0], seq_map), conv_spec,
                     pl.BlockSpec((ns,) + state_blocks[1], seq_map)]
    else:
        in_arrays.append(streamed(xc))
        in_specs.append(pl.BlockSpec((ns, CHUNK, CONV_DIM), row_map))
    for arr in list(params) + consts:
        in_arrays.append(arr)
        in_specs.append(full(arr.shape))

    out_shape = [jax.ShapeDtypeStruct((ns, rows_per_stream, D_MODEL), bf16)]
    out_specs = [pl.BlockSpec((ns, CHUNK, D_MODEL), row_map)]
    if is_sample:
        out_shape.append(jax.ShapeDtypeStruct((ns, taps, seqs_per_stream, CONV_DIM), f32))
        out_specs.append(conv_spec)
    out_shape += [jax.ShapeDtypeStruct((ns, seqs_per_stream) + blk[1:], f32) for blk in state_blocks]
    out_specs += [pl.BlockSpec((ns,) + blk, seq_map) for blk in state_blocks]
    for w in cast_weights:
        _, w_rows, w_cols = w.shape
        slab = w_rows // N_LOAD
        assert w_rows % N_LOAD == 0 and slab % BF16_ROWS == 0 and grid[0] * grid[1] >= N_LOAD
        step = lambda b, c: jnp.minimum(b * n_chunks + c, N_LOAD - 1)
        in_arrays.append(w)
        in_specs.append(pl.BlockSpec((None, slab, w_cols), lambda b, c: (0, step(b, c), 0)))
        out_shape.append(jax.ShapeDtypeStruct((w_rows, w_cols), bf16))
        out_specs.append(pl.BlockSpec((slab, w_cols), lambda b, c: (step(b, c), 0)))
    if is_sample:
        scratch = [pltpu.VMEM((ns, n_seq, SUBLANES + seq_rows, CONV_DIM), f32)]
    else:
        scratch = [pltpu.VMEM((ns, SSD_STATE, D_SSD), f32), pltpu.VMEM((ns, D_GLA, D_QK), f32)]
    results = pl.pallas_call(
        functools.partial(_mixer_body, seq_rows, len(cast_weights)),
        grid=grid, in_specs=in_specs, out_specs=out_specs, out_shape=out_shape, scratch_shapes=scratch,
        compiler_params=pltpu.CompilerParams(dimension_semantics=sems, vmem_limit_bytes=VMEM_LIMIT_BYTES),
        name="mixer_sample" if is_sample else "mixer_prompt",
    )(*in_arrays)
    merged = lambda arr: arr.reshape((arr.shape[0] * arr.shape[1],) + arr.shape[2:])
    if is_sample:
        mixed, nconv, nssd, ngla = results
        return merged(mixed), merged(jnp.transpose(nconv, (0, 2, 1, 3))), merged(nssd), merged(ngla)
    mixed, nssd, ngla, *cast = results
    return merged(mixed), merged(nssd), merged(ngla), cast


def _row(vec):
    return vec.reshape(1, -1).astype(f32)


def _pad_lanes(vec, width):
    return jnp.concatenate([vec.astype(f32), jnp.zeros((width - vec.shape[0],), f32)]).reshape(1, width)


def _mixer_params(conv_w, conv_b, dt_bias, a_log, d_skip, g_ssd_norm, w_gk2, b_gk, g_gla_norm):
    rep = lambda vec: _row(jnp.repeat(vec, SSD_HEAD_DIM))
    wgk = jnp.zeros((PB_COLS, D_QK), f32).at[N_SSD_HEADS:N_SSD_HEADS + GLA_RANK, :].set(w_gk2.astype(f32))
    return [conv_w.astype(f32), _row(conv_b), rep(d_skip), _row(g_ssd_norm),
            _pad_lanes(dt_bias, PB_COLS), _pad_lanes(a_log, PB_COLS), _stack_x3(wgk), _row(b_gk), _row(g_gla_norm)]


def kernel(x_prompt, x_sample, state_ssd, state_conv, state_gla, g_ffn1, w_ffn1_in, w_ffn1_out, g_mix, w_in, conv_w, conv_b, dt_bias, a_log, d_skip, g_ssd_norm, w_gk2, b_gk, g_gla_norm, w_out, g_ffn2, w_ffn2_in, w_ffn2_out, g_final):
    bp, lp, _ = x_prompt.shape
    bs, ls, _ = x_sample.shape
    assert w_in.shape[0] == 1, "single-layer step: the final norm is fused into the layer's last kernel"
    xp = x_prompt.reshape(bp * lp, D_MODEL)
    xs = x_sample.reshape(bs * ls, D_MODEL)
    params = _mixer_params(conv_w[0], conv_b[0], dt_bias[0], a_log[0], d_skip[0], g_ssd_norm[0],
                           w_gk2[0], b_gk[0], g_gla_norm[0])
    x1p, x1s, pap, pas, pbp, pbs, xcp, p_conv = _token_a(
        xp, xs, _row(g_ffn1[0]), w_ffn1_in.astype(f32), w_ffn1_out.astype(f32), _row(g_mix[0]), w_in.astype(f32),
        params[0], params[1], lp)
    mxp, p_ssd, p_gla, (w_out_bf, w_up2_bf, w_dn2_bf) = _mixer(
        pap, pbp, params, None, bp, lp, xc=xcp,
        cast_weights=(w_out.astype(f32), w_ffn2_in.astype(f32), w_ffn2_out.astype(f32)))
    s_states = (state_ssd[0].reshape(bs, D_SSD, SSD_STATE), state_conv[0], state_gla[0].reshape(bs, D_QK, GLA_DV))
    mxs, s_conv, s_ssd, s_gla = _mixer(pas, pbs, params, s_states, bs, ls)
    yp, ys = _token_b(x1p, x1s, mxp, mxs, w_out_bf, _row(g_ffn2[0]), w_up2_bf, w_dn2_bf, _row(g_final))
    return (yp.reshape(bp, lp, D_MODEL), ys.reshape(bs, ls, D_MODEL),
            p_ssd.reshape(1, bp, N_SSD_HEADS, SSD_HEAD_DIM, SSD_STATE), p_conv[None],
            p_gla.reshape(1, bp, N_GLA_HEADS, GLA_DK, GLA_DV),
            s_ssd.reshape(1, bs, N_SSD_HEADS, SSD_HEAD_DIM, SSD_STATE), s_conv[None],
            s_gla.reshape(1, bs, N_GLA_HEADS, GLA_DK, GLA_DV))
```

```python
import functools

import jax
import jax.numpy as jnp
import numpy as np
from jax import lax
from jax.experimental import pallas as pl
from jax.experimental.pallas import tpu as pltpu

f32 = jnp.float32
bf16 = jnp.bfloat16

D_MODEL = 1024
D_SSD = 512
SSD_HEAD_DIM = 64
N_SSD_HEADS = 8
SSD_STATE = 128
SSD_GROUPS = 2
CONV_W = 4
CONV_DIM = D_SSD + 2 * SSD_GROUPS * SSD_STATE
D_GLA = 512
N_GLA_HEADS = 4
GLA_DV = 128
GLA_DK = 64
GLA_RANK = 16
GLA_GATE_NORMALIZER = 16.0
D_FF = 2816
EPS = 1e-6
D_QK = N_GLA_HEADS * GLA_DK
LOG2E = float(np.log2(np.e))

SUBLANES = 8
LANES = 128
BF16_ROWS = 16
VMEM_LIMIT_BYTES = 60 * 1024 * 1024

CHUNK = 128
TOKEN_TILE = 256
TOKEN_B_TILE = 512

PA_COLS = D_SSD + CONV_DIM + 2 * D_QK + 2 * D_GLA
PB_COLS = LANES
OFF_Z, OFF_XBC, OFF_Q, OFF_K, OFF_V, OFF_G = 0, 512, 1536, 1792, 2048, 2560


def _dot(a, b):
    return jnp.dot(a, b, preferred_element_type=f32)


def _dot_nt(a, b):
    return lax.dot_general(a, b, (((1,), (1,)), ((), ())), preferred_element_type=f32)


def _split3(x):
    hi = x.astype(bf16)
    r1 = x - hi.astype(f32)
    mid = r1.astype(bf16)
    lo = (r1 - mid.astype(f32)).astype(bf16)
    return hi, mid, lo


def _sel_dot(sel3, x):
    return _dot(sel3, jnp.concatenate(_split3(x), axis=0))


def _tiled3(sel):
    return np.concatenate([sel] * 3, axis=1)


def _split2(x):
    hi = x.astype(bf16)
    return hi, (x - hi.astype(f32)).astype(bf16)


def _dot_x3(a, b_stack):
    a_hi, a_lo = _split2(a)
    return _dot(jnp.concatenate([a_hi, a_hi, a_lo], axis=1), b_stack)


def _stack_x3(b):
    b_hi, b_lo = _split2(b)
    return jnp.concatenate([b_hi, b_lo, b_hi], axis=0)


def _rms(x, g):
    return x * lax.rsqrt(jnp.mean(x * x, axis=-1, keepdims=True) + EPS) * g


N_LOAD = 16

W_IN_DT = D_SSD + CONV_DIM
W_IN_Q = W_IN_DT + N_SSD_HEADS
W_IN_LR = W_IN_Q + 2 * D_QK + 2 * D_GLA
W_IN_COLS = W_IN_LR + GLA_RANK


def _load_slab(step, slab_ref, dst_ref, regroup=None):
    rows = slab_ref.shape[0]
    assert rows % BF16_ROWS == 0
    val = slab_ref[...]
    if regroup is not None:
        val = regroup(val)
    dst_ref[pl.ds(pl.multiple_of(step * rows, BF16_ROWS), rows), :] = val.astype(bf16)


def _regroup_w_in(w):
    small = jnp.concatenate([w[:, W_IN_DT:W_IN_Q], w[:, W_IN_LR:W_IN_COLS],
                             jnp.zeros((w.shape[0], PB_COLS - N_SSD_HEADS - GLA_RANK), w.dtype)], axis=1)
    return jnp.concatenate([w[:, :W_IN_DT], w[:, W_IN_Q:W_IN_LR], small], axis=1)


ROW_STREAMS = 2


def _token_a_phases(row0, n_rows, x_ref, x1_ref, pa_ref, pb_ref, g1, wup, wdn, gmix, win, under_first_matmul=None,
                    keep_xbc=None):
    rows = pl.ds(row0, n_rows)
    x = x_ref[rows, :]
    gu = _dot(_rms(x, g1[...]).astype(bf16), wup[...])
    if under_first_matmul is not None:
        under_first_matmul()
    yield
    act = (jax.nn.silu(gu[:, :D_FF]) * gu[:, D_FF:]).astype(bf16)
    x1 = x + 0.5 * _dot(act, wdn[...])
    yield
    x1_ref[rows, :] = x1
    pr = _dot(_rms(x1, gmix[...]).astype(bf16), win[...])
    yield
    pa_ref[rows, :] = pr[:, :PA_COLS].astype(bf16)
    pb_ref[rows, :] = pr[:, PA_COLS:]
    if keep_xbc is not None:
        keep_xbc[pl.ds(SUBLANES + row0, n_rows), :] = pr[:, OFF_XBC:OFF_XBC + CONV_DIM]


def _token_a_body(n_prompt, tiles_per_seq, xp, xs, g1, wup_f, wdn_f, gmix, win_f, convw, convb,
                  x1p, x1s, pap, pas, pbp, pbs, xc, nconv, wup, wdn, win, cbuf):
    i = pl.program_id(0)
    tile = xp.shape[0]
    part = tile // ROW_STREAMS

    def shift_one_row(a):
        rot = pltpu.roll(a, 1, axis=1)
        before = jnp.concatenate([rot[-1:], rot[:-1]], axis=0)
        first = lax.broadcasted_iota(jnp.int32, (1, SUBLANES, 1), 1) == 0
        return jnp.where(first, before, rot)

    def conv_previous_tile():
        t = lax.rem(i - N_LOAD - 1, tiles_per_seq)
        cbuf[0:SUBLANES, :] = jnp.where(t == 0, 0.0, cbuf[0:SUBLANES, :])
        ext = cbuf[...].reshape(tile // SUBLANES + 1, SUBLANES, CONV_DIM)
        acc = ext * convw[0:1, :]
        for tap in range(1, CONV_W):
            acc = shift_one_row(acc) + ext * convw[tap:tap + 1, :]
        xc[...] = jax.nn.silu(acc[1:].reshape(tile, CONV_DIM) + convb[...]).astype(bf16)
        nconv[0] = cbuf[SUBLANES + tile - (CONV_W - 1):SUBLANES + tile, :]
        cbuf[0:SUBLANES, :] = cbuf[tile:tile + SUBLANES, :]

    def streams(x_ref, x1_ref, pa_ref, pb_ref, pending, keep_xbc):
        _round_robin([_token_a_phases(j * part, part, x_ref, x1_ref, pa_ref, pb_ref, g1, wup, wdn, gmix, win,
                                      pending if j == 0 else None, keep_xbc) for j in range(ROW_STREAMS)])

    @pl.when(i < N_LOAD)
    def _():
        _load_slab(i, wup_f, wup)
        _load_slab(i, wdn_f, wdn)
        _load_slab(i, win_f, win, _regroup_w_in)

    @pl.when(i == 0)
    def _():
        cbuf[...] = jnp.zeros(cbuf.shape, f32)

    first_sample = N_LOAD + n_prompt
    pl.when(jnp.logical_and(i >= N_LOAD, i < first_sample))(
        lambda: streams(xp, x1p, pap, pbp, conv_previous_tile, cbuf))
    pl.when(i == first_sample)(conv_previous_tile)
    pl.when(i >= first_sample)(lambda: streams(xs, x1s, pas, pbs, None, None))


def _token_b_phases(rows, x_ref, m_ref, y_ref, wout, g2, wup, wdn, gfin):
    x2 = x_ref[rows, :] + _dot(m_ref[rows, :], wout[...])
    yield
    gu = _dot(_rms(x2, g2[...]).astype(bf16), wup[...])
    yield
    act = (jax.nn.silu(gu[:, :D_FF]) * gu[:, D_FF:]).astype(bf16)
    x3 = x2 + 0.5 * _dot(act, wdn[...])
    yield
    y_ref[rows, :] = _rms(x3, gfin[...])


def _token_b_body(n_prompt, x1p, x1s, mxp, mxs, wout, g2, wup, wdn, gfin, yp, ys):
    def compute(x_ref, m_ref, y_ref):
        part = x_ref.shape[0] // ROW_STREAMS
        _round_robin([_token_b_phases(pl.ds(j * part, part), x_ref, m_ref, y_ref, wout, g2, wup, wdn, gfin)
                      for j in range(ROW_STREAMS)])

    i = pl.program_id(0)
    pl.when(i < n_prompt)(lambda: compute(x1p, mxp, yp))
    pl.when(i >= n_prompt)(lambda: compute(x1s, mxs, ys))


def _two_group_specs(n_prompt, cols, tile, first=0):
    prompt = pl.BlockSpec((tile, cols), lambda i: (jnp.clip(i - first, 0, n_prompt - 1), 0))
    sample = pl.BlockSpec((tile, cols), lambda i: (jnp.maximum(i - first - n_prompt, 0), 0))
    return prompt, sample


def _whole(shape):
    return pl.BlockSpec(shape, lambda i: (0,) * len(shape), pipeline_mode=pl.Buffered(1))


def _slabs(weight):
    _, rows, cols = weight.shape
    assert rows % N_LOAD == 0
    return pl.BlockSpec((None, rows // N_LOAD, cols), lambda i: (0, jnp.minimum(i, N_LOAD - 1), 0))


def _token_a(xp, xs, g1, wup, wdn, gmix, win, conv_w, conv_b, prompt_len):
    tp, ts = xp.shape[0], xs.shape[0]
    tile = TOKEN_TILE
    n_prompt, n_sample = tp // tile, ts // tile
    assert tp % tile == 0 and ts % tile == 0 and win.shape[2] == W_IN_COLS and prompt_len % tile == 0
    tiles_per_seq = prompt_len // tile
    n_seq = tp // prompt_len
    xin = _two_group_specs(n_prompt, D_MODEL, tile, N_LOAD)
    pa = _two_group_specs(n_prompt, PA_COLS, tile, N_LOAD)
    pb = _two_group_specs(n_prompt, PB_COLS, tile, N_LOAD)
    conv_tile = lambda i: jnp.clip(i - N_LOAD - 1, 0, n_prompt - 1)
    return pl.pallas_call(
        functools.partial(_token_a_body, n_prompt, tiles_per_seq),
        grid=(N_LOAD + n_prompt + n_sample,),
        in_specs=[*xin, _whole(g1.shape), _slabs(wup), _slabs(wdn), _whole(gmix.shape), _slabs(win),
                  _whole(conv_w.shape), _whole(conv_b.shape)],
        out_specs=[*xin, *pa, *pb, pl.BlockSpec((tile, CONV_DIM), lambda i: (conv_tile(i), 0)),
                   pl.BlockSpec((1, CONV_W - 1, CONV_DIM), lambda i: (conv_tile(i) // tiles_per_seq, 0, 0))],
        out_shape=[jax.ShapeDtypeStruct((tp, D_MODEL), f32), jax.ShapeDtypeStruct((ts, D_MODEL), f32),
                   jax.ShapeDtypeStruct((tp, PA_COLS), bf16), jax.ShapeDtypeStruct((ts, PA_COLS), bf16),
                   jax.ShapeDtypeStruct((tp, PB_COLS), f32), jax.ShapeDtypeStruct((ts, PB_COLS), f32),
                   jax.ShapeDtypeStruct((tp, CONV_DIM), bf16),
                   jax.ShapeDtypeStruct((n_seq, CONV_W - 1, CONV_DIM), f32)],
        scratch_shapes=[pltpu.VMEM(wup.shape[1:], bf16), pltpu.VMEM(wdn.shape[1:], bf16),
                        pltpu.VMEM((D_MODEL, PA_COLS + PB_COLS), bf16),
                        pltpu.VMEM((SUBLANES + tile, CONV_DIM), f32)],
        compiler_params=pltpu.CompilerParams(dimension_semantics=("arbitrary",), vmem_limit_bytes=VMEM_LIMIT_BYTES),
        name="token_a",
    )(xp, xs, g1, wup, wdn, gmix, win, conv_w, conv_b)


def _token_b(x1p, x1s, mxp, mxs, wout, g2, wup, wdn, gfin):
    tp, ts = x1p.shape[0], x1s.shape[0]
    tile = TOKEN_B_TILE
    assert tp % tile == 0 and ts % tile == 0
    n_prompt, n_sample = tp // tile, ts // tile
    xin = _two_group_specs(n_prompt, D_MODEL, tile)
    return pl.pallas_call(
        functools.partial(_token_b_body, n_prompt),
        grid=(n_prompt + n_sample,),
        in_specs=[*xin, *xin, _whole(wout.shape), _whole(g2.shape), _whole(wup.shape), _whole(wdn.shape),
                  _whole(gfin.shape)],
        out_specs=[*xin],
        out_shape=[jax.ShapeDtypeStruct((tp, D_MODEL), f32), jax.ShapeDtypeStruct((ts, D_MODEL), f32)],
        compiler_params=pltpu.CompilerParams(dimension_semantics=("arbitrary",), vmem_limit_bytes=VMEM_LIMIT_BYTES),
        name="token_b",
    )(x1p, x1s, mxp, mxs, wout, g2, wup, wdn, gfin)


def _level_halves(seq_rows):
    return [h for h in (1, 2, 4, 8, 16, 32, 64) if 2 * h <= seq_rows]


def _matmul_levels(seq_rows):
    return [h for h in _level_halves(seq_rows) if h < SUBLANES]


def _mixer_consts(seq_rows):
    t = np.arange(CHUNK)
    is_sample = seq_rows < CHUNK
    same = (t[:, None] // seq_rows) == (t[None, :] // seq_rows)
    tri = (same & (t[None, :] <= t[:, None])).astype(np.float32)
    last = ((t[:, None] // seq_rows) * seq_rows + seq_rows - 1 == t[None, :]).astype(np.float32)
    cum_gla, pair_mask = [tri], [np.eye(CHUNK, dtype=np.float32)]
    for h in _level_halves(seq_rows):
        ref_row = (t // (2 * h)) * (2 * h) + h
        if h in _matmul_levels(seq_rows):
            cum_gla.append((ref_row[:, None] == t[None, :]).astype(np.float32) @ tri)
        right = (t % (2 * h)) >= h
        blk = (t[:, None] // (2 * h)) == (t[None, :] // (2 * h))
        pair_mask.append((blk & right[:, None] & ~right[None, :]).astype(np.float32))
    cum_ssd = [tri]
    if is_sample:
        cum_gla.append(last @ tri)
        cum_ssd.append(last @ tri)
    hq = np.arange(D_QK) // GLA_DK
    hv = np.arange(D_GLA) // GLA_DV
    head_of_lane = np.arange(D_SSD) // SSD_HEAD_DIM
    expand = (np.arange(PB_COLS)[:, None] == head_of_lane[None, :]).astype(np.float32)
    consts = [jnp.asarray(_tiled3(np.concatenate(cum_ssd, 0)), bf16),
              jnp.asarray(_tiled3(np.concatenate(cum_gla, 0)), bf16),
              jnp.asarray(np.stack(pair_mask, 0), f32),
              jnp.asarray((hv[:, None] == hq[None, :]).astype(np.float32), f32),
              jnp.asarray(np.concatenate([expand] * 3, 0), bf16)]
    return consts


PROMPT_STREAMS = 8
SAMPLE_STREAMS = 1
N_SHARED_REFS = 14
EPILOGUE = "epilogue"


def _round_robin(streams):
    live = list(streams)
    waiting = []
    while live:
        for s in list(live):
            try:
                if next(s) == EPILOGUE:
                    live.remove(s)
                    waiting.append(s)
            except StopIteration:
                live.remove(s)
    for s in waiting:
        for _ in s:
            pass


def _mixer_body(seq_rows, n_cast, *refs):
    is_sample = seq_rows < CHUNK
    n_streams = refs[0].shape[0]
    n_in = 5 if is_sample else 3
    n_out = 4 if is_sample else 3
    n_shared = N_SHARED_REFS
    ins, shared = refs[:n_in], refs[n_in:n_in + n_shared]
    cast_in = refs[n_in + n_shared:n_in + n_shared + n_cast]
    outs = refs[n_in + n_shared + n_cast:n_in + n_shared + n_cast + n_out]
    cast_out = refs[n_in + n_shared + n_cast + n_out:n_in + n_shared + 2 * n_cast + n_out]
    scratch = refs[n_in + n_shared + 2 * n_cast + n_out:]
    is_last = None
    if not is_sample:
        st_ref, stbd_ref = scratch
        c_idx = pl.program_id(1)
        is_last = c_idx == pl.num_programs(1) - 1

        @pl.when(pl.program_id(0) * pl.num_programs(1) + c_idx < N_LOAD)
        def _():
            for src, dst in zip(cast_in, cast_out):
                dst[...] = src[...].astype(bf16)

        @pl.when(c_idx == 0)
        def _():
            st_ref[...] = jnp.zeros(st_ref.shape, f32)
            stbd_ref[...] = jnp.zeros(stbd_ref.shape, f32)

    at = lambda group, j: [r.at[j] for r in group]
    _round_robin([_mixer_phases(seq_rows, at(ins, j), shared, at(outs, j), at(scratch, j), is_last)
                  for j in range(n_streams)])


def _mixer_phases(seq_rows, ins, shared, outs, scratch, is_last):
    is_sample = seq_rows < CHUNK
    n_seq = CHUNK // seq_rows
    halves = _level_halves(seq_rows)
    n_lvl = len(halves)
    if is_sample:
        pa_ref, pb_ref, ssd0_ref, conv0_ref, gla0_ref = ins
        (cbuf,) = scratch
        mixed_ref, nconv_ref, nssd_ref, ngla_ref = outs
    else:
        pa_ref, pb_ref, xc_ref = ins
        st_ref, stbd_ref = scratch
        mixed_ref, nssd_ref, ngla_ref = outs
    (convw_ref, convb_ref, dsk_e_ref, gssd_ref, dtb_c_ref, alog_c_ref, wgk_ref, bgk_ref, ggla_ref,
     cum_ssd_ref, cum_gla_ref, pair_mask_ref, bdmask_ref, expand_ref) = shared

    C = CHUNK
    rows = lax.broadcasted_iota(jnp.int32, (C, 1), 0)
    lane_qk = lax.shift_right_logical(lax.broadcasted_iota(jnp.int32, (1, D_QK), 1), int(np.log2(GLA_DK)))

    if is_sample:
        conv = jnp.broadcast_to(convb_ref[...], (C, CONV_DIM))
        for tap in range(CONV_W - 1):
            cbuf[:, SUBLANES - (CONV_W - 1) + tap, :] = conv0_ref[tap]
        cbuf[:, SUBLANES:SUBLANES + seq_rows, :] = (
            pa_ref[:, OFF_XBC:OFF_XBC + CONV_DIM].astype(f32).reshape(n_seq, seq_rows, CONV_DIM))
        for i in range(CONV_W):
            shift = CONV_W - 1 - i
            win = cbuf[:, SUBLANES - shift:SUBLANES - shift + seq_rows, :].reshape(C, CONV_DIM)
            conv = conv + win * convw_ref[i:i + 1, :]
        for tap in range(CONV_W - 1):
            nconv_ref[tap] = cbuf[:, SUBLANES + seq_rows - (CONV_W - 1) + tap, :]
        xc = jax.nn.silu(conv)
    else:
        xc = xc_ref[...]
    xs = xc[:, :D_SSD].astype(f32)
    bm = xc[:, D_SSD:D_SSD + SSD_GROUPS * SSD_STATE].astype(bf16)
    cm = xc[:, D_SSD + SSD_GROUPS * SSD_STATE:].astype(bf16)

    small = pb_ref[...]
    dtp_c = jax.nn.softplus(small + dtb_c_ref[...])
    cums = _sel_dot(cum_ssd_ref[...], dtp_c * (-LOG2E * jnp.exp(alog_c_ref[...])))
    yield
    acum_c = cums[:C]
    acum_t = acum_c.T
    wide = _dot(jnp.concatenate(_split3(jnp.concatenate([cums, dtp_c], axis=0)), axis=1), expand_ref[...])
    yield
    acum_e, dtp_e = wide[:C], wide[-C:]
    acum_last = wide[C:2 * C] if is_sample else acum_e[C - 1:C, :]
    causal = cum_ssd_ref[0:C, 0:C].astype(f32) > 0
    xdt = xs * dtp_e
    xdt_bf = xdt.astype(bf16)
    lane = lax.broadcasted_iota(jnp.int32, (1, LANES), 1)
    left = lane < SSD_HEAD_DIM
    heads_per_group = N_SSD_HEADS // SSD_GROUPS
    gw = heads_per_group * SSD_HEAD_DIM
    y_parts = []
    for grp in range(SSD_GROUPS):
        cb = _dot_nt(cm[:, grp * SSD_STATE:(grp + 1) * SSD_STATE], bm[:, grp * SSD_STATE:(grp + 1) * SSD_STATE])
        for pair in range(heads_per_group // 2):
            sc = []
            for hh in range(2):
                h = grp * heads_per_group + pair * 2 + hh
                seg = acum_c[:, h:h + 1] - acum_t[h:h + 1, :]
                sc.append((cb * jnp.exp2(jnp.where(causal, seg, -jnp.inf))).astype(bf16))
            lo = (grp * heads_per_group + pair * 2) * SSD_HEAD_DIM
            xp = xdt_bf[:, lo:lo + LANES]
            bd = jnp.concatenate([jnp.where(left, xp, 0), jnp.where(left, 0, xp)], axis=0)
            y_parts.append(_dot(jnp.concatenate(sc, axis=1), bd))
            yield
    y = jnp.concatenate(y_parts, axis=1)

    to_end = jnp.exp2(jnp.minimum(acum_last - acum_e, 0.0))
    xs_end = (xdt * to_end).astype(bf16)
    e_acum = jnp.exp2(acum_e)
    bm_t = bm.T

    def ssd_inter(st):
        st_bf = st.astype(bf16)
        return jnp.concatenate([_dot(cm[:, grp * SSD_STATE:(grp + 1) * SSD_STATE], st_bf[:, grp * gw:(grp + 1) * gw])
                                for grp in range(SSD_GROUPS)], axis=1) * e_acum

    def ssd_update(st, decay_row, xs_rows):
        upd = jnp.concatenate([_dot(bm_t[grp * SSD_STATE:(grp + 1) * SSD_STATE, :], xs_rows[:, grp * gw:(grp + 1) * gw])
                               for grp in range(SSD_GROUPS)], axis=1)
        return st * decay_row + upd

    q = pa_ref[:, OFF_Q:OFF_Q + D_QK].astype(f32) * (GLA_DK ** -0.5)
    k = pa_ref[:, OFF_K:OFF_K + D_QK].astype(f32)
    v_bf = pa_ref[:, OFF_V:OFF_V + D_GLA]
    gk = _dot_x3(small, wgk_ref[...]) + bgk_ref[...]
    log_a = jax.nn.log_sigmoid(gk) * (LOG2E / GLA_GATE_NORMALIZER)
    bsel = _sel_dot(cum_gla_ref[...], log_a)
    yield
    bcum = bsel[:C]
    n_mm = len(_matmul_levels(seq_rows))
    b_last = bsel[(n_mm + 1) * C:] if is_sample else bcum[C - 1:C, :]
    k_bf = k.astype(bf16)

    def b_ref(lvl, half):
        if lvl < n_mm:
            return bsel[(lvl + 1) * C:(lvl + 2) * C]
        return jnp.concatenate([jnp.broadcast_to(bcum[r + half:r + half + 1, :], (2 * half, D_QK))
                                for r in range(0, C, 2 * half)], axis=0)

    def head_rows(x_bf):
        return jnp.concatenate([jnp.where(lane_qk == h, x_bf, jnp.zeros_like(x_bf)) for h in range(N_GLA_HEADS)], axis=0)

    a = _dot_nt(head_rows(q.astype(bf16)), k_bf)
    yield
    m = pair_mask_ref[0]
    att = [a[h * C:(h + 1) * C] * m for h in range(N_GLA_HEADS)]
    for lvl, half in enumerate(halves):
        decay = jnp.exp2(-jnp.abs(bcum - b_ref(lvl, half)))
        in_right = (rows & half) != 0
        u = jnp.where(in_right, q, k) * decay
        u_bf = u.astype(bf16)
        m = pair_mask_ref[lvl + 1]
        if half < SUBLANES:
            a = _dot_nt(head_rows(u_bf), u_bf)
            yield
            for h in range(N_GLA_HEADS):
                att[h] = att[h] + a[h * C:(h + 1) * C] * m
        else:
            starts = range(half, C, 2 * half)
            pick = lambda x: jnp.concatenate([x[r:r + half] for r in starts], axis=0)
            a = _dot_nt(head_rows(pick(u).astype(bf16)), u_bf)
            yield
            m_right = pick(m)
            for h in range(N_GLA_HEADS):
                upd = a[h * (C // 2):(h + 1) * (C // 2)] * m_right
                parts = []
                for i, r in enumerate(starts):
                    parts += [att[h][r - half:r], att[h][r:r + half] + upd[i * half:(i + 1) * half]]
                att[h] = jnp.concatenate(parts, axis=0)
    o = jnp.concatenate([_dot(att[h].astype(bf16), v_bf[:, h * GLA_DV:(h + 1) * GLA_DV])
                         for h in range(N_GLA_HEADS)], axis=1)
    yield

    q_in = (q * jnp.exp2(bcum)).astype(bf16)
    kd = (k * jnp.exp2(jnp.minimum(b_last - bcum, 0.0))).astype(bf16)
    v_t = v_bf.T
    bdmask = bdmask_ref[...]

    def gla_inter(stbd):
        return _dot_nt(q_in, stbd.astype(bf16))

    def gla_update(stbd, decay_row, kd_rows):
        return stbd * decay_row + bdmask * _dot(v_t, kd_rows)

    def gla_heads(stbd):
        acc = jnp.where(lane_qk == 0, stbd[0:GLA_DV, :], 0.0)
        for h in range(1, N_GLA_HEADS):
            acc = acc + jnp.where(lane_qk == h, stbd[h * GLA_DV:(h + 1) * GLA_DV, :], 0.0)
        return acc.T

    if not is_sample:
        st = st_ref[...]
        stbd = stbd_ref[...]
        y = y + ssd_inter(st)
        o = o + gla_inter(stbd)
        yield
        st_ref[...] = ssd_update(st, jnp.exp2(acum_last), xs_end)
        stbd_ref[...] = gla_update(stbd, jnp.exp2(b_last), kd)
        yield
    else:
        seq_of_row = lax.shift_right_logical(rows, int(np.log2(seq_rows)))
        bm_f32 = xc[:, D_SSD:D_SSD + SSD_GROUPS * SSD_STATE]
        cm_f32 = xc[:, D_SSD + SSD_GROUPS * SSD_STATE:]
        xs_end_f32 = xdt * to_end
        e_last_c = jnp.exp2(cums[C:])
        q_in_f32 = q * jnp.exp2(bcum)
        y_rows, o_rows = [], []
        g0_next = gla0_ref[0].T
        for s in range(n_seq):
            g0 = g0_next
            if s + 1 < n_seq:
                g0_next = gla0_ref[s + 1].T
            mine = seq_of_row == s
            r0 = s * seq_rows
            own = slice(r0, r0 + seq_rows)
            st = ssd0_ref[s]
            c_rows = cm_f32[own].astype(bf16)
            b_rows = bm_f32[own].astype(bf16)
            x_rows = xs_end_f32[own].astype(bf16)
            y_grp, st_new = [], []
            for grp in range(SSD_GROUPS):
                st_g = st[grp * gw:(grp + 1) * gw]
                y_grp.append(_dot_nt(c_rows[:, grp * SSD_STATE:(grp + 1) * SSD_STATE], st_g.astype(bf16)))
                upd = lax.dot_general(x_rows[:, grp * gw:(grp + 1) * gw], b_rows[:, grp * SSD_STATE:(grp + 1) * SSD_STATE],
                                      (((0,), (0,)), ((), ())), preferred_element_type=f32)
                for hh in range(heads_per_group):
                    h = grp * heads_per_group + hh
                    rows_h = slice(hh * SSD_HEAD_DIM, (hh + 1) * SSD_HEAD_DIM)
                    decay_h = jnp.broadcast_to(e_last_c[r0:r0 + 1, h:h + 1], (SSD_HEAD_DIM, SSD_STATE))
                    st_new.append(st_g[rows_h] * decay_h + upd[rows_h])
            y_rows.append(jnp.concatenate(y_grp, axis=1))
            nssd_ref[s] = jnp.concatenate(st_new, axis=0)
            stbd = jnp.concatenate([jnp.where(lane_qk == h, g0, 0.0) for h in range(N_GLA_HEADS)], axis=0)
            o_rows.append(_dot_nt(q_in_f32[r0:r0 + seq_rows].astype(bf16), stbd.astype(bf16)))
            ngla_ref[s] = gla_heads(gla_update(stbd, jnp.exp2(b_last[r0:r0 + 1, :]), jnp.where(mine, kd, 0)))
            yield
        y = y + jnp.concatenate(y_rows, axis=0) * e_acum
        o = o + jnp.concatenate(o_rows, axis=0)

    z = pa_ref[:, OFF_Z:OFF_Z + D_SSD].astype(f32)
    g = pa_ref[:, OFF_G:OFF_G + D_GLA].astype(f32)
    y = (y + dsk_e_ref[...] * xs) * jax.nn.silu(z)
    gsz = D_SSD // SSD_GROUPS
    y = jnp.concatenate([_rms(y[:, i * gsz:(i + 1) * gsz], gssd_ref[:, i * gsz:(i + 1) * gsz])
                         for i in range(SSD_GROUPS)], axis=1)
    o = jnp.concatenate([_rms(o[:, h * GLA_DV:(h + 1) * GLA_DV], ggla_ref[...]) for h in range(N_GLA_HEADS)], axis=1)
    o = o * jax.nn.silu(g)
    mixed_ref[:, :D_SSD] = y.astype(bf16)
    mixed_ref[:, D_SSD:] = o.astype(bf16)

    if not is_sample:
        yield EPILOGUE

        @pl.when(is_last)
        def _():
            nssd_ref[0] = st_ref[...].T
            ngla_ref[0] = gla_heads(stbd_ref[...])


def _mixer(pa, pb, params, states, n_batch, seq_len, xc=None, cast_weights=()):
    is_sample = states is not None
    assert not (is_sample and cast_weights) and is_sample == (xc is None)
    seq_rows = seq_len if is_sample else CHUNK
    assert CHUNK % seq_rows == 0 and (is_sample or seq_len % CHUNK == 0)
    n_seq = CHUNK // seq_rows
    consts = _mixer_consts(seq_rows)
    ns = SAMPLE_STREAMS if is_sample else PROMPT_STREAMS
    assert n_batch % (ns * n_seq) == 0
    rows_per_stream = n_batch * seq_len // ns
    seqs_per_stream = n_batch // ns

    if is_sample:
        grid = (seqs_per_stream // n_seq,)
        row_map = lambda c: (0, c, 0)
        seq_map = lambda c: (0, c, 0, 0)
        full = lambda shape: pl.BlockSpec(shape, lambda c: (0,) * len(shape))
        sems = ("arbitrary",)
    else:
        n_chunks = seq_len // CHUNK
        grid = (seqs_per_stream, n_chunks)
        row_map = lambda b, c: (0, b * n_chunks + c, 0)
        seq_map = lambda b, c: (0, b, 0, 0)
        full = lambda shape: pl.BlockSpec(shape, lambda b, c: (0,) * len(shape))
        sems = ("arbitrary", "arbitrary")

    def streamed(arr):
        return arr.reshape((ns, arr.shape[0] // ns) + arr.shape[1:])

    taps = CONV_W - 1
    conv_spec = pl.BlockSpec((ns, taps, n_seq, CONV_DIM), lambda *g: (0, 0, seq_map(*g)[1], 0))
    state_blocks = [(n_seq, D_SSD, SSD_STATE), (n_seq, D_QK, GLA_DV)]
    in_arrays = [streamed(pa), streamed(pb)]
    in_specs = [pl.BlockSpec((ns, CHUNK, PA_COLS), row_map), pl.BlockSpec((ns, CHUNK, PB_COLS), row_map)]
    if is_sample:
        ssd0, conv0, gla0 = states
        conv0_t = jnp.transpose(streamed(conv0), (0, 2, 1, 3))
        in_arrays += [streamed(ssd0), conv0_t, streamed(gla0)]
        in_specs += [pl.BlockSpec((ns,) + state_blocks[0], seq_map), conv_spec,
                     pl.BlockSpec((ns,) + state_blocks[1], seq_map)]
    else:
        in_arrays.append(streamed(xc))
        in_specs.append(pl.BlockSpec((ns, CHUNK, CONV_DIM), row_map))
    for arr in list(params) + consts:
        in_arrays.append(arr)
        in_specs.append(full(arr.shape))

    out_shape = [jax.ShapeDtypeStruct((ns, rows_per_stream, D_MODEL), bf16)]
    out_specs = [pl.BlockSpec((ns, CHUNK, D_MODEL), row_map)]
    if is_sample:
        out_shape.append(jax.ShapeDtypeStruct((ns, taps, seqs_per_stream, CONV_DIM), f32))
        out_specs.append(conv_spec)
    out_shape += [jax.ShapeDtypeStruct((ns, seqs_per_stream) + blk[1:], f32) for blk in state_blocks]
    out_specs += [pl.BlockSpec((ns,) + blk, seq_map) for blk in state_blocks]
    for w in cast_weights:
        _, w_rows, w_cols = w.shape
        slab = w_rows // N_LOAD
        assert w_rows % N_LOAD == 0 and slab % BF16_ROWS == 0 and grid[0] * grid[1] >= N_LOAD
        step = lambda b, c: jnp.minimum(b * n_chunks + c, N_LOAD - 1)
        in_arrays.append(w)
        in_specs.append(pl.BlockSpec((None, slab, w_cols), lambda b, c: (0, step(b, c), 0)))
        out_shape.append(jax.ShapeDtypeStruct((w_rows, w_cols), bf16))
        out_specs.append(pl.BlockSpec((slab, w_cols), lambda b, c: (step(b, c), 0)))
    if is_sample:
        scratch = [pltpu.VMEM((ns, n_seq, SUBLANES + seq_rows, CONV_DIM), f32)]
    else:
        scratch = [pltpu.VMEM((ns, SSD_STATE, D_SSD), f32), pltpu.VMEM((ns, D_GLA, D_QK), f32)]
    results = pl.pallas_call(
        functools.partial(_mixer_body, seq_rows, len(cast_weights)),
        grid=grid, in_specs=in_specs, out_specs=out_specs, out_shape=out_shape, scratch_shapes=scratch,
        compiler_params=pltpu.CompilerParams(dimension_semantics=sems, vmem_limit_bytes=VMEM_LIMIT_BYTES),
        name="mixer_sample" if is_sample else "mixer_prompt",
    )(*in_arrays)
    merged = lambda arr: arr.reshape((arr.shape[0] * arr.shape[1],) + arr.shape[2:])
    if is_sample:
        mixed, nconv, nssd, ngla = results
        return merged(mixed), merged(jnp.transpose(nconv, (0, 2, 1, 3))), merged(nssd), merged(ngla)
    mixed, nssd, ngla, *cast = results
    return merged(mixed), merged(nssd), merged(ngla), cast


def _row(vec):
    return vec.reshape(1, -1).astype(f32)


def _pad_lanes(vec, width):
    return jnp.concatenate([vec.astype(f32), jnp.zeros((width - vec.shape[0],), f32)]).reshape(1, width)


def _mixer_params(conv_w, conv_b, dt_bias, a_log, d_skip, g_ssd_norm, w_gk2, b_gk, g_gla_norm):
    rep = lambda vec: _row(jnp.repeat(vec, SSD_HEAD_DIM))
    wgk = jnp.zeros((PB_COLS, D_QK), f32).at[N_SSD_HEADS:N_SSD_HEADS + GLA_RANK, :].set(w_gk2.astype(f32))
    return [conv_w.astype(f32), _row(conv_b), rep(d_skip), _row(g_ssd_norm),
            _pad_lanes(dt_bias, PB_COLS), _pad_lanes(a_log, PB_COLS), _stack_x3(wgk), _row(b_gk), _row(g_gla_norm)]


def kernel(x_prompt, x_sample, state_ssd, state_conv, state_gla, g_ffn1, w_ffn1_in, w_ffn1_out, g_mix, w_in, conv_w, conv_b, dt_bias, a_log, d_skip, g_ssd_norm, w_gk2, b_gk, g_gla_norm, w_out, g_ffn2, w_ffn2_in, w_ffn2_out, g_final):
    bp, lp, _ = x_prompt.shape
    bs, ls, _ = x_sample.shape
    assert w_in.shape[0] == 1, "single-layer step: the final norm is fused into the layer's last kernel"
    xp = x_prompt.reshape(bp * lp, D_MODEL)
    xs = x_sample.reshape(bs * ls, D_MODEL)
    params = _mixer_params(conv_w[0], conv_b[0], dt_bias[0], a_log[0], d_skip[0], g_ssd_norm[0],
                           w_gk2[0], b_gk[0], g_gla_norm[0])
    x1p, x1s, pap, pas, pbp, pbs, xcp, p_conv = _token_a(
        xp, xs, _row(g_ffn1[0]), w_ffn1_in.astype(f32), w_ffn1_out.astype(f32), _row(g_mix[0]), w_in.astype(f32),
        params[0], params[1], lp)
    mxp, p_ssd, p_gla, (w_out_bf, w_up2_bf, w_dn2_bf) = _mixer(
        pap, pbp, params, None, bp, lp, xc=xcp,
        cast_weights=(w_out.astype(f32), w_ffn2_in.astype(f32), w_ffn2_out.astype(f32)))
    s_states = (state_ssd[0].reshape(bs, D_SSD, SSD_STATE), state_conv[0], state_gla[0].reshape(bs, D_QK, GLA_DV))
    mxs, s_conv, s_ssd, s_gla = _mixer(pas, pbs, params, s_states, bs, ls)
    yp, ys = _token_b(x1p, x1s, mxp, mxs, w_out_bf, _row(g_ffn2[0]), w_up2_bf, w_dn2_bf, _row(g_final))
    return (yp.reshape(bp, lp, D_MODEL), ys.reshape(bs, ls, D_MODEL),
            p_ssd.reshape(1, bp, N_SSD_HEADS, SSD_HEAD_DIM, SSD_STATE), p_conv[None],
            p_gla.reshape(1, bp, N_GLA_HEADS, GLA_DK, GLA_DV),
            s_ssd.reshape(1, bs, N_SSD_HEADS, SSD_HEAD_DIM, SSD_STATE), s_conv[None],
            s_gla.reshape(1, bs, N_GLA_HEADS, GLA_DK, GLA_DV))
```

```python
import functools

import jax
import jax.numpy as jnp
import numpy as np
from jax import lax
from jax.experimental import pallas as pl
from jax.experimental.pallas import tpu as pltpu

f32 = jnp.float32
bf16 = jnp.bfloat16

D_MODEL = 1024
D_SSD = 512
SSD_HEAD_DIM = 64
N_SSD_HEADS = 8
SSD_STATE = 128
SSD_GROUPS = 2
CONV_W = 4
CONV_DIM = D_SSD + 2 * SSD_GROUPS * SSD_STATE
D_GLA = 512
N_GLA_HEADS = 4
GLA_DV = 128
GLA_DK = 64
GLA_RANK = 16
GLA_GATE_NORMALIZER = 16.0
D_FF = 2816
EPS = 1e-6
D_QK = N_GLA_HEADS * GLA_DK
LOG2E = float(np.log2(np.e))

SUBLANES = 8
LANES = 128
BF16_ROWS = 16
VMEM_LIMIT_BYTES = 60 * 1024 * 1024

CHUNK = 128
TOKEN_TILE = 256
TOKEN_B_TILE = 512

PA_COLS = D_SSD + CONV_DIM + 2 * D_QK + 2 * D_GLA
PB_COLS = LANES
OFF_Z, OFF_XBC, OFF_Q, OFF_K, OFF_V, OFF_G = 0, 512, 1536, 1792, 2048, 2560


def _dot(a, b):
    return jnp.dot(a, b, preferred_element_type=f32)


def _dot_nt(a, b):
    return lax.dot_general(a, b, (((1,), (1,)), ((), ())), preferred_element_type=f32)


def _split3(x):
    hi = x.astype(bf16)
    r1 = x - hi.astype(f32)
    mid = r1.astype(bf16)
    lo = (r1 - mid.astype(f32)).astype(bf16)
    return hi, mid, lo


def _sel_dot(sel3, x):
    return _dot(sel3, jnp.concatenate(_split3(x), axis=0))


def _tiled3(sel):
    return np.concatenate([sel] * 3, axis=1)


def _split2(x):
    hi = x.astype(bf16)
    return hi, (x - hi.astype(f32)).astype(bf16)


def _dot_x3(a, b_stack):
    a_hi, a_lo = _split2(a)
    return _dot(jnp.concatenate([a_hi, a_hi, a_lo], axis=1), b_stack)


def _stack_x3(b):
    b_hi, b_lo = _split2(b)
    return jnp.concatenate([b_hi, b_lo, b_hi], axis=0)


def _rms(x, g):
    return x * lax.rsqrt(jnp.mean(x * x, axis=-1, keepdims=True) + EPS) * g


N_LOAD = 16

W_IN_DT = D_SSD + CONV_DIM
W_IN_Q = W_IN_DT + N_SSD_HEADS
W_IN_LR = W_IN_Q + 2 * D_QK + 2 * D_GLA
W_IN_COLS = W_IN_LR + GLA_RANK


def _load_slab(step, slab_ref, dst_ref, regroup=None):
    rows = slab_ref.shape[0]
    assert rows % BF16_ROWS == 0
    val = slab_ref[...]
    if regroup is not None:
        val = regroup(val)
    dst_ref[pl.ds(pl.multiple_of(step * rows, BF16_ROWS), rows), :] = val.astype(bf16)


def _regroup_w_in(w):
    small = jnp.concatenate([w[:, W_IN_DT:W_IN_Q], w[:, W_IN_LR:W_IN_COLS],
                             jnp.zeros((w.shape[0], PB_COLS - N_SSD_HEADS - GLA_RANK), w.dtype)], axis=1)
    return jnp.concatenate([w[:, :W_IN_DT], w[:, W_IN_Q:W_IN_LR], small], axis=1)


ROW_STREAMS = 2


def _token_a_phases(row0, n_rows, x_ref, x1_ref, pa_ref, pb_ref, g1, wup, wdn, gmix, win, under_first_matmul=None,
                    keep_xbc=None):
    rows = pl.ds(row0, n_rows)
    x = x_ref[rows, :]
    gu = _dot(_rms(x, g1[...]).astype(bf16), wup[...])
    if under_first_matmul is not None:
        under_first_matmul()
    yield
    act = (jax.nn.silu(gu[:, :D_FF]) * gu[:, D_FF:]).astype(bf16)
    x1 = x + 0.5 * _dot(act, wdn[...])
    yield
    x1_ref[rows, :] = x1
    pr = _dot(_rms(x1, gmix[...]).astype(bf16), win[...])
    yield
    pa_ref[rows, :] = pr[:, :PA_COLS].astype(bf16)
    pb_ref[rows, :] = pr[:, PA_COLS:]
    if keep_xbc is not None:
        keep_xbc[pl.ds(SUBLANES + row0, n_rows), :] = pr[:, OFF_XBC:OFF_XBC + CONV_DIM]


def _token_a_body(n_prompt, tiles_per_seq, xp, xs, g1, wup_f, wdn_f, gmix, win_f, convw, convb,
                  x1p, x1s, pap, pas, pbp, pbs, xc, nconv, wup, wdn, win, cbuf):
    i = pl.program_id(0)
    tile = xp.shape[0]
    part = tile // ROW_STREAMS

    def shift_one_row(a):
        above = jnp.concatenate([a[-1:], a[:-1]], axis=0)
        last = lax.broadcasted_iota(jnp.int32, (1, SUBLANES, 1), 1) == SUBLANES - 1
        return pltpu.roll(jnp.where(last, above, a), 1, axis=1)

    def conv_previous_tile():
        t = lax.rem(i - N_LOAD - 1, tiles_per_seq)
        cbuf[0:SUBLANES, :] = jnp.where(t == 0, 0.0, cbuf[0:SUBLANES, :])
        ext = cbuf[...].reshape(tile // SUBLANES + 1, SUBLANES, CONV_DIM)
        acc = ext * convw[0:1, :]
        for tap in range(1, CONV_W):
            acc = shift_one_row(acc) + ext * convw[tap:tap + 1, :]
        xc[...] = jax.nn.silu(acc[1:].reshape(tile, CONV_DIM) + convb[...]).astype(bf16)
        nconv[0] = cbuf[SUBLANES + tile - (CONV_W - 1):SUBLANES + tile, :]
        cbuf[0:SUBLANES, :] = cbuf[tile:tile + SUBLANES, :]

    def streams(x_ref, x1_ref, pa_ref, pb_ref, pending, keep_xbc):
        _round_robin([_token_a_phases(j * part, part, x_ref, x1_ref, pa_ref, pb_ref, g1, wup, wdn, gmix, win,
                                      pending if j == 0 else None, keep_xbc) for j in range(ROW_STREAMS)])

    @pl.when(i < N_LOAD)
    def _():
        _load_slab(i, wup_f, wup)
        _load_slab(i, wdn_f, wdn)
        _load_slab(i, win_f, win, _regroup_w_in)

    @pl.when(i == 0)
    def _():
        cbuf[...] = jnp.zeros(cbuf.shape, f32)

    first_sample = N_LOAD + n_prompt
    pl.when(jnp.logical_and(i >= N_LOAD, i < first_sample))(
        lambda: streams(xp, x1p, pap, pbp, conv_previous_tile, cbuf))
    pl.when(i == first_sample)(conv_previous_tile)
    pl.when(i >= first_sample)(lambda: streams(xs, x1s, pas, pbs, None, None))


def _token_b_phases(rows, x_ref, m_ref, y_ref, wout, g2, wup, wdn, gfin):
    x2 = x_ref[rows, :] + _dot(m_ref[rows, :], wout[...])
    yield
    gu = _dot(_rms(x2, g2[...]).astype(bf16), wup[...])
    yield
    act = (jax.nn.silu(gu[:, :D_FF]) * gu[:, D_FF:]).astype(bf16)
    x3 = x2 + 0.5 * _dot(act, wdn[...])
    yield
    y_ref[rows, :] = _rms(x3, gfin[...])


def _token_b_body(n_prompt, x1p, x1s, mxp, mxs, wout, g2, wup, wdn, gfin, yp, ys):
    def compute(x_ref, m_ref, y_ref):
        part = x_ref.shape[0] // ROW_STREAMS
        _round_robin([_token_b_phases(pl.ds(j * part, part), x_ref, m_ref, y_ref, wout, g2, wup, wdn, gfin)
                      for j in range(ROW_STREAMS)])

    i = pl.program_id(0)
    pl.when(i < n_prompt)(lambda: compute(x1p, mxp, yp))
    pl.when(i >= n_prompt)(lambda: compute(x1s, mxs, ys))


def _two_group_specs(n_prompt, cols, tile, first=0):
    prompt = pl.BlockSpec((tile, cols), lambda i: (jnp.clip(i - first, 0, n_prompt - 1), 0))
    sample = pl.BlockSpec((tile, cols), lambda i: (jnp.maximum(i - first - n_prompt, 0), 0))
    return prompt, sample


def _whole(shape):
    return pl.BlockSpec(shape, lambda i: (0,) * len(shape), pipeline_mode=pl.Buffered(1))


def _slabs(weight):
    _, rows, cols = weight.shape
    assert rows % N_LOAD == 0
    return pl.BlockSpec((None, rows // N_LOAD, cols), lambda i: (0, jnp.minimum(i, N_LOAD - 1), 0))


def _token_a(xp, xs, g1, wup, wdn, gmix, win, conv_w, conv_b, prompt_len):
    tp, ts = xp.shape[0], xs.shape[0]
    tile = TOKEN_TILE
    n_prompt, n_sample = tp // tile, ts // tile
    assert tp % tile == 0 and ts % tile == 0 and win.shape[2] == W_IN_COLS and prompt_len % tile == 0
    tiles_per_seq = prompt_len // tile
    n_seq = tp // prompt_len
    xin = _two_group_specs(n_prompt, D_MODEL, tile, N_LOAD)
    pa = _two_group_specs(n_prompt, PA_COLS, tile, N_LOAD)
    pb = _two_group_specs(n_prompt, PB_COLS, tile, N_LOAD)
    conv_tile = lambda i: jnp.clip(i - N_LOAD - 1, 0, n_prompt - 1)
    return pl.pallas_call(
        functools.partial(_token_a_body, n_prompt, tiles_per_seq),
        grid=(N_LOAD + n_prompt + n_sample,),
        in_specs=[*xin, _whole(g1.shape), _slabs(wup), _slabs(wdn), _whole(gmix.shape), _slabs(win),
                  _whole(conv_w.shape), _whole(conv_b.shape)],
        out_specs=[*xin, *pa, *pb, pl.BlockSpec((tile, CONV_DIM), lambda i: (conv_tile(i), 0)),
                   pl.BlockSpec((1, CONV_W - 1, CONV_DIM), lambda i: (conv_tile(i) // tiles_per_seq, 0, 0))],
        out_shape=[jax.ShapeDtypeStruct((tp, D_MODEL), f32), jax.ShapeDtypeStruct((ts, D_MODEL), f32),
                   jax.ShapeDtypeStruct((tp, PA_COLS), bf16), jax.ShapeDtypeStruct((ts, PA_COLS), bf16),
                   jax.ShapeDtypeStruct((tp, PB_COLS), f32), jax.ShapeDtypeStruct((ts, PB_COLS), f32),
                   jax.ShapeDtypeStruct((tp, CONV_DIM), bf16),
                   jax.ShapeDtypeStruct((n_seq, CONV_W - 1, CONV_DIM), f32)],
        scratch_shapes=[pltpu.VMEM(wup.shape[1:], bf16), pltpu.VMEM(wdn.shape[1:], bf16),
                        pltpu.VMEM((D_MODEL, PA_COLS + PB_COLS), bf16),
                        pltpu.VMEM((SUBLANES + tile, CONV_DIM), f32)],
        compiler_params=pltpu.CompilerParams(dimension_semantics=("arbitrary",), vmem_limit_bytes=VMEM_LIMIT_BYTES),
        name="token_a",
    )(xp, xs, g1, wup, wdn, gmix, win, conv_w, conv_b)


def _token_b(x1p, x1s, mxp, mxs, wout, g2, wup, wdn, gfin):
    tp, ts = x1p.shape[0], x1s.shape[0]
    tile = TOKEN_B_TILE
    assert tp % tile == 0 and ts % tile == 0
    n_prompt, n_sample = tp // tile, ts // tile
    xin = _two_group_specs(n_prompt, D_MODEL, tile)
    return pl.pallas_call(
        functools.partial(_token_b_body, n_prompt),
        grid=(n_prompt + n_sample,),
        in_specs=[*xin, *xin, _whole(wout.shape), _whole(g2.shape), _whole(wup.shape), _whole(wdn.shape),
                  _whole(gfin.shape)],
        out_specs=[*xin],
        out_shape=[jax.ShapeDtypeStruct((tp, D_MODEL), f32), jax.ShapeDtypeStruct((ts, D_MODEL), f32)],
        compiler_params=pltpu.CompilerParams(dimension_semantics=("arbitrary",), vmem_limit_bytes=VMEM_LIMIT_BYTES),
        name="token_b",
    )(x1p, x1s, mxp, mxs, wout, g2, wup, wdn, gfin)


def _level_halves(seq_rows):
    return [h for h in (1, 2, 4, 8, 16, 32, 64) if 2 * h <= seq_rows]


def _matmul_levels(seq_rows):
    return [h for h in _level_halves(seq_rows) if h < SUBLANES]


def _mixer_consts(seq_rows):
    t = np.arange(CHUNK)
    is_sample = seq_rows < CHUNK
    same = (t[:, None] // seq_rows) == (t[None, :] // seq_rows)
    tri = (same & (t[None, :] <= t[:, None])).astype(np.float32)
    last = ((t[:, None] // seq_rows) * seq_rows + seq_rows - 1 == t[None, :]).astype(np.float32)
    cum_gla, pair_mask = [tri], [np.eye(CHUNK, dtype=np.float32)]
    for h in _level_halves(seq_rows):
        ref_row = (t // (2 * h)) * (2 * h) + h
        if h in _matmul_levels(seq_rows):
            cum_gla.append((ref_row[:, None] == t[None, :]).astype(np.float32) @ tri)
        right = (t % (2 * h)) >= h
        blk = (t[:, None] // (2 * h)) == (t[None, :] // (2 * h))
        pair_mask.append((blk & right[:, None] & ~right[None, :]).astype(np.float32))
    cum_ssd = [tri]
    if is_sample:
        cum_gla.append(last @ tri)
        cum_ssd.append(last @ tri)
    hq = np.arange(D_QK) // GLA_DK
    hv = np.arange(D_GLA) // GLA_DV
    head_of_lane = np.arange(D_SSD) // SSD_HEAD_DIM
    expand = (np.arange(PB_COLS)[:, None] == head_of_lane[None, :]).astype(np.float32)
    consts = [jnp.asarray(_tiled3(np.concatenate(cum_ssd, 0)), bf16),
              jnp.asarray(_tiled3(np.concatenate(cum_gla, 0)), bf16),
              jnp.asarray(np.stack(pair_mask, 0), f32),
              jnp.asarray((hv[:, None] == hq[None, :]).astype(np.float32), f32),
              jnp.asarray(np.concatenate([expand] * 3, 0), bf16)]
    return consts


PROMPT_STREAMS = 8
SAMPLE_STREAMS = 1
N_SHARED_REFS = 14
EPILOGUE = "epilogue"


def _round_robin(streams):
    live = list(streams)
    waiting = []
    while live:
        for s in list(live):
            try:
                if next(s) == EPILOGUE:
                    live.remove(s)
                    waiting.append(s)
            except StopIteration:
                live.remove(s)
    for s in waiting:
        for _ in s:
            pass


def _mixer_body(seq_rows, n_cast, *refs):
    is_sample = seq_rows < CHUNK
    n_streams = refs[0].shape[0]
    n_in = 5 if is_sample else 3
    n_out = 4 if is_sample else 3
    n_shared = N_SHARED_REFS
    ins, shared = refs[:n_in], refs[n_in:n_in + n_shared]
    cast_in = refs[n_in + n_shared:n_in + n_shared + n_cast]
    outs = refs[n_in + n_shared + n_cast:n_in + n_shared + n_cast + n_out]
    cast_out = refs[n_in + n_shared + n_cast + n_out:n_in + n_shared + 2 * n_cast + n_out]
    scratch = refs[n_in + n_shared + 2 * n_cast + n_out:]
    is_last = None
    if not is_sample:
        st_ref, stbd_ref = scratch
        c_idx = pl.program_id(1)
        is_last = c_idx == pl.num_programs(1) - 1

        @pl.when(pl.program_id(0) * pl.num_programs(1) + c_idx < N_LOAD)
        def _():
            for src, dst in zip(cast_in, cast_out):
                dst[...] = src[...].astype(bf16)

        @pl.when(c_idx == 0)
        def _():
            st_ref[...] = jnp.zeros(st_ref.shape, f32)
            stbd_ref[...] = jnp.zeros(stbd_ref.shape, f32)

    at = lambda group, j: [r.at[j] for r in group]
    _round_robin([_mixer_phases(seq_rows, at(ins, j), shared, at(outs, j), at(scratch, j), is_last)
                  for j in range(n_streams)])


def _mixer_phases(seq_rows, ins, shared, outs, scratch, is_last):
    is_sample = seq_rows < CHUNK
    n_seq = CHUNK // seq_rows
    halves = _level_halves(seq_rows)
    n_lvl = len(halves)
    if is_sample:
        pa_ref, pb_ref, ssd0_ref, conv0_ref, gla0_ref = ins
        (cbuf,) = scratch
        mixed_ref, nconv_ref, nssd_ref, ngla_ref = outs
    else:
        pa_ref, pb_ref, xc_ref = ins
        st_ref, stbd_ref = scratch
        mixed_ref, nssd_ref, ngla_ref = outs
    (convw_ref, convb_ref, dsk_e_ref, gssd_ref, dtb_c_ref, alog_c_ref, wgk_ref, bgk_ref, ggla_ref,
     cum_ssd_ref, cum_gla_ref, pair_mask_ref, bdmask_ref, expand_ref) = shared

    C = CHUNK
    rows = lax.broadcasted_iota(jnp.int32, (C, 1), 0)
    lane_qk = lax.shift_right_logical(lax.broadcasted_iota(jnp.int32, (1, D_QK), 1), int(np.log2(GLA_DK)))

    if is_sample:
        conv = jnp.broadcast_to(convb_ref[...], (C, CONV_DIM))
        for tap in range(CONV_W - 1):
            cbuf[:, SUBLANES - (CONV_W - 1) + tap, :] = conv0_ref[tap]
        cbuf[:, SUBLANES:SUBLANES + seq_rows, :] = (
            pa_ref[:, OFF_XBC:OFF_XBC + CONV_DIM].astype(f32).reshape(n_seq, seq_rows, CONV_DIM))
        for i in range(CONV_W):
            shift = CONV_W - 1 - i
            win = cbuf[:, SUBLANES - shift:SUBLANES - shift + seq_rows, :].reshape(C, CONV_DIM)
            conv = conv + win * convw_ref[i:i + 1, :]
        for tap in range(CONV_W - 1):
            nconv_ref[tap] = cbuf[:, SUBLANES + seq_rows - (CONV_W - 1) + tap, :]
        xc = jax.nn.silu(conv)
    else:
        xc = xc_ref[...]
    xs = xc[:, :D_SSD].astype(f32)
    bm = xc[:, D_SSD:D_SSD + SSD_GROUPS * SSD_STATE].astype(bf16)
    cm = xc[:, D_SSD + SSD_GROUPS * SSD_STATE:].astype(bf16)

    small = pb_ref[...]
    dtp_c = jax.nn.softplus(small + dtb_c_ref[...])
    cums = _sel_dot(cum_ssd_ref[...], dtp_c * (-LOG2E * jnp.exp(alog_c_ref[...])))
    yield
    acum_c = cums[:C]
    acum_t = acum_c.T
    wide = _dot(jnp.concatenate(_split3(jnp.concatenate([cums, dtp_c], axis=0)), axis=1), expand_ref[...])
    yield
    acum_e, dtp_e = wide[:C], wide[-C:]
    acum_last = wide[C:2 * C] if is_sample else acum_e[C - 1:C, :]
    causal = cum_ssd_ref[0:C, 0:C].astype(f32) > 0
    xdt = xs * dtp_e
    xdt_bf = xdt.astype(bf16)
    lane = lax.broadcasted_iota(jnp.int32, (1, LANES), 1)
    left = lane < SSD_HEAD_DIM
    heads_per_group = N_SSD_HEADS // SSD_GROUPS
    gw = heads_per_group * SSD_HEAD_DIM
    y_parts = []
    for grp in range(SSD_GROUPS):
        cb = _dot_nt(cm[:, grp * SSD_STATE:(grp + 1) * SSD_STATE], bm[:, grp * SSD_STATE:(grp + 1) * SSD_STATE])
        for pair in range(heads_per_group // 2):
            sc = []
            for hh in range(2):
                h = grp * heads_per_group + pair * 2 + hh
                seg = acum_c[:, h:h + 1] - acum_t[h:h + 1, :]
                sc.append((cb * jnp.exp2(jnp.where(causal, seg, -jnp.inf))).astype(bf16))
            lo = (grp * heads_per_group + pair * 2) * SSD_HEAD_DIM
            xp = xdt_bf[:, lo:lo + LANES]
            bd = jnp.concatenate([jnp.where(left, xp, 0), jnp.where(left, 0, xp)], axis=0)
            y_parts.append(_dot(jnp.concatenate(sc, axis=1), bd))
            yield
    y = jnp.concatenate(y_parts, axis=1)

    to_end = jnp.exp2(jnp.minimum(acum_last - acum_e, 0.0))
    xs_end = (xdt * to_end).astype(bf16)
    e_acum = jnp.exp2(acum_e)
    bm_t = bm.T

    def ssd_inter(st):
        st_bf = st.astype(bf16)
        return jnp.concatenate([_dot(cm[:, grp * SSD_STATE:(grp + 1) * SSD_STATE], st_bf[:, grp * gw:(grp + 1) * gw])
                                for grp in range(SSD_GROUPS)], axis=1) * e_acum

    def ssd_update(st, decay_row, xs_rows):
        upd = jnp.concatenate([_dot(bm_t[grp * SSD_STATE:(grp + 1) * SSD_STATE, :], xs_rows[:, grp * gw:(grp + 1) * gw])
                               for grp in range(SSD_GROUPS)], axis=1)
        return st * decay_row + upd

    q = pa_ref[:, OFF_Q:OFF_Q + D_QK].astype(f32) * (GLA_DK ** -0.5)
    k = pa_ref[:, OFF_K:OFF_K + D_QK].astype(f32)
    v_bf = pa_ref[:, OFF_V:OFF_V + D_GLA]
    gk = _dot_x3(small, wgk_ref[...]) + bgk_ref[...]
    log_a = jax.nn.log_sigmoid(gk) * (LOG2E / GLA_GATE_NORMALIZER)
    bsel = _sel_dot(cum_gla_ref[...], log_a)
    yield
    bcum = bsel[:C]
    n_mm = len(_matmul_levels(seq_rows))
    b_last = bsel[(n_mm + 1) * C:] if is_sample else bcum[C - 1:C, :]
    k_bf = k.astype(bf16)

    def b_ref(lvl, half):
        if lvl < n_mm:
            return bsel[(lvl + 1) * C:(lvl + 2) * C]
        return jnp.concatenate([jnp.broadcast_to(bcum[r + half:r + half + 1, :], (2 * half, D_QK))
                                for r in range(0, C, 2 * half)], axis=0)

    def head_rows(x_bf):
        return jnp.concatenate([jnp.where(lane_qk == h, x_bf, jnp.zeros_like(x_bf)) for h in range(N_GLA_HEADS)], axis=0)

    a = _dot_nt(head_rows(q.astype(bf16)), k_bf)
    yield
    m = pair_mask_ref[0]
    att = [a[h * C:(h + 1) * C] * m for h in range(N_GLA_HEADS)]
    for lvl, half in enumerate(halves):
        decay = jnp.exp2(-jnp.abs(bcum - b_ref(lvl, half)))
        in_right = (rows & half) != 0
        u = jnp.where(in_right, q, k) * decay
        u_bf = u.astype(bf16)
        m = pair_mask_ref[lvl + 1]
        if half < SUBLANES:
            a = _dot_nt(head_rows(u_bf), u_bf)
            yield
            for h in range(N_GLA_HEADS):
                att[h] = att[h] + a[h * C:(h + 1) * C] * m
        else:
            starts = range(half, C, 2 * half)
            pick = lambda x: jnp.concatenate([x[r:r + half] for r in starts], axis=0)
            a = _dot_nt(head_rows(pick(u).astype(bf16)), u_bf)
            yield
            m_right = pick(m)
            for h in range(N_GLA_HEADS):
                upd = a[h * (C // 2):(h + 1) * (C // 2)] * m_right
                parts = []
                for i, r in enumerate(starts):
                    parts += [att[h][r - half:r], att[h][r:r + half] + upd[i * half:(i + 1) * half]]
                att[h] = jnp.concatenate(parts, axis=0)
    o = jnp.concatenate([_dot(att[h].astype(bf16), v_bf[:, h * GLA_DV:(h + 1) * GLA_DV])
                         for h in range(N_GLA_HEADS)], axis=1)
    yield

    q_in = (q * jnp.exp2(bcum)).astype(bf16)
    kd = (k * jnp.exp2(jnp.minimum(b_last - bcum, 0.0))).astype(bf16)
    v_t = v_bf.T
    bdmask = bdmask_ref[...]

    def gla_inter(stbd):
        return _dot_nt(q_in, stbd.astype(bf16))

    def gla_update(stbd, decay_row, kd_rows):
        return stbd * decay_row + bdmask * _dot(v_t, kd_rows)

    def gla_heads(stbd):
        acc = jnp.where(lane_qk == 0, stbd[0:GLA_DV, :], 0.0)
        for h in range(1, N_GLA_HEADS):
            acc = acc + jnp.where(lane_qk == h, stbd[h * GLA_DV:(h + 1) * GLA_DV, :], 0.0)
        return acc.T

    if not is_sample:
        st = st_ref[...]
        stbd = stbd_ref[...]
        y = y + ssd_inter(st)
        o = o + gla_inter(stbd)
        yield
        st_ref[...] = ssd_update(st, jnp.exp2(acum_last), xs_end)
        stbd_ref[...] = gla_update(stbd, jnp.exp2(b_last), kd)
        yield
    else:
        seq_of_row = lax.shift_right_logical(rows, int(np.log2(seq_rows)))
        bm_f32 = xc[:, D_SSD:D_SSD + SSD_GROUPS * SSD_STATE]
        cm_f32 = xc[:, D_SSD + SSD_GROUPS * SSD_STATE:]
        xs_end_f32 = xdt * to_end
        e_last_c = jnp.exp2(cums[C:])
        q_in_f32 = q * jnp.exp2(bcum)
        y_rows, o_rows = [], []
        g0_next = gla0_ref[0].T
        for s in range(n_seq):
            g0 = g0_next
            if s + 1 < n_seq:
                g0_next = gla0_ref[s + 1].T
            mine = seq_of_row == s
            r0 = s * seq_rows
            own = slice(r0, r0 + seq_rows)
            st = ssd0_ref[s]
            c_rows = cm_f32[own].astype(bf16)
            b_rows = bm_f32[own].astype(bf16)
            x_rows = xs_end_f32[own].astype(bf16)
            y_grp, st_new = [], []
            for grp in range(SSD_GROUPS):
                st_g = st[grp * gw:(grp + 1) * gw]
                y_grp.append(_dot_nt(c_rows[:, grp * SSD_STATE:(grp + 1) * SSD_STATE], st_g.astype(bf16)))
                upd = lax.dot_general(x_rows[:, grp * gw:(grp + 1) * gw], b_rows[:, grp * SSD_STATE:(grp + 1) * SSD_STATE],
                                      (((0,), (0,)), ((), ())), preferred_element_type=f32)
                for hh in range(heads_per_group):
                    h = grp * heads_per_group + hh
                    rows_h = slice(hh * SSD_HEAD_DIM, (hh + 1) * SSD_HEAD_DIM)
                    decay_h = jnp.broadcast_to(e_last_c[r0:r0 + 1, h:h + 1], (SSD_HEAD_DIM, SSD_STATE))
                    st_new.append(st_g[rows_h] * decay_h + upd[rows_h])
            y_rows.append(jnp.concatenate(y_grp, axis=1))
            nssd_ref[s] = jnp.concatenate(st_new, axis=0)
            stbd = jnp.concatenate([jnp.where(lane_qk == h, g0, 0.0) for h in range(N_GLA_HEADS)], axis=0)
            o_rows.append(_dot_nt(q_in_f32[r0:r0 + seq_rows].astype(bf16), stbd.astype(bf16)))
            ngla_ref[s] = gla_heads(gla_update(stbd, jnp.exp2(b_last[r0:r0 + 1, :]), jnp.where(mine, kd, 0)))
            yield
        y = y + jnp.concatenate(y_rows, axis=0) * e_acum
        o = o + jnp.concatenate(o_rows, axis=0)

    z = pa_ref[:, OFF_Z:OFF_Z + D_SSD].astype(f32)
    g = pa_ref[:, OFF_G:OFF_G + D_GLA].astype(f32)
    y = (y + dsk_e_ref[...] * xs) * jax.nn.silu(z)
    gsz = D_SSD // SSD_GROUPS
    y = jnp.concatenate([_rms(y[:, i * gsz:(i + 1) * gsz], gssd_ref[:, i * gsz:(i + 1) * gsz])
                         for i in range(SSD_GROUPS)], axis=1)
    o = jnp.concatenate([_rms(o[:, h * GLA_DV:(h + 1) * GLA_DV], ggla_ref[...]) for h in range(N_GLA_HEADS)], axis=1)
    o = o * jax.nn.silu(g)
    mixed_ref[:, :D_SSD] = y.astype(bf16)
    mixed_ref[:, D_SSD:] = o.astype(bf16)

    if not is_sample:
        yield EPILOGUE

        @pl.when(is_last)
        def _():
            nssd_ref[0] = st_ref[...].T
            ngla_ref[0] = gla_heads(stbd_ref[...])


def _mixer(pa, pb, params, states, n_batch, seq_len, xc=None, cast_weights=()):
    is_sample = states is not None
    assert not (is_sample and cast_weights) and is_sample == (xc is None)
    seq_rows = seq_len if is_sample else CHUNK
    assert CHUNK % seq_rows == 0 and (is_sample or seq_len % CHUNK == 0)
    n_seq = CHUNK // seq_rows
    consts = _mixer_consts(seq_rows)
    ns = SAMPLE_STREAMS if is_sample else PROMPT_STREAMS
    assert n_batch % (ns * n_seq) == 0
    rows_per_stream = n_batch * seq_len // ns
    seqs_per_stream = n_batch // ns

    if is_sample:
        grid = (seqs_per_stream // n_seq,)
        row_map = lambda c: (0, c, 0)
        seq_map = lambda c: (0, c, 0, 0)
        full = lambda shape: pl.BlockSpec(shape, lambda c: (0,) * len(shape))
        sems = ("arbitrary",)
    else:
        n_chunks = seq_len // CHUNK
        grid = (seqs_per_stream, n_chunks)
        row_map = lambda b, c: (0, b * n_chunks + c, 0)
        seq_map = lambda b, c: (0, b, 0, 0)
        full = lambda shape: pl.BlockSpec(shape, lambda b, c: (0,) * len(shape))
        sems = ("arbitrary", "arbitrary")

    def streamed(arr):
        return arr.reshape((ns, arr.shape[0] // ns) + arr.shape[1:])

    taps = CONV_W - 1
    conv_spec = pl.BlockSpec((ns, taps, n_seq, CONV_DIM), lambda *g: (0, 0, seq_map(*g)[1], 0))
    state_blocks = [(n_seq, D_SSD, SSD_STATE), (n_seq, D_QK, GLA_DV)]
    in_arrays = [streamed(pa), streamed(pb)]
    in_specs = [pl.BlockSpec((ns, CHUNK, PA_COLS), row_map), pl.BlockSpec((ns, CHUNK, PB_COLS), row_map)]
    if is_sample:
        ssd0, conv0, gla0 = states
        conv0_t = jnp.transpose(streamed(conv0), (0, 2, 1, 3))
        in_arrays += [streamed(ssd0), conv0_t, streamed(gla0)]
        in_specs += [pl.BlockSpec((ns,) + state_blocks[0], seq_map), conv_spec,
                     pl.BlockSpec((ns,) + state_blocks[1], seq_map)]
    else:
        in_arrays.append(streamed(xc))
        in_specs.append(pl.BlockSpec((ns, CHUNK, CONV_DIM), row_map))
    for arr in list(params) + consts:
        in_arrays.append(arr)
        in_specs.append(full(arr.shape))

    out_shape = [jax.ShapeDtypeStruct((ns, rows_per_stream, D_MODEL), bf16)]
    out_specs = [pl.BlockSpec((ns, CHUNK, D_MODEL), row_map)]
    if is_sample:
        out_shape.append(jax.ShapeDtypeStruct((ns, taps, seqs_per_stream, CONV_DIM), f32))
        out_specs.append(conv_spec)
    out_shape += [jax.ShapeDtypeStruct((ns, seqs_per_stream) + blk[1:], f32) for blk in state_blocks]
    out_specs += [pl.BlockSpec((ns,) + blk, seq_map) for blk in state_blocks]
    for w in cast_weights:
        _, w_rows, w_cols = w.shape
        slab = w_rows // N_LOAD
        assert w_rows % N_LOAD == 0 and slab % BF16_ROWS == 0 and grid[0] * grid[1] >= N_LOAD
        step = lambda b, c: jnp.minimum(b * n_chunks + c, N_LOAD - 1)
        in_arrays.append(w)
        in_specs.append(pl.BlockSpec((None, slab, w_cols), lambda b, c: (0, step(b, c), 0)))
        out_shape.append(jax.ShapeDtypeStruct((w_rows, w_cols), bf16))
        out_specs.append(pl.BlockSpec((slab, w_cols), lambda b, c: (step(b, c), 0)))
    if is_sample:
        scratch = [pltpu.VMEM((ns, n_seq, SUBLANES + seq_rows, CONV_DIM), f32)]
    else:
        scratch = [pltpu.VMEM((ns, SSD_STATE, D_SSD), f32), pltpu.VMEM((ns, D_GLA, D_QK), f32)]
    results = pl.pallas_call(
        functools.partial(_mixer_body, seq_rows, len(cast_weights)),
        grid=grid, in_specs=in_specs, out_specs=out_specs, out_shape=out_shape, scratch_shapes=scratch,
        compiler_params=pltpu.CompilerParams(dimension_semantics=sems, vmem_limit_bytes=VMEM_LIMIT_BYTES),
        name="mixer_sample" if is_sample else "mixer_prompt",
    )(*in_arrays)
    merged = lambda arr: arr.reshape((arr.shape[0] * arr.shape[1],) + arr.shape[2:])
    if is_sample:
        mixed, nconv, nssd, ngla = results
        return merged(mixed), merged(jnp.transpose(nconv, (0, 2, 1, 3))), merged(nssd), merged(ngla)
    mixed, nssd, ngla, *cast = results
    return merged(mixed), merged(nssd), merged(ngla), cast


def _row(vec):
    return vec.reshape(1, -1).astype(f32)


def _pad_lanes(vec, width):
    return jnp.concatenate([vec.astype(f32), jnp.zeros((width - vec.shape[0],), f32)]).reshape(1, width)


def _mixer_params(conv_w, conv_b, dt_bias, a_log, d_skip, g_ssd_norm, w_gk2, b_gk, g_gla_norm):
    rep = lambda vec: _row(jnp.repeat(vec, SSD_HEAD_DIM))
    wgk = jnp.zeros((PB_COLS, D_QK), f32).at[N_SSD_HEADS:N_SSD_HEADS + GLA_RANK, :].set(w_gk2.astype(f32))
    return [conv_w.astype(f32), _row(conv_b), rep(d_skip), _row(g_ssd_norm),
            _pad_lanes(dt_bias, PB_COLS), _pad_lanes(a_log, PB_COLS), _stack_x3(wgk), _row(b_gk), _row(g_gla_norm)]


def kernel(x_prompt, x_sample, state_ssd, state_conv, state_gla, g_ffn1, w_ffn1_in, w_ffn1_out, g_mix, w_in, conv_w, conv_b, dt_bias, a_log, d_skip, g_ssd_norm, w_gk2, b_gk, g_gla_norm, w_out, g_ffn2, w_ffn2_in, w_ffn2_out, g_final):
    bp, lp, _ = x_prompt.shape
    bs, ls, _ = x_sample.shape
    assert w_in.shape[0] == 1, "single-layer step: the final norm is fused into the layer's last kernel"
    xp = x_prompt.reshape(bp * lp, D_MODEL)
    xs = x_sample.reshape(bs * ls, D_MODEL)
    params = _mixer_params(conv_w[0], conv_b[0], dt_bias[0], a_log[0], d_skip[0], g_ssd_norm[0],
                           w_gk2[0], b_gk[0], g_gla_norm[0])
    x1p, x1s, pap, pas, pbp, pbs, xcp, p_conv = _token_a(
        xp, xs, _row(g_ffn1[0]), w_ffn1_in.astype(f32), w_ffn1_out.astype(f32), _row(g_mix[0]), w_in.astype(f32),
        params[0], params[1], lp)
    mxp, p_ssd, p_gla, (w_out_bf, w_up2_bf, w_dn2_bf) = _mixer(
        pap, pbp, params, None, bp, lp, xc=xcp,
        cast_weights=(w_out.astype(f32), w_ffn2_in.astype(f32), w_ffn2_out.astype(f32)))
    s_states = (state_ssd[0].reshape(bs, D_SSD, SSD_STATE), state_conv[0], state_gla[0].reshape(bs, D_QK, GLA_DV))
    mxs, s_conv, s_ssd, s_gla = _mixer(pas, pbs, params, s_states, bs, ls)
    yp, ys = _token_b(x1p, x1s, mxp, mxs, w_out_bf, _row(g_ffn2[0]), w_up2_bf, w_dn2_bf, _row(g_final))
    return (yp.reshape(bp, lp, D_MODEL), ys.reshape(bs, ls, D_MODEL),
            p_ssd.reshape(1, bp, N_SSD_HEADS, SSD_HEAD_DIM, SSD_STATE), p_conv[None],
            p_gla.reshape(1, bp, N_GLA_HEADS, GLA_DK, GLA_DV),
            s_ssd.reshape(1, bs, N_SSD_HEADS, SSD_HEAD_DIM, SSD_STATE), s_conv[None],
            s_gla.reshape(1, bs, N_GLA_HEADS, GLA_DK, GLA_DV))
```

```python
import functools

import jax
import jax.numpy as jnp
import numpy as np
from jax import lax
from jax.experimental import pallas as pl
from jax.experimental.pallas import tpu as pltpu

f32 = jnp.float32
bf16 = jnp.bfloat16

D_MODEL = 1024
D_SSD = 512
SSD_HEAD_DIM = 64
N_SSD_HEADS = 8
SSD_STATE = 128
SSD_GROUPS = 2
CONV_W = 4
CONV_DIM = D_SSD + 2 * SSD_GROUPS * SSD_STATE
D_GLA = 512
N_GLA_HEADS = 4
GLA_DV = 128
GLA_DK = 64
GLA_RANK = 16
GLA_GATE_NORMALIZER = 16.0
D_FF = 2816
EPS = 1e-6
D_QK = N_GLA_HEADS * GLA_DK
LOG2E = float(np.log2(np.e))

SUBLANES = 8
LANES = 128
BF16_ROWS = 16
VMEM_LIMIT_BYTES = 60 * 1024 * 1024

CHUNK = 128
TOKEN_TILE = 256
TOKEN_B_TILE = 512

PA_COLS = D_SSD + CONV_DIM + 2 * D_QK + 2 * D_GLA
PB_COLS = LANES
OFF_Z, OFF_XBC, OFF_Q, OFF_K, OFF_V, OFF_G = 0, 512, 1536, 1792, 2048, 2560


def _dot(a, b):
    return jnp.dot(a, b, preferred_element_type=f32)


def _dot_nt(a, b):
    return lax.dot_general(a, b, (((1,), (1,)), ((), ())), preferred_element_type=f32)


def _split3(x):
    hi = x.astype(bf16)
    r1 = x - hi.astype(f32)
    mid = r1.astype(bf16)
    lo = (r1 - mid.astype(f32)).astype(bf16)
    return hi, mid, lo


def _sel_dot(sel3, x):
    return _dot(sel3, jnp.concatenate(_split3(x), axis=0))


def _tiled3(sel):
    return np.concatenate([sel] * 3, axis=1)


def _split2(x):
    hi = x.astype(bf16)
    return hi, (x - hi.astype(f32)).astype(bf16)


def _dot_x3(a, b_stack):
    a_hi, a_lo = _split2(a)
    return _dot(jnp.concatenate([a_hi, a_hi, a_lo], axis=1), b_stack)


def _stack_x3(b):
    b_hi, b_lo = _split2(b)
    return jnp.concatenate([b_hi, b_lo, b_hi], axis=0)


def _rms(x, g):
    return x * lax.rsqrt(jnp.mean(x * x, axis=-1, keepdims=True) + EPS) * g


N_LOAD = 16

W_IN_DT = D_SSD + CONV_DIM
W_IN_Q = W_IN_DT + N_SSD_HEADS
W_IN_LR = W_IN_Q + 2 * D_QK + 2 * D_GLA
W_IN_COLS = W_IN_LR + GLA_RANK


def _load_slab(step, slab_ref, dst_ref, regroup=None):
    rows = slab_ref.shape[0]
    assert rows % BF16_ROWS == 0
    val = slab_ref[...]
    if regroup is not None:
        val = regroup(val)
    dst_ref[pl.ds(pl.multiple_of(step * rows, BF16_ROWS), rows), :] = val.astype(bf16)


def _regroup_w_in(w):
    small = jnp.concatenate([w[:, W_IN_DT:W_IN_Q], w[:, W_IN_LR:W_IN_COLS],
                             jnp.zeros((w.shape[0], PB_COLS - N_SSD_HEADS - GLA_RANK), w.dtype)], axis=1)
    return jnp.concatenate([w[:, :W_IN_DT], w[:, W_IN_Q:W_IN_LR], small], axis=1)


ROW_STREAMS = 2


def _token_a_phases(row0, n_rows, x_ref, x1_ref, pa_ref, pb_ref, g1, wup, wdn, gmix, win, under_first_matmul=None,
                    keep_xbc=None):
    rows = pl.ds(row0, n_rows)
    x = x_ref[rows, :]
    gu = _dot(_rms(x, g1[...]).astype(bf16), wup[...])
    if under_first_matmul is not None:
        under_first_matmul()
    yield
    act = (jax.nn.silu(gu[:, :D_FF]) * gu[:, D_FF:]).astype(bf16)
    x1 = x + 0.5 * _dot(act, wdn[...])
    yield
    x1_ref[rows, :] = x1
    pr = _dot(_rms(x1, gmix[...]).astype(bf16), win[...])
    yield
    pb_ref[rows, :] = pr[:, PA_COLS:]
    if keep_xbc is None:
        pa_ref[rows, :] = pr[:, :PA_COLS].astype(bf16)
    else:
        pa_ref[rows, :OFF_XBC] = pr[:, :OFF_XBC].astype(bf16)
        pa_ref[rows, OFF_XBC:] = pr[:, OFF_XBC + CONV_DIM:PA_COLS].astype(bf16)
        keep_xbc[pl.ds(SUBLANES + row0, n_rows), :] = pr[:, OFF_XBC:OFF_XBC + CONV_DIM]


def _token_a_body(n_prompt, tiles_per_seq, xp, xs, g1, wup_f, wdn_f, gmix, win_f, convw, convb,
                  x1p, x1s, pap, pas, pbp, pbs, xc, nconv, wup, wdn, win, cbuf):
    i = pl.program_id(0)
    tile = xp.shape[0]
    part = tile // ROW_STREAMS

    def shift_one_row(a):
        above = jnp.concatenate([a[-1:], a[:-1]], axis=0)
        last = lax.broadcasted_iota(jnp.int32, (1, SUBLANES, 1), 1) == SUBLANES - 1
        return pltpu.roll(jnp.where(last, above, a), 1, axis=1)

    def conv_previous_tile():
        t = lax.rem(i - N_LOAD - 1, tiles_per_seq)
        cbuf[0:SUBLANES, :] = jnp.where(t == 0, 0.0, cbuf[0:SUBLANES, :])
        ext = cbuf[...].reshape(tile // SUBLANES + 1, SUBLANES, CONV_DIM)
        acc = ext * convw[0:1, :]
        for tap in range(1, CONV_W):
            acc = shift_one_row(acc) + ext * convw[tap:tap + 1, :]
        xc[...] = jax.nn.silu(acc[1:].reshape(tile, CONV_DIM) + convb[...]).astype(bf16)
        nconv[0] = cbuf[SUBLANES + tile - (CONV_W - 1):SUBLANES + tile, :]
        cbuf[0:SUBLANES, :] = cbuf[tile:tile + SUBLANES, :]

    def streams(x_ref, x1_ref, pa_ref, pb_ref, pending, keep_xbc):
        _round_robin([_token_a_phases(j * part, part, x_ref, x1_ref, pa_ref, pb_ref, g1, wup, wdn, gmix, win,
                                      pending if j == 0 else None, keep_xbc) for j in range(ROW_STREAMS)])

    @pl.when(i < N_LOAD)
    def _():
        _load_slab(i, wup_f, wup)
        _load_slab(i, wdn_f, wdn)
        _load_slab(i, win_f, win, _regroup_w_in)

    @pl.when(i == 0)
    def _():
        cbuf[...] = jnp.zeros(cbuf.shape, f32)

    first_sample = N_LOAD + n_prompt
    pl.when(jnp.logical_and(i >= N_LOAD, i < first_sample))(
        lambda: streams(xp, x1p, pap, pbp, conv_previous_tile, cbuf))
    pl.when(i == first_sample)(conv_previous_tile)
    pl.when(i >= first_sample)(lambda: streams(xs, x1s, pas, pbs, None, None))


def _token_b_phases(rows, x_ref, m_ref, y_ref, wout, g2, wup, wdn, gfin):
    x2 = x_ref[rows, :] + _dot(m_ref[rows, :], wout[...])
    yield
    gu = _dot(_rms(x2, g2[...]).astype(bf16), wup[...])
    yield
    act = (jax.nn.silu(gu[:, :D_FF]) * gu[:, D_FF:]).astype(bf16)
    x3 = x2 + 0.5 * _dot(act, wdn[...])
    yield
    y_ref[rows, :] = _rms(x3, gfin[...])


def _token_b_body(n_prompt, x1p, x1s, mxp, mxs, wout, g2, wup, wdn, gfin, yp, ys):
    def compute(x_ref, m_ref, y_ref):
        part = x_ref.shape[0] // ROW_STREAMS
        _round_robin([_token_b_phases(pl.ds(j * part, part), x_ref, m_ref, y_ref, wout, g2, wup, wdn, gfin)
                      for j in range(ROW_STREAMS)])

    i = pl.program_id(0)
    pl.when(i < n_prompt)(lambda: compute(x1p, mxp, yp))
    pl.when(i >= n_prompt)(lambda: compute(x1s, mxs, ys))


def _two_group_specs(n_prompt, cols, tile, first=0):
    prompt = pl.BlockSpec((tile, cols), lambda i: (jnp.clip(i - first, 0, n_prompt - 1), 0))
    sample = pl.BlockSpec((tile, cols), lambda i: (jnp.maximum(i - first - n_prompt, 0), 0))
    return prompt, sample


def _whole(shape):
    return pl.BlockSpec(shape, lambda i: (0,) * len(shape), pipeline_mode=pl.Buffered(1))


def _slabs(weight):
    _, rows, cols = weight.shape
    assert rows % N_LOAD == 0
    return pl.BlockSpec((None, rows // N_LOAD, cols), lambda i: (0, jnp.minimum(i, N_LOAD - 1), 0))


def _token_a(xp, xs, g1, wup, wdn, gmix, win, conv_w, conv_b, prompt_len):
    tp, ts = xp.shape[0], xs.shape[0]
    tile = TOKEN_TILE
    n_prompt, n_sample = tp // tile, ts // tile
    assert tp % tile == 0 and ts % tile == 0 and win.shape[2] == W_IN_COLS and prompt_len % tile == 0
    tiles_per_seq = prompt_len // tile
    n_seq = tp // prompt_len
    xin = _two_group_specs(n_prompt, D_MODEL, tile, N_LOAD)
    pa = (_two_group_specs(n_prompt, PA_COLS - CONV_DIM, tile, N_LOAD)[0],
          _two_group_specs(n_prompt, PA_COLS, tile, N_LOAD)[1])
    pb = _two_group_specs(n_prompt, PB_COLS, tile, N_LOAD)
    conv_tile = lambda i: jnp.clip(i - N_LOAD - 1, 0, n_prompt - 1)
    return pl.pallas_call(
        functools.partial(_token_a_body, n_prompt, tiles_per_seq),
        grid=(N_LOAD + n_prompt + n_sample,),
        in_specs=[*xin, _whole(g1.shape), _slabs(wup), _slabs(wdn), _whole(gmix.shape), _slabs(win),
                  _whole(conv_w.shape), _whole(conv_b.shape)],
        out_specs=[*xin, *pa, *pb, pl.BlockSpec((tile, CONV_DIM), lambda i: (conv_tile(i), 0)),
                   pl.BlockSpec((1, CONV_W - 1, CONV_DIM), lambda i: (conv_tile(i) // tiles_per_seq, 0, 0))],
        out_shape=[jax.ShapeDtypeStruct((tp, D_MODEL), f32), jax.ShapeDtypeStruct((ts, D_MODEL), f32),
                   jax.ShapeDtypeStruct((tp, PA_COLS - CONV_DIM), bf16), jax.ShapeDtypeStruct((ts, PA_COLS), bf16),
                   jax.ShapeDtypeStruct((tp, PB_COLS), f32), jax.ShapeDtypeStruct((ts, PB_COLS), f32),
                   jax.ShapeDtypeStruct((tp, CONV_DIM), bf16),
                   jax.ShapeDtypeStruct((n_seq, CONV_W - 1, CONV_DIM), f32)],
        scratch_shapes=[pltpu.VMEM(wup.shape[1:], bf16), pltpu.VMEM(wdn.shape[1:], bf16),
                        pltpu.VMEM((D_MODEL, PA_COLS + PB_COLS), bf16),
                        pltpu.VMEM((SUBLANES + tile, CONV_DIM), f32)],
        compiler_params=pltpu.CompilerParams(dimension_semantics=("arbitrary",), vmem_limit_bytes=VMEM_LIMIT_BYTES),
        name="token_a",
    )(xp, xs, g1, wup, wdn, gmix, win, conv_w, conv_b)


def _token_b(x1p, x1s, mxp, mxs, wout, g2, wup, wdn, gfin):
    tp, ts = x1p.shape[0], x1s.shape[0]
    tile = TOKEN_B_TILE
    assert tp % tile == 0 and ts % tile == 0
    n_prompt, n_sample = tp // tile, ts // tile
    xin = _two_group_specs(n_prompt, D_MODEL, tile)
    return pl.pallas_call(
        functools.partial(_token_b_body, n_prompt),
        grid=(n_prompt + n_sample,),
        in_specs=[*xin, *xin, _whole(wout.shape), _whole(g2.shape), _whole(wup.shape), _whole(wdn.shape),
                  _whole(gfin.shape)],
        out_specs=[*xin],
        out_shape=[jax.ShapeDtypeStruct((tp, D_MODEL), f32), jax.ShapeDtypeStruct((ts, D_MODEL), f32)],
        compiler_params=pltpu.CompilerParams(dimension_semantics=("arbitrary",), vmem_limit_bytes=VMEM_LIMIT_BYTES),
        name="token_b",
    )(x1p, x1s, mxp, mxs, wout, g2, wup, wdn, gfin)


def _level_halves(seq_rows):
    return [h for h in (1, 2, 4, 8, 16, 32, 64) if 2 * h <= seq_rows]


def _matmul_levels(seq_rows):
    return [h for h in _level_halves(seq_rows) if h < SUBLANES]


def _mixer_consts(seq_rows):
    t = np.arange(CHUNK)
    is_sample = seq_rows < CHUNK
    same = (t[:, None] // seq_rows) == (t[None, :] // seq_rows)
    tri = (same & (t[None, :] <= t[:, None])).astype(np.float32)
    last = ((t[:, None] // seq_rows) * seq_rows + seq_rows - 1 == t[None, :]).astype(np.float32)
    cum_gla, pair_mask = [tri], [np.eye(CHUNK, dtype=np.float32)]
    for h in _level_halves(seq_rows):
        ref_row = (t // (2 * h)) * (2 * h) + h
        if h in _matmul_levels(seq_rows):
            cum_gla.append((ref_row[:, None] == t[None, :]).astype(np.float32) @ tri)
        right = (t % (2 * h)) >= h
        blk = (t[:, None] // (2 * h)) == (t[None, :] // (2 * h))
        pair_mask.append((blk & right[:, None] & ~right[None, :]).astype(np.float32))
    cum_ssd = [tri]
    if is_sample:
        cum_gla.append(last @ tri)
        cum_ssd.append(last @ tri)
    hq = np.arange(D_QK) // GLA_DK
    hv = np.arange(D_GLA) // GLA_DV
    head_of_lane = np.arange(D_SSD) // SSD_HEAD_DIM
    expand = (np.arange(PB_COLS)[:, None] == head_of_lane[None, :]).astype(np.float32)
    consts = [jnp.asarray(_tiled3(np.concatenate(cum_ssd, 0)), bf16),
              jnp.asarray(_tiled3(np.concatenate(cum_gla, 0)), bf16),
              jnp.asarray(np.stack(pair_mask, 0), f32),
              jnp.asarray((hv[:, None] == hq[None, :]).astype(np.float32), f32),
              jnp.asarray(np.concatenate([expand] * 3, 0), bf16)]
    return consts


PROMPT_STREAMS = 8
SAMPLE_STREAMS = 1
N_SHARED_REFS = 14
EPILOGUE = "epilogue"


def _round_robin(streams):
    live = list(streams)
    waiting = []
    while live:
        for s in list(live):
            try:
                if next(s) == EPILOGUE:
                    live.remove(s)
                    waiting.append(s)
            except StopIteration:
                live.remove(s)
    for s in waiting:
        for _ in s:
            pass


def _mixer_body(seq_rows, n_cast, *refs):
    is_sample = seq_rows < CHUNK
    n_streams = refs[0].shape[0]
    n_in = 5 if is_sample else 3
    n_out = 4 if is_sample else 3
    n_shared = N_SHARED_REFS
    ins, shared = refs[:n_in], refs[n_in:n_in + n_shared]
    cast_in = refs[n_in + n_shared:n_in + n_shared + n_cast]
    outs = refs[n_in + n_shared + n_cast:n_in + n_shared + n_cast + n_out]
    cast_out = refs[n_in + n_shared + n_cast + n_out:n_in + n_shared + 2 * n_cast + n_out]
    scratch = refs[n_in + n_shared + 2 * n_cast + n_out:]
    is_last = None
    if not is_sample:
        st_ref, stbd_ref = scratch
        c_idx = pl.program_id(1)
        is_last = c_idx == pl.num_programs(1) - 1

        @pl.when(pl.program_id(0) * pl.num_programs(1) + c_idx < N_LOAD)
        def _():
            for src, dst in zip(cast_in, cast_out):
                dst[...] = src[...].astype(bf16)

        @pl.when(c_idx == 0)
        def _():
            st_ref[...] = jnp.zeros(st_ref.shape, f32)
            stbd_ref[...] = jnp.zeros(stbd_ref.shape, f32)

    at = lambda group, j: [r.at[j] for r in group]
    _round_robin([_mixer_phases(seq_rows, at(ins, j), shared, at(outs, j), at(scratch, j), is_last)
                  for j in range(n_streams)])


def _mixer_phases(seq_rows, ins, shared, outs, scratch, is_last):
    is_sample = seq_rows < CHUNK
    n_seq = CHUNK // seq_rows
    halves = _level_halves(seq_rows)
    n_lvl = len(halves)
    if is_sample:
        pa_ref, pb_ref, ssd0_ref, conv0_ref, gla0_ref = ins
        (cbuf,) = scratch
        mixed_ref, nconv_ref, nssd_ref, ngla_ref = outs
    else:
        pa_ref, pb_ref, xc_ref = ins
        st_ref, stbd_ref = scratch
        mixed_ref, nssd_ref, ngla_ref = outs
    (convw_ref, convb_ref, dsk_e_ref, gssd_ref, dtb_c_ref, alog_c_ref, wgk_ref, bgk_ref, ggla_ref,
     cum_ssd_ref, cum_gla_ref, pair_mask_ref, bdmask_ref, expand_ref) = shared

    C = CHUNK
    no_xbc = 0 if is_sample else CONV_DIM
    rows = lax.broadcasted_iota(jnp.int32, (C, 1), 0)
    lane_qk = lax.shift_right_logical(lax.broadcasted_iota(jnp.int32, (1, D_QK), 1), int(np.log2(GLA_DK)))

    if is_sample:
        conv = jnp.broadcast_to(convb_ref[...], (C, CONV_DIM))
        for tap in range(CONV_W - 1):
            cbuf[:, SUBLANES - (CONV_W - 1) + tap, :] = conv0_ref[tap]
        cbuf[:, SUBLANES:SUBLANES + seq_rows, :] = (
            pa_ref[:, OFF_XBC:OFF_XBC + CONV_DIM].astype(f32).reshape(n_seq, seq_rows, CONV_DIM))
        for i in range(CONV_W):
            shift = CONV_W - 1 - i
            win = cbuf[:, SUBLANES - shift:SUBLANES - shift + seq_rows, :].reshape(C, CONV_DIM)
            conv = conv + win * convw_ref[i:i + 1, :]
        for tap in range(CONV_W - 1):
            nconv_ref[tap] = cbuf[:, SUBLANES + seq_rows - (CONV_W - 1) + tap, :]
        xc = jax.nn.silu(conv)
    else:
        xc = xc_ref[...]
    xs = xc[:, :D_SSD].astype(f32)
    bm = xc[:, D_SSD:D_SSD + SSD_GROUPS * SSD_STATE].astype(bf16)
    cm = xc[:, D_SSD + SSD_GROUPS * SSD_STATE:].astype(bf16)

    small = pb_ref[...]
    dtp_c = jax.nn.softplus(small + dtb_c_ref[...])
    cums = _sel_dot(cum_ssd_ref[...], dtp_c * (-LOG2E * jnp.exp(alog_c_ref[...])))
    yield
    acum_c = cums[:C]
    acum_t = acum_c.T
    wide = _dot(jnp.concatenate(_split3(jnp.concatenate([cums, dtp_c], axis=0)), axis=1), expand_ref[...])
    yield
    acum_e, dtp_e = wide[:C], wide[-C:]
    acum_last = wide[C:2 * C] if is_sample else acum_e[C - 1:C, :]
    causal = cum_ssd_ref[0:C, 0:C].astype(f32) > 0
    xdt = xs * dtp_e
    xdt_bf = xdt.astype(bf16)
    lane = lax.broadcasted_iota(jnp.int32, (1, LANES), 1)
    left = lane < SSD_HEAD_DIM
    heads_per_group = N_SSD_HEADS // SSD_GROUPS
    gw = heads_per_group * SSD_HEAD_DIM
    y_parts = []
    for grp in range(SSD_GROUPS):
        cb = _dot_nt(cm[:, grp * SSD_STATE:(grp + 1) * SSD_STATE], bm[:, grp * SSD_STATE:(grp + 1) * SSD_STATE])
        for pair in range(heads_per_group // 2):
            sc = []
            for hh in range(2):
                h = grp * heads_per_group + pair * 2 + hh
                seg = acum_c[:, h:h + 1] - acum_t[h:h + 1, :]
                sc.append((cb * jnp.exp2(jnp.where(causal, seg, -jnp.inf))).astype(bf16))
            lo = (grp * heads_per_group + pair * 2) * SSD_HEAD_DIM
            xp = xdt_bf[:, lo:lo + LANES]
            bd = jnp.concatenate([jnp.where(left, xp, 0), jnp.where(left, 0, xp)], axis=0)
            y_parts.append(_dot(jnp.concatenate(sc, axis=1), bd))
            yield
    y = jnp.concatenate(y_parts, axis=1)

    to_end = jnp.exp2(jnp.minimum(acum_last - acum_e, 0.0))
    xs_end = (xdt * to_end).astype(bf16)
    e_acum = jnp.exp2(acum_e)
    bm_t = bm.T

    def ssd_inter(st):
        st_bf = st.astype(bf16)
        return jnp.concatenate([_dot(cm[:, grp * SSD_STATE:(grp + 1) * SSD_STATE], st_bf[:, grp * gw:(grp + 1) * gw])
                                for grp in range(SSD_GROUPS)], axis=1) * e_acum

    def ssd_update(st, decay_row, xs_rows):
        upd = jnp.concatenate([_dot(bm_t[grp * SSD_STATE:(grp + 1) * SSD_STATE, :], xs_rows[:, grp * gw:(grp + 1) * gw])
                               for grp in range(SSD_GROUPS)], axis=1)
        return st * decay_row + upd

    q = pa_ref[:, OFF_Q - no_xbc:OFF_Q - no_xbc + D_QK].astype(f32) * (GLA_DK ** -0.5)
    k = pa_ref[:, OFF_K - no_xbc:OFF_K - no_xbc + D_QK].astype(f32)
    v_bf = pa_ref[:, OFF_V - no_xbc:OFF_V - no_xbc + D_GLA]
    gk = _dot_x3(small, wgk_ref[...]) + bgk_ref[...]
    log_a = jax.nn.log_sigmoid(gk) * (LOG2E / GLA_GATE_NORMALIZER)
    bsel = _sel_dot(cum_gla_ref[...], log_a)
    yield
    bcum = bsel[:C]
    n_mm = len(_matmul_levels(seq_rows))
    b_last = bsel[(n_mm + 1) * C:] if is_sample else bcum[C - 1:C, :]
    k_bf = k.astype(bf16)

    def b_ref(lvl, half):
        if lvl < n_mm:
            return bsel[(lvl + 1) * C:(lvl + 2) * C]
        return jnp.concatenate([jnp.broadcast_to(bcum[r + half:r + half + 1, :], (2 * half, D_QK))
                                for r in range(0, C, 2 * half)], axis=0)

    def head_rows(x_bf):
        return jnp.concatenate([jnp.where(lane_qk == h, x_bf, jnp.zeros_like(x_bf)) for h in range(N_GLA_HEADS)], axis=0)

    a = _dot_nt(head_rows(q.astype(bf16)), k_bf)
    yield
    m = pair_mask_ref[0]
    att = [a[h * C:(h + 1) * C] * m for h in range(N_GLA_HEADS)]
    for lvl, half in enumerate(halves):
        decay = jnp.exp2(-jnp.abs(bcum - b_ref(lvl, half)))
        in_right = (rows & half) != 0
        u = jnp.where(in_right, q, k) * decay
        u_bf = u.astype(bf16)
        m = pair_mask_ref[lvl + 1]
        if half < SUBLANES:
            a = _dot_nt(head_rows(u_bf), u_bf)
            yield
            for h in range(N_GLA_HEADS):
                att[h] = att[h] + a[h * C:(h + 1) * C] * m
        else:
            starts = range(half, C, 2 * half)
            pick = lambda x: jnp.concatenate([x[r:r + half] for r in starts], axis=0)
            a = _dot_nt(head_rows(pick(u).astype(bf16)), u_bf)
            yield
            m_right = pick(m)
            for h in range(N_GLA_HEADS):
                upd = a[h * (C // 2):(h + 1) * (C // 2)] * m_right
                parts = []
                for i, r in enumerate(starts):
                    parts += [att[h][r - half:r], att[h][r:r + half] + upd[i * half:(i + 1) * half]]
                att[h] = jnp.concatenate(parts, axis=0)
    o = jnp.concatenate([_dot(att[h].astype(bf16), v_bf[:, h * GLA_DV:(h + 1) * GLA_DV])
                         for h in range(N_GLA_HEADS)], axis=1)
    yield

    q_in = (q * jnp.exp2(bcum)).astype(bf16)
    kd = (k * jnp.exp2(jnp.minimum(b_last - bcum, 0.0))).astype(bf16)
    v_t = v_bf.T
    bdmask = bdmask_ref[...]

    def gla_inter(stbd):
        return _dot_nt(q_in, stbd.astype(bf16))

    def gla_update(stbd, decay_row, kd_rows):
        return stbd * decay_row + bdmask * _dot(v_t, kd_rows)

    def gla_heads(stbd):
        acc = jnp.where(lane_qk == 0, stbd[0:GLA_DV, :], 0.0)
        for h in range(1, N_GLA_HEADS):
            acc = acc + jnp.where(lane_qk == h, stbd[h * GLA_DV:(h + 1) * GLA_DV, :], 0.0)
        return acc.T

    if not is_sample:
        st = st_ref[...]
        stbd = stbd_ref[...]
        y = y + ssd_inter(st)
        o = o + gla_inter(stbd)
        yield
        st_ref[...] = ssd_update(st, jnp.exp2(acum_last), xs_end)
        stbd_ref[...] = gla_update(stbd, jnp.exp2(b_last), kd)
        yield
    else:
        seq_of_row = lax.shift_right_logical(rows, int(np.log2(seq_rows)))
        bm_f32 = xc[:, D_SSD:D_SSD + SSD_GROUPS * SSD_STATE]
        cm_f32 = xc[:, D_SSD + SSD_GROUPS * SSD_STATE:]
        xs_end_f32 = xdt * to_end
        e_last_c = jnp.exp2(cums[C:])
        q_in_f32 = q * jnp.exp2(bcum)
        y_rows, o_rows = [], []
        g0_next = gla0_ref[0].T
        for s in range(n_seq):
            g0 = g0_next
            if s + 1 < n_seq:
                g0_next = gla0_ref[s + 1].T
            mine = seq_of_row == s
            r0 = s * seq_rows
            own = slice(r0, r0 + seq_rows)
            st = ssd0_ref[s]
            c_rows = cm_f32[own].astype(bf16)
            b_rows = bm_f32[own].astype(bf16)
            x_rows = xs_end_f32[own].astype(bf16)
            y_grp, st_new = [], []
            for grp in range(SSD_GROUPS):
                st_g = st[grp * gw:(grp + 1) * gw]
                y_grp.append(_dot_nt(c_rows[:, grp * SSD_STATE:(grp + 1) * SSD_STATE], st_g.astype(bf16)))
                upd = lax.dot_general(x_rows[:, grp * gw:(grp + 1) * gw], b_rows[:, grp * SSD_STATE:(grp + 1) * SSD_STATE],
                                      (((0,), (0,)), ((), ())), preferred_element_type=f32)
                for hh in range(heads_per_group):
                    h = grp * heads_per_group + hh
                    rows_h = slice(hh * SSD_HEAD_DIM, (hh + 1) * SSD_HEAD_DIM)
                    decay_h = jnp.broadcast_to(e_last_c[r0:r0 + 1, h:h + 1], (SSD_HEAD_DIM, SSD_STATE))
                    st_new.append(st_g[rows_h] * decay_h + upd[rows_h])
            y_rows.append(jnp.concatenate(y_grp, axis=1))
            nssd_ref[s] = jnp.concatenate(st_new, axis=0)
            stbd = jnp.concatenate([jnp.where(lane_qk == h, g0, 0.0) for h in range(N_GLA_HEADS)], axis=0)
            o_rows.append(_dot_nt(q_in_f32[r0:r0 + seq_rows].astype(bf16), stbd.astype(bf16)))
            ngla_ref[s] = gla_heads(gla_update(stbd, jnp.exp2(b_last[r0:r0 + 1, :]), jnp.where(mine, kd, 0)))
            yield
        y = y + jnp.concatenate(y_rows, axis=0) * e_acum
        o = o + jnp.concatenate(o_rows, axis=0)

    z = pa_ref[:, OFF_Z:OFF_Z + D_SSD].astype(f32)
    g = pa_ref[:, OFF_G - no_xbc:OFF_G - no_xbc + D_GLA].astype(f32)
    y = (y + dsk_e_ref[...] * xs) * jax.nn.silu(z)
    gsz = D_SSD // SSD_GROUPS
    y = jnp.concatenate([_rms(y[:, i * gsz:(i + 1) * gsz], gssd_ref[:, i * gsz:(i + 1) * gsz])
                         for i in range(SSD_GROUPS)], axis=1)
    o = jnp.concatenate([_rms(o[:, h * GLA_DV:(h + 1) * GLA_DV], ggla_ref[...]) for h in range(N_GLA_HEADS)], axis=1)
    o = o * jax.nn.silu(g)
    mixed_ref[:, :D_SSD] = y.astype(bf16)
    mixed_ref[:, D_SSD:] = o.astype(bf16)

    if not is_sample:
        yield EPILOGUE

        @pl.when(is_last)
        def _():
            nssd_ref[0] = st_ref[...].T
            ngla_ref[0] = gla_heads(stbd_ref[...])


def _mixer(pa, pb, params, states, n_batch, seq_len, xc=None, cast_weights=()):
    is_sample = states is not None
    assert not (is_sample and cast_weights) and is_sample == (xc is None)
    seq_rows = seq_len if is_sample else CHUNK
    assert CHUNK % seq_rows == 0 and (is_sample or seq_len % CHUNK == 0)
    n_seq = CHUNK // seq_rows
    consts = _mixer_consts(seq_rows)
    ns = SAMPLE_STREAMS if is_sample else PROMPT_STREAMS
    assert n_batch % (ns * n_seq) == 0
    rows_per_stream = n_batch * seq_len // ns
    seqs_per_stream = n_batch // ns

    if is_sample:
        grid = (seqs_per_stream // n_seq,)
        row_map = lambda c: (0, c, 0)
        seq_map = lambda c: (0, c, 0, 0)
        full = lambda shape: pl.BlockSpec(shape, lambda c: (0,) * len(shape))
        sems = ("arbitrary",)
    else:
        n_chunks = seq_len // CHUNK
        grid = (seqs_per_stream, n_chunks)
        row_map = lambda b, c: (0, b * n_chunks + c, 0)
        seq_map = lambda b, c: (0, b, 0, 0)
        full = lambda shape: pl.BlockSpec(shape, lambda b, c: (0,) * len(shape))
        sems = ("arbitrary", "arbitrary")

    def streamed(arr):
        return arr.reshape((ns, arr.shape[0] // ns) + arr.shape[1:])

    taps = CONV_W - 1
    conv_spec = pl.BlockSpec((ns, taps, n_seq, CONV_DIM), lambda *g: (0, 0, seq_map(*g)[1], 0))
    state_blocks = [(n_seq, D_SSD, SSD_STATE), (n_seq, D_QK, GLA_DV)]
    in_arrays = [streamed(pa), streamed(pb)]
    assert pa.shape[1] == (PA_COLS if is_sample else PA_COLS - CONV_DIM)
    in_specs = [pl.BlockSpec((ns, CHUNK, pa.shape[1]), row_map), pl.BlockSpec((ns, CHUNK, PB_COLS), row_map)]
    if is_sample:
        ssd0, conv0, gla0 = states
        conv0_t = jnp.transpose(streamed(conv0), (0, 2, 1, 3))
        in_arrays += [streamed(ssd0), conv0_t, streamed(gla0)]
        in_specs += [pl.BlockSpec((ns,) + state_blocks[0], seq_map), conv_spec,
                     pl.BlockSpec((ns,) + state_blocks[1], seq_map)]
    else:
        in_arrays.append(streamed(xc))
        in_specs.append(pl.BlockSpec((ns, CHUNK, CONV_DIM), row_map))
    for arr in list(params) + consts:
        in_arrays.append(arr)
        in_specs.append(full(arr.shape))

    out_shape = [jax.ShapeDtypeStruct((ns, rows_per_stream, D_MODEL), bf16)]
    out_specs = [pl.BlockSpec((ns, CHUNK, D_MODEL), row_map)]
    if is_sample:
        out_shape.append(jax.ShapeDtypeStruct((ns, taps, seqs_per_stream, CONV_DIM), f32))
        out_specs.append(conv_spec)
    out_shape += [jax.ShapeDtypeStruct((ns, seqs_per_stream) + blk[1:], f32) for blk in state_blocks]
    out_specs += [pl.BlockSpec((ns,) + blk, seq_map) for blk in state_blocks]
    for w in cast_weights:
        _, w_rows, w_cols = w.shape
        slab = w_rows // N_LOAD
        assert w_rows % N_LOAD == 0 and slab % BF16_ROWS == 0 and grid[0] * grid[1] >= N_LOAD
        step = lambda b, c: jnp.minimum(b * n_chunks + c, N_LOAD - 1)
        in_arrays.append(w)
        in_specs.append(pl.BlockSpec((None, slab, w_cols), lambda b, c: (0, step(b, c), 0)))
        out_shape.append(jax.ShapeDtypeStruct((w_rows, w_cols), bf16))
        out_specs.append(pl.BlockSpec((slab, w_cols), lambda b, c: (step(b, c), 0)))
    if is_sample:
        scratch = [pltpu.VMEM((ns, n_seq, SUBLANES + seq_rows, CONV_DIM), f32)]
    else:
        scratch = [pltpu.VMEM((ns, SSD_STATE, D_SSD), f32), pltpu.VMEM((ns, D_GLA, D_QK), f32)]
    results = pl.pallas_call(
        functools.partial(_mixer_body, seq_rows, len(cast_weights)),
        grid=grid, in_specs=in_specs, out_specs=out_specs, out_shape=out_shape, scratch_shapes=scratch,
        compiler_params=pltpu.CompilerParams(dimension_semantics=sems, vmem_limit_bytes=VMEM_LIMIT_BYTES),
        name="mixer_sample" if is_sample else "mixer_prompt",
    )(*in_arrays)
    merged = lambda arr: arr.reshape((arr.shape[0] * arr.shape[1],) + arr.shape[2:])
    if is_sample:
        mixed, nconv, nssd, ngla = results
        return merged(mixed), merged(jnp.transpose(nconv, (0, 2, 1, 3))), merged(nssd), merged(ngla)
    mixed, nssd, ngla, *cast = results
    return merged(mixed), merged(nssd), merged(ngla), cast


def _row(vec):
    return vec.reshape(1, -1).astype(f32)


def _pad_lanes(vec, width):
    return jnp.concatenate([vec.astype(f32), jnp.zeros((width - vec.shape[0],), f32)]).reshape(1, width)


def _mixer_params(conv_w, conv_b, dt_bias, a_log, d_skip, g_ssd_norm, w_gk2, b_gk, g_gla_norm):
    rep = lambda vec: _row(jnp.repeat(vec, SSD_HEAD_DIM))
    wgk = jnp.zeros((PB_COLS, D_QK), f32).at[N_SSD_HEADS:N_SSD_HEADS + GLA_RANK, :].set(w_gk2.astype(f32))
    return [conv_w.astype(f32), _row(conv_b), rep(d_skip), _row(g_ssd_norm),
            _pad_lanes(dt_bias, PB_COLS), _pad_lanes(a_log, PB_COLS), _stack_x3(wgk), _row(b_gk), _row(g_gla_norm)]


def kernel(x_prompt, x_sample, state_ssd, state_conv, state_gla, g_ffn1, w_ffn1_in, w_ffn1_out, g_mix, w_in, conv_w, conv_b, dt_bias, a_log, d_skip, g_ssd_norm, w_gk2, b_gk, g_gla_norm, w_out, g_ffn2, w_ffn2_in, w_ffn2_out, g_final):
    bp, lp, _ = x_prompt.shape
    bs, ls, _ = x_sample.shape
    assert w_in.shape[0] == 1, "single-layer step: the final norm is fused into the layer's last kernel"
    xp = x_prompt.reshape(bp * lp, D_MODEL)
    xs = x_sample.reshape(bs * ls, D_MODEL)
    params = _mixer_params(conv_w[0], conv_b[0], dt_bias[0], a_log[0], d_skip[0], g_ssd_norm[0],
                           w_gk2[0], b_gk[0], g_gla_norm[0])
    x1p, x1s, pap, pas, pbp, pbs, xcp, p_conv = _token_a(
        xp, xs, _row(g_ffn1[0]), w_ffn1_in.astype(f32), w_ffn1_out.astype(f32), _row(g_mix[0]), w_in.astype(f32),
        params[0], params[1], lp)
    mxp, p_ssd, p_gla, (w_out_bf, w_up2_bf, w_dn2_bf) = _mixer(
        pap, pbp, params, None, bp, lp, xc=xcp,
        cast_weights=(w_out.astype(f32), w_ffn2_in.astype(f32), w_ffn2_out.astype(f32)))
    s_states = (state_ssd[0].reshape(bs, D_SSD, SSD_STATE), state_conv[0], state_gla[0].reshape(bs, D_QK, GLA_DV))
    mxs, s_conv, s_ssd, s_gla = _mixer(pas, pbs, params, s_states, bs, ls)
    yp, ys = _token_b(x1p, x1s, mxp, mxs, w_out_bf, _row(g_ffn2[0]), w_up2_bf, w_dn2_bf, _row(g_final))
    return (yp.reshape(bp, lp, D_MODEL), ys.reshape(bs, ls, D_MODEL),
            p_ssd.reshape(1, bp, N_SSD_HEADS, SSD_HEAD_DIM, SSD_STATE), p_conv[None],
            p_gla.reshape(1, bp, N_GLA_HEADS, GLA_DK, GLA_DV),
            s_ssd.reshape(1, bs, N_SSD_HEADS, SSD_HEAD_DIM, SSD_STATE), s_conv[None],
            s_gla.reshape(1, bs, N_GLA_HEADS, GLA_DK, GLA_DV))
```

```python
import functools

import jax
import jax.numpy as jnp
import numpy as np
from jax import lax
from jax.experimental import pallas as pl
from jax.experimental.pallas import tpu as pltpu

f32 = jnp.float32
bf16 = jnp.bfloat16

D_MODEL = 1024
D_SSD = 512
SSD_HEAD_DIM = 64
N_SSD_HEADS = 8
SSD_STATE = 128
SSD_GROUPS = 2
CONV_W = 4
CONV_DIM = D_SSD + 2 * SSD_GROUPS * SSD_STATE
D_GLA = 512
N_GLA_HEADS = 4
GLA_DV = 128
GLA_DK = 64
GLA_RANK = 16
GLA_GATE_NORMALIZER = 16.0
D_FF = 2816
EPS = 1e-6
D_QK = N_GLA_HEADS * GLA_DK
LOG2E = float(np.log2(np.e))

SUBLANES = 8
LANES = 128
BF16_ROWS = 16
VMEM_LIMIT_BYTES = 60 * 1024 * 1024

CHUNK = 128
TOKEN_TILE = 256
TOKEN_B_TILE = 512

PA_COLS = D_SSD + CONV_DIM + 2 * D_QK + 2 * D_GLA
PB_COLS = LANES
OFF_Z, OFF_XBC, OFF_Q, OFF_K, OFF_V, OFF_G = 0, 512, 1536, 1792, 2048, 2560


def _dot(a, b):
    return jnp.dot(a, b, preferred_element_type=f32)


def _dot_nt(a, b):
    return lax.dot_general(a, b, (((1,), (1,)), ((), ())), preferred_element_type=f32)


def _split3(x):
    hi = x.astype(bf16)
    r1 = x - hi.astype(f32)
    mid = r1.astype(bf16)
    lo = (r1 - mid.astype(f32)).astype(bf16)
    return hi, mid, lo


def _sel_dot(sel3, x):
    return _dot(sel3, jnp.concatenate(_split3(x), axis=0))


def _tiled3(sel):
    return np.concatenate([sel] * 3, axis=1)


def _split2(x):
    hi = x.astype(bf16)
    return hi, (x - hi.astype(f32)).astype(bf16)


def _dot_x3(a, b_stack):
    a_hi, a_lo = _split2(a)
    return _dot(jnp.concatenate([a_hi, a_hi, a_lo], axis=1), b_stack)


def _stack_x3(b):
    b_hi, b_lo = _split2(b)
    return jnp.concatenate([b_hi, b_lo, b_hi], axis=0)


def _rms(x, g):
    return x * lax.rsqrt(jnp.mean(x * x, axis=-1, keepdims=True) + EPS) * g


N_LOAD = 16

W_IN_DT = D_SSD + CONV_DIM
W_IN_Q = W_IN_DT + N_SSD_HEADS
W_IN_LR = W_IN_Q + 2 * D_QK + 2 * D_GLA
W_IN_COLS = W_IN_LR + GLA_RANK


def _load_slab(step, slab_ref, dst_ref, regroup=None):
    rows = slab_ref.shape[0]
    assert rows % BF16_ROWS == 0
    val = slab_ref[...]
    if regroup is not None:
        val = regroup(val)
    dst_ref[pl.ds(pl.multiple_of(step * rows, BF16_ROWS), rows), :] = val.astype(bf16)


def _regroup_w_in(w):
    small = jnp.concatenate([w[:, W_IN_DT:W_IN_Q], w[:, W_IN_LR:W_IN_COLS],
                             jnp.zeros((w.shape[0], PB_COLS - N_SSD_HEADS - GLA_RANK), w.dtype)], axis=1)
    return jnp.concatenate([w[:, :W_IN_DT], w[:, W_IN_Q:W_IN_LR], small], axis=1)


ROW_STREAMS = 2


def _token_a_phases(row0, n_rows, x_ref, x1_ref, pa_ref, pb_ref, g1, wup, wdn, gmix, win, under_first_matmul=None,
                    keep_xbc=None):
    rows = pl.ds(row0, n_rows)
    x = x_ref[rows, :]
    gu = _dot(_rms(x, g1[...]).astype(bf16), wup[...])
    if under_first_matmul is not None:
        under_first_matmul()
    yield
    act = (jax.nn.silu(gu[:, :D_FF]) * gu[:, D_FF:]).astype(bf16)
    x1 = x + 0.5 * _dot(act, wdn[...])
    yield
    x1_ref[rows, :] = x1
    pr = _dot(_rms(x1, gmix[...]).astype(bf16), win[...])
    yield
    pb_ref[rows, :] = pr[:, PA_COLS:]
    pa_ref[rows, :OFF_XBC] = jax.nn.silu(pr[:, OFF_Z:OFF_Z + D_SSD]).astype(bf16)
    g_gate = jax.nn.silu(pr[:, OFF_G:OFF_G + D_GLA]).astype(bf16)
    if keep_xbc is None:
        pa_ref[rows, OFF_XBC:OFF_G] = pr[:, OFF_XBC:OFF_G].astype(bf16)
        pa_ref[rows, OFF_G:] = g_gate
    else:
        pa_ref[rows, OFF_XBC:OFF_G - CONV_DIM] = pr[:, OFF_Q:OFF_G].astype(bf16)
        pa_ref[rows, OFF_G - CONV_DIM:] = g_gate
        keep_xbc[pl.ds(SUBLANES + row0, n_rows), :] = pr[:, OFF_XBC:OFF_XBC + CONV_DIM]


def _token_a_body(n_prompt, tiles_per_seq, xp, xs, g1, wup_f, wdn_f, gmix, win_f, convw, convb,
                  x1p, x1s, pap, pas, pbp, pbs, xc, nconv, wup, wdn, win, cbuf):
    i = pl.program_id(0)
    tile = xp.shape[0]
    part = tile // ROW_STREAMS

    def shift_one_row(a):
        above = jnp.concatenate([a[-1:], a[:-1]], axis=0)
        last = lax.broadcasted_iota(jnp.int32, (1, SUBLANES, 1), 1) == SUBLANES - 1
        return pltpu.roll(jnp.where(last, above, a), 1, axis=1)

    def conv_previous_tile():
        t = lax.rem(i - N_LOAD - 1, tiles_per_seq)
        cbuf[0:SUBLANES, :] = jnp.where(t == 0, 0.0, cbuf[0:SUBLANES, :])
        ext = cbuf[...].reshape(tile // SUBLANES + 1, SUBLANES, CONV_DIM)
        acc = ext * convw[0:1, :]
        for tap in range(1, CONV_W):
            acc = shift_one_row(acc) + ext * convw[tap:tap + 1, :]
        xc[...] = jax.nn.silu(acc[1:].reshape(tile, CONV_DIM) + convb[...]).astype(bf16)
        nconv[0] = cbuf[SUBLANES + tile - (CONV_W - 1):SUBLANES + tile, :]
        cbuf[0:SUBLANES, :] = cbuf[tile:tile + SUBLANES, :]

    def streams(x_ref, x1_ref, pa_ref, pb_ref, pending, keep_xbc):
        _round_robin([_token_a_phases(j * part, part, x_ref, x1_ref, pa_ref, pb_ref, g1, wup, wdn, gmix, win,
                                      pending if j == 0 else None, keep_xbc) for j in range(ROW_STREAMS)])

    @pl.when(i < N_LOAD)
    def _():
        _load_slab(i, wup_f, wup)
        _load_slab(i, wdn_f, wdn)
        _load_slab(i, win_f, win, _regroup_w_in)

    @pl.when(i == 0)
    def _():
        cbuf[...] = jnp.zeros(cbuf.shape, f32)

    first_sample = N_LOAD + n_prompt
    pl.when(jnp.logical_and(i >= N_LOAD, i < first_sample))(
        lambda: streams(xp, x1p, pap, pbp, conv_previous_tile, cbuf))
    pl.when(i == first_sample)(conv_previous_tile)
    pl.when(i >= first_sample)(lambda: streams(xs, x1s, pas, pbs, None, None))


def _token_b_phases(rows, x_ref, m_ref, y_ref, wout, g2, wup, wdn, gfin):
    x2 = x_ref[rows, :] + _dot(m_ref[rows, :], wout[...])
    yield
    gu = _dot(_rms(x2, g2[...]).astype(bf16), wup[...])
    yield
    act = (jax.nn.silu(gu[:, :D_FF]) * gu[:, D_FF:]).astype(bf16)
    x3 = x2 + 0.5 * _dot(act, wdn[...])
    yield
    y_ref[rows, :] = _rms(x3, gfin[...])


def _token_b_body(n_prompt, x1p, x1s, mxp, mxs, wout, g2, wup, wdn, gfin, yp, ys):
    def compute(x_ref, m_ref, y_ref):
        part = x_ref.shape[0] // ROW_STREAMS
        _round_robin([_token_b_phases(pl.ds(j * part, part), x_ref, m_ref, y_ref, wout, g2, wup, wdn, gfin)
                      for j in range(ROW_STREAMS)])

    i = pl.program_id(0)
    pl.when(i < n_prompt)(lambda: compute(x1p, mxp, yp))
    pl.when(i >= n_prompt)(lambda: compute(x1s, mxs, ys))


def _two_group_specs(n_prompt, cols, tile, first=0):
    prompt = pl.BlockSpec((tile, cols), lambda i: (jnp.clip(i - first, 0, n_prompt - 1), 0))
    sample = pl.BlockSpec((tile, cols), lambda i: (jnp.maximum(i - first - n_prompt, 0), 0))
    return prompt, sample


def _whole(shape):
    return pl.BlockSpec(shape, lambda i: (0,) * len(shape), pipeline_mode=pl.Buffered(1))


def _slabs(weight):
    _, rows, cols = weight.shape
    assert rows % N_LOAD == 0
    return pl.BlockSpec((None, rows // N_LOAD, cols), lambda i: (0, jnp.minimum(i, N_LOAD - 1), 0))


def _token_a(xp, xs, g1, wup, wdn, gmix, win, conv_w, conv_b, prompt_len):
    tp, ts = xp.shape[0], xs.shape[0]
    tile = TOKEN_TILE
    n_prompt, n_sample = tp // tile, ts // tile
    assert tp % tile == 0 and ts % tile == 0 and win.shape[2] == W_IN_COLS and prompt_len % tile == 0
    tiles_per_seq = prompt_len // tile
    n_seq = tp // prompt_len
    xin = _two_group_specs(n_prompt, D_MODEL, tile, N_LOAD)
    pa = (_two_group_specs(n_prompt, PA_COLS - CONV_DIM, tile, N_LOAD)[0],
          _two_group_specs(n_prompt, PA_COLS, tile, N_LOAD)[1])
    pb = _two_group_specs(n_prompt, PB_COLS, tile, N_LOAD)
    conv_tile = lambda i: jnp.clip(i - N_LOAD - 1, 0, n_prompt - 1)
    return pl.pallas_call(
        functools.partial(_token_a_body, n_prompt, tiles_per_seq),
        grid=(N_LOAD + n_prompt + n_sample,),
        in_specs=[*xin, _whole(g1.shape), _slabs(wup), _slabs(wdn), _whole(gmix.shape), _slabs(win),
                  _whole(conv_w.shape), _whole(conv_b.shape)],
        out_specs=[*xin, *pa, *pb, pl.BlockSpec((tile, CONV_DIM), lambda i: (conv_tile(i), 0)),
                   pl.BlockSpec((1, CONV_W - 1, CONV_DIM), lambda i: (conv_tile(i) // tiles_per_seq, 0, 0))],
        out_shape=[jax.ShapeDtypeStruct((tp, D_MODEL), f32), jax.ShapeDtypeStruct((ts, D_MODEL), f32),
                   jax.ShapeDtypeStruct((tp, PA_COLS - CONV_DIM), bf16), jax.ShapeDtypeStruct((ts, PA_COLS), bf16),
                   jax.ShapeDtypeStruct((tp, PB_COLS), f32), jax.ShapeDtypeStruct((ts, PB_COLS), f32),
                   jax.ShapeDtypeStruct((tp, CONV_DIM), bf16),
                   jax.ShapeDtypeStruct((n_seq, CONV_W - 1, CONV_DIM), f32)],
        scratch_shapes=[pltpu.VMEM(wup.shape[1:], bf16), pltpu.VMEM(wdn.shape[1:], bf16),
                        pltpu.VMEM((D_MODEL, PA_COLS + PB_COLS), bf16),
                        pltpu.VMEM((SUBLANES + tile, CONV_DIM), f32)],
        compiler_params=pltpu.CompilerParams(dimension_semantics=("arbitrary",), vmem_limit_bytes=VMEM_LIMIT_BYTES),
        name="token_a",
    )(xp, xs, g1, wup, wdn, gmix, win, conv_w, conv_b)


def _token_b(x1p, x1s, mxp, mxs, wout, g2, wup, wdn, gfin):
    tp, ts = x1p.shape[0], x1s.shape[0]
    tile = TOKEN_B_TILE
    assert tp % tile == 0 and ts % tile == 0
    n_prompt, n_sample = tp // tile, ts // tile
    xin = _two_group_specs(n_prompt, D_MODEL, tile)
    return pl.pallas_call(
        functools.partial(_token_b_body, n_prompt),
        grid=(n_prompt + n_sample,),
        in_specs=[*xin, *xin, _whole(wout.shape), _whole(g2.shape), _whole(wup.shape), _whole(wdn.shape),
                  _whole(gfin.shape)],
        out_specs=[*xin],
        out_shape=[jax.ShapeDtypeStruct((tp, D_MODEL), f32), jax.ShapeDtypeStruct((ts, D_MODEL), f32)],
        compiler_params=pltpu.CompilerParams(dimension_semantics=("arbitrary",), vmem_limit_bytes=VMEM_LIMIT_BYTES),
        name="token_b",
    )(x1p, x1s, mxp, mxs, wout, g2, wup, wdn, gfin)


def _level_halves(seq_rows):
    return [h for h in (1, 2, 4, 8, 16, 32, 64) if 2 * h <= seq_rows]


def _matmul_levels(seq_rows):
    return [h for h in _level_halves(seq_rows) if h < SUBLANES]


def _mixer_consts(seq_rows):
    t = np.arange(CHUNK)
    is_sample = seq_rows < CHUNK
    same = (t[:, None] // seq_rows) == (t[None, :] // seq_rows)
    tri = (same & (t[None, :] <= t[:, None])).astype(np.float32)
    last = ((t[:, None] // seq_rows) * seq_rows + seq_rows - 1 == t[None, :]).astype(np.float32)
    cum_gla, pair_mask = [tri], [np.eye(CHUNK, dtype=np.float32)]
    for h in _level_halves(seq_rows):
        ref_row = (t // (2 * h)) * (2 * h) + h
        if h in _matmul_levels(seq_rows):
            cum_gla.append((ref_row[:, None] == t[None, :]).astype(np.float32) @ tri)
        right = (t % (2 * h)) >= h
        blk = (t[:, None] // (2 * h)) == (t[None, :] // (2 * h))
        pair_mask.append((blk & right[:, None] & ~right[None, :]).astype(np.float32))
    cum_ssd = [tri]
    if is_sample:
        cum_gla.append(last @ tri)
        cum_ssd.append(last @ tri)
    hq = np.arange(D_QK) // GLA_DK
    hv = np.arange(D_GLA) // GLA_DV
    head_of_lane = np.arange(D_SSD) // SSD_HEAD_DIM
    expand = (np.arange(PB_COLS)[:, None] == head_of_lane[None, :]).astype(np.float32)
    consts = [jnp.asarray(_tiled3(np.concatenate(cum_ssd, 0)), bf16),
              jnp.asarray(_tiled3(np.concatenate(cum_gla, 0)), bf16),
              jnp.asarray(np.stack(pair_mask, 0), f32),
              jnp.asarray((hv[:, None] == hq[None, :]).astype(np.float32), f32),
              jnp.asarray(np.concatenate([expand] * 3, 0), bf16)]
    return consts


PROMPT_STREAMS = 8
SAMPLE_STREAMS = 1
N_SHARED_REFS = 14
EPILOGUE = "epilogue"


def _round_robin(streams):
    live = list(streams)
    waiting = []
    while live:
        for s in list(live):
            try:
                if next(s) == EPILOGUE:
                    live.remove(s)
                    waiting.append(s)
            except StopIteration:
                live.remove(s)
    for s in waiting:
        for _ in s:
            pass


def _mixer_body(seq_rows, n_cast, *refs):
    is_sample = seq_rows < CHUNK
    n_streams = refs[0].shape[0]
    n_in = 5 if is_sample else 3
    n_out = 4 if is_sample else 3
    n_shared = N_SHARED_REFS
    ins, shared = refs[:n_in], refs[n_in:n_in + n_shared]
    cast_in = refs[n_in + n_shared:n_in + n_shared + n_cast]
    outs = refs[n_in + n_shared + n_cast:n_in + n_shared + n_cast + n_out]
    cast_out = refs[n_in + n_shared + n_cast + n_out:n_in + n_shared + 2 * n_cast + n_out]
    scratch = refs[n_in + n_shared + 2 * n_cast + n_out:]
    is_last = None
    if not is_sample:
        st_ref, stbd_ref = scratch
        c_idx = pl.program_id(1)
        is_last = c_idx == pl.num_programs(1) - 1

        @pl.when(pl.program_id(0) * pl.num_programs(1) + c_idx < N_LOAD)
        def _():
            for src, dst in zip(cast_in, cast_out):
                dst[...] = src[...].astype(bf16)

        @pl.when(c_idx == 0)
        def _():
            st_ref[...] = jnp.zeros(st_ref.shape, f32)
            stbd_ref[...] = jnp.zeros(stbd_ref.shape, f32)

    at = lambda group, j: [r.at[j] for r in group]
    _round_robin([_mixer_phases(seq_rows, at(ins, j), shared, at(outs, j), at(scratch, j), is_last)
                  for j in range(n_streams)])


def _mixer_phases(seq_rows, ins, shared, outs, scratch, is_last):
    is_sample = seq_rows < CHUNK
    n_seq = CHUNK // seq_rows
    halves = _level_halves(seq_rows)
    n_lvl = len(halves)
    if is_sample:
        pa_ref, pb_ref, ssd0_ref, conv0_ref, gla0_ref = ins
        (cbuf,) = scratch
        mixed_ref, nconv_ref, nssd_ref, ngla_ref = outs
    else:
        pa_ref, pb_ref, xc_ref = ins
        st_ref, stbd_ref = scratch
        mixed_ref, nssd_ref, ngla_ref = outs
    (convw_ref, convb_ref, dsk_e_ref, gssd_ref, dtb_c_ref, alog_c_ref, wgk_ref, bgk_ref, ggla_ref,
     cum_ssd_ref, cum_gla_ref, pair_mask_ref, bdmask_ref, expand_ref) = shared

    C = CHUNK
    no_xbc = 0 if is_sample else CONV_DIM
    rows = lax.broadcasted_iota(jnp.int32, (C, 1), 0)
    lane_qk = lax.shift_right_logical(lax.broadcasted_iota(jnp.int32, (1, D_QK), 1), int(np.log2(GLA_DK)))

    if is_sample:
        conv = jnp.broadcast_to(convb_ref[...], (C, CONV_DIM))
        for tap in range(CONV_W - 1):
            cbuf[:, SUBLANES - (CONV_W - 1) + tap, :] = conv0_ref[tap]
        cbuf[:, SUBLANES:SUBLANES + seq_rows, :] = (
            pa_ref[:, OFF_XBC:OFF_XBC + CONV_DIM].astype(f32).reshape(n_seq, seq_rows, CONV_DIM))
        for i in range(CONV_W):
            shift = CONV_W - 1 - i
            win = cbuf[:, SUBLANES - shift:SUBLANES - shift + seq_rows, :].reshape(C, CONV_DIM)
            conv = conv + win * convw_ref[i:i + 1, :]
        for tap in range(CONV_W - 1):
            nconv_ref[tap] = cbuf[:, SUBLANES + seq_rows - (CONV_W - 1) + tap, :]
        xc = jax.nn.silu(conv)
    else:
        xc = xc_ref[...]
    xs = xc[:, :D_SSD].astype(f32)
    bm = xc[:, D_SSD:D_SSD + SSD_GROUPS * SSD_STATE].astype(bf16)
    cm = xc[:, D_SSD + SSD_GROUPS * SSD_STATE:].astype(bf16)

    small = pb_ref[...]
    dtp_c = jax.nn.softplus(small + dtb_c_ref[...])
    cums = _sel_dot(cum_ssd_ref[...], dtp_c * (-LOG2E * jnp.exp(alog_c_ref[...])))
    yield
    acum_c = cums[:C]
    acum_t = acum_c.T
    wide = _dot(jnp.concatenate(_split3(jnp.concatenate([cums, dtp_c], axis=0)), axis=1), expand_ref[...])
    yield
    acum_e, dtp_e = wide[:C], wide[-C:]
    acum_last = wide[C:2 * C] if is_sample else acum_e[C - 1:C, :]
    causal = cum_ssd_ref[0:C, 0:C].astype(f32) > 0
    xdt = xs * dtp_e
    xdt_bf = xdt.astype(bf16)
    lane = lax.broadcasted_iota(jnp.int32, (1, LANES), 1)
    left = lane < SSD_HEAD_DIM
    heads_per_group = N_SSD_HEADS // SSD_GROUPS
    gw = heads_per_group * SSD_HEAD_DIM
    y_parts = []
    for grp in range(SSD_GROUPS):
        cb = _dot_nt(cm[:, grp * SSD_STATE:(grp + 1) * SSD_STATE], bm[:, grp * SSD_STATE:(grp + 1) * SSD_STATE])
        for pair in range(heads_per_group // 2):
            sc = []
            for hh in range(2):
                h = grp * heads_per_group + pair * 2 + hh
                seg = acum_c[:, h:h + 1] - acum_t[h:h + 1, :]
                sc.append((cb * jnp.exp2(jnp.where(causal, seg, -jnp.inf))).astype(bf16))
            lo = (grp * heads_per_group + pair * 2) * SSD_HEAD_DIM
            xp = xdt_bf[:, lo:lo + LANES]
            bd = jnp.concatenate([jnp.where(left, xp, 0), jnp.where(left, 0, xp)], axis=0)
            y_parts.append(_dot(jnp.concatenate(sc, axis=1), bd))
            yield
    y = jnp.concatenate(y_parts, axis=1)

    to_end = jnp.exp2(jnp.minimum(acum_last - acum_e, 0.0))
    xs_end = (xdt * to_end).astype(bf16)
    e_acum = jnp.exp2(acum_e)
    bm_t = bm.T

    def ssd_inter(st):
        st_bf = st.astype(bf16)
        return jnp.concatenate([_dot(cm[:, grp * SSD_STATE:(grp + 1) * SSD_STATE], st_bf[:, grp * gw:(grp + 1) * gw])
                                for grp in range(SSD_GROUPS)], axis=1) * e_acum

    def ssd_update(st, decay_row, xs_rows):
        upd = jnp.concatenate([_dot(bm_t[grp * SSD_STATE:(grp + 1) * SSD_STATE, :], xs_rows[:, grp * gw:(grp + 1) * gw])
                               for grp in range(SSD_GROUPS)], axis=1)
        return st * decay_row + upd

    q = pa_ref[:, OFF_Q - no_xbc:OFF_Q - no_xbc + D_QK].astype(f32) * (GLA_DK ** -0.5)
    k = pa_ref[:, OFF_K - no_xbc:OFF_K - no_xbc + D_QK].astype(f32)
    v_bf = pa_ref[:, OFF_V - no_xbc:OFF_V - no_xbc + D_GLA]
    gk = _dot_x3(small, wgk_ref[...]) + bgk_ref[...]
    log_a = jax.nn.log_sigmoid(gk) * (LOG2E / GLA_GATE_NORMALIZER)
    bsel = _sel_dot(cum_gla_ref[...], log_a)
    yield
    bcum = bsel[:C]
    n_mm = len(_matmul_levels(seq_rows))
    b_last = bsel[(n_mm + 1) * C:] if is_sample else bcum[C - 1:C, :]
    k_bf = k.astype(bf16)

    def b_ref(lvl, half):
        if lvl < n_mm:
            return bsel[(lvl + 1) * C:(lvl + 2) * C]
        return jnp.concatenate([jnp.broadcast_to(bcum[r + half:r + half + 1, :], (2 * half, D_QK))
                                for r in range(0, C, 2 * half)], axis=0)

    def head_rows(x_bf):
        return jnp.concatenate([jnp.where(lane_qk == h, x_bf, jnp.zeros_like(x_bf)) for h in range(N_GLA_HEADS)], axis=0)

    a = _dot_nt(head_rows(q.astype(bf16)), k_bf)
    yield
    m = pair_mask_ref[0]
    att = [a[h * C:(h + 1) * C] * m for h in range(N_GLA_HEADS)]
    for lvl, half in enumerate(halves):
        decay = jnp.exp2(-jnp.abs(bcum - b_ref(lvl, half)))
        in_right = (rows & half) != 0
        u = jnp.where(in_right, q, k) * decay
        u_bf = u.astype(bf16)
        m = pair_mask_ref[lvl + 1]
        if half < SUBLANES:
            a = _dot_nt(head_rows(u_bf), u_bf)
            yield
            for h in range(N_GLA_HEADS):
                att[h] = att[h] + a[h * C:(h + 1) * C] * m
        else:
            starts = range(half, C, 2 * half)
            pick = lambda x: jnp.concatenate([x[r:r + half] for r in starts], axis=0)
            a = _dot_nt(head_rows(pick(u).astype(bf16)), u_bf)
            yield
            m_right = pick(m)
            for h in range(N_GLA_HEADS):
                upd = a[h * (C // 2):(h + 1) * (C // 2)] * m_right
                parts = []
                for i, r in enumerate(starts):
                    parts += [att[h][r - half:r], att[h][r:r + half] + upd[i * half:(i + 1) * half]]
                att[h] = jnp.concatenate(parts, axis=0)
    o = jnp.concatenate([_dot(att[h].astype(bf16), v_bf[:, h * GLA_DV:(h + 1) * GLA_DV])
                         for h in range(N_GLA_HEADS)], axis=1)
    yield

    q_in = (q * jnp.exp2(bcum)).astype(bf16)
    kd = (k * jnp.exp2(jnp.minimum(b_last - bcum, 0.0))).astype(bf16)
    v_t = v_bf.T
    bdmask = bdmask_ref[...]

    def gla_inter(stbd):
        return _dot_nt(q_in, stbd.astype(bf16))

    def gla_update(stbd, decay_row, kd_rows):
        return stbd * decay_row + bdmask * _dot(v_t, kd_rows)

    def gla_heads(stbd):
        acc = jnp.where(lane_qk == 0, stbd[0:GLA_DV, :], 0.0)
        for h in range(1, N_GLA_HEADS):
            acc = acc + jnp.where(lane_qk == h, stbd[h * GLA_DV:(h + 1) * GLA_DV, :], 0.0)
        return acc.T

    if not is_sample:
        st = st_ref[...]
        stbd = stbd_ref[...]
        y = y + ssd_inter(st)
        o = o + gla_inter(stbd)
        yield
        st_ref[...] = ssd_update(st, jnp.exp2(acum_last), xs_end)
        stbd_ref[...] = gla_update(stbd, jnp.exp2(b_last), kd)
        yield
    else:
        seq_of_row = lax.shift_right_logical(rows, int(np.log2(seq_rows)))
        bm_f32 = xc[:, D_SSD:D_SSD + SSD_GROUPS * SSD_STATE]
        cm_f32 = xc[:, D_SSD + SSD_GROUPS * SSD_STATE:]
        xs_end_f32 = xdt * to_end
        e_last_c = jnp.exp2(cums[C:])
        q_in_f32 = q * jnp.exp2(bcum)
        y_rows, o_rows = [], []
        g0_next = gla0_ref[0].T
        for s in range(n_seq):
            g0 = g0_next
            if s + 1 < n_seq:
                g0_next = gla0_ref[s + 1].T
            mine = seq_of_row == s
            r0 = s * seq_rows
            own = slice(r0, r0 + seq_rows)
            st = ssd0_ref[s]
            c_rows = cm_f32[own].astype(bf16)
            b_rows = bm_f32[own].astype(bf16)
            x_rows = xs_end_f32[own].astype(bf16)
            y_grp, st_new = [], []
            for grp in range(SSD_GROUPS):
                st_g = st[grp * gw:(grp + 1) * gw]
                y_grp.append(_dot_nt(c_rows[:, grp * SSD_STATE:(grp + 1) * SSD_STATE], st_g.astype(bf16)))
                upd = lax.dot_general(x_rows[:, grp * gw:(grp + 1) * gw], b_rows[:, grp * SSD_STATE:(grp + 1) * SSD_STATE],
                                      (((0,), (0,)), ((), ())), preferred_element_type=f32)
                for hh in range(heads_per_group):
                    h = grp * heads_per_group + hh
                    rows_h = slice(hh * SSD_HEAD_DIM, (hh + 1) * SSD_HEAD_DIM)
                    decay_h = jnp.broadcast_to(e_last_c[r0:r0 + 1, h:h + 1], (SSD_HEAD_DIM, SSD_STATE))
                    st_new.append(st_g[rows_h] * decay_h + upd[rows_h])
            y_rows.append(jnp.concatenate(y_grp, axis=1))
            nssd_ref[s] = jnp.concatenate(st_new, axis=0)
            stbd = jnp.concatenate([jnp.where(lane_qk == h, g0, 0.0) for h in range(N_GLA_HEADS)], axis=0)
            o_rows.append(_dot_nt(q_in_f32[r0:r0 + seq_rows].astype(bf16), stbd.astype(bf16)))
            ngla_ref[s] = gla_heads(gla_update(stbd, jnp.exp2(b_last[r0:r0 + 1, :]), jnp.where(mine, kd, 0)))
            yield
        y = y + jnp.concatenate(y_rows, axis=0) * e_acum
        o = o + jnp.concatenate(o_rows, axis=0)

    z_gate = pa_ref[:, OFF_Z:OFF_Z + D_SSD].astype(f32)
    g_gate = pa_ref[:, OFF_G - no_xbc:OFF_G - no_xbc + D_GLA].astype(f32)
    y = (y + dsk_e_ref[...] * xs) * z_gate
    gsz = D_SSD // SSD_GROUPS
    y = jnp.concatenate([_rms(y[:, i * gsz:(i + 1) * gsz], gssd_ref[:, i * gsz:(i + 1) * gsz])
                         for i in range(SSD_GROUPS)], axis=1)
    o = jnp.concatenate([_rms(o[:, h * GLA_DV:(h + 1) * GLA_DV], ggla_ref[...]) for h in range(N_GLA_HEADS)], axis=1)
    o = o * g_gate
    mixed_ref[:, :D_SSD] = y.astype(bf16)
    mixed_ref[:, D_SSD:] = o.astype(bf16)

    if not is_sample:
        yield EPILOGUE

        @pl.when(is_last)
        def _():
            nssd_ref[0] = st_ref[...].T
            ngla_ref[0] = gla_heads(stbd_ref[...])


def _mixer(pa, pb, params, states, n_batch, seq_len, xc=None, cast_weights=()):
    is_sample = states is not None
    assert not (is_sample and cast_weights) and is_sample == (xc is None)
    seq_rows = seq_len if is_sample else CHUNK
    assert CHUNK % seq_rows == 0 and (is_sample or seq_len % CHUNK == 0)
    n_seq = CHUNK // seq_rows
    consts = _mixer_consts(seq_rows)
    ns = SAMPLE_STREAMS if is_sample else PROMPT_STREAMS
    assert n_batch % (ns * n_seq) == 0
    rows_per_stream = n_batch * seq_len // ns
    seqs_per_stream = n_batch // ns

    if is_sample:
        grid = (seqs_per_stream // n_seq,)
        row_map = lambda c: (0, c, 0)
        seq_map = lambda c: (0, c, 0, 0)
        full = lambda shape: pl.BlockSpec(shape, lambda c: (0,) * len(shape))
        sems = ("arbitrary",)
    else:
        n_chunks = seq_len // CHUNK
        grid = (seqs_per_stream, n_chunks)
        row_map = lambda b, c: (0, b * n_chunks + c, 0)
        seq_map = lambda b, c: (0, b, 0, 0)
        full = lambda shape: pl.BlockSpec(shape, lambda b, c: (0,) * len(shape))
        sems = ("arbitrary", "arbitrary")

    def streamed(arr):
        return arr.reshape((ns, arr.shape[0] // ns) + arr.shape[1:])

    taps = CONV_W - 1
    conv_spec = pl.BlockSpec((ns, taps, n_seq, CONV_DIM), lambda *g: (0, 0, seq_map(*g)[1], 0))
    state_blocks = [(n_seq, D_SSD, SSD_STATE), (n_seq, D_QK, GLA_DV)]
    in_arrays = [streamed(pa), streamed(pb)]
    assert pa.shape[1] == (PA_COLS if is_sample else PA_COLS - CONV_DIM)
    in_specs = [pl.BlockSpec((ns, CHUNK, pa.shape[1]), row_map), pl.BlockSpec((ns, CHUNK, PB_COLS), row_map)]
    if is_sample:
        ssd0, conv0, gla0 = states
        conv0_t = jnp.transpose(streamed(conv0), (0, 2, 1, 3))
        in_arrays += [streamed(ssd0), conv0_t, streamed(gla0)]
        in_specs += [pl.BlockSpec((ns,) + state_blocks[0], seq_map), conv_spec,
                     pl.BlockSpec((ns,) + state_blocks[1], seq_map)]
    else:
        in_arrays.append(streamed(xc))
        in_specs.append(pl.BlockSpec((ns, CHUNK, CONV_DIM), row_map))
    for arr in list(params) + consts:
        in_arrays.append(arr)
        in_specs.append(full(arr.shape))

    out_shape = [jax.ShapeDtypeStruct((ns, rows_per_stream, D_MODEL), bf16)]
    out_specs = [pl.BlockSpec((ns, CHUNK, D_MODEL), row_map)]
    if is_sample:
        out_shape.append(jax.ShapeDtypeStruct((ns, taps, seqs_per_stream, CONV_DIM), f32))
        out_specs.append(conv_spec)
    out_shape += [jax.ShapeDtypeStruct((ns, seqs_per_stream) + blk[1:], f32) for blk in state_blocks]
    out_specs += [pl.BlockSpec((ns,) + blk, seq_map) for blk in state_blocks]
    for w in cast_weights:
        _, w_rows, w_cols = w.shape
        slab = w_rows // N_LOAD
        assert w_rows % N_LOAD == 0 and slab % BF16_ROWS == 0 and grid[0] * grid[1] >= N_LOAD
        step = lambda b, c: jnp.minimum(b * n_chunks + c, N_LOAD - 1)
        in_arrays.append(w)
        in_specs.append(pl.BlockSpec((None, slab, w_cols), lambda b, c: (0, step(b, c), 0)))
        out_shape.append(jax.ShapeDtypeStruct((w_rows, w_cols), bf16))
        out_specs.append(pl.BlockSpec((slab, w_cols), lambda b, c: (step(b, c), 0)))
    if is_sample:
        scratch = [pltpu.VMEM((ns, n_seq, SUBLANES + seq_rows, CONV_DIM), f32)]
    else:
        scratch = [pltpu.VMEM((ns, SSD_STATE, D_SSD), f32), pltpu.VMEM((ns, D_GLA, D_QK), f32)]
    results = pl.pallas_call(
        functools.partial(_mixer_body, seq_rows, len(cast_weights)),
        grid=grid, in_specs=in_specs, out_specs=out_specs, out_shape=out_shape, scratch_shapes=scratch,
        compiler_params=pltpu.CompilerParams(dimension_semantics=sems, vmem_limit_bytes=VMEM_LIMIT_BYTES),
        name="mixer_sample" if is_sample else "mixer_prompt",
    )(*in_arrays)
    merged = lambda arr: arr.reshape((arr.shape[0] * arr.shape[1],) + arr.shape[2:])
    if is_sample:
        mixed, nconv, nssd, ngla = results
        return merged(mixed), merged(jnp.transpose(nconv, (0, 2, 1, 3))), merged(nssd), merged(ngla)
    mixed, nssd, ngla, *cast = results
    return merged(mixed), merged(nssd), merged(ngla), cast


def _row(vec):
    return vec.reshape(1, -1).astype(f32)


def _pad_lanes(vec, width):
    return jnp.concatenate([vec.astype(f32), jnp.zeros((width - vec.shape[0],), f32)]).reshape(1, width)


def _mixer_params(conv_w, conv_b, dt_bias, a_log, d_skip, g_ssd_norm, w_gk2, b_gk, g_gla_norm):
    rep = lambda vec: _row(jnp.repeat(vec, SSD_HEAD_DIM))
    wgk = jnp.zeros((PB_COLS, D_QK), f32).at[N_SSD_HEADS:N_SSD_HEADS + GLA_RANK, :].set(w_gk2.astype(f32))
    return [conv_w.astype(f32), _row(conv_b), rep(d_skip), _row(g_ssd_norm),
            _pad_lanes(dt_bias, PB_COLS), _pad_lanes(a_log, PB_COLS), _stack_x3(wgk), _row(b_gk), _row(g_gla_norm)]


def kernel(x_prompt, x_sample, state_ssd, state_conv, state_gla, g_ffn1, w_ffn1_in, w_ffn1_out, g_mix, w_in, conv_w, conv_b, dt_bias, a_log, d_skip, g_ssd_norm, w_gk2, b_gk, g_gla_norm, w_out, g_ffn2, w_ffn2_in, w_ffn2_out, g_final):
    bp, lp, _ = x_prompt.shape
    bs, ls, _ = x_sample.shape
    assert w_in.shape[0] == 1, "single-layer step: the final norm is fused into the layer's last kernel"
    xp = x_prompt.reshape(bp * lp, D_MODEL)
    xs = x_sample.reshape(bs * ls, D_MODEL)
    params = _mixer_params(conv_w[0], conv_b[0], dt_bias[0], a_log[0], d_skip[0], g_ssd_norm[0],
                           w_gk2[0], b_gk[0], g_gla_norm[0])
    x1p, x1s, pap, pas, pbp, pbs, xcp, p_conv = _token_a(
        xp, xs, _row(g_ffn1[0]), w_ffn1_in.astype(f32), w_ffn1_out.astype(f32), _row(g_mix[0]), w_in.astype(f32),
        params[0], params[1], lp)
    mxp, p_ssd, p_gla, (w_out_bf, w_up2_bf, w_dn2_bf) = _mixer(
        pap, pbp, params, None, bp, lp, xc=xcp,
        cast_weights=(w_out.astype(f32), w_ffn2_in.astype(f32), w_ffn2_out.astype(f32)))
    s_states = (state_ssd[0].reshape(bs, D_SSD, SSD_STATE), state_conv[0], state_gla[0].reshape(bs, D_QK, GLA_DV))
    mxs, s_conv, s_ssd, s_gla = _mixer(pas, pbs, params, s_states, bs, ls)
    yp, ys = _token_b(x1p, x1s, mxp, mxs, w_out_bf, _row(g_ffn2[0]), w_up2_bf, w_dn2_bf, _row(g_final))
    return (yp.reshape(bp, lp, D_MODEL), ys.reshape(bs, ls, D_MODEL),
            p_ssd.reshape(1, bp, N_SSD_HEADS, SSD_HEAD_DIM, SSD_STATE), p_conv[None],
            p_gla.reshape(1, bp, N_GLA_HEADS, GLA_DK, GLA_DV),
            s_ssd.reshape(1, bs, N_SSD_HEADS, SSD_HEAD_DIM, SSD_STATE), s_conv[None],
            s_gla.reshape(1, bs, N_GLA_HEADS, GLA_DK, GLA_DV))
```

```python
import functools

import jax
import jax.numpy as jnp
import numpy as np
from jax import lax
from jax.experimental import pallas as pl
from jax.experimental.pallas import tpu as pltpu

f32 = jnp.float32
bf16 = jnp.bfloat16

D_MODEL = 1024
D_SSD = 512
SSD_HEAD_DIM = 64
N_SSD_HEADS = 8
SSD_STATE = 128
SSD_GROUPS = 2
CONV_W = 4
CONV_DIM = D_SSD + 2 * SSD_GROUPS * SSD_STATE
D_GLA = 512
N_GLA_HEADS = 4
GLA_DV = 128
GLA_DK = 64
GLA_RANK = 16
GLA_GATE_NORMALIZER = 16.0
D_FF = 2816
EPS = 1e-6
D_QK = N_GLA_HEADS * GLA_DK
LOG2E = float(np.log2(np.e))

SUBLANES = 8
LANES = 128
BF16_ROWS = 16
VMEM_LIMIT_BYTES = 60 * 1024 * 1024

CHUNK = 128
TOKEN_TILE = 256
TOKEN_B_TILE = 512

PA_COLS = D_SSD + CONV_DIM + 2 * D_QK + 2 * D_GLA
PB_COLS = LANES
OFF_Z, OFF_XBC, OFF_Q, OFF_K, OFF_V, OFF_G = 0, 512, 1536, 1792, 2048, 2560


def _dot(a, b):
    return jnp.dot(a, b, preferred_element_type=f32)


def _dot_nt(a, b):
    return lax.dot_general(a, b, (((1,), (1,)), ((), ())), preferred_element_type=f32)


def _split3(x):
    hi = x.astype(bf16)
    r1 = x - hi.astype(f32)
    mid = r1.astype(bf16)
    lo = (r1 - mid.astype(f32)).astype(bf16)
    return hi, mid, lo


def _sel_dot(sel3, x):
    return _dot(sel3, jnp.concatenate(_split3(x), axis=0))


def _tiled3(sel):
    return np.concatenate([sel] * 3, axis=1)


def _split2(x):
    hi = x.astype(bf16)
    return hi, (x - hi.astype(f32)).astype(bf16)


def _dot_x3(a, b_stack):
    a_hi, a_lo = _split2(a)
    return _dot(jnp.concatenate([a_hi, a_hi, a_lo], axis=1), b_stack)


def _stack_x3(b):
    b_hi, b_lo = _split2(b)
    return jnp.concatenate([b_hi, b_lo, b_hi], axis=0)


def _rms(x, g):
    return x * lax.rsqrt(jnp.mean(x * x, axis=-1, keepdims=True) + EPS) * g


N_LOAD = 16

W_IN_DT = D_SSD + CONV_DIM
W_IN_Q = W_IN_DT + N_SSD_HEADS
W_IN_LR = W_IN_Q + 2 * D_QK + 2 * D_GLA
W_IN_COLS = W_IN_LR + GLA_RANK


def _load_slab(step, slab_ref, dst_ref, regroup=None):
    rows = slab_ref.shape[0]
    assert rows % BF16_ROWS == 0
    val = slab_ref[...]
    if regroup is not None:
        val = regroup(val)
    dst_ref[pl.ds(pl.multiple_of(step * rows, BF16_ROWS), rows), :] = val.astype(bf16)


def _regroup_w_in(w):
    small = jnp.concatenate([w[:, W_IN_DT:W_IN_Q], w[:, W_IN_LR:W_IN_COLS],
                             jnp.zeros((w.shape[0], PB_COLS - N_SSD_HEADS - GLA_RANK), w.dtype)], axis=1)
    return jnp.concatenate([w[:, :W_IN_DT], w[:, W_IN_Q:W_IN_LR], small], axis=1)


ROW_STREAMS = 2


def _token_a_phases(row0, n_rows, x_ref, x1_ref, pa_ref, pb_ref, g1, wup, wdn, gmix, win, under_first_matmul=None,
                    keep_xbc=None):
    rows = pl.ds(row0, n_rows)
    x = x_ref[rows, :]
    gu = _dot(_rms(x, g1[...]).astype(bf16), wup[...])
    if under_first_matmul is not None:
        under_first_matmul()
    yield
    act = (jax.nn.silu(gu[:, :D_FF]) * gu[:, D_FF:]).astype(bf16)
    x1 = x + 0.5 * _dot(act, wdn[...])
    yield
    x1_ref[rows, :] = x1
    pr = _dot(_rms(x1, gmix[...]).astype(bf16), win[...])
    yield
    pb_ref[rows, :] = pr[:, PA_COLS:]
    pa_ref[rows, :OFF_XBC] = jax.nn.silu(pr[:, OFF_Z:OFF_Z + D_SSD]).astype(bf16)
    g_gate = jax.nn.silu(pr[:, OFF_G:OFF_G + D_GLA]).astype(bf16)
    if keep_xbc is None:
        pa_ref[rows, OFF_XBC:OFF_G] = pr[:, OFF_XBC:OFF_G].astype(bf16)
        pa_ref[rows, OFF_G:] = g_gate
    else:
        pa_ref[rows, OFF_XBC:OFF_G - CONV_DIM] = pr[:, OFF_Q:OFF_G].astype(bf16)
        pa_ref[rows, OFF_G - CONV_DIM:] = g_gate
        keep_xbc[pl.ds(row0, n_rows), :] = pr[:, OFF_XBC:OFF_XBC + CONV_DIM]


def _token_a_body(n_prompt, tiles_per_seq, xp, xs, g1, wup_f, wdn_f, gmix, win_f, convw, convb,
                  x1p, x1s, pap, pas, pbp, pbs, xc, nconv, wup, wdn, win, cbuf, stash):
    i = pl.program_id(0)
    tile = xp.shape[0]
    part = tile // ROW_STREAMS

    def shift_one_row(a):
        above = jnp.concatenate([a[-1:], a[:-1]], axis=0)
        last = lax.broadcasted_iota(jnp.int32, (1, SUBLANES, 1), 1) == SUBLANES - 1
        return pltpu.roll(jnp.where(last, above, a), 1, axis=1)

    def conv_previous_tile():
        t = lax.rem(i - N_LOAD - 1, tiles_per_seq)
        cbuf[SUBLANES:, :] = stash[...]
        cbuf[0:SUBLANES, :] = jnp.where(t == 0, 0.0, cbuf[0:SUBLANES, :])
        ext = cbuf[...].reshape(tile // SUBLANES + 1, SUBLANES, CONV_DIM)
        acc = ext * convw[0:1, :]
        for tap in range(1, CONV_W):
            acc = shift_one_row(acc) + ext * convw[tap:tap + 1, :]
        xc[...] = jax.nn.silu(acc[1:].reshape(tile, CONV_DIM) + convb[...]).astype(bf16)
        nconv[0] = cbuf[SUBLANES + tile - (CONV_W - 1):SUBLANES + tile, :]
        cbuf[0:SUBLANES, :] = cbuf[tile:tile + SUBLANES, :]

    def streams(x_ref, x1_ref, pa_ref, pb_ref, pending, keep_xbc):
        _round_robin([_token_a_phases(j * part, part, x_ref, x1_ref, pa_ref, pb_ref, g1, wup, wdn, gmix, win,
                                      pending if j == 0 else None, keep_xbc) for j in range(ROW_STREAMS)])

    @pl.when(i < N_LOAD)
    def _():
        _load_slab(i, wup_f, wup)
        _load_slab(i, wdn_f, wdn)
        _load_slab(i, win_f, win, _regroup_w_in)

    @pl.when(i == 0)
    def _():
        cbuf[...] = jnp.zeros(cbuf.shape, f32)
        stash[...] = jnp.zeros(stash.shape, f32)

    first_sample = N_LOAD + n_prompt
    pl.when(jnp.logical_and(i >= N_LOAD, i < first_sample))(
        lambda: streams(xp, x1p, pap, pbp, conv_previous_tile, stash))
    pl.when(i == first_sample)(conv_previous_tile)
    pl.when(i >= first_sample)(lambda: streams(xs, x1s, pas, pbs, None, None))


def _token_b_phases(rows, x_ref, m_ref, y_ref, wout, g2, wup, wdn, gfin):
    x2 = x_ref[rows, :] + _dot(m_ref[rows, :], wout[...])
    yield
    gu = _dot(_rms(x2, g2[...]).astype(bf16), wup[...])
    yield
    act = (jax.nn.silu(gu[:, :D_FF]) * gu[:, D_FF:]).astype(bf16)
    x3 = x2 + 0.5 * _dot(act, wdn[...])
    yield
    y_ref[rows, :] = _rms(x3, gfin[...])


def _token_b_body(n_prompt, x1p, x1s, mxp, mxs, wout, g2, wup, wdn, gfin, yp, ys):
    def compute(x_ref, m_ref, y_ref):
        part = x_ref.shape[0] // ROW_STREAMS
        _round_robin([_token_b_phases(pl.ds(j * part, part), x_ref, m_ref, y_ref, wout, g2, wup, wdn, gfin)
                      for j in range(ROW_STREAMS)])

    i = pl.program_id(0)
    pl.when(i < n_prompt)(lambda: compute(x1p, mxp, yp))
    pl.when(i >= n_prompt)(lambda: compute(x1s, mxs, ys))


def _two_group_specs(n_prompt, cols, tile, first=0):
    prompt = pl.BlockSpec((tile, cols), lambda i: (jnp.clip(i - first, 0, n_prompt - 1), 0))
    sample = pl.BlockSpec((tile, cols), lambda i: (jnp.maximum(i - first - n_prompt, 0), 0))
    return prompt, sample


def _whole(shape):
    return pl.BlockSpec(shape, lambda i: (0,) * len(shape), pipeline_mode=pl.Buffered(1))


def _slabs(weight):
    _, rows, cols = weight.shape
    assert rows % N_LOAD == 0
    return pl.BlockSpec((None, rows // N_LOAD, cols), lambda i: (0, jnp.minimum(i, N_LOAD - 1), 0))


def _token_a(xp, xs, g1, wup, wdn, gmix, win, conv_w, conv_b, prompt_len):
    tp, ts = xp.shape[0], xs.shape[0]
    tile = TOKEN_TILE
    n_prompt, n_sample = tp // tile, ts // tile
    assert tp % tile == 0 and ts % tile == 0 and win.shape[2] == W_IN_COLS and prompt_len % tile == 0
    tiles_per_seq = prompt_len // tile
    n_seq = tp // prompt_len
    xin = _two_group_specs(n_prompt, D_MODEL, tile, N_LOAD)
    pa = (_two_group_specs(n_prompt, PA_COLS - CONV_DIM, tile, N_LOAD)[0],
          _two_group_specs(n_prompt, PA_COLS, tile, N_LOAD)[1])
    pb = _two_group_specs(n_prompt, PB_COLS, tile, N_LOAD)
    conv_tile = lambda i: jnp.clip(i - N_LOAD - 1, 0, n_prompt - 1)
    return pl.pallas_call(
        functools.partial(_token_a_body, n_prompt, tiles_per_seq),
        grid=(N_LOAD + n_prompt + n_sample,),
        in_specs=[*xin, _whole(g1.shape), _slabs(wup), _slabs(wdn), _whole(gmix.shape), _slabs(win),
                  _whole(conv_w.shape), _whole(conv_b.shape)],
        out_specs=[*xin, *pa, *pb, pl.BlockSpec((tile, CONV_DIM), lambda i: (conv_tile(i), 0)),
                   pl.BlockSpec((1, CONV_W - 1, CONV_DIM), lambda i: (conv_tile(i) // tiles_per_seq, 0, 0))],
        out_shape=[jax.ShapeDtypeStruct((tp, D_MODEL), f32), jax.ShapeDtypeStruct((ts, D_MODEL), f32),
                   jax.ShapeDtypeStruct((tp, PA_COLS - CONV_DIM), bf16), jax.ShapeDtypeStruct((ts, PA_COLS), bf16),
                   jax.ShapeDtypeStruct((tp, PB_COLS), f32), jax.ShapeDtypeStruct((ts, PB_COLS), f32),
                   jax.ShapeDtypeStruct((tp, CONV_DIM), bf16),
                   jax.ShapeDtypeStruct((n_seq, CONV_W - 1, CONV_DIM), f32)],
        scratch_shapes=[pltpu.VMEM(wup.shape[1:], bf16), pltpu.VMEM(wdn.shape[1:], bf16),
                        pltpu.VMEM((D_MODEL, PA_COLS + PB_COLS), bf16),
                        pltpu.VMEM((SUBLANES + tile, CONV_DIM), f32), pltpu.VMEM((tile, CONV_DIM), f32)],
        compiler_params=pltpu.CompilerParams(dimension_semantics=("arbitrary",), vmem_limit_bytes=VMEM_LIMIT_BYTES),
        name="token_a",
    )(xp, xs, g1, wup, wdn, gmix, win, conv_w, conv_b)


def _token_b(x1p, x1s, mxp, mxs, wout, g2, wup, wdn, gfin):
    tp, ts = x1p.shape[0], x1s.shape[0]
    tile = TOKEN_B_TILE
    assert tp % tile == 0 and ts % tile == 0
    n_prompt, n_sample = tp // tile, ts // tile
    xin = _two_group_specs(n_prompt, D_MODEL, tile)
    return pl.pallas_call(
        functools.partial(_token_b_body, n_prompt),
        grid=(n_prompt + n_sample,),
        in_specs=[*xin, *xin, _whole(wout.shape), _whole(g2.shape), _whole(wup.shape), _whole(wdn.shape),
                  _whole(gfin.shape)],
        out_specs=[*xin],
        out_shape=[jax.ShapeDtypeStruct((tp, D_MODEL), f32), jax.ShapeDtypeStruct((ts, D_MODEL), f32)],
        compiler_params=pltpu.CompilerParams(dimension_semantics=("arbitrary",), vmem_limit_bytes=VMEM_LIMIT_BYTES),
        name="token_b",
    )(x1p, x1s, mxp, mxs, wout, g2, wup, wdn, gfin)


def _level_halves(seq_rows):
    return [h for h in (1, 2, 4, 8, 16, 32, 64) if 2 * h <= seq_rows]


def _matmul_levels(seq_rows):
    return [h for h in _level_halves(seq_rows) if h < SUBLANES]


def _mixer_consts(seq_rows):
    t = np.arange(CHUNK)
    is_sample = seq_rows < CHUNK
    same = (t[:, None] // seq_rows) == (t[None, :] // seq_rows)
    tri = (same & (t[None, :] <= t[:, None])).astype(np.float32)
    last = ((t[:, None] // seq_rows) * seq_rows + seq_rows - 1 == t[None, :]).astype(np.float32)
    cum_gla, pair_mask = [tri], [np.eye(CHUNK, dtype=np.float32)]
    for h in _level_halves(seq_rows):
        ref_row = (t // (2 * h)) * (2 * h) + h
        if h in _matmul_levels(seq_rows):
            cum_gla.append((ref_row[:, None] == t[None, :]).astype(np.float32) @ tri)
        right = (t % (2 * h)) >= h
        blk = (t[:, None] // (2 * h)) == (t[None, :] // (2 * h))
        pair_mask.append((blk & right[:, None] & ~right[None, :]).astype(np.float32))
    cum_ssd = [tri]
    if is_sample:
        cum_gla.append(last @ tri)
        cum_ssd.append(last @ tri)
    hq = np.arange(D_QK) // GLA_DK
    hv = np.arange(D_GLA) // GLA_DV
    head_of_lane = np.arange(D_SSD) // SSD_HEAD_DIM
    expand = (np.arange(PB_COLS)[:, None] == head_of_lane[None, :]).astype(np.float32)
    consts = [jnp.asarray(_tiled3(np.concatenate(cum_ssd, 0)), bf16),
              jnp.asarray(_tiled3(np.concatenate(cum_gla, 0)), bf16),
              jnp.asarray(np.stack(pair_mask, 0), f32),
              jnp.asarray((hv[:, None] == hq[None, :]).astype(np.float32), f32),
              jnp.asarray(np.concatenate([expand] * 3, 0), bf16)]
    return consts


PROMPT_STREAMS = 8
SAMPLE_STREAMS = 1
N_SHARED_REFS = 14
EPILOGUE = "epilogue"


def _round_robin(streams):
    live = list(streams)
    waiting = []
    while live:
        for s in list(live):
            try:
                if next(s) == EPILOGUE:
                    live.remove(s)
                    waiting.append(s)
            except StopIteration:
                live.remove(s)
    for s in waiting:
        for _ in s:
            pass


def _mixer_body(seq_rows, n_cast, *refs):
    is_sample = seq_rows < CHUNK
    n_streams = refs[0].shape[0]
    n_in = 5 if is_sample else 3
    n_out = 4 if is_sample else 3
    n_shared = N_SHARED_REFS
    ins, shared = refs[:n_in], refs[n_in:n_in + n_shared]
    cast_in = refs[n_in + n_shared:n_in + n_shared + n_cast]
    outs = refs[n_in + n_shared + n_cast:n_in + n_shared + n_cast + n_out]
    cast_out = refs[n_in + n_shared + n_cast + n_out:n_in + n_shared + 2 * n_cast + n_out]
    scratch = refs[n_in + n_shared + 2 * n_cast + n_out:]
    is_last = None
    if not is_sample:
        st_ref, stbd_ref = scratch
        c_idx = pl.program_id(1)
        is_last = c_idx == pl.num_programs(1) - 1

        @pl.when(pl.program_id(0) * pl.num_programs(1) + c_idx < N_LOAD)
        def _():
            for src, dst in zip(cast_in, cast_out):
                dst[...] = src[...].astype(bf16)

        @pl.when(c_idx == 0)
        def _():
            st_ref[...] = jnp.zeros(st_ref.shape, f32)
            stbd_ref[...] = jnp.zeros(stbd_ref.shape, f32)

    at = lambda group, j: [r.at[j] for r in group]
    _round_robin([_mixer_phases(seq_rows, at(ins, j), shared, at(outs, j), at(scratch, j), is_last)
                  for j in range(n_streams)])


def _mixer_phases(seq_rows, ins, shared, outs, scratch, is_last):
    is_sample = seq_rows < CHUNK
    n_seq = CHUNK // seq_rows
    halves = _level_halves(seq_rows)
    n_lvl = len(halves)
    if is_sample:
        pa_ref, pb_ref, ssd0_ref, conv0_ref, gla0_ref = ins
        (cbuf,) = scratch
        mixed_ref, nconv_ref, nssd_ref, ngla_ref = outs
    else:
        pa_ref, pb_ref, xc_ref = ins
        st_ref, stbd_ref = scratch
        mixed_ref, nssd_ref, ngla_ref = outs
    (convw_ref, convb_ref, dsk_e_ref, gssd_ref, dtb_c_ref, alog_c_ref, wgk_ref, bgk_ref, ggla_ref,
     cum_ssd_ref, cum_gla_ref, pair_mask_ref, bdmask_ref, expand_ref) = shared

    C = CHUNK
    no_xbc = 0 if is_sample else CONV_DIM
    rows = lax.broadcasted_iota(jnp.int32, (C, 1), 0)
    lane_qk = lax.shift_right_logical(lax.broadcasted_iota(jnp.int32, (1, D_QK), 1), int(np.log2(GLA_DK)))

    if is_sample:
        conv = jnp.broadcast_to(convb_ref[...], (C, CONV_DIM))
        for tap in range(CONV_W - 1):
            cbuf[:, SUBLANES - (CONV_W - 1) + tap, :] = conv0_ref[tap]
        cbuf[:, SUBLANES:SUBLANES + seq_rows, :] = (
            pa_ref[:, OFF_XBC:OFF_XBC + CONV_DIM].astype(f32).reshape(n_seq, seq_rows, CONV_DIM))
        for i in range(CONV_W):
            shift = CONV_W - 1 - i
            win = cbuf[:, SUBLANES - shift:SUBLANES - shift + seq_rows, :].reshape(C, CONV_DIM)
            conv = conv + win * convw_ref[i:i + 1, :]
        for tap in range(CONV_W - 1):
            nconv_ref[tap] = cbuf[:, SUBLANES + seq_rows - (CONV_W - 1) + tap, :]
        xc = jax.nn.silu(conv)
    else:
        xc = xc_ref[...]
    xs = xc[:, :D_SSD].astype(f32)
    bm = xc[:, D_SSD:D_SSD + SSD_GROUPS * SSD_STATE].astype(bf16)
    cm = xc[:, D_SSD + SSD_GROUPS * SSD_STATE:].astype(bf16)

    small = pb_ref[...]
    dtp_c = jax.nn.softplus(small + dtb_c_ref[...])
    cums = _sel_dot(cum_ssd_ref[...], dtp_c * (-LOG2E * jnp.exp(alog_c_ref[...])))
    yield
    acum_c = cums[:C]
    acum_t = acum_c.T
    wide = _dot(jnp.concatenate(_split3(jnp.concatenate([cums, dtp_c], axis=0)), axis=1), expand_ref[...])
    yield
    acum_e, dtp_e = wide[:C], wide[-C:]
    acum_last = wide[C:2 * C] if is_sample else acum_e[C - 1:C, :]
    causal = cum_ssd_ref[0:C, 0:C].astype(f32) > 0
    xdt = xs * dtp_e
    xdt_bf = xdt.astype(bf16)
    lane = lax.broadcasted_iota(jnp.int32, (1, LANES), 1)
    left = lane < SSD_HEAD_DIM
    heads_per_group = N_SSD_HEADS // SSD_GROUPS
    gw = heads_per_group * SSD_HEAD_DIM
    y_parts = []
    for grp in range(SSD_GROUPS):
        cb = _dot_nt(cm[:, grp * SSD_STATE:(grp + 1) * SSD_STATE], bm[:, grp * SSD_STATE:(grp + 1) * SSD_STATE])
        for pair in range(heads_per_group // 2):
            sc = []
            for hh in range(2):
                h = grp * heads_per_group + pair * 2 + hh
                seg = acum_c[:, h:h + 1] - acum_t[h:h + 1, :]
                sc.append((cb * jnp.exp2(jnp.where(causal, seg, -jnp.inf))).astype(bf16))
            lo = (grp * heads_per_group + pair * 2) * SSD_HEAD_DIM
            xp = xdt_bf[:, lo:lo + LANES]
            bd = jnp.concatenate([jnp.where(left, xp, 0), jnp.where(left, 0, xp)], axis=0)
            y_parts.append(_dot(jnp.concatenate(sc, axis=1), bd))
            yield
    y = jnp.concatenate(y_parts, axis=1)

    to_end = jnp.exp2(jnp.minimum(acum_last - acum_e, 0.0))
    xs_end = (xdt * to_end).astype(bf16)
    e_acum = jnp.exp2(acum_e)
    bm_t = bm.T

    def ssd_inter(st):
        st_bf = st.astype(bf16)
        return jnp.concatenate([_dot(cm[:, grp * SSD_STATE:(grp + 1) * SSD_STATE], st_bf[:, grp * gw:(grp + 1) * gw])
                                for grp in range(SSD_GROUPS)], axis=1) * e_acum

    def ssd_update(st, decay_row, xs_rows):
        upd = jnp.concatenate([_dot(bm_t[grp * SSD_STATE:(grp + 1) * SSD_STATE, :], xs_rows[:, grp * gw:(grp + 1) * gw])
                               for grp in range(SSD_GROUPS)], axis=1)
        return st * decay_row + upd

    q = pa_ref[:, OFF_Q - no_xbc:OFF_Q - no_xbc + D_QK].astype(f32) * (GLA_DK ** -0.5)
    k = pa_ref[:, OFF_K - no_xbc:OFF_K - no_xbc + D_QK].astype(f32)
    v_bf = pa_ref[:, OFF_V - no_xbc:OFF_V - no_xbc + D_GLA]
    gk = _dot_x3(small, wgk_ref[...]) + bgk_ref[...]
    log_a = jax.nn.log_sigmoid(gk) * (LOG2E / GLA_GATE_NORMALIZER)
    bsel = _sel_dot(cum_gla_ref[...], log_a)
    yield
    bcum = bsel[:C]
    n_mm = len(_matmul_levels(seq_rows))
    b_last = bsel[(n_mm + 1) * C:] if is_sample else bcum[C - 1:C, :]
    k_bf = k.astype(bf16)

    def b_ref(lvl, half):
        if lvl < n_mm:
            return bsel[(lvl + 1) * C:(lvl + 2) * C]
        return jnp.concatenate([jnp.broadcast_to(bcum[r + half:r + half + 1, :], (2 * half, D_QK))
                                for r in range(0, C, 2 * half)], axis=0)

    def head_rows(x_bf):
        return jnp.concatenate([jnp.where(lane_qk == h, x_bf, jnp.zeros_like(x_bf)) for h in range(N_GLA_HEADS)], axis=0)

    a = _dot_nt(head_rows(q.astype(bf16)), k_bf)
    yield
    m = pair_mask_ref[0]
    att = [a[h * C:(h + 1) * C] * m for h in range(N_GLA_HEADS)]
    for lvl, half in enumerate(halves):
        decay = jnp.exp2(-jnp.abs(bcum - b_ref(lvl, half)))
        in_right = (rows & half) != 0
        u = jnp.where(in_right, q, k) * decay
        u_bf = u.astype(bf16)
        m = pair_mask_ref[lvl + 1]
        if half < SUBLANES:
            a = _dot_nt(head_rows(u_bf), u_bf)
            yield
            for h in range(N_GLA_HEADS):
                att[h] = att[h] + a[h * C:(h + 1) * C] * m
        else:
            starts = range(half, C, 2 * half)
            pick = lambda x: jnp.concatenate([x[r:r + half] for r in starts], axis=0)
            a = _dot_nt(head_rows(pick(u).astype(bf16)), u_bf)
            yield
            m_right = pick(m)
            for h in range(N_GLA_HEADS):
                upd = a[h * (C // 2):(h + 1) * (C // 2)] * m_right
                parts = []
                for i, r in enumerate(starts):
                    parts += [att[h][r - half:r], att[h][r:r + half] + upd[i * half:(i + 1) * half]]
                att[h] = jnp.concatenate(parts, axis=0)
    o = jnp.concatenate([_dot(att[h].astype(bf16), v_bf[:, h * GLA_DV:(h + 1) * GLA_DV])
                         for h in range(N_GLA_HEADS)], axis=1)
    yield

    q_in = (q * jnp.exp2(bcum)).astype(bf16)
    kd = (k * jnp.exp2(jnp.minimum(b_last - bcum, 0.0))).astype(bf16)
    v_t = v_bf.T
    bdmask = bdmask_ref[...]

    def gla_inter(stbd):
        return _dot_nt(q_in, stbd.astype(bf16))

    def gla_update(stbd, decay_row, kd_rows):
        return stbd * decay_row + bdmask * _dot(v_t, kd_rows)

    def gla_heads(stbd):
        acc = jnp.where(lane_qk == 0, stbd[0:GLA_DV, :], 0.0)
        for h in range(1, N_GLA_HEADS):
            acc = acc + jnp.where(lane_qk == h, stbd[h * GLA_DV:(h + 1) * GLA_DV, :], 0.0)
        return acc.T

    if not is_sample:
        st = st_ref[...]
        stbd = stbd_ref[...]
        y = y + ssd_inter(st)
        o = o + gla_inter(stbd)
        yield
        st_ref[...] = ssd_update(st, jnp.exp2(acum_last), xs_end)
        stbd_ref[...] = gla_update(stbd, jnp.exp2(b_last), kd)
        yield
    else:
        seq_of_row = lax.shift_right_logical(rows, int(np.log2(seq_rows)))
        bm_f32 = xc[:, D_SSD:D_SSD + SSD_GROUPS * SSD_STATE]
        cm_f32 = xc[:, D_SSD + SSD_GROUPS * SSD_STATE:]
        xs_end_f32 = xdt * to_end
        e_last_c = jnp.exp2(cums[C:])
        q_in_f32 = q * jnp.exp2(bcum)
        y_rows, o_rows = [], []
        g0_next = gla0_ref[0].T
        for s in range(n_seq):
            g0 = g0_next
            if s + 1 < n_seq:
                g0_next = gla0_ref[s + 1].T
            mine = seq_of_row == s
            r0 = s * seq_rows
            own = slice(r0, r0 + seq_rows)
            st = ssd0_ref[s]
            c_rows = cm_f32[own].astype(bf16)
            b_rows = bm_f32[own].astype(bf16)
            x_rows = xs_end_f32[own].astype(bf16)
            y_grp, st_new = [], []
            for grp in range(SSD_GROUPS):
                st_g = st[grp * gw:(grp + 1) * gw]
                y_grp.append(_dot_nt(c_rows[:, grp * SSD_STATE:(grp + 1) * SSD_STATE], st_g.astype(bf16)))
                upd = lax.dot_general(x_rows[:, grp * gw:(grp + 1) * gw], b_rows[:, grp * SSD_STATE:(grp + 1) * SSD_STATE],
                                      (((0,), (0,)), ((), ())), preferred_element_type=f32)
                for hh in range(heads_per_group):
                    h = grp * heads_per_group + hh
                    rows_h = slice(hh * SSD_HEAD_DIM, (hh + 1) * SSD_HEAD_DIM)
                    decay_h = jnp.broadcast_to(e_last_c[r0:r0 + 1, h:h + 1], (SSD_HEAD_DIM, SSD_STATE))
                    st_new.append(st_g[rows_h] * decay_h + upd[rows_h])
            y_rows.append(jnp.concatenate(y_grp, axis=1))
            nssd_ref[s] = jnp.concatenate(st_new, axis=0)
            stbd = jnp.concatenate([jnp.where(lane_qk == h, g0, 0.0) for h in range(N_GLA_HEADS)], axis=0)
            o_rows.append(_dot_nt(q_in_f32[r0:r0 + seq_rows].astype(bf16), stbd.astype(bf16)))
            ngla_ref[s] = gla_heads(gla_update(stbd, jnp.exp2(b_last[r0:r0 + 1, :]), jnp.where(mine, kd, 0)))
            yield
        y = y + jnp.concatenate(y_rows, axis=0) * e_acum
        o = o + jnp.concatenate(o_rows, axis=0)

    z_gate = pa_ref[:, OFF_Z:OFF_Z + D_SSD].astype(f32)
    g_gate = pa_ref[:, OFF_G - no_xbc:OFF_G - no_xbc + D_GLA].astype(f32)
    y = (y + dsk_e_ref[...] * xs) * z_gate
    gsz = D_SSD // SSD_GROUPS
    y = jnp.concatenate([_rms(y[:, i * gsz:(i + 1) * gsz], gssd_ref[:, i * gsz:(i + 1) * gsz])
                         for i in range(SSD_GROUPS)], axis=1)
    o = jnp.concatenate([_rms(o[:, h * GLA_DV:(h + 1) * GLA_DV], ggla_ref[...]) for h in range(N_GLA_HEADS)], axis=1)
    o = o * g_gate
    mixed_ref[:, :D_SSD] = y.astype(bf16)
    mixed_ref[:, D_SSD:] = o.astype(bf16)

    if not is_sample:
        yield EPILOGUE

        @pl.when(is_last)
        def _():
            nssd_ref[0] = st_ref[...].T
            ngla_ref[0] = gla_heads(stbd_ref[...])


def _mixer(pa, pb, params, states, n_batch, seq_len, xc=None, cast_weights=()):
    is_sample = states is not None
    assert not (is_sample and cast_weights) and is_sample == (xc is None)
    seq_rows = seq_len if is_sample else CHUNK
    assert CHUNK % seq_rows == 0 and (is_sample or seq_len % CHUNK == 0)
    n_seq = CHUNK // seq_rows
    consts = _mixer_consts(seq_rows)
    ns = SAMPLE_STREAMS if is_sample else PROMPT_STREAMS
    assert n_batch % (ns * n_seq) == 0
    rows_per_stream = n_batch * seq_len // ns
    seqs_per_stream = n_batch // ns

    if is_sample:
        grid = (seqs_per_stream // n_seq,)
        row_map = lambda c: (0, c, 0)
        seq_map = lambda c: (0, c, 0, 0)
        full = lambda shape: pl.BlockSpec(shape, lambda c: (0,) * len(shape))
        sems = ("arbitrary",)
    else:
        n_chunks = seq_len // CHUNK
        grid = (seqs_per_stream, n_chunks)
        row_map = lambda b, c: (0, b * n_chunks + c, 0)
        seq_map = lambda b, c: (0, b, 0, 0)
        full = lambda shape: pl.BlockSpec(shape, lambda b, c: (0,) * len(shape))
        sems = ("arbitrary", "arbitrary")

    def streamed(arr):
        return arr.reshape((ns, arr.shape[0] // ns) + arr.shape[1:])

    taps = CONV_W - 1
    conv_spec = pl.BlockSpec((ns, taps, n_seq, CONV_DIM), lambda *g: (0, 0, seq_map(*g)[1], 0))
    state_blocks = [(n_seq, D_SSD, SSD_STATE), (n_seq, D_QK, GLA_DV)]
    in_arrays = [streamed(pa), streamed(pb)]
    assert pa.shape[1] == (PA_COLS if is_sample else PA_COLS - CONV_DIM)
    in_specs = [pl.BlockSpec((ns, CHUNK, pa.shape[1]), row_map), pl.BlockSpec((ns, CHUNK, PB_COLS), row_map)]
    if is_sample:
        ssd0, conv0, gla0 = states
        conv0_t = jnp.transpose(streamed(conv0), (0, 2, 1, 3))
        in_arrays += [streamed(ssd0), conv0_t, streamed(gla0)]
        in_specs += [pl.BlockSpec((ns,) + state_blocks[0], seq_map), conv_spec,
                     pl.BlockSpec((ns,) + state_blocks[1], seq_map)]
    else:
        in_arrays.append(streamed(xc))
        in_specs.append(pl.BlockSpec((ns, CHUNK, CONV_DIM), row_map))
    for arr in list(params) + consts:
        in_arrays.append(arr)
        in_specs.append(full(arr.shape))

    out_shape = [jax.ShapeDtypeStruct((ns, rows_per_stream, D_MODEL), bf16)]
    out_specs = [pl.BlockSpec((ns, CHUNK, D_MODEL), row_map)]
    if is_sample:
        out_shape.append(jax.ShapeDtypeStruct((ns, taps, seqs_per_stream, CONV_DIM), f32))
        out_specs.append(conv_spec)
    out_shape += [jax.ShapeDtypeStruct((ns, seqs_per_stream) + blk[1:], f32) for blk in state_blocks]
    out_specs += [pl.BlockSpec((ns,) + blk, seq_map) for blk in state_blocks]
    for w in cast_weights:
        _, w_rows, w_cols = w.shape
        slab = w_rows // N_LOAD
        assert w_rows % N_LOAD == 0 and slab % BF16_ROWS == 0 and grid[0] * grid[1] >= N_LOAD
        step = lambda b, c: jnp.minimum(b * n_chunks + c, N_LOAD - 1)
        in_arrays.append(w)
        in_specs.append(pl.BlockSpec((None, slab, w_cols), lambda b, c: (0, step(b, c), 0)))
        out_shape.append(jax.ShapeDtypeStruct((w_rows, w_cols), bf16))
        out_specs.append(pl.BlockSpec((slab, w_cols), lambda b, c: (step(b, c), 0)))
    if is_sample:
        scratch = [pltpu.VMEM((ns, n_seq, SUBLANES + seq_rows, CONV_DIM), f32)]
    else:
        scratch = [pltpu.VMEM((ns, SSD_STATE, D_SSD), f32), pltpu.VMEM((ns, D_GLA, D_QK), f32)]
    results = pl.pallas_call(
        functools.partial(_mixer_body, seq_rows, len(cast_weights)),
        grid=grid, in_specs=in_specs, out_specs=out_specs, out_shape=out_shape, scratch_shapes=scratch,
        compiler_params=pltpu.CompilerParams(dimension_semantics=sems, vmem_limit_bytes=VMEM_LIMIT_BYTES),
        name="mixer_sample" if is_sample else "mixer_prompt",
    )(*in_arrays)
    merged = lambda arr: arr.reshape((arr.shape[0] * arr.shape[1],) + arr.shape[2:])
    if is_sample:
        mixed, nconv, nssd, ngla = results
        return merged(mixed), merged(jnp.transpose(nconv, (0, 2, 1, 3))), merged(nssd), merged(ngla)
    mixed, nssd, ngla, *cast = results
    return merged(mixed), merged(nssd), merged(ngla), cast


def _row(vec):
    return vec.reshape(1, -1).astype(f32)


def _pad_lanes(vec, width):
    return jnp.concatenate([vec.astype(f32), jnp.zeros((width - vec.shape[0],), f32)]).reshape(1, width)


def _mixer_params(conv_w, conv_b, dt_bias, a_log, d_skip, g_ssd_norm, w_gk2, b_gk, g_gla_norm):
    rep = lambda vec: _row(jnp.repeat(vec, SSD_HEAD_DIM))
    wgk = jnp.zeros((PB_COLS, D_QK), f32).at[N_SSD_HEADS:N_SSD_HEADS + GLA_RANK, :].set(w_gk2.astype(f32))
    return [conv_w.astype(f32), _row(conv_b), rep(d_skip), _row(g_ssd_norm),
            _pad_lanes(dt_bias, PB_COLS), _pad_lanes(a_log, PB_COLS), _stack_x3(wgk), _row(b_gk), _row(g_gla_norm)]


def kernel(x_prompt, x_sample, state_ssd, state_conv, state_gla, g_ffn1, w_ffn1_in, w_ffn1_out, g_mix, w_in, conv_w, conv_b, dt_bias, a_log, d_skip, g_ssd_norm, w_gk2, b_gk, g_gla_norm, w_out, g_ffn2, w_ffn2_in, w_ffn2_out, g_final):
    bp, lp, _ = x_prompt.shape
    bs, ls, _ = x_sample.shape
    assert w_in.shape[0] == 1, "single-layer step: the final norm is fused into the layer's last kernel"
    xp = x_prompt.reshape(bp * lp, D_MODEL)
    xs = x_sample.reshape(bs * ls, D_MODEL)
    params = _mixer_params(conv_w[0], conv_b[0], dt_bias[0], a_log[0], d_skip[0], g_ssd_norm[0],
                           w_gk2[0], b_gk[0], g_gla_norm[0])
    x1p, x1s, pap, pas, pbp, pbs, xcp, p_conv = _token_a(
        xp, xs, _row(g_ffn1[0]), w_ffn1_in.astype(f32), w_ffn1_out.astype(f32), _row(g_mix[0]), w_in.astype(f32),
        params[0], params[1], lp)
    mxp, p_ssd, p_gla, (w_out_bf, w_up2_bf, w_dn2_bf) = _mixer(
        pap, pbp, params, None, bp, lp, xc=xcp,
        cast_weights=(w_out.astype(f32), w_ffn2_in.astype(f32), w_ffn2_out.astype(f32)))
    s_states = (state_ssd[0].reshape(bs, D_SSD, SSD_STATE), state_conv[0], state_gla[0].reshape(bs, D_QK, GLA_DV))
    mxs, s_conv, s_ssd, s_gla = _mixer(pas, pbs, params, s_states, bs, ls)
    yp, ys = _token_b(x1p, x1s, mxp, mxs, w_out_bf, _row(g_ffn2[0]), w_up2_bf, w_dn2_bf, _row(g_final))
    return (yp.reshape(bp, lp, D_MODEL), ys.reshape(bs, ls, D_MODEL),
            p_ssd.reshape(1, bp, N_SSD_HEADS, SSD_HEAD_DIM, SSD_STATE), p_conv[None],
            p_gla.reshape(1, bp, N_GLA_HEADS, GLA_DK, GLA_DV),
            s_ssd.reshape(1, bs, N_SSD_HEADS, SSD_HEAD_DIM, SSD_STATE), s_conv[None],
            s_gla.reshape(1, bs, N_GLA_HEADS, GLA_DK, GLA_DV))
```
